```python
import jax, jax.numpy as jnp
from jax import lax
import numpy as np

D_MODEL = 2048
BATCH = 4
SEQ = 4096
DEPTH = 2

GRID_W = 64
CTX_LEN = 256
HEAD_DIM = 64
RWKV_HEADS = D_MODEL // (2 * HEAD_DIM)
RWKV_DIM = RWKV_HEADS * HEAD_DIM
DECAY_LORA = 64
ICLR_LORA = 64
GATE_LORA = 160
RWKV_COLS = 3 * RWKV_DIM + 2 * DECAY_LORA + 2 * ICLR_LORA + GATE_LORA
GQA_Q_HEADS = D_MODEL // (2 * HEAD_DIM)
GQA_KV_HEADS = GQA_Q_HEADS // 4
GQA_COLS = (GQA_Q_HEADS + 2 * GQA_KV_HEADS) * HEAD_DIM
IN_COLS_AB = GQA_COLS + RWKV_COLS
OUT_COLS_AB = GQA_Q_HEADS * HEAD_DIM + RWKV_DIM
NA_HEADS = D_MODEL // HEAD_DIM
NA_ROWS_MAX = 8
NA_COLS = 16
D_FF = 4 * D_MODEL
Q_BLOCK = 128
ROPE_THETA = 10000.0
ROPE_PAIRS_PER_AXIS = HEAD_DIM // 4
NORM_EPS = 1e-6
LNX_EPS = 64e-5

kernel_name = 'hybrid_dit_rwkv7_gqa_natten'


def _split(t, sizes):
    return jnp.split(t, [int(s) for s in np.cumsum(sizes)[:-1]], axis=-1)


def _heads(t, n_heads):
    return t.reshape(t.shape[:-1] + (n_heads, t.shape[-1] // n_heads))


def _rms_norm(x, g):
    xf = x.astype(jnp.float32)
    y = xf * lax.rsqrt(jnp.mean(xf * xf, axis=-1, keepdims=True) + NORM_EPS)
    return (y * g.astype(jnp.float32)).astype(x.dtype)


def _modulate(h, shift, scale):
    return h * (1.0 + scale) + shift


def _sq_relu_mlp(h, w1, w2):
    return jnp.square(jax.nn.relu(h @ w1)) @ w2


def _qshift_grid(p):
    b, t, ch = p.shape
    rows = t // GRID_W
    p4 = p.reshape(b, rows, GRID_W, ch // 4, 4)
    from_left = jnp.pad(p4[..., 0], ((0, 0), (0, 0), (1, 0), (0, 0)))[:, :, :-1]
    from_right = jnp.pad(p4[..., 1], ((0, 0), (0, 0), (0, 1), (0, 0)))[:, :, 1:]
    from_up = jnp.pad(p4[..., 2], ((0, 0), (1, 0), (0, 0), (0, 0)))[:, :-1]
    from_down = jnp.pad(p4[..., 3], ((0, 0), (0, 1), (0, 0), (0, 0)))[:, 1:]
    return jnp.stack([from_left, from_right, from_up, from_down], axis=-1).reshape(b, t, ch)


def _shift_seq(p):
    b, t, ch = p.shape
    p2 = p.reshape(b, t, ch // 2, 2)
    prev = jnp.pad(p2[..., 0], ((0, 0), (1, 0), (0, 0)))[:, :-1]
    nxt = jnp.pad(p2[..., 1], ((0, 0), (0, 1), (0, 0)))[:, 1:]
    return jnp.stack([prev, nxt], axis=-1).reshape(b, t, ch)


def _axial_rope(n):
    t = jnp.arange(n, dtype=jnp.int32)
    row = (t // GRID_W).astype(jnp.float32)
    col = (t % GRID_W).astype(jnp.float32)
    inv = ROPE_THETA ** (-jnp.arange(ROPE_PAIRS_PER_AXIS, dtype=jnp.float32) / ROPE_PAIRS_PER_AXIS)
    ang = jnp.concatenate([row[:, None] * inv, col[:, None] * inv], axis=-1)
    return jnp.cos(ang), jnp.sin(ang)


def _rope(x, cos, sin):
    half = x.shape[-1] // 2
    xf = x.astype(jnp.float32)
    x1, x2 = xf[..., :half], xf[..., half:]
    cs, sn = cos[None, :, None, :], sin[None, :, None, :]
    return jnp.concatenate([x1 * cs - x2 * sn, x2 * cs + x1 * sn], axis=-1).astype(x.dtype)


def _gqa_dense(q5, k, v):
    s = jnp.einsum('bqkgd,bskd->bkgqs', q5, k).astype(jnp.float32)
    p = jax.nn.softmax(s, axis=-1).astype(v.dtype)
    return jnp.einsum('bkgqs,bskd->bqkgd', p, v)


def _gqa_blocks(q, k, v):
    b, t, hq, dh = q.shape
    hkv = k.shape[2]
    nb = t // Q_BLOCK
    qb = q.reshape(b, nb, Q_BLOCK, hkv, hq // hkv, dh).transpose(1, 0, 2, 3, 4, 5)
    o = lax.map(lambda qi: _gqa_dense(qi, k, v), qb)
    return o.transpose(1, 0, 2, 3, 4, 5).reshape(b, t, hq * dh)


def _rwkv7_scan(r, decay, k, v, a, b, s0, reverse):
    def step(s, inp):
        r_t, w_t, k_t, v_t, a_t, b_t = inp
        sa = jnp.einsum('bhij,bhj->bhi', s, a_t)
        s = s * w_t[:, :, None, :] + sa[..., None] * b_t[:, :, None, :] + v_t[..., None] * k_t[:, :, None, :]
        return s, jnp.einsum('bhij,bhj->bhi', s, r_t)
    xs = tuple(jnp.moveaxis(t, 1, 0) for t in (r, decay, k, v, a, b))
    s_final, ys = lax.scan(step, s0, xs, reverse=reverse)
    return jnp.moveaxis(ys, 0, 1), s_final


def _rwkv7_prep(xm, w0_f, w0_b, ww2_f, ww2_b, a0_f, a0_b, wa2_f, wa2_b, wg2, k_k, k_a, r_k):
    f32 = jnp.float32
    r, k, v, xw_f, xw_b, xa_f, xa_b, xg = _split(
        xm, (RWKV_DIM,) * 3 + (DECAY_LORA,) * 2 + (ICLR_LORA,) * 2 + (GATE_LORA,))
    kk = _heads((k * k_k).astype(f32), RWKV_HEADS)
    kk = kk * lax.rsqrt(jnp.maximum(jnp.sum(kk * kk, axis=-1, keepdims=True), 1e-12))
    rh = _heads(r.astype(f32), RWKV_HEADS)
    vh = _heads(v.astype(f32), RWKV_HEADS)
    dirs = []
    for w0, ww2, a0, wa2, xw, xa in ((w0_f, ww2_f, a0_f, wa2_f, xw_f, xa_f),
                                     (w0_b, ww2_b, a0_b, wa2_b, xw_b, xa_b)):
        log_w = -jax.nn.softplus(-(w0 + jnp.tanh(xw) @ ww2).astype(f32)) - 0.5
        decay = jnp.exp(-jnp.exp(log_w))
        iclr = jax.nn.sigmoid((a0 + xa @ wa2).astype(f32))
        k_dir = _heads(k.astype(f32) * (1.0 + (iclr - 1.0) * k_a), RWKV_HEADS)
        iclr = _heads(iclr, RWKV_HEADS)
        bonus = jnp.sum(rh * k_dir * r_k, axis=-1, keepdims=True) * vh
        dirs.append((_heads(decay, RWKV_HEADS), k_dir, -kk, kk * iclr, bonus))
    gate = jax.nn.sigmoid(xg) @ wg2
    return rh, vh, dirs, gate


def _rwkv7_bidirectional(xm_lat, xm_ctx, w0_f, w0_b, ww2_f, ww2_b, a0_f, a0_b, wa2_f, wa2_b,
                         wg2, k_k, k_a, r_k, lnx_g, lnx_b):
    r_l, v_l, dirs_l, gate_l = _rwkv7_prep(xm_lat, w0_f, w0_b, ww2_f, ww2_b, a0_f, a0_b, wa2_f, wa2_b, wg2, k_k, k_a, r_k)
    r_c, v_c, dirs_c, gate_c = _rwkv7_prep(xm_ctx, w0_f, w0_b, ww2_f, ww2_b, a0_f, a0_b, wa2_f, wa2_b, wg2, k_k, k_a, r_k)
    s0 = jnp.zeros(r_c.shape[:1] + (RWKV_HEADS, HEAD_DIM, HEAD_DIM), jnp.float32)
    ys_l = []
    ys_c = []
    for dl, dc, reverse in zip(dirs_l, dirs_c, (False, True)):
        y_c, s_ctx = _rwkv7_scan(r_c, dc[0], dc[1], v_c, dc[2], dc[3], s0, reverse)
        y_l, _ = _rwkv7_scan(r_l, dl[0], dl[1], v_l, dl[2], dl[3], s_ctx, reverse)
        ys_l.append(y_l)
        ys_c.append(y_c)

    def finish(ys, dirs, gate, dtype):
        y = ys[0] + ys[1]
        mu = jnp.mean(y, axis=-1, keepdims=True)
        var = jnp.mean(jnp.square(y - mu), axis=-1, keepdims=True)
        yn = ((y - mu) * lax.rsqrt(var + LNX_EPS)).reshape(y.shape[:2] + (RWKV_DIM,))
        bonus = (dirs[0][4] + dirs[1][4]).reshape(yn.shape)
        return ((yn * lnx_g + lnx_b + bonus) * gate).astype(dtype)

    return finish(ys_l, dirs_l, gate_l, xm_lat.dtype), finish(ys_c, dirs_c, gate_c, xm_ctx.dtype)


def _mixer_rwkv7_gqa(h_lat, h_ctx, w_in, shift_mu, w0_f, w0_b, ww2_f, ww2_b, a0_f, a0_b, wa2_f, wa2_b,
                     wg2, k_k, k_a, r_k, lnx_g, lnx_b, q_norm, k_norm, w_out, need_ctx):
    p_lat = h_lat @ w_in
    p_ctx = h_ctx @ w_in
    qkv_sizes = (GQA_Q_HEADS * HEAD_DIM, GQA_KV_HEADS * HEAD_DIM, GQA_KV_HEADS * HEAD_DIM)
    q_l, k_l, v_l = _split(p_lat[..., :GQA_COLS], qkv_sizes)
    q_c, k_c, v_c = _split(p_ctx[..., :GQA_COLS], qkv_sizes)
    scale = HEAD_DIM ** -0.5
    cos, sin = _axial_rope(h_lat.shape[1])
    q_l = _rope(_rms_norm(_heads(q_l, GQA_Q_HEADS), q_norm), cos, sin) * scale
    k_l = _rope(_rms_norm(_heads(k_l, GQA_KV_HEADS), k_norm), cos, sin)
    q_c = _rms_norm(_heads(q_c, GQA_Q_HEADS), q_norm) * scale
    k_c = _rms_norm(_heads(k_c, GQA_KV_HEADS), k_norm)
    v_l = _heads(v_l, GQA_KV_HEADS)
    v_c = _heads(v_c, GQA_KV_HEADS)
    k_all = jnp.concatenate([k_l, k_c], axis=1)
    v_all = jnp.concatenate([v_l, v_c], axis=1)
    o_gqa_l = _gqa_blocks(q_l, k_all, v_all)

    rw_l = p_lat[..., GQA_COLS:]
    rw_c = p_ctx[..., GQA_COLS:]
    rw_l = rw_l + shift_mu * (_qshift_grid(rw_l) - rw_l)
    rw_c = rw_c + shift_mu * (_shift_seq(rw_c) - rw_c)
    o_rwkv_l, o_rwkv_c = _rwkv7_bidirectional(rw_l, rw_c, w0_f, w0_b, ww2_f, ww2_b, a0_f, a0_b, wa2_f, wa2_b,
                                              wg2, k_k, k_a, r_k, lnx_g, lnx_b)
    out_l = jnp.concatenate([o_gqa_l, o_rwkv_l], axis=-1) @ w_out
    if not need_ctx:
        return out_l, None
    b, n = q_c.shape[:2]
    q_c5 = q_c.reshape(b, n, GQA_KV_HEADS, GQA_Q_HEADS // GQA_KV_HEADS, HEAD_DIM)
    o_gqa_c = _gqa_dense(q_c5, k_c, v_c).reshape(b, n, GQA_Q_HEADS * HEAD_DIM)
    out_c = jnp.concatenate([o_gqa_c, o_rwkv_c], axis=-1) @ w_out
    return out_l, out_c


def _neighbourhood_attention(q, k, v, k_ctx, v_ctx, rpb):
    b, t, h, dh = q.shape
    rows = t // GRID_W
    kr = min(NA_ROWS_MAX, rows)
    kc = NA_COLS
    nk = kr * kc
    q_rows = q.reshape(b, rows, GRID_W, h, dh).transpose(1, 0, 2, 3, 4)
    col = jnp.arange(GRID_W, dtype=jnp.int32)
    col_start = jnp.clip(col - kc // 2, 0, GRID_W - kc)
    key_cols = col_start[:, None] + jnp.arange(kc, dtype=jnp.int32)[None, :]
    dcol = key_cols - col[:, None] + (NA_COLS - 1)

    def row_step(args):
        i, qi = args
        row_start = jnp.clip(i - kr // 2, 0, rows - kr)
        key_rows = row_start + jnp.arange(kr, dtype=jnp.int32)
        idx = (key_rows[None, :, None] * GRID_W + key_cols[:, None, :]).reshape(GRID_W, nk)
        kg = jnp.take(k, idx, axis=1)
        vg = jnp.take(v, idx, axis=1)
        drow = key_rows - i + (NA_ROWS_MAX - 1)
        bias = rpb[:, drow][:, :, dcol]
        bias = bias.transpose(0, 2, 1, 3).reshape(h, GRID_W, nk).astype(jnp.float32)
        s_win = jnp.einsum('bqhd,bqnhd->bhqn', qi, kg).astype(jnp.float32) + bias
        s_ctx = jnp.einsum('bqhd,bchd->bhqc', qi, k_ctx).astype(jnp.float32)
        p = jax.nn.softmax(jnp.concatenate([s_win, s_ctx], axis=-1), axis=-1).astype(v.dtype)
        return (jnp.einsum('bhqn,bqnhd->bqhd', p[..., :nk], vg)
                + jnp.einsum('bhqc,bchd->bqhd', p[..., nk:], v_ctx))

    o = lax.map(row_step, (jnp.arange(rows, dtype=jnp.int32), q_rows))
    return o.transpose(1, 0, 2, 3, 4).reshape(b, t, h * dh)


def _mixer_neighbourhood(h_lat, h_ctx, w_qkv, rpb, w_out, need_ctx):
    width = NA_HEADS * HEAD_DIM
    scale = HEAD_DIM ** -0.5
    q_l, k_l, v_l = [_heads(t, NA_HEADS) for t in _split(h_lat @ w_qkv, (width, width, width))]
    if need_ctx:
        q_c, k_c, v_c = [_heads(t, NA_HEADS) for t in _split(h_ctx @ w_qkv, (width, width, width))]
    else:
        k_c, v_c = [_heads(t, NA_HEADS) for t in _split(h_ctx @ w_qkv[:, width:], (width, width))]
    out_l = _neighbourhood_attention(q_l * scale, k_l, v_l, k_c, v_c, rpb) @ w_out
    if not need_ctx:
        return out_l, None
    b, n = q_c.shape[:2]
    o_c = _gqa_dense((q_c * scale)[:, :, :, None, :], k_c, v_c).reshape(b, n, width)
    return out_l, o_c @ w_out


def setup_inputs(seed: int = 0) -> dict:
    key = jax.random.key(seed)
    keys = jax.random.split(key, 64)
    counter = [0]

    def nxt():
        kk = keys[counter[0]]
        counter[0] += 1
        return kk

    def nrm(shape, scale):
        return jax.random.normal(nxt(), shape, jnp.float32) * scale

    def gain(shape):
        return 1.0 + nrm(shape, 0.1)

    def unif(shape, lo, hi):
        return jax.random.uniform(nxt(), shape, jnp.float32, lo, hi)

    d = D_MODEL
    inp = {}
    inp['x'] = nrm((BATCH, SEQ, d), 1.0)
    inp['c'] = nrm((BATCH, d), 1.0)
    inp['ctx'] = nrm((BATCH, CTX_LEN, d), 1.0)
    inp['c_ctx'] = nrm((d,), 1.0)
    inp['l0_norm1'] = gain((d,))
    inp['l0_norm2'] = gain((d,))
    inp['l0_ada_w'] = nrm((d, 6 * d), 0.5 * d ** -0.5)
    inp['l0_ada_b'] = nrm((6 * d,), 0.01)
    inp['l0_w_in'] = nrm((d, IN_COLS_AB), d ** -0.5)
    inp['l0_shift_mu'] = unif((RWKV_COLS,), 0.0, 1.0)
    inp['l0_w0_f'] = unif((RWKV_DIM,), -4.0, 1.0)
    inp['l0_w0_b'] = unif((RWKV_DIM,), -4.0, 1.0)
    inp['l0_ww2_f'] = nrm((DECAY_LORA, RWKV_DIM), 0.1)
    inp['l0_ww2_b'] = nrm((DECAY_LORA, RWKV_DIM), 0.1)
    inp['l0_a0_f'] = nrm((RWKV_DIM,), 0.1)
    inp['l0_a0_b'] = nrm((RWKV_DIM,), 0.1)
    inp['l0_wa2_f'] = nrm((ICLR_LORA, RWKV_DIM), 0.1)
    inp['l0_wa2_b'] = nrm((ICLR_LORA, RWKV_DIM), 0.1)
    inp['l0_wg2'] = nrm((GATE_LORA, RWKV_DIM), GATE_LORA ** -0.5)
    inp['l0_k_k'] = 0.85 + nrm((RWKV_DIM,), 0.1)
    inp['l0_k_a'] = gain((RWKV_DIM,))
    inp['l0_r_k'] = nrm((RWKV_HEADS, HEAD_DIM), 0.1)
    inp['l0_lnx_g'] = gain((RWKV_DIM,))
    inp['l0_lnx_b'] = nrm((RWKV_DIM,), 0.01)
    inp['l0_q_norm'] = gain((HEAD_DIM,))
    inp['l0_k_norm'] = gain((HEAD_DIM,))
    inp['l0_w_out'] = nrm((OUT_COLS_AB, d), OUT_COLS_AB ** -0.5)
    inp['l0_mlp_w1'] = nrm((d, D_FF), d ** -0.5)
    inp['l0_mlp_w2'] = nrm((D_FF, d), D_FF ** -0.5)
    inp['l1_norm1'] = gain((d,))
    inp['l1_norm2'] = gain((d,))
    inp['l1_ada_w'] = nrm((d, 6 * d), 0.5 * d ** -0.5)
    inp['l1_ada_b'] = nrm((6 * d,), 0.01)
    inp['l1_w_qkv'] = nrm((d, 3 * NA_HEADS * HEAD_DIM), d ** -0.5)
    inp['l1_rpb'] = nrm((NA_HEADS, 2 * NA_ROWS_MAX - 1, 2 * NA_COLS - 1), 0.02)
    inp['l1_w_out'] = nrm((NA_HEADS * HEAD_DIM, d), (NA_HEADS * HEAD_DIM) ** -0.5)
    inp['l1_mlp_w1'] = nrm((d, D_FF), d ** -0.5)
    inp['l1_mlp_w2'] = nrm((D_FF, d), D_FF ** -0.5)
    inp['final_norm'] = gain((d,))
    return inp


def reference(x, c, ctx, c_ctx,
              l0_norm1, l0_norm2, l0_ada_w, l0_ada_b, l0_w_in, l0_shift_mu,
              l0_w0_f, l0_w0_b, l0_ww2_f, l0_ww2_b, l0_a0_f, l0_a0_b, l0_wa2_f, l0_wa2_b,
              l0_wg2, l0_k_k, l0_k_a, l0_r_k, l0_lnx_g, l0_lnx_b, l0_q_norm, l0_k_norm,
              l0_w_out, l0_mlp_w1, l0_mlp_w2,
              l1_norm1, l1_norm2, l1_ada_w, l1_ada_b, l1_w_qkv, l1_rpb, l1_w_out,
              l1_mlp_w1, l1_mlp_w2, final_norm):
    norm1 = (l0_norm1, l1_norm1)
    norm2 = (l0_norm2, l1_norm2)
    ada_w = (l0_ada_w, l1_ada_w)
    ada_b = (l0_ada_b, l1_ada_b)
    mlp_w1 = (l0_mlp_w1, l1_mlp_w1)
    mlp_w2 = (l0_mlp_w2, l1_mlp_w2)
    for layer in range(DEPTH):
        last = layer == DEPTH - 1
        mod = jax.nn.silu(c) @ ada_w[layer] + ada_b[layer]
        mod_c = jax.nn.silu(c_ctx) @ ada_w[layer] + ada_b[layer]
        sh1, sc1, g1, sh2, sc2, g2 = jnp.split(mod[:, None, :], 6, axis=-1)
        csh1, csc1, cg1, csh2, csc2, cg2 = jnp.split(mod_c, 6, axis=-1)
        h = _modulate(_rms_norm(x, norm1[layer]), sh1, sc1)
        hc = _modulate(_rms_norm(ctx, norm1[layer]), csh1, csc1)
        if layer % 2 == 0:
            o, oc = _mixer_rwkv7_gqa(h, hc, l0_w_in, l0_shift_mu, l0_w0_f, l0_w0_b, l0_ww2_f, l0_ww2_b,
                                     l0_a0_f, l0_a0_b, l0_wa2_f, l0_wa2_b, l0_wg2, l0_k_k, l0_k_a, l0_r_k,
                                     l0_lnx_g, l0_lnx_b, l0_q_norm, l0_k_norm, l0_w_out, not last)
        else:
            o, oc = _mixer_neighbourhood(h, hc, l1_w_qkv, l1_rpb, l1_w_out, not last)
        x = x + g1 * o
        x = x + g2 * _sq_relu_mlp(_modulate(_rms_norm(x, norm2[layer]), sh2, sc2), mlp_w1[layer], mlp_w2[layer])
        if not last:
            ctx = ctx + cg1 * oc
            ctx = ctx + cg2 * _sq_relu_mlp(_modulate(_rms_norm(ctx, norm2[layer]), csh2, csc2),
                                           mlp_w1[layer], mlp_w2[layer])
    return _rms_norm(x, final_norm)
```

```python
import functools

import jax
import jax.numpy as jnp
import numpy as np
from jax import lax
from jax.experimental import pallas as pl
from jax.experimental.pallas import tpu as pltpu

F32 = jnp.float32
MXU = jnp.bfloat16

LANES = 128
HEAD_DIM = 64
HEADS_PER_BLOCK = LANES // HEAD_DIM
GRID_W = 64
NORM_EPS = 1e-6
LNX_EPS = 64e-5
ROPE_THETA = 10000.0
NA_ROWS = 8
NA_COLS = 16
NA_GROUP = 4
NA_WIN = NA_GROUP + NA_ROWS
MASK_VALUE = -1e30
SCAN_CHUNK = 64
SCAN_PAIRS = 4
VMEM_LIMIT = 56 * 1024 * 1024


def _cparams(sem):
    return pltpu.CompilerParams(dimension_semantics=sem, vmem_limit_bytes=VMEM_LIMIT)


def _mm(a, b):
    return jnp.dot(a.astype(MXU), b.astype(MXU), preferred_element_type=F32)


def _mm_nt(a, b):
    return lax.dot_general(a.astype(MXU), b.astype(MXU), (((1,), (1,)), ((), ())),
                           preferred_element_type=F32)


def _mm_tn(a, b):
    return lax.dot_general(a.astype(MXU), b.astype(MXU), (((0,), (0,)), ((), ())),
                           preferred_element_type=F32)


def _split3(x):
    hi = x.astype(MXU)
    r1 = x - hi.astype(F32)
    mid = r1.astype(MXU)
    lo = (r1 - mid.astype(F32)).astype(MXU)
    return hi, mid, lo


def _mm_exact_lhs(m, x):
    hi, mid, lo = _split3(x)
    mb = m.astype(MXU)
    return (jnp.dot(mb, hi, preferred_element_type=F32) + jnp.dot(mb, mid, preferred_element_type=F32)
            + jnp.dot(mb, lo, preferred_element_type=F32))


def _head_ones():
    r = lax.broadcasted_iota(jnp.int32, (LANES, LANES), 0) // HEAD_DIM
    c = lax.broadcasted_iota(jnp.int32, (LANES, LANES), 1) // HEAD_DIM
    return jnp.where(r == c, 1.0, 0.0).astype(MXU)


def _headsum(x, ones):
    hi, mid, lo = _split3(x)
    return (jnp.dot(hi, ones, preferred_element_type=F32) + jnp.dot(mid, ones, preferred_element_type=F32)
            + jnp.dot(lo, ones, preferred_element_type=F32))


def _lane(shape):
    return lax.broadcasted_iota(jnp.int32, shape, len(shape) - 1)


def _sigmoid(x):
    return 1.0 / (1.0 + jnp.exp(-x))


def _ada_kernel(c_ref, w_ref, b_ref, o_ref):
    c = c_ref[...]
    a = c * _sigmoid(c)
    o_ref[...] = _mm(a, w_ref[...]) + b_ref[...]


def _ada_mod(cc, w, bias):
    d, n = w.shape
    tn = 1024
    return pl.pallas_call(
        _ada_kernel,
        out_shape=jax.ShapeDtypeStruct((cc.shape[0], n), F32),
        grid=(n // tn,),
        in_specs=[pl.BlockSpec((cc.shape[0], d), lambda j: (0, 0)),
                  pl.BlockSpec((d, tn), lambda j: (0, j)),
                  pl.BlockSpec((1, tn), lambda j: (0, j))],
        out_specs=pl.BlockSpec((cc.shape[0], tn), lambda j: (0, j)),
        compiler_params=_cparams(("arbitrary",)),
        name="ada_mod",
    )(cc, w, bias.reshape(1, n))


def _norm_mod(x, g, shift, scale):
    y = x * lax.rsqrt(jnp.mean(x * x, axis=-1, keepdims=True) + NORM_EPS)
    return (y * g) * (1.0 + scale) + shift


def _inproj_kernel(x_ref, g_ref, sh_ref, sc_ref, w_ref, o_ref, a_scr):
    @pl.when(pl.program_id(2) == 0)
    def _():
        a_scr[...] = _norm_mod(x_ref[0], g_ref[...], sh_ref[0], sc_ref[0]).astype(a_scr.dtype)

    o_ref[0] = jnp.dot(a_scr[...], w_ref[...], preferred_element_type=F32).astype(o_ref.dtype)


def _inproj(x, g, shift, scale, w, tm, tn, out_dtype):
    b, t, d = x.shape
    n = w.shape[1]
    return pl.pallas_call(
        _inproj_kernel,
        out_shape=jax.ShapeDtypeStruct((b, t, n), out_dtype),
        grid=(b, t // tm, n // tn),
        in_specs=[pl.BlockSpec((1, tm, d), lambda bi, i, j: (bi, i, 0)),
                  pl.BlockSpec((1, d), lambda bi, i, j: (0, 0)),
                  pl.BlockSpec((1, 1, d), lambda bi, i, j: (bi, 0, 0)),
                  pl.BlockSpec((1, 1, d), lambda bi, i, j: (bi, 0, 0)),
                  pl.BlockSpec((d, tn), lambda bi, i, j: (0, j))],
        out_specs=pl.BlockSpec((1, tm, tn), lambda bi, i, j: (bi, i, j)),
        scratch_shapes=[pltpu.VMEM((tm, d), MXU)],
        compiler_params=_cparams(("parallel", "parallel", "arbitrary")),
        name="inproj",
    )(x, g.reshape(1, d), shift, scale, w)


def _swap_half_heads(x):
    first = (_lane(x.shape) % HEAD_DIM) < HEAD_DIM // 2
    return jnp.where(first, pltpu.roll(x, LANES - HEAD_DIM // 2, 1), pltpu.roll(x, HEAD_DIM // 2, 1))


def _gqa_prep_kernel(q_ref, kv_ref, cos_ref, sin_ref, qg_ref, kg_ref, qo_ref, ko_ref, vo_ref, *, n_q, n_kv):
    ones = _head_ones()
    cos = cos_ref[...]
    sin = sin_ref[...]
    first_head = _lane(cos.shape) < HEAD_DIM

    def norm_rope(x, g):
        ms = _headsum(x * x, ones) * (1.0 / HEAD_DIM)
        y = x * lax.rsqrt(ms + NORM_EPS) * g
        return y * cos + _swap_half_heads(y) * sin

    def dup(x, half):
        rolled = pltpu.roll(x, HEAD_DIM, 1)
        return jnp.where(first_head, x, rolled) if half == 0 else jnp.where(first_head, rolled, x)

    for j in range(n_q):
        x = q_ref[0, :, j * LANES:(j + 1) * LANES]
        qo_ref[0, :, j * LANES:(j + 1) * LANES] = (norm_rope(x, qg_ref[...]) * HEAD_DIM ** -0.5).astype(qo_ref.dtype)
    for j in range(n_kv):
        k = norm_rope(kv_ref[0, :, j * LANES:(j + 1) * LANES], kg_ref[...])
        v = kv_ref[0, :, (n_kv + j) * LANES:(n_kv + j + 1) * LANES]
        for half in range(HEADS_PER_BLOCK):
            ko_ref[0, HEADS_PER_BLOCK * j + half] = dup(k, half).astype(ko_ref.dtype)
            vo_ref[0, HEADS_PER_BLOCK * j + half] = dup(v, half).astype(vo_ref.dtype)


def _gqa_prep(p, cos, sin, qg, kg, q_width, kv_width, tm):
    b, t, _ = p.shape
    n_q = q_width // LANES
    n_kv = kv_width // LANES
    kvh = kv_width // HEAD_DIM
    kern = functools.partial(_gqa_prep_kernel, n_q=n_q, n_kv=n_kv)
    kv_spec = pl.BlockSpec((1, kvh, tm, LANES), lambda bi, i: (bi, 0, i, 0))
    return pl.pallas_call(
        kern,
        out_shape=(jax.ShapeDtypeStruct((b, t, q_width), MXU),
                   jax.ShapeDtypeStruct((b, kvh, t, LANES), MXU),
                   jax.ShapeDtypeStruct((b, kvh, t, LANES), MXU)),
        grid=(b, t // tm),
        in_specs=[pl.BlockSpec((1, tm, q_width), lambda bi, i: (bi, i, 0)),
                  pl.BlockSpec((1, tm, 2 * kv_width), lambda bi, i: (bi, i, q_width // (2 * kv_width))),
                  pl.BlockSpec((tm, LANES), lambda bi, i: (i, 0)),
                  pl.BlockSpec((tm, LANES), lambda bi, i: (i, 0)),
                  pl.BlockSpec((1, LANES), lambda bi, i: (0, 0)),
                  pl.BlockSpec((1, LANES), lambda bi, i: (0, 0))],
        out_specs=(pl.BlockSpec((1, tm, q_width), lambda bi, i: (bi, i, 0)), kv_spec, kv_spec),
        compiler_params=_cparams(("parallel", "parallel")),
        name="gqa_prep",
    )(p, p, cos, sin, qg, kg)


def _softmax_pv(q, keys, values, biases):
    first_head = _lane(q.shape) < HEAD_DIM
    outs = []
    for half in range(HEADS_PER_BLOCK):
        zero = jnp.zeros_like(q)
        qm = jnp.where(first_head, q, zero) if half == 0 else jnp.where(first_head, zero, q)
        scores = []
        for k, bias in zip(keys, biases):
            s = _mm_nt(qm, k)
            scores.append(s if bias is None else s + bias[half])
        m = functools.reduce(jnp.maximum, [jnp.max(s, axis=-1, keepdims=True) for s in scores])
        ps = [jnp.exp(s - m) for s in scores]
        l = functools.reduce(jnp.add, [jnp.sum(p, axis=-1, keepdims=True) for p in ps])
        o = functools.reduce(jnp.add, [_mm(p, v) for p, v in zip(ps, values)])
        outs.append(o / l)
    return jnp.where(first_head, outs[0], outs[1])


def _gqa_attn_kernel(q_ref, *refs, n_sets):
    o_ref = refs[-1]
    keys = [refs[2 * i][0, 0] for i in range(n_sets)]
    values = [refs[2 * i + 1][0, 0] for i in range(n_sets)]
    o_ref[0] = _softmax_pv(q_ref[0], keys, values, [None] * n_sets).astype(o_ref.dtype)


def _gqa_attn(q, kv_sets, tq):
    b, t, width = q.shape
    n_blocks = width // LANES
    kvh = kv_sets[0][0].shape[1]
    blocks_per_kv = n_blocks // kvh
    in_specs = [pl.BlockSpec((1, tq, LANES), lambda bi, p, i: (bi, i, p))]
    args = [q]
    for k, v in kv_sets:
        s = k.shape[2]
        spec = pl.BlockSpec((1, 1, s, LANES), lambda bi, p, i: (bi, p // blocks_per_kv, 0, 0))
        in_specs += [spec, spec]
        args += [k, v]
    return pl.pallas_call(
        functools.partial(_gqa_attn_kernel, n_sets=len(kv_sets)),
        out_shape=jax.ShapeDtypeStruct((b, t, width), MXU),
        grid=(b, n_blocks, t // tq),
        in_specs=in_specs,
        out_specs=pl.BlockSpec((1, tq, LANES), lambda bi, p, i: (bi, i, p)),
        compiler_params=_cparams(("parallel", "parallel", "arbitrary")),
        name="gqa_attn",
    )(*args)


def _rwkv_prep_kernel(p_ref, up_ref, dn_ref, mu_ref, lw_w_ref, la_w_ref, lg_w_ref, w0_ref, a0_ref, kk_ref, ka_ref,
                      rk_ref, r_o, v_o, kk_o, lw_o, kd_o, bb_o, bonus_o, gate_o, xm_scr, *, is_ctx, dim):
    tm = p_ref.shape[1]
    n_all = p_ref.shape[2] // LANES
    n_dim = dim // LANES
    i = pl.program_id(1)
    n_i = pl.num_programs(1)
    row = lax.broadcasted_iota(jnp.int32, (tm, LANES), 0)
    cls = _lane((tm, LANES)) % 4
    if is_ctx:
        first = row == 0
        last = row == tm - 1
    else:
        first = (row % GRID_W) == 0
        last = (row % GRID_W) == GRID_W - 1
        up_ok = jnp.logical_or(row >= GRID_W, i > 0)
        dn_ok = jnp.logical_or(row < tm - GRID_W, i < n_i - 1)

    for j in range(n_all):
        sl = slice(j * LANES, (j + 1) * LANES)
        p = p_ref[0, :, sl]
        prev = jnp.where(first, 0.0, pltpu.roll(p, 1, 0))
        nxt = jnp.where(last, 0.0, pltpu.roll(p, tm - 1, 0))
        if is_ctx:
            sh = jnp.where(cls % 2 == 0, prev, nxt)
        else:
            up = jnp.concatenate([up_ref[0, :, sl], p[:tm - GRID_W]], axis=0)
            dn = jnp.concatenate([p[GRID_W:], dn_ref[0, :, sl]], axis=0)
            up = jnp.where(up_ok, up, 0.0)
            dn = jnp.where(dn_ok, dn, 0.0)
            sh = jnp.where(cls == 0, prev, jnp.where(cls == 1, nxt, jnp.where(cls == 2, up, dn)))
        xm_scr[:, sl] = p + mu_ref[:, sl] * (sh - p)

    ones = _head_ones()
    x_w = jnp.tanh(xm_scr[:, 3 * dim:3 * dim + LANES]).astype(MXU)
    x_a = xm_scr[:, 3 * dim + LANES:3 * dim + 2 * LANES].astype(MXU)
    x_g = _sigmoid(xm_scr[:, 3 * dim + 2 * LANES:3 * dim + 4 * LANES]).astype(MXU)
    for j in range(n_dim):
        sl = slice(j * LANES, (j + 1) * LANES)
        r = xm_scr[:, sl]
        k = xm_scr[:, dim + j * LANES:dim + (j + 1) * LANES]
        v = xm_scr[:, 2 * dim + j * LANES:2 * dim + (j + 1) * LANES]
        kkr = k * kk_ref[:, sl]
        kk = kkr * lax.rsqrt(jnp.maximum(_headsum(kkr * kkr, ones), 1e-12))
        r_o[0, :, sl] = r
        v_o[0, :, sl] = v
        kk_o[0, :, sl] = kk
        gate_o[0, :, sl] = jnp.dot(x_g, lg_w_ref[:, sl], preferred_element_type=F32)
        bonus = jnp.zeros_like(r)
        for d in range(2):
            dsl = slice(d * dim + j * LANES, d * dim + (j + 1) * LANES)
            z = w0_ref[:, dsl] + jnp.dot(x_w, lw_w_ref[:, dsl], preferred_element_type=F32)
            softplus = jnp.maximum(-z, 0.0) + jnp.log(1.0 + jnp.exp(-jnp.abs(z)))
            lw_o[d, 0, :, sl] = -jnp.exp(-softplus - 0.5)
            iclr = _sigmoid(a0_ref[:, dsl] + jnp.dot(x_a, la_w_ref[:, dsl], preferred_element_type=F32))
            kd = k * (1.0 + (iclr - 1.0) * ka_ref[:, sl])
            kd_o[d, 0, :, sl] = kd
            bb_o[d, 0, :, sl] = kk * iclr
            bonus = bonus + _headsum(r * kd * rk_ref[:, sl], ones) * v
        bonus_o[0, :, sl] = bonus


def _rwkv_prep(p, mu, lw_w, la_w, lg_w, w0, a0, k_k, k_a, r_k, dim, tm, is_ctx):
    b, t, width = p.shape
    if is_ctx:
        assert t == tm
    else:
        assert tm % GRID_W == 0 and t % tm == 0
    hb = tm // GRID_W
    n_halo = t // GRID_W
    kern = functools.partial(_rwkv_prep_kernel, is_ctx=is_ctx, dim=dim)
    vec = lambda n: pl.BlockSpec((1, n), lambda bi, i: (0, 0))
    mat = lambda k, n: pl.BlockSpec((k, n), lambda bi, i: (0, 0))
    one = jax.ShapeDtypeStruct((b, t, dim), F32)
    two = jax.ShapeDtypeStruct((2, b, t, dim), F32)
    one_spec = pl.BlockSpec((1, tm, dim), lambda bi, i: (bi, i, 0))
    two_spec = pl.BlockSpec((2, 1, tm, dim), lambda bi, i: (0, bi, i, 0))
    return pl.pallas_call(
        kern,
        out_shape=(one, one, one, two, two, two, one, one),
        grid=(b, t // tm),
        in_specs=[pl.BlockSpec((1, tm, width), lambda bi, i: (bi, i, 0)),
                  pl.BlockSpec((1, GRID_W, width), lambda bi, i: (bi, jnp.maximum(i * hb - 1, 0), 0)),
                  pl.BlockSpec((1, GRID_W, width), lambda bi, i: (bi, jnp.minimum((i + 1) * hb, n_halo - 1), 0)),
                  vec(width), mat(LANES, 2 * dim), mat(LANES, 2 * dim), mat(2 * LANES, dim),
                  vec(2 * dim), vec(2 * dim), vec(dim), vec(dim), vec(dim)],
        out_specs=(one_spec, one_spec, one_spec, two_spec, two_spec, two_spec, one_spec, one_spec),
        scratch_shapes=[pltpu.VMEM((tm, width), F32)],
        compiler_params=_cparams(("parallel", "parallel")),
        name="rwkv_prep_ctx" if is_ctx else "rwkv_prep",
    )(p, p, p, mu, lw_w, la_w, lg_w, w0, a0, k_k, k_a, r_k)


def _scan_pair(r, v, kk, lw, kd, bb, h, sgn):
    c = r.shape[0]
    n = 2 * c
    ri = lax.broadcasted_iota(jnp.int32, (n, n), 0)
    ci = lax.broadcasted_iota(jnp.int32, (n, n), 1)
    same = (ri // c) == (ci // c)
    dt = ((ri % c) - (ci % c)) * sgn
    strict = jnp.logical_and(same, dt > 0)
    incl = jnp.logical_and(same, dt >= 0)
    eye = ri == ci
    head_lanes = (ri // c) == (ci // HEAD_DIM)

    stack = lambda x: jnp.concatenate([x, x], axis=0)
    lw2 = stack(lw)
    cum = _mm_exact_lhs(jnp.where(incl, 1.0, 0.0), lw2)
    tot = _mm_exact_lhs(jnp.where(same, 1.0, 0.0), lw2)
    e_in = jnp.exp(cum)
    e_ex = jnp.exp(cum - lw2)
    e_inv = jnp.exp(-cum)
    e_fin = jnp.exp(tot - cum)
    zero = jnp.zeros_like(lw2)
    at = jnp.where(head_lanes, -stack(kk) * e_ex, zero)
    rt = jnp.where(head_lanes, stack(r) * e_in, zero)
    bt = stack(bb) * e_inv
    kt = stack(kd) * e_inv
    bhat = jnp.where(head_lanes, stack(bb) * e_fin, zero)
    khat = jnp.where(head_lanes, stack(kd) * e_fin, zero)
    vbd = jnp.where(head_lanes, stack(v), zero)

    a_ab = jnp.where(strict, _mm_nt(at, bt), zero)
    a_ak = jnp.where(strict, _mm_nt(at, kt), zero)
    a_rb = jnp.where(incl, _mm_nt(rt, bt), zero)
    a_rk = jnp.where(incl, _mm_nt(rt, kt), zero)

    inv = jnp.where(eye, 1.0, 0.0) + a_ab
    power = a_ab
    for _ in range(int(np.log2(c)) - 1):
        power = _mm(power, power)
        inv = inv + _mm(inv, power)

    w1 = _mm(inv, at)
    w2 = _mm(inv, _mm(a_ak, vbd))
    py = rt + _mm(a_rb, w1)
    yl = _mm(a_rb, w2) + _mm(a_rk, vbd)
    m = jnp.where(eye, jnp.exp(tot), zero) + _mm_tn(bhat, w1)
    nn = _mm_tn(bhat, w2) + _mm_tn(khat, vbd)

    y2 = _mm(py, h) + yl
    return y2[:c] + y2[c:], _mm(m, h) + nn


def _scan_kernel(r_ref, v_ref, kk_ref, lw_ref, kd_ref, bb_ref, h0_ref, y_ref, ht_ref, h_scr, *, pairs):
    d = pl.program_id(0)
    ci = pl.program_id(2)
    sgn = 1 - 2 * d

    @pl.when(ci == 0)
    def _():
        h_scr[...] = h0_ref[0, 0]

    for j in range(pairs):
        sl = slice(j * LANES, (j + 1) * LANES)
        y, h = _scan_pair(r_ref[0, :, sl], v_ref[0, :, sl], kk_ref[0, :, sl], lw_ref[0, 0, :, sl],
                          kd_ref[0, 0, :, sl], bb_ref[0, 0, :, sl], h_scr[j], sgn)
        y_ref[0, 0, :, sl] = y
        h_scr[j] = h

    @pl.when(ci == pl.num_programs(2) - 1)
    def _():
        ht_ref[0, 0] = h_scr[...]


def _rwkv_scan(r, v, kk, lw, kd, bb, h0):
    b, t, dim = r.shape
    c = SCAN_CHUNK
    nch = t // c
    width = SCAN_PAIRS * LANES
    nblk = dim // width

    def chunk(d, ci):
        return ci + d * (nch - 1 - 2 * ci)

    one = pl.BlockSpec((1, c, width), lambda d, g, ci: (g // nblk, chunk(d, ci), g % nblk))
    two = pl.BlockSpec((1, 1, c, width), lambda d, g, ci: (d, g // nblk, chunk(d, ci), g % nblk))
    st = pl.BlockSpec((1, 1, SCAN_PAIRS, LANES, LANES), lambda d, g, ci: (d, g // nblk, g % nblk, 0, 0))
    return pl.pallas_call(
        functools.partial(_scan_kernel, pairs=SCAN_PAIRS),
        out_shape=(jax.ShapeDtypeStruct((2, b, t, dim), F32), jax.ShapeDtypeStruct(h0.shape, F32)),
        grid=(2, b * nblk, nch),
        in_specs=[one, one, one, two, two, two, st],
        out_specs=(two, st),
        scratch_shapes=[pltpu.VMEM((SCAN_PAIRS, LANES, LANES), F32)],
        compiler_params=_cparams(("parallel", "parallel", "arbitrary")),
        name="rwkv_scan",
    )(r, v, kk, lw, kd, bb, h0)


def _rwkv_finish_kernel(y_ref, bonus_ref, gate_ref, g_ref, b_ref, o_ref):
    ones = _head_ones()
    for j in range(o_ref.shape[2] // LANES):
        sl = slice(j * LANES, (j + 1) * LANES)
        y = y_ref[0, 0, :, sl] + y_ref[1, 0, :, sl]
        mu = _headsum(y, ones) * (1.0 / HEAD_DIM)
        dlt = y - mu
        var = _headsum(dlt * dlt, ones) * (1.0 / HEAD_DIM)
        yn = dlt * lax.rsqrt(var + LNX_EPS)
        o_ref[0, :, sl] = ((yn * g_ref[:, sl] + b_ref[:, sl] + bonus_ref[0, :, sl]) * gate_ref[0, :, sl]).astype(o_ref.dtype)


def _rwkv_finish(y, bonus, gate, lnx_g, lnx_b, tm):
    _, b, t, dim = y.shape
    one_spec = pl.BlockSpec((1, tm, dim), lambda bi, i: (bi, i, 0))
    vec = pl.BlockSpec((1, dim), lambda bi, i: (0, 0))
    return pl.pallas_call(
        _rwkv_finish_kernel,
        out_shape=jax.ShapeDtypeStruct((b, t, dim), MXU),
        grid=(b, t // tm),
        in_specs=[pl.BlockSpec((2, 1, tm, dim), lambda bi, i: (0, bi, i, 0)), one_spec, one_spec, vec, vec],
        out_specs=one_spec,
        compiler_params=_cparams(("parallel", "parallel")),
        name="rwkv_finish",
    )(y, bonus, gate, lnx_g.reshape(1, dim), lnx_b.reshape(1, dim))


def _outproj_kernel(*refs, n_in):
    x_ref, gate_ref = refs[0], refs[1]
    o_ref = refs[-1]
    acc = None
    for i in range(n_in):
        part = jnp.dot(refs[2 + 2 * i][0], refs[3 + 2 * i][...], preferred_element_type=F32)
        acc = part if acc is None else acc + part
    o_ref[0] = x_ref[0] + gate_ref[0] * acc


def _outproj(x, gate, parts, tm):
    b, t, d = x.shape
    in_specs = [pl.BlockSpec((1, tm, d), lambda bi, i: (bi, i, 0)),
                pl.BlockSpec((1, 1, d), lambda bi, i: (bi, 0, 0))]
    args = [x, gate]
    for a, w in parts:
        k = w.shape[0]
        in_specs += [pl.BlockSpec((1, tm, k), lambda bi, i: (bi, i, 0)), pl.BlockSpec((k, d), lambda bi, i: (0, 0))]
        args += [a, w]
    return pl.pallas_call(
        functools.partial(_outproj_kernel, n_in=len(parts)),
        out_shape=jax.ShapeDtypeStruct((b, t, d), F32),
        grid=(b, t // tm),
        in_specs=in_specs,
        out_specs=pl.BlockSpec((1, tm, d), lambda bi, i: (bi, i, 0)),
        compiler_params=_cparams(("parallel", "parallel")),
        name="outproj",
    )(*args)


def _mlp_kernel(x_ref, g_ref, sh_ref, sc_ref, gate_ref, w1_ref, w2_ref, fg_ref, o_ref, a_scr, acc_scr, *, final_norm):
    f = pl.program_id(2)

    @pl.when(f == 0)
    def _():
        a_scr[...] = _norm_mod(x_ref[0], g_ref[...], sh_ref[0], sc_ref[0]).astype(a_scr.dtype)
        acc_scr[...] = jnp.zeros_like(acc_scr)

    h = jnp.dot(a_scr[...], w1_ref[...], preferred_element_type=F32)
    h = jnp.square(jnp.maximum(h, 0.0))
    acc_scr[...] += jnp.dot(h.astype(MXU), w2_ref[...], preferred_element_type=F32)

    @pl.when(f == pl.num_programs(2) - 1)
    def _():
        y = x_ref[0] + gate_ref[0] * acc_scr[...]
        if final_norm:
            y = y * lax.rsqrt(jnp.mean(y * y, axis=-1, keepdims=True) + NORM_EPS) * fg_ref[...]
        o_ref[0] = y


def _mlp(x, g, shift, scale, gate, w1, w2, final_g, tm, tf, final_norm):
    b, t, d = x.shape
    ff = w1.shape[1]
    vec3 = pl.BlockSpec((1, 1, d), lambda bi, i, f: (bi, 0, 0))
    vec2 = pl.BlockSpec((1, d), lambda bi, i, f: (0, 0))
    return pl.pallas_call(
        functools.partial(_mlp_kernel, final_norm=final_norm),
        out_shape=jax.ShapeDtypeStruct((b, t, d), F32),
        grid=(b, t // tm, ff // tf),
        in_specs=[pl.BlockSpec((1, tm, d), lambda bi, i, f: (bi, i, 0)), vec2, vec3, vec3, vec3,
                  pl.BlockSpec((d, tf), lambda bi, i, f: (0, f)),
                  pl.BlockSpec((tf, d), lambda bi, i, f: (f, 0)), vec2],
        out_specs=pl.BlockSpec((1, tm, d), lambda bi, i, f: (bi, i, 0)),
        scratch_shapes=[pltpu.VMEM((tm, d), MXU), pltpu.VMEM((tm, d), F32)],
        compiler_params=_cparams(("parallel", "parallel", "arbitrary")),
        name="mlp",
    )(x, g.reshape(1, d), shift, scale, gate, w1, w2, final_g.reshape(1, d))


def _na_kernel(q_ref, k_ref, v_ref, kc_ref, vc_ref, bias_ref, o_ref, *, n_win_rows):
    g = pl.program_id(2)
    n_g = pl.num_programs(2)
    win_row = jnp.clip(g * NA_GROUP - NA_ROWS // 2, 0, n_g * NA_GROUP - n_win_rows)
    typ = jnp.where(g == 0, 0, jnp.where(g == n_g - 1, 2, 1))
    start = pl.multiple_of(win_row * GRID_W, GRID_W)
    k_win = k_ref[0, pl.ds(start, n_win_rows * GRID_W), :]
    v_win = v_ref[0, pl.ds(start, n_win_rows * GRID_W), :]
    bias = bias_ref[0, typ]
    o = _softmax_pv(q_ref[0], [k_win, kc_ref[0]], [v_win, vc_ref[0]], [bias, None])
    o_ref[0] = o.astype(o_ref.dtype)


def _na_bias_table(rpb, n_heads):
    rho = np.arange(NA_GROUP)[None, :] + np.array([0, NA_ROWS // 2, NA_ROWS])[:, None]
    kap = np.arange(NA_WIN)
    lo = np.stack([np.zeros(NA_GROUP, np.int64), rho[1] - NA_ROWS // 2, np.full(NA_GROUP, NA_WIN - NA_ROWS)])
    row_ok = (kap[None, None, :] >= lo[:, :, None]) & (kap[None, None, :] < lo[:, :, None] + NA_ROWS)
    drow = np.clip(kap[None, None, :] - rho[:, :, None] + NA_ROWS - 1, 0, 2 * NA_ROWS - 2)
    col = np.arange(GRID_W)
    c0 = np.clip(col - NA_COLS // 2, 0, GRID_W - NA_COLS)
    col_ok = (col[None, :] >= c0[:, None]) & (col[None, :] < c0[:, None] + NA_COLS)
    dcol = np.clip(col[None, :] - col[:, None] + NA_COLS - 1, 0, 2 * NA_COLS - 2)
    shape = (3, NA_GROUP, GRID_W, NA_WIN, GRID_W)
    drow_i = np.broadcast_to(drow[:, :, None, :, None], shape)
    dcol_i = np.broadcast_to(dcol[None, None, :, None, :], shape)
    ok = np.broadcast_to(row_ok[:, :, None, :, None] & col_ok[None, None, :, None, :], shape)
    tab = jnp.where(ok, rpb[:, drow_i, dcol_i], MASK_VALUE)
    tab = tab.reshape(n_heads // 2, 2, 3, NA_GROUP * GRID_W, NA_WIN * GRID_W)
    return tab.transpose(0, 2, 1, 3, 4)


def _na_attn(qkv, kvc, bias, width):
    b, t, _ = qkv.shape
    ctx = kvc.shape[1]
    nb = width // LANES
    tq = NA_GROUP * GRID_W
    rows = t // GRID_W
    assert rows % NA_GROUP == 0 and rows >= NA_WIN
    return pl.pallas_call(
        functools.partial(_na_kernel, n_win_rows=NA_WIN),
        out_shape=jax.ShapeDtypeStruct((b, t, width), MXU),
        grid=(nb, b, rows // NA_GROUP),
        in_specs=[pl.BlockSpec((1, tq, LANES), lambda p, bi, g: (bi, g, p)),
                  pl.BlockSpec((1, t, LANES), lambda p, bi, g: (bi, 0, nb + p)),
                  pl.BlockSpec((1, t, LANES), lambda p, bi, g: (bi, 0, 2 * nb + p)),
                  pl.BlockSpec((1, ctx, LANES), lambda p, bi, g: (bi, 0, p)),
                  pl.BlockSpec((1, ctx, LANES), lambda p, bi, g: (bi, 0, nb + p)),
                  pl.BlockSpec((1, 3, 2, tq, NA_WIN * GRID_W), lambda p, bi, g: (p, 0, 0, 0, 0))],
        out_specs=pl.BlockSpec((1, tq, LANES), lambda p, bi, g: (bi, g, p)),
        compiler_params=_cparams(("parallel", "parallel", "arbitrary")),
        name="na_attn",
    )(qkv, qkv, qkv, kvc, kvc, bias)


def _rope_tables(t):
    pos = jnp.arange(t, dtype=jnp.int32)
    row = (pos // GRID_W).astype(F32)
    col = (pos % GRID_W).astype(F32)
    pairs = HEAD_DIM // 4
    inv = ROPE_THETA ** (-jnp.arange(pairs, dtype=F32) / pairs)
    ang = jnp.concatenate([row[:, None] * inv, col[:, None] * inv], axis=-1)
    cos, sin = jnp.cos(ang), jnp.sin(ang)
    cos_h = jnp.concatenate([cos, cos], axis=-1)
    sin_h = jnp.concatenate([-sin, sin], axis=-1)
    return jnp.tile(cos_h, (1, HEADS_PER_BLOCK)), jnp.tile(sin_h, (1, HEADS_PER_BLOCK))


def _block_diag2(top, bottom):
    z_t = jnp.zeros_like(top)
    z_b = jnp.zeros_like(bottom)
    return jnp.concatenate([jnp.concatenate([top, z_t], axis=1), jnp.concatenate([z_b, bottom], axis=1)], axis=0)


def kernel(x, c, ctx, c_ctx, l0_norm1, l0_norm2, l0_ada_w, l0_ada_b, l0_w_in, l0_shift_mu, l0_w0_f, l0_w0_b, l0_ww2_f, l0_ww2_b, l0_a0_f, l0_a0_b, l0_wa2_f, l0_wa2_b, l0_wg2, l0_k_k, l0_k_a, l0_r_k, l0_lnx_g, l0_lnx_b, l0_q_norm, l0_k_norm, l0_w_out, l0_mlp_w1, l0_mlp_w2, l1_norm1, l1_norm2, l1_ada_w, l1_ada_b, l1_w_qkv, l1_rpb, l1_w_out, l1_mlp_w1, l1_mlp_w2, final_norm):
    b, t, d = x.shape
    n_ctx = ctx.shape[1]
    dim = l0_w0_f.shape[0]
    q_width = l0_w_out.shape[0] - dim
    gqa_cols = l0_w_in.shape[1] - l0_shift_mu.shape[0]
    kv_width = (gqa_cols - q_width) // 2
    rw_cols = l0_shift_mu.shape[0]
    rw_pad = 3 * dim + 4 * LANES
    assert rw_cols <= rw_pad and 3 * dim + 2 * LANES == rw_cols - l0_wg2.shape[0]
    tm = min(512, t)
    tm_c = n_ctx

    cc = jnp.zeros((8, d), F32).at[:b].set(c).at[b].set(c_ctx)

    def modulation(ada_w, ada_b):
        mod = _ada_mod(cc, ada_w, ada_b)
        lat = mod[:b].reshape(b, 6, 1, d)
        cx = jnp.broadcast_to(mod[b].reshape(1, 6, 1, d), (b, 6, 1, d))
        return [lat[:, i] for i in range(6)], [cx[:, i] for i in range(6)]

    mod_l, mod_c = modulation(l0_ada_w, l0_ada_b)

    w_gqa = l0_w_in[:, :gqa_cols].astype(MXU)
    w_rw = jnp.pad(l0_w_in[:, gqa_cols:], ((0, 0), (0, rw_pad - rw_cols))).astype(MXU)
    mu = jnp.pad(l0_shift_mu, (0, rw_pad - rw_cols)).reshape(1, rw_pad)
    lw_w = _block_diag2(l0_ww2_f, l0_ww2_b).astype(MXU)
    la_w = _block_diag2(l0_wa2_f, l0_wa2_b).astype(MXU)
    lg_w = jnp.pad(l0_wg2, ((0, 2 * LANES - l0_wg2.shape[0]), (0, 0))).astype(MXU)
    w0 = jnp.concatenate([l0_w0_f, l0_w0_b]).reshape(1, 2 * dim)
    a0 = jnp.concatenate([l0_a0_f, l0_a0_b]).reshape(1, 2 * dim)
    k_k = l0_k_k.reshape(1, dim)
    k_a = l0_k_a.reshape(1, dim)
    r_k = l0_r_k.reshape(1, dim)
    qg = jnp.tile(l0_q_norm, HEADS_PER_BLOCK).reshape(1, LANES)
    kg = jnp.tile(l0_k_norm, HEADS_PER_BLOCK).reshape(1, LANES)
    cos_l, sin_l = _rope_tables(t)
    cos_c, sin_c = jnp.ones((n_ctx, LANES), F32), jnp.zeros((n_ctx, LANES), F32)
    w_out_gqa = l0_w_out[:q_width].astype(MXU)
    w_out_rw = l0_w_out[q_width:].astype(MXU)

    def half_layer0(xs, mod, tm_, cos, sin, is_ctx):
        pg = _inproj(xs, l0_norm1, mod[0], mod[1], w_gqa, tm_, 512, F32)
        pr = _inproj(xs, l0_norm1, mod[0], mod[1], w_rw, tm_, 512, F32)
        q, kd, vd = _gqa_prep(pg, cos, sin, qg, kg, q_width, kv_width, tm_)
        prep = _rwkv_prep(pr, mu, lw_w, la_w, lg_w, w0, a0, k_k, k_a, r_k, dim, 256, is_ctx)
        return q, kd, vd, prep

    q_c, kd_c, vd_c, prep_c = half_layer0(ctx, mod_c, tm_c, cos_c, sin_c, True)
    q_l, kd_l, vd_l, prep_l = half_layer0(x, mod_l, tm, cos_l, sin_l, False)

    o_gqa_l = _gqa_attn(q_l, [(kd_l, vd_l), (kd_c, vd_c)], 256)
    o_gqa_c = _gqa_attn(q_c, [(kd_c, vd_c)], n_ctx)

    h0 = jnp.zeros((2, b, dim // LANES, LANES, LANES), F32)
    y_c, h_c = _rwkv_scan(*prep_c[:6], h0)
    y_l, _ = _rwkv_scan(*prep_l[:6], h_c)
    o_rw_l = _rwkv_finish(y_l, prep_l[6], prep_l[7], l0_lnx_g, l0_lnx_b, tm)
    o_rw_c = _rwkv_finish(y_c, prep_c[6], prep_c[7], l0_lnx_g, l0_lnx_b, tm_c)

    w1 = l0_mlp_w1.astype(MXU)
    w2 = l0_mlp_w2.astype(MXU)
    x = _outproj(x, mod_l[2], [(o_gqa_l, w_out_gqa), (o_rw_l, w_out_rw)], tm)
    x = _mlp(x, l0_norm2, mod_l[3], mod_l[4], mod_l[5], w1, w2, final_norm, tm, 512, False)
    ctx = _outproj(ctx, mod_c[2], [(o_gqa_c, w_out_gqa), (o_rw_c, w_out_rw)], tm_c)
    ctx = _mlp(ctx, l0_norm2, mod_c[3], mod_c[4], mod_c[5], w1, w2, final_norm, tm_c, 512, False)

    mod_l, mod_c = modulation(l1_ada_w, l1_ada_b)
    width = l1_w_out.shape[0]
    n_heads = width // HEAD_DIM
    scale = jnp.concatenate([jnp.full((width,), HEAD_DIM ** -0.5, F32), jnp.ones((2 * width,), F32)])
    w_qkv = (l1_w_qkv * scale).astype(MXU)
    qkv = _inproj(x, l1_norm1, mod_l[0], mod_l[1], w_qkv, tm, 512, MXU)
    kvc = _inproj(ctx, l1_norm1, mod_c[0], mod_c[1], w_qkv[:, width:], tm_c, 512, MXU)
    bias = _na_bias_table(l1_rpb, n_heads)
    o_na = _na_attn(qkv, kvc, bias, width)
    x = _outproj(x, mod_l[2], [(o_na, l1_w_out.astype(MXU))], tm)
    x = _mlp(x, l1_norm2, mod_l[3], mod_l[4], mod_l[5], l1_mlp_w1.astype(MXU), l1_mlp_w2.astype(MXU),
             final_norm, tm, 512, True)
    return x
```

```python
import functools

import jax
import jax.numpy as jnp
import numpy as np
from jax import lax
from jax.experimental import pallas as pl
from jax.experimental.pallas import tpu as pltpu

F32 = jnp.float32
MXU = jnp.bfloat16

LANES = 128
HEAD_DIM = 64
HEADS_PER_BLOCK = LANES // HEAD_DIM
GRID_W = 64
NORM_EPS = 1e-6
LNX_EPS = 64e-5
ROPE_THETA = 10000.0
NA_ROWS = 8
NA_COLS = 16
NA_GROUP = 4
NA_WIN = NA_GROUP + NA_ROWS
MASK_VALUE = -1e30
SCAN_CHUNK = 64
SCAN_PAIRS = 8
VMEM_LIMIT = 56 * 1024 * 1024


def _cparams(sem):
    return pltpu.CompilerParams(dimension_semantics=sem, vmem_limit_bytes=VMEM_LIMIT)


def _mm(a, b):
    return jnp.dot(a.astype(MXU), b.astype(MXU), preferred_element_type=F32)


def _mm_nt(a, b):
    return lax.dot_general(a.astype(MXU), b.astype(MXU), (((1,), (1,)), ((), ())),
                           preferred_element_type=F32)


def _mm_tn(a, b):
    return lax.dot_general(a.astype(MXU), b.astype(MXU), (((0,), (0,)), ((), ())),
                           preferred_element_type=F32)


def _split3(x):
    hi = x.astype(MXU)
    r1 = x - hi.astype(F32)
    mid = r1.astype(MXU)
    lo = (r1 - mid.astype(F32)).astype(MXU)
    return hi, mid, lo


def _mm_exact_lhs(m, x):
    hi, mid, lo = _split3(x)
    mb = m.astype(MXU)
    return (jnp.dot(mb, hi, preferred_element_type=F32) + jnp.dot(mb, mid, preferred_element_type=F32)
            + jnp.dot(mb, lo, preferred_element_type=F32))


def _head_ones():
    r = lax.broadcasted_iota(jnp.int32, (LANES, LANES), 0) // HEAD_DIM
    c = lax.broadcasted_iota(jnp.int32, (LANES, LANES), 1) // HEAD_DIM
    return jnp.where(r == c, 1.0, 0.0).astype(MXU)


def _headsum(x, ones):
    hi, mid, lo = _split3(x)
    return (jnp.dot(hi, ones, preferred_element_type=F32) + jnp.dot(mid, ones, preferred_element_type=F32)
            + jnp.dot(lo, ones, preferred_element_type=F32))


def _lane(shape):
    return lax.broadcasted_iota(jnp.int32, shape, len(shape) - 1)


def _sigmoid(x):
    return 1.0 / (1.0 + jnp.exp(-x))


def _ada_kernel(c_ref, w_ref, b_ref, o_ref):
    c = c_ref[...]
    a = c * _sigmoid(c)
    o_ref[...] = _mm(a, w_ref[...]) + b_ref[...]


def _ada_mod(cc, w, bias):
    d, n = w.shape
    tn = 1024
    return pl.pallas_call(
        _ada_kernel,
        out_shape=jax.ShapeDtypeStruct((cc.shape[0], n), F32),
        grid=(n // tn,),
        in_specs=[pl.BlockSpec((cc.shape[0], d), lambda j: (0, 0)),
                  pl.BlockSpec((d, tn), lambda j: (0, j)),
                  pl.BlockSpec((1, tn), lambda j: (0, j))],
        out_specs=pl.BlockSpec((cc.shape[0], tn), lambda j: (0, j)),
        compiler_params=_cparams(("arbitrary",)),
        name="ada_mod",
    )(cc, w, bias.reshape(1, n))


def _norm_mod(x, g, shift, scale):
    y = x * lax.rsqrt(jnp.mean(x * x, axis=-1, keepdims=True) + NORM_EPS)
    return (y * g) * (1.0 + scale) + shift


def _inproj_kernel(x_ref, g_ref, sh_ref, sc_ref, w_ref, o_ref, a_scr):
    @pl.when(pl.program_id(2) == 0)
    def _():
        a_scr[...] = _norm_mod(x_ref[0], g_ref[...], sh_ref[0], sc_ref[0]).astype(a_scr.dtype)

    o_ref[0] = jnp.dot(a_scr[...], w_ref[...], preferred_element_type=F32).astype(o_ref.dtype)


def _inproj(x, g, shift, scale, w, tm, tn, out_dtype):
    b, t, d = x.shape
    n = w.shape[1]
    return pl.pallas_call(
        _inproj_kernel,
        out_shape=jax.ShapeDtypeStruct((b, t, n), out_dtype),
        grid=(b, t // tm, n // tn),
        in_specs=[pl.BlockSpec((1, tm, d), lambda bi, i, j: (bi, i, 0)),
                  pl.BlockSpec((1, d), lambda bi, i, j: (0, 0)),
                  pl.BlockSpec((1, 1, d), lambda bi, i, j: (bi, 0, 0)),
                  pl.BlockSpec((1, 1, d), lambda bi, i, j: (bi, 0, 0)),
                  pl.BlockSpec((d, tn), lambda bi, i, j: (0, j))],
        out_specs=pl.BlockSpec((1, tm, tn), lambda bi, i, j: (bi, i, j)),
        scratch_shapes=[pltpu.VMEM((tm, d), MXU)],
        compiler_params=_cparams(("parallel", "parallel", "arbitrary")),
        name="inproj",
    )(x, g.reshape(1, d), shift, scale, w)


def _swap_half_heads(x):
    first = (_lane(x.shape) % HEAD_DIM) < HEAD_DIM // 2
    return jnp.where(first, pltpu.roll(x, LANES - HEAD_DIM // 2, 1), pltpu.roll(x, HEAD_DIM // 2, 1))


def _gqa_prep_kernel(q_ref, kv_ref, cos_ref, sin_ref, qg_ref, kg_ref, qo_ref, ko_ref, vo_ref, *, n_q, n_kv):
    ones = _head_ones()
    cos = cos_ref[...]
    sin = sin_ref[...]
    first_head = _lane(cos.shape) < HEAD_DIM

    def norm_rope(x, g):
        ms = _headsum(x * x, ones) * (1.0 / HEAD_DIM)
        y = x * lax.rsqrt(ms + NORM_EPS) * g
        return y * cos + _swap_half_heads(y) * sin

    def dup(x, half):
        rolled = pltpu.roll(x, HEAD_DIM, 1)
        return jnp.where(first_head, x, rolled) if half == 0 else jnp.where(first_head, rolled, x)

    for j in range(n_q):
        x = q_ref[0, :, j * LANES:(j + 1) * LANES]
        qo_ref[0, :, j * LANES:(j + 1) * LANES] = (norm_rope(x, qg_ref[...]) * HEAD_DIM ** -0.5).astype(qo_ref.dtype)
    for j in range(n_kv):
        k = norm_rope(kv_ref[0, :, j * LANES:(j + 1) * LANES], kg_ref[...])
        v = kv_ref[0, :, (n_kv + j) * LANES:(n_kv + j + 1) * LANES]
        for half in range(HEADS_PER_BLOCK):
            ko_ref[0, HEADS_PER_BLOCK * j + half] = dup(k, half).astype(ko_ref.dtype)
            vo_ref[0, HEADS_PER_BLOCK * j + half] = dup(v, half).astype(vo_ref.dtype)


def _gqa_prep(p, cos, sin, qg, kg, q_width, kv_width, tm):
    b, t, _ = p.shape
    n_q = q_width // LANES
    n_kv = kv_width // LANES
    kvh = kv_width // HEAD_DIM
    kern = functools.partial(_gqa_prep_kernel, n_q=n_q, n_kv=n_kv)
    kv_spec = pl.BlockSpec((1, kvh, tm, LANES), lambda bi, i: (bi, 0, i, 0))
    return pl.pallas_call(
        kern,
        out_shape=(jax.ShapeDtypeStruct((b, t, q_width), MXU),
                   jax.ShapeDtypeStruct((b, kvh, t, LANES), MXU),
                   jax.ShapeDtypeStruct((b, kvh, t, LANES), MXU)),
        grid=(b, t // tm),
        in_specs=[pl.BlockSpec((1, tm, q_width), lambda bi, i: (bi, i, 0)),
                  pl.BlockSpec((1, tm, 2 * kv_width), lambda bi, i: (bi, i, q_width // (2 * kv_width))),
                  pl.BlockSpec((tm, LANES), lambda bi, i: (i, 0)),
                  pl.BlockSpec((tm, LANES), lambda bi, i: (i, 0)),
                  pl.BlockSpec((1, LANES), lambda bi, i: (0, 0)),
                  pl.BlockSpec((1, LANES), lambda bi, i: (0, 0))],
        out_specs=(pl.BlockSpec((1, tm, q_width), lambda bi, i: (bi, i, 0)), kv_spec, kv_spec),
        compiler_params=_cparams(("parallel", "parallel")),
        name="gqa_prep",
    )(p, p, cos, sin, qg, kg)


def _softmax_pv(q, keys, values, biases):
    first_head = _lane(q.shape) < HEAD_DIM
    outs = []
    for half in range(HEADS_PER_BLOCK):
        zero = jnp.zeros_like(q)
        qm = jnp.where(first_head, q, zero) if half == 0 else jnp.where(first_head, zero, q)
        scores = []
        for k, bias in zip(keys, biases):
            s = _mm_nt(qm, k)
            scores.append(s if bias is None else s + bias[half])
        m = functools.reduce(jnp.maximum, [jnp.max(s, axis=-1, keepdims=True) for s in scores])
        ps = [jnp.exp(s - m) for s in scores]
        l = functools.reduce(jnp.add, [jnp.sum(p, axis=-1, keepdims=True) for p in ps])
        o = functools.reduce(jnp.add, [_mm(p, v) for p, v in zip(ps, values)])
        outs.append(o / l)
    return jnp.where(first_head, outs[0], outs[1])


def _gqa_attn_kernel(q_ref, *refs, n_sets):
    o_ref = refs[-1]
    keys = [refs[2 * i][0, 0] for i in range(n_sets)]
    values = [refs[2 * i + 1][0, 0] for i in range(n_sets)]
    o_ref[0] = _softmax_pv(q_ref[0], keys, values, [None] * n_sets).astype(o_ref.dtype)


def _gqa_attn(q, kv_sets, tq):
    b, t, width = q.shape
    n_blocks = width // LANES
    kvh = kv_sets[0][0].shape[1]
    blocks_per_kv = n_blocks // kvh
    in_specs = [pl.BlockSpec((1, tq, LANES), lambda bi, p, i: (bi, i, p))]
    args = [q]
    for k, v in kv_sets:
        s = k.shape[2]
        spec = pl.BlockSpec((1, 1, s, LANES), lambda bi, p, i: (bi, p // blocks_per_kv, 0, 0))
        in_specs += [spec, spec]
        args += [k, v]
    return pl.pallas_call(
        functools.partial(_gqa_attn_kernel, n_sets=len(kv_sets)),
        out_shape=jax.ShapeDtypeStruct((b, t, width), MXU),
        grid=(b, n_blocks, t // tq),
        in_specs=in_specs,
        out_specs=pl.BlockSpec((1, tq, LANES), lambda bi, p, i: (bi, i, p)),
        compiler_params=_cparams(("parallel", "parallel", "arbitrary")),
        name="gqa_attn",
    )(*args)


def _rwkv_prep_kernel(p_ref, up_ref, dn_ref, mu_ref, lw_w_ref, la_w_ref, lg_w_ref, w0_ref, a0_ref, kk_ref, ka_ref,
                      rk_ref, r_o, v_o, kk_o, lw_o, kd_o, bb_o, bonus_o, gate_o, xm_scr, *, is_ctx, dim):
    tm = p_ref.shape[1]
    n_all = p_ref.shape[2] // LANES
    n_dim = dim // LANES
    i = pl.program_id(1)
    n_i = pl.num_programs(1)
    row = lax.broadcasted_iota(jnp.int32, (tm, LANES), 0)
    cls = _lane((tm, LANES)) % 4
    if is_ctx:
        first = row == 0
        last = row == tm - 1
    else:
        first = (row % GRID_W) == 0
        last = (row % GRID_W) == GRID_W - 1
        up_ok = jnp.logical_or(row >= GRID_W, i > 0)
        dn_ok = jnp.logical_or(row < tm - GRID_W, i < n_i - 1)

    for j in range(n_all):
        sl = slice(j * LANES, (j + 1) * LANES)
        p = p_ref[0, :, sl]
        prev = jnp.where(first, 0.0, pltpu.roll(p, 1, 0))
        nxt = jnp.where(last, 0.0, pltpu.roll(p, tm - 1, 0))
        if is_ctx:
            sh = jnp.where(cls % 2 == 0, prev, nxt)
        else:
            up = jnp.concatenate([up_ref[0, :, sl], p[:tm - GRID_W]], axis=0)
            dn = jnp.concatenate([p[GRID_W:], dn_ref[0, :, sl]], axis=0)
            up = jnp.where(up_ok, up, 0.0)
            dn = jnp.where(dn_ok, dn, 0.0)
            sh = jnp.where(cls == 0, prev, jnp.where(cls == 1, nxt, jnp.where(cls == 2, up, dn)))
        xm_scr[:, sl] = p + mu_ref[:, sl] * (sh - p)

    ones = _head_ones()
    x_w = jnp.tanh(xm_scr[:, 3 * dim:3 * dim + LANES]).astype(MXU)
    x_a = xm_scr[:, 3 * dim + LANES:3 * dim + 2 * LANES].astype(MXU)
    x_g = _sigmoid(xm_scr[:, 3 * dim + 2 * LANES:3 * dim + 4 * LANES]).astype(MXU)
    for j in range(n_dim):
        sl = slice(j * LANES, (j + 1) * LANES)
        r = xm_scr[:, sl]
        k = xm_scr[:, dim + j * LANES:dim + (j + 1) * LANES]
        v = xm_scr[:, 2 * dim + j * LANES:2 * dim + (j + 1) * LANES]
        kkr = k * kk_ref[:, sl]
        kk = kkr * lax.rsqrt(jnp.maximum(_headsum(kkr * kkr, ones), 1e-12))
        r_o[0, :, sl] = r
        v_o[0, :, sl] = v
        kk_o[0, :, sl] = kk
        gate_o[0, :, sl] = jnp.dot(x_g, lg_w_ref[:, sl], preferred_element_type=F32)
        bonus = jnp.zeros_like(r)
        for d in range(2):
            dsl = slice(d * dim + j * LANES, d * dim + (j + 1) * LANES)
            z = w0_ref[:, dsl] + jnp.dot(x_w, lw_w_ref[:, dsl], preferred_element_type=F32)
            softplus = jnp.maximum(-z, 0.0) + jnp.log(1.0 + jnp.exp(-jnp.abs(z)))
            lw_o[d, 0, :, sl] = -jnp.exp(-softplus - 0.5)
            iclr = _sigmoid(a0_ref[:, dsl] + jnp.dot(x_a, la_w_ref[:, dsl], preferred_element_type=F32))
            kd = k * (1.0 + (iclr - 1.0) * ka_ref[:, sl])
            kd_o[d, 0, :, sl] = kd
            bb_o[d, 0, :, sl] = kk * iclr
            bonus = bonus + _headsum(r * kd * rk_ref[:, sl], ones) * v
        bonus_o[0, :, sl] = bonus


def _rwkv_prep(p, mu, lw_w, la_w, lg_w, w0, a0, k_k, k_a, r_k, dim, tm, is_ctx):
    b, t, width = p.shape
    if is_ctx:
        assert t == tm
    else:
        assert tm % GRID_W == 0 and t % tm == 0
    hb = tm // GRID_W
    n_halo = t // GRID_W
    kern = functools.partial(_rwkv_prep_kernel, is_ctx=is_ctx, dim=dim)
    vec = lambda n: pl.BlockSpec((1, n), lambda bi, i: (0, 0))
    mat = lambda k, n: pl.BlockSpec((k, n), lambda bi, i: (0, 0))
    one = jax.ShapeDtypeStruct((b, t, dim), F32)
    two = jax.ShapeDtypeStruct((2, b, t, dim), F32)
    one_spec = pl.BlockSpec((1, tm, dim), lambda bi, i: (bi, i, 0))
    two_spec = pl.BlockSpec((2, 1, tm, dim), lambda bi, i: (0, bi, i, 0))
    return pl.pallas_call(
        kern,
        out_shape=(one, one, one, two, two, two, one, one),
        grid=(b, t // tm),
        in_specs=[pl.BlockSpec((1, tm, width), lambda bi, i: (bi, i, 0)),
                  pl.BlockSpec((1, GRID_W, width), lambda bi, i: (bi, jnp.maximum(i * hb - 1, 0), 0)),
                  pl.BlockSpec((1, GRID_W, width), lambda bi, i: (bi, jnp.minimum((i + 1) * hb, n_halo - 1), 0)),
                  vec(width), mat(LANES, 2 * dim), mat(LANES, 2 * dim), mat(2 * LANES, dim),
                  vec(2 * dim), vec(2 * dim), vec(dim), vec(dim), vec(dim)],
        out_specs=(one_spec, one_spec, one_spec, two_spec, two_spec, two_spec, one_spec, one_spec),
        scratch_shapes=[pltpu.VMEM((tm, width), F32)],
        compiler_params=_cparams(("parallel", "parallel")),
        name="rwkv_prep_ctx" if is_ctx else "rwkv_prep",
    )(p, p, p, mu, lw_w, la_w, lg_w, w0, a0, k_k, k_a, r_k)


def _scan_chunk(r, v, kk, lw, kd, bb, h, sgn):
    blocks = range(len(r))
    c = r[0].shape[0]
    n = 2 * c
    ri = lax.broadcasted_iota(jnp.int32, (n, n), 0)
    ci = lax.broadcasted_iota(jnp.int32, (n, n), 1)
    same = (ri // c) == (ci // c)
    dt = ((ri % c) - (ci % c)) * sgn
    strict = jnp.logical_and(same, dt > 0)
    incl = jnp.logical_and(same, dt >= 0)
    eye = ri == ci
    head_lanes = (ri // c) == (ci // HEAD_DIM)
    zero = jnp.zeros((n, n), F32)
    ti = lax.broadcasted_iota(jnp.int32, (c, c), 0)
    tj = lax.broadcasted_iota(jnp.int32, (c, c), 1)
    tri = jnp.where((ti - tj) * sgn >= 0, 1.0, 0.0)

    def stack(x):
        return jnp.concatenate([x, x], axis=0)

    def own(x):
        return jnp.where(head_lanes, stack(x), zero)

    cum = [_mm_exact_lhs(tri, lw[p]) for p in blocks]
    tot = [jnp.sum(lw[p], axis=0, keepdims=True) for p in blocks]
    e_inv = [jnp.exp(-cum[p]) for p in blocks]
    e_fin = [jnp.exp(tot[p] - cum[p]) for p in blocks]
    at = [own(-kk[p] * jnp.exp(cum[p] - lw[p])) for p in blocks]
    rt = [own(r[p] * jnp.exp(cum[p])) for p in blocks]
    bt = [stack(bb[p] * e_inv[p]) for p in blocks]
    kt = [stack(kd[p] * e_inv[p]) for p in blocks]
    bhat = [own(bb[p] * e_fin[p]) for p in blocks]
    khat = [own(kd[p] * e_fin[p]) for p in blocks]
    vbd = [own(v[p]) for p in blocks]

    a_ab = [jnp.where(strict, _mm_nt(at[p], bt[p]), zero) for p in blocks]
    a_ak = [jnp.where(strict, _mm_nt(at[p], kt[p]), zero) for p in blocks]
    a_rb = [jnp.where(incl, _mm_nt(rt[p], bt[p]), zero) for p in blocks]
    a_rk = [jnp.where(incl, _mm_nt(rt[p], kt[p]), zero) for p in blocks]

    ident = jnp.where(eye, 1.0, 0.0)
    inv = [ident + a_ab[p] for p in blocks]
    power = a_ab
    for _ in range(int(np.log2(c)) - 1):
        power = [_mm(power[p], power[p]) for p in blocks]
        inv = [inv[p] + _mm(inv[p], power[p]) for p in blocks]

    av = [_mm(a_ak[p], vbd[p]) for p in blocks]
    w1 = [_mm(inv[p], at[p]) for p in blocks]
    w2 = [_mm(inv[p], av[p]) for p in blocks]
    py = [rt[p] + _mm(a_rb[p], w1[p]) for p in blocks]
    yl = [_mm(a_rb[p], w2[p]) + _mm(a_rk[p], vbd[p]) for p in blocks]
    m = [jnp.where(eye, jnp.exp(tot[p]), zero) + _mm_tn(bhat[p], w1[p]) for p in blocks]
    nn = [_mm_tn(bhat[p], w2[p]) + _mm_tn(khat[p], vbd[p]) for p in blocks]

    y2 = [_mm(py[p], h[p]) + yl[p] for p in blocks]
    h_new = [_mm(m[p], h[p]) + nn[p] for p in blocks]
    return [y[:c] + y[c:] for y in y2], h_new


def _scan_kernel(r_ref, v_ref, kk_ref, lw_ref, kd_ref, bb_ref, h0_ref, y_ref, ht_ref, h_scr, *, pairs):
    d = pl.program_id(0)
    ci = pl.program_id(2)
    sgn = 1 - 2 * d

    @pl.when(ci == 0)
    def _():
        h_scr[...] = h0_ref[0, 0]

    sls = [slice(j * LANES, (j + 1) * LANES) for j in range(pairs)]
    ys, hs = _scan_chunk([r_ref[0, :, sl] for sl in sls], [v_ref[0, :, sl] for sl in sls],
                         [kk_ref[0, :, sl] for sl in sls], [lw_ref[0, 0, :, sl] for sl in sls],
                         [kd_ref[0, 0, :, sl] for sl in sls], [bb_ref[0, 0, :, sl] for sl in sls],
                         [h_scr[j] for j in range(pairs)], sgn)
    for j, sl in enumerate(sls):
        y_ref[0, 0, :, sl] = ys[j]
        h_scr[j] = hs[j]

    @pl.when(ci == pl.num_programs(2) - 1)
    def _():
        ht_ref[0, 0] = h_scr[...]


def _rwkv_scan(r, v, kk, lw, kd, bb, h0):
    b, t, dim = r.shape
    c = SCAN_CHUNK
    nch = t // c
    width = SCAN_PAIRS * LANES
    nblk = dim // width

    def chunk(d, ci):
        return ci + d * (nch - 1 - 2 * ci)

    one = pl.BlockSpec((1, c, width), lambda d, g, ci: (g // nblk, chunk(d, ci), g % nblk))
    two = pl.BlockSpec((1, 1, c, width), lambda d, g, ci: (d, g // nblk, chunk(d, ci), g % nblk))
    st = pl.BlockSpec((1, 1, SCAN_PAIRS, LANES, LANES), lambda d, g, ci: (d, g // nblk, g % nblk, 0, 0))
    return pl.pallas_call(
        functools.partial(_scan_kernel, pairs=SCAN_PAIRS),
        out_shape=(jax.ShapeDtypeStruct((2, b, t, dim), F32), jax.ShapeDtypeStruct(h0.shape, F32)),
        grid=(2, b * nblk, nch),
        in_specs=[one, one, one, two, two, two, st],
        out_specs=(two, st),
        scratch_shapes=[pltpu.VMEM((SCAN_PAIRS, LANES, LANES), F32)],
        compiler_params=_cparams(("parallel", "parallel", "arbitrary")),
        name="rwkv_scan",
    )(r, v, kk, lw, kd, bb, h0)


def _rwkv_finish_kernel(y_ref, bonus_ref, gate_ref, g_ref, b_ref, o_ref):
    ones = _head_ones()
    for j in range(o_ref.shape[2] // LANES):
        sl = slice(j * LANES, (j + 1) * LANES)
        y = y_ref[0, 0, :, sl] + y_ref[1, 0, :, sl]
        mu = _headsum(y, ones) * (1.0 / HEAD_DIM)
        dlt = y - mu
        var = _headsum(dlt * dlt, ones) * (1.0 / HEAD_DIM)
        yn = dlt * lax.rsqrt(var + LNX_EPS)
        o_ref[0, :, sl] = ((yn * g_ref[:, sl] + b_ref[:, sl] + bonus_ref[0, :, sl]) * gate_ref[0, :, sl]).astype(o_ref.dtype)


def _rwkv_finish(y, bonus, gate, lnx_g, lnx_b, tm):
    _, b, t, dim = y.shape
    one_spec = pl.BlockSpec((1, tm, dim), lambda bi, i: (bi, i, 0))
    vec = pl.BlockSpec((1, dim), lambda bi, i: (0, 0))
    return pl.pallas_call(
        _rwkv_finish_kernel,
        out_shape=jax.ShapeDtypeStruct((b, t, dim), MXU),
        grid=(b, t // tm),
        in_specs=[pl.BlockSpec((2, 1, tm, dim), lambda bi, i: (0, bi, i, 0)), one_spec, one_spec, vec, vec],
        out_specs=one_spec,
        compiler_params=_cparams(("parallel", "parallel")),
        name="rwkv_finish",
    )(y, bonus, gate, lnx_g.reshape(1, dim), lnx_b.reshape(1, dim))


def _outproj_kernel(*refs, n_in):
    x_ref, gate_ref = refs[0], refs[1]
    o_ref = refs[-1]
    acc = None
    for i in range(n_in):
        part = jnp.dot(refs[2 + 2 * i][0], refs[3 + 2 * i][...], preferred_element_type=F32)
        acc = part if acc is None else acc + part
    o_ref[0] = x_ref[0] + gate_ref[0] * acc


def _outproj(x, gate, parts, tm):
    b, t, d = x.shape
    in_specs = [pl.BlockSpec((1, tm, d), lambda bi, i: (bi, i, 0)),
                pl.BlockSpec((1, 1, d), lambda bi, i: (bi, 0, 0))]
    args = [x, gate]
    for a, w in parts:
        k = w.shape[0]
        in_specs += [pl.BlockSpec((1, tm, k), lambda bi, i: (bi, i, 0)), pl.BlockSpec((k, d), lambda bi, i: (0, 0))]
        args += [a, w]
    return pl.pallas_call(
        functools.partial(_outproj_kernel, n_in=len(parts)),
        out_shape=jax.ShapeDtypeStruct((b, t, d), F32),
        grid=(b, t // tm),
        in_specs=in_specs,
        out_specs=pl.BlockSpec((1, tm, d), lambda bi, i: (bi, i, 0)),
        compiler_params=_cparams(("parallel", "parallel")),
        name="outproj",
    )(*args)


def _mlp_kernel(x_ref, g_ref, sh_ref, sc_ref, gate_ref, w1_ref, w2_ref, fg_ref, o_ref, a_scr, acc_scr, *, final_norm):
    f = pl.program_id(2)

    @pl.when(f == 0)
    def _():
        a_scr[...] = _norm_mod(x_ref[0], g_ref[...], sh_ref[0], sc_ref[0]).astype(a_scr.dtype)
        acc_scr[...] = jnp.zeros_like(acc_scr)

    h = jnp.dot(a_scr[...], w1_ref[...], preferred_element_type=F32)
    h = jnp.square(jnp.maximum(h, 0.0))
    acc_scr[...] += jnp.dot(h.astype(MXU), w2_ref[...], preferred_element_type=F32)

    @pl.when(f == pl.num_programs(2) - 1)
    def _():
        y = x_ref[0] + gate_ref[0] * acc_scr[...]
        if final_norm:
            y = y * lax.rsqrt(jnp.mean(y * y, axis=-1, keepdims=True) + NORM_EPS) * fg_ref[...]
        o_ref[0] = y


def _mlp(x, g, shift, scale, gate, w1, w2, final_g, tm, tf, final_norm):
    b, t, d = x.shape
    ff = w1.shape[1]
    vec3 = pl.BlockSpec((1, 1, d), lambda bi, i, f: (bi, 0, 0))
    vec2 = pl.BlockSpec((1, d), lambda bi, i, f: (0, 0))
    return pl.pallas_call(
        functools.partial(_mlp_kernel, final_norm=final_norm),
        out_shape=jax.ShapeDtypeStruct((b, t, d), F32),
        grid=(b, t // tm, ff // tf),
        in_specs=[pl.BlockSpec((1, tm, d), lambda bi, i, f: (bi, i, 0)), vec2, vec3, vec3, vec3,
                  pl.BlockSpec((d, tf), lambda bi, i, f: (0, f)),
                  pl.BlockSpec((tf, d), lambda bi, i, f: (f, 0)), vec2],
        out_specs=pl.BlockSpec((1, tm, d), lambda bi, i, f: (bi, i, 0)),
        scratch_shapes=[pltpu.VMEM((tm, d), MXU), pltpu.VMEM((tm, d), F32)],
        compiler_params=_cparams(("parallel", "parallel", "arbitrary")),
        name="mlp",
    )(x, g.reshape(1, d), shift, scale, gate, w1, w2, final_g.reshape(1, d))


NA_GROUP_OFFSETS = (0, NA_ROWS // 2, NA_ROWS)
NA_STRIP_PAD = NA_WIN - NA_ROWS
NA_STRIP_BLOCKS = 2 * NA_WIN


def _na_window_lo(typ, local_row):
    return (0, local_row, NA_WIN - NA_ROWS)[typ]


def _na_kernel(q_ref, k_ref, v_ref, kc_ref, vc_ref, tc_ref, o_ref, bias_scr, *, n_win_rows):
    g = pl.program_id(2)
    n_g = pl.num_programs(2)
    n_keys = n_win_rows * GRID_W

    @pl.when(jnp.logical_and(pl.program_id(1) == 0, g == 0))
    def _():
        key_row = _lane((GRID_W, n_keys)) // GRID_W
        for typ in range(3):
            for half in range(HEADS_PER_BLOCK):
                for lr in range(NA_GROUP):
                    rho = NA_GROUP_OFFSETS[typ] + lr
                    off = (NA_WIN - 1 - rho) * GRID_W
                    strip = tc_ref[0, half, :, off:off + n_keys]
                    lo = _na_window_lo(typ, lr)
                    ok = jnp.logical_and(key_row >= lo, key_row < lo + NA_ROWS)
                    bias_scr[typ, half, lr * GRID_W:(lr + 1) * GRID_W, :] = jnp.where(ok, strip, MASK_VALUE)

    win_row = jnp.clip(g * NA_GROUP - NA_ROWS // 2, 0, n_g * NA_GROUP - n_win_rows)
    typ = jnp.where(g == 0, 0, jnp.where(g == n_g - 1, 2, 1))
    start = pl.multiple_of(win_row * GRID_W, GRID_W)
    k_win = k_ref[0, pl.ds(start, n_keys), :]
    v_win = v_ref[0, pl.ds(start, n_keys), :]
    bias = bias_scr[typ]
    o = _softmax_pv(q_ref[0], [k_win, kc_ref[0]], [v_win, vc_ref[0]], [bias, None])
    o_ref[0] = o.astype(o_ref.dtype)


def _na_col_table(rpb, n_heads):
    col = np.arange(GRID_W)
    c0 = np.clip(col - NA_COLS // 2, 0, GRID_W - NA_COLS)
    col_ok = (col[None, :] >= c0[:, None]) & (col[None, :] < c0[:, None] + NA_COLS)
    dcol = col[None, :] - col[:, None] + NA_COLS - 1
    onehot = (dcol[None] == np.arange(2 * NA_COLS - 1)[:, None, None]) & col_ok[None]
    tc = jnp.einsum("hrd,dqk->hqrk", rpb, jnp.asarray(onehot, F32), precision=lax.Precision.HIGHEST)
    tc = jnp.where(col_ok[None, :, None, :], tc, MASK_VALUE)
    n_dr = 2 * NA_ROWS - 1
    tc = jnp.pad(tc, ((0, 0), (0, 0), (NA_STRIP_PAD, NA_STRIP_BLOCKS - NA_STRIP_PAD - n_dr), (0, 0)),
                 constant_values=MASK_VALUE)
    return tc.reshape(n_heads // 2, HEADS_PER_BLOCK, GRID_W, NA_STRIP_BLOCKS * GRID_W)


def _na_attn(qkv, kvc, col_table, width):
    b, t, _ = qkv.shape
    ctx = kvc.shape[1]
    nb = width // LANES
    tq = NA_GROUP * GRID_W
    rows = t // GRID_W
    assert rows % NA_GROUP == 0 and rows >= NA_WIN
    return pl.pallas_call(
        functools.partial(_na_kernel, n_win_rows=NA_WIN),
        out_shape=jax.ShapeDtypeStruct((b, t, width), MXU),
        grid=(nb, b, rows // NA_GROUP),
        in_specs=[pl.BlockSpec((1, tq, LANES), lambda p, bi, g: (bi, g, p)),
                  pl.BlockSpec((1, t, LANES), lambda p, bi, g: (bi, 0, nb + p)),
                  pl.BlockSpec((1, t, LANES), lambda p, bi, g: (bi, 0, 2 * nb + p)),
                  pl.BlockSpec((1, ctx, LANES), lambda p, bi, g: (bi, 0, p)),
                  pl.BlockSpec((1, ctx, LANES), lambda p, bi, g: (bi, 0, nb + p)),
                  pl.BlockSpec((1, HEADS_PER_BLOCK, GRID_W, NA_STRIP_BLOCKS * GRID_W), lambda p, bi, g: (p, 0, 0, 0))],
        out_specs=pl.BlockSpec((1, tq, LANES), lambda p, bi, g: (bi, g, p)),
        scratch_shapes=[pltpu.VMEM((3, HEADS_PER_BLOCK, tq, NA_WIN * GRID_W), F32)],
        compiler_params=_cparams(("arbitrary", "arbitrary", "arbitrary")),
        name="na_attn",
    )(qkv, qkv, qkv, kvc, kvc, col_table)


def _rope_tables(t):
    pos = jnp.arange(t, dtype=jnp.int32)
    row = (pos // GRID_W).astype(F32)
    col = (pos % GRID_W).astype(F32)
    pairs = HEAD_DIM // 4
    inv = ROPE_THETA ** (-jnp.arange(pairs, dtype=F32) / pairs)
    ang = jnp.concatenate([row[:, None] * inv, col[:, None] * inv], axis=-1)
    cos, sin = jnp.cos(ang), jnp.sin(ang)
    cos_h = jnp.concatenate([cos, cos], axis=-1)
    sin_h = jnp.concatenate([-sin, sin], axis=-1)
    return jnp.tile(cos_h, (1, HEADS_PER_BLOCK)), jnp.tile(sin_h, (1, HEADS_PER_BLOCK))


def _block_diag2(top, bottom):
    z_t = jnp.zeros_like(top)
    z_b = jnp.zeros_like(bottom)
    return jnp.concatenate([jnp.concatenate([top, z_t], axis=1), jnp.concatenate([z_b, bottom], axis=1)], axis=0)


def kernel(x, c, ctx, c_ctx, l0_norm1, l0_norm2, l0_ada_w, l0_ada_b, l0_w_in, l0_shift_mu, l0_w0_f, l0_w0_b, l0_ww2_f, l0_ww2_b, l0_a0_f, l0_a0_b, l0_wa2_f, l0_wa2_b, l0_wg2, l0_k_k, l0_k_a, l0_r_k, l0_lnx_g, l0_lnx_b, l0_q_norm, l0_k_norm, l0_w_out, l0_mlp_w1, l0_mlp_w2, l1_norm1, l1_norm2, l1_ada_w, l1_ada_b, l1_w_qkv, l1_rpb, l1_w_out, l1_mlp_w1, l1_mlp_w2, final_norm):
    b, t, d = x.shape
    n_ctx = ctx.shape[1]
    dim = l0_w0_f.shape[0]
    q_width = l0_w_out.shape[0] - dim
    gqa_cols = l0_w_in.shape[1] - l0_shift_mu.shape[0]
    kv_width = (gqa_cols - q_width) // 2
    rw_cols = l0_shift_mu.shape[0]
    rw_pad = 3 * dim + 4 * LANES
    assert rw_cols <= rw_pad and 3 * dim + 2 * LANES == rw_cols - l0_wg2.shape[0]
    tm = min(512, t)
    tm_c = n_ctx

    cc = jnp.zeros((8, d), F32).at[:b].set(c).at[b].set(c_ctx)

    def modulation(ada_w, ada_b):
        mod = _ada_mod(cc, ada_w, ada_b)
        lat = mod[:b].reshape(b, 6, 1, d)
        cx = jnp.broadcast_to(mod[b].reshape(1, 6, 1, d), (b, 6, 1, d))
        return [lat[:, i] for i in range(6)], [cx[:, i] for i in range(6)]

    mod_l, mod_c = modulation(l0_ada_w, l0_ada_b)

    w_gqa = l0_w_in[:, :gqa_cols].astype(MXU)
    w_rw = jnp.pad(l0_w_in[:, gqa_cols:], ((0, 0), (0, rw_pad - rw_cols))).astype(MXU)
    mu = jnp.pad(l0_shift_mu, (0, rw_pad - rw_cols)).reshape(1, rw_pad)
    lw_w = _block_diag2(l0_ww2_f, l0_ww2_b).astype(MXU)
    la_w = _block_diag2(l0_wa2_f, l0_wa2_b).astype(MXU)
    lg_w = jnp.pad(l0_wg2, ((0, 2 * LANES - l0_wg2.shape[0]), (0, 0))).astype(MXU)
    w0 = jnp.concatenate([l0_w0_f, l0_w0_b]).reshape(1, 2 * dim)
    a0 = jnp.concatenate([l0_a0_f, l0_a0_b]).reshape(1, 2 * dim)
    k_k = l0_k_k.reshape(1, dim)
    k_a = l0_k_a.reshape(1, dim)
    r_k = l0_r_k.reshape(1, dim)
    qg = jnp.tile(l0_q_norm, HEADS_PER_BLOCK).reshape(1, LANES)
    kg = jnp.tile(l0_k_norm, HEADS_PER_BLOCK).reshape(1, LANES)
    cos_l, sin_l = _rope_tables(t)
    cos_c, sin_c = jnp.ones((n_ctx, LANES), F32), jnp.zeros((n_ctx, LANES), F32)
    w_out_gqa = l0_w_out[:q_width].astype(MXU)
    w_out_rw = l0_w_out[q_width:].astype(MXU)

    def half_layer0(xs, mod, tm_, cos, sin, is_ctx):
        pg = _inproj(xs, l0_norm1, mod[0], mod[1], w_gqa, tm_, 512, F32)
        pr = _inproj(xs, l0_norm1, mod[0], mod[1], w_rw, tm_, 512, F32)
        q, kd, vd = _gqa_prep(pg, cos, sin, qg, kg, q_width, kv_width, tm_)
        prep = _rwkv_prep(pr, mu, lw_w, la_w, lg_w, w0, a0, k_k, k_a, r_k, dim, 256, is_ctx)
        return q, kd, vd, prep

    q_c, kd_c, vd_c, prep_c = half_layer0(ctx, mod_c, tm_c, cos_c, sin_c, True)
    q_l, kd_l, vd_l, prep_l = half_layer0(x, mod_l, tm, cos_l, sin_l, False)

    o_gqa_l = _gqa_attn(q_l, [(kd_l, vd_l), (kd_c, vd_c)], 256)
    o_gqa_c = _gqa_attn(q_c, [(kd_c, vd_c)], n_ctx)

    h0 = jnp.zeros((2, b, dim // LANES, LANES, LANES), F32)
    y_c, h_c = _rwkv_scan(*prep_c[:6], h0)
    y_l, _ = _rwkv_scan(*prep_l[:6], h_c)
    o_rw_l = _rwkv_finish(y_l, prep_l[6], prep_l[7], l0_lnx_g, l0_lnx_b, tm)
    o_rw_c = _rwkv_finish(y_c, prep_c[6], prep_c[7], l0_lnx_g, l0_lnx_b, tm_c)

    w1 = l0_mlp_w1.astype(MXU)
    w2 = l0_mlp_w2.astype(MXU)
    x = _outproj(x, mod_l[2], [(o_gqa_l, w_out_gqa), (o_rw_l, w_out_rw)], tm)
    x = _mlp(x, l0_norm2, mod_l[3], mod_l[4], mod_l[5], w1, w2, final_norm, tm, 512, False)
    ctx = _outproj(ctx, mod_c[2], [(o_gqa_c, w_out_gqa), (o_rw_c, w_out_rw)], tm_c)
    ctx = _mlp(ctx, l0_norm2, mod_c[3], mod_c[4], mod_c[5], w1, w2, final_norm, tm_c, 512, False)

    mod_l, mod_c = modulation(l1_ada_w, l1_ada_b)
    width = l1_w_out.shape[0]
    n_heads = width // HEAD_DIM
    scale = jnp.concatenate([jnp.full((width,), HEAD_DIM ** -0.5, F32), jnp.ones((2 * width,), F32)])
    w_qkv = (l1_w_qkv * scale).astype(MXU)
    qkv = _inproj(x, l1_norm1, mod_l[0], mod_l[1], w_qkv, tm, 512, MXU)
    kvc = _inproj(ctx, l1_norm1, mod_c[0], mod_c[1], w_qkv[:, width:], tm_c, 512, MXU)
    o_na = _na_attn(qkv, kvc, _na_col_table(l1_rpb, n_heads), width)
    x = _outproj(x, mod_l[2], [(o_na, l1_w_out.astype(MXU))], tm)
    x = _mlp(x, l1_norm2, mod_l[3], mod_l[4], mod_l[5], l1_mlp_w1.astype(MXU), l1_mlp_w2.astype(MXU),
             final_norm, tm, 512, True)
    return x
```

```python
import functools

import jax
import jax.numpy as jnp
import numpy as np
from jax import lax
from jax.experimental import pallas as pl
from jax.experimental.pallas import tpu as pltpu

F32 = jnp.float32
MXU = jnp.bfloat16

LANES = 128
HEAD_DIM = 64
HEADS_PER_BLOCK = LANES // HEAD_DIM
GRID_W = 64
NORM_EPS = 1e-6
LNX_EPS = 64e-5
ROPE_THETA = 10000.0
NA_ROWS = 8
NA_COLS = 16
NA_GROUP = 4
NA_WIN = NA_GROUP + NA_ROWS
MASK_VALUE = -1e30
LOG2E = float(np.log2(np.e))
ATTN_SCALE = HEAD_DIM ** -0.5 * LOG2E
ATTN_KEY_CHUNK = 1024
INPROJ_MAX_COLS = 2048
MLP_MAX_HIDDEN = 1024
SCAN_CHUNK = 64
SCAN_PAIRS = 8
SCAN_STEP_CHUNKS = 2
VMEM_LIMIT = 56 * 1024 * 1024


def _cparams(sem):
    return pltpu.CompilerParams(dimension_semantics=sem, vmem_limit_bytes=VMEM_LIMIT)


def _mm(a, b):
    return jnp.dot(a.astype(MXU), b.astype(MXU), preferred_element_type=F32)


def _mm_nt(a, b):
    return lax.dot_general(a.astype(MXU), b.astype(MXU), (((1,), (1,)), ((), ())),
                           preferred_element_type=F32)


def _mm_tn(a, b):
    return lax.dot_general(a.astype(MXU), b.astype(MXU), (((0,), (0,)), ((), ())),
                           preferred_element_type=F32)


def _split3(x):
    hi = x.astype(MXU)
    r1 = x - hi.astype(F32)
    mid = r1.astype(MXU)
    lo = (r1 - mid.astype(F32)).astype(MXU)
    return hi, mid, lo


def _mm_exact_lhs(m, x):
    hi, mid, lo = _split3(x)
    mb = m.astype(MXU)
    return (jnp.dot(mb, hi, preferred_element_type=F32) + jnp.dot(mb, mid, preferred_element_type=F32)
            + jnp.dot(mb, lo, preferred_element_type=F32))


def _head_ones():
    r = lax.broadcasted_iota(jnp.int32, (LANES, LANES), 0) // HEAD_DIM
    c = lax.broadcasted_iota(jnp.int32, (LANES, LANES), 1) // HEAD_DIM
    return jnp.where(r == c, 1.0, 0.0).astype(MXU)


def _headsum(x, ones):
    hi, mid, lo = _split3(x)
    return (jnp.dot(hi, ones, preferred_element_type=F32) + jnp.dot(mid, ones, preferred_element_type=F32)
            + jnp.dot(lo, ones, preferred_element_type=F32))


def _lane(shape):
    return lax.broadcasted_iota(jnp.int32, shape, len(shape) - 1)


def _sigmoid(x):
    return 1.0 / (1.0 + jnp.exp(-x))


def _ada_kernel(c_ref, w_ref, b_ref, o_ref):
    c = c_ref[...]
    a = c * _sigmoid(c)
    o_ref[...] = _mm(a, w_ref[...]) + b_ref[...]


def _ada_mod(cc, w, bias):
    d, n = w.shape
    tn = 1024
    return pl.pallas_call(
        _ada_kernel,
        out_shape=jax.ShapeDtypeStruct((cc.shape[0], n), F32),
        grid=(n // tn,),
        in_specs=[pl.BlockSpec((cc.shape[0], d), lambda j: (0, 0)),
                  pl.BlockSpec((d, tn), lambda j: (0, j)),
                  pl.BlockSpec((1, tn), lambda j: (0, j))],
        out_specs=pl.BlockSpec((cc.shape[0], tn), lambda j: (0, j)),
        compiler_params=_cparams(("arbitrary",)),
        name="ada_mod",
    )(cc, w, bias.reshape(1, n))


def _norm_mod(x, g, shift, scale):
    y = x * lax.rsqrt(jnp.mean(x * x, axis=-1, keepdims=True) + NORM_EPS)
    return (y * g) * (1.0 + scale) + shift


def _inproj_kernel(x_ref, g_ref, sh_ref, sc_ref, w_ref, o_ref, a_scr):
    @pl.when(pl.program_id(2) == 0)
    def _():
        a_scr[...] = _norm_mod(x_ref[0], g_ref[...], sh_ref[0], sc_ref[0]).astype(a_scr.dtype)

    o_ref[0] = jnp.dot(a_scr[...], w_ref[...], preferred_element_type=F32).astype(o_ref.dtype)


def _col_tile(n, cap):
    return max(tn for tn in range(LANES, cap + 1, LANES) if n % tn == 0)


def _inproj(x, g, shift, scale, w, tm, out_dtype):
    b, t, d = x.shape
    n = w.shape[1]
    tn = _col_tile(n, INPROJ_MAX_COLS)
    return pl.pallas_call(
        _inproj_kernel,
        out_shape=jax.ShapeDtypeStruct((b, t, n), out_dtype),
        grid=(b, t // tm, n // tn),
        in_specs=[pl.BlockSpec((1, tm, d), lambda bi, i, j: (bi, i, 0)),
                  pl.BlockSpec((1, d), lambda bi, i, j: (0, 0)),
                  pl.BlockSpec((1, 1, d), lambda bi, i, j: (bi, 0, 0)),
                  pl.BlockSpec((1, 1, d), lambda bi, i, j: (bi, 0, 0)),
                  pl.BlockSpec((d, tn), lambda bi, i, j: (0, j))],
        out_specs=pl.BlockSpec((1, tm, tn), lambda bi, i, j: (bi, i, j)),
        scratch_shapes=[pltpu.VMEM((tm, d), MXU)],
        compiler_params=_cparams(("parallel", "parallel", "arbitrary")),
        name="inproj",
    )(x, g.reshape(1, d), shift, scale, w)


def _swap_half_heads(x):
    first = (_lane(x.shape) % HEAD_DIM) < HEAD_DIM // 2
    return jnp.where(first, pltpu.roll(x, LANES - HEAD_DIM // 2, 1), pltpu.roll(x, HEAD_DIM // 2, 1))


def _gqa_prep_kernel(q_ref, kv_ref, cos_ref, sin_ref, qg_ref, kg_ref, qo_ref, ko_ref, vo_ref, *, n_q, n_kv):
    ones = _head_ones()
    cos = cos_ref[...]
    sin = sin_ref[...]
    first_head = _lane(cos.shape) < HEAD_DIM

    def norm_rope(x, g):
        ms = _headsum(x * x, ones) * (1.0 / HEAD_DIM)
        y = x * lax.rsqrt(ms + NORM_EPS) * g
        return y * cos + _swap_half_heads(y) * sin

    def dup(x, half):
        rolled = pltpu.roll(x, HEAD_DIM, 1)
        return jnp.where(first_head, x, rolled) if half == 0 else jnp.where(first_head, rolled, x)

    for j in range(n_q):
        x = q_ref[0, :, j * LANES:(j + 1) * LANES]
        qo_ref[0, :, j * LANES:(j + 1) * LANES] = (norm_rope(x, qg_ref[...]) * ATTN_SCALE).astype(qo_ref.dtype)
    for j in range(n_kv):
        k = norm_rope(kv_ref[0, :, j * LANES:(j + 1) * LANES], kg_ref[...])
        v = kv_ref[0, :, (n_kv + j) * LANES:(n_kv + j + 1) * LANES]
        for half in range(HEADS_PER_BLOCK):
            ko_ref[0, HEADS_PER_BLOCK * j + half] = dup(k, half).astype(ko_ref.dtype)
            vo_ref[0, HEADS_PER_BLOCK * j + half] = dup(v, half).astype(vo_ref.dtype)


def _gqa_prep(p, cos, sin, qg, kg, q_width, kv_width, tm):
    b, t, _ = p.shape
    n_q = q_width // LANES
    n_kv = kv_width // LANES
    kvh = kv_width // HEAD_DIM
    kern = functools.partial(_gqa_prep_kernel, n_q=n_q, n_kv=n_kv)
    kv_spec = pl.BlockSpec((1, kvh, tm, LANES), lambda bi, i: (bi, 0, i, 0))
    return pl.pallas_call(
        kern,
        out_shape=(jax.ShapeDtypeStruct((b, t, q_width), MXU),
                   jax.ShapeDtypeStruct((b, kvh, t, LANES), MXU),
                   jax.ShapeDtypeStruct((b, kvh, t, LANES), MXU)),
        grid=(b, t // tm),
        in_specs=[pl.BlockSpec((1, tm, q_width), lambda bi, i: (bi, i, 0)),
                  pl.BlockSpec((1, tm, 2 * kv_width), lambda bi, i: (bi, i, q_width // (2 * kv_width))),
                  pl.BlockSpec((tm, LANES), lambda bi, i: (i, 0)),
                  pl.BlockSpec((tm, LANES), lambda bi, i: (i, 0)),
                  pl.BlockSpec((1, LANES), lambda bi, i: (0, 0)),
                  pl.BlockSpec((1, LANES), lambda bi, i: (0, 0))],
        out_specs=(pl.BlockSpec((1, tm, q_width), lambda bi, i: (bi, i, 0)), kv_spec, kv_spec),
        compiler_params=_cparams(("parallel", "parallel")),
        name="gqa_prep",
    )(p, p, cos, sin, qg, kg)


def _softmax_pv(q, chunks):
    first_head = _lane(q.shape) < HEAD_DIM
    zero = jnp.zeros_like(q)
    qs = [jnp.where(first_head, q, zero), jnp.where(first_head, zero, q)]
    heads = range(HEADS_PER_BLOCK)
    s = [[None] * len(chunks) for _ in heads]
    m = [None for _ in heads]
    for c, (load_k, _, load_bias) in enumerate(chunks):
        k = load_k()
        for h in heads:
            sc = _mm_nt(qs[h], k)
            if load_bias is not None:
                sc = sc + load_bias(h)
            s[h][c] = sc
            mc = jnp.max(sc, axis=-1, keepdims=True)
            m[h] = mc if m[h] is None else jnp.maximum(m[h], mc)
    l = [None for _ in heads]
    acc = [None for _ in heads]
    for c, (_, load_v, _) in enumerate(chunks):
        v = load_v()
        for h in heads:
            p = jnp.exp2(s[h][c] - m[h])
            lc = jnp.sum(p, axis=-1, keepdims=True)
            pv = _mm(p, v)
            l[h] = lc if l[h] is None else l[h] + lc
            acc[h] = pv if acc[h] is None else acc[h] + pv
    return jnp.where(first_head, acc[0] / l[0], acc[1] / l[1])


def _key_chunks(k_ref, v_ref, idx, n_keys, chunk):
    out = []
    for start in range(0, n_keys, chunk):
        size = min(chunk, n_keys - start)
        out.append((lambda s=start, z=size: k_ref[idx + (pl.ds(s, z), slice(None))],
                    lambda s=start, z=size: v_ref[idx + (pl.ds(s, z), slice(None))], None))
    return out


def _gqa_attn_kernel(q_ref, *refs, n_sets):
    o_ref = refs[-1]
    chunks = []
    for i in range(n_sets):
        chunks += _key_chunks(refs[2 * i], refs[2 * i + 1], (0, 0), refs[2 * i].shape[2], ATTN_KEY_CHUNK)
    o_ref[0] = _softmax_pv(q_ref[0], chunks).astype(o_ref.dtype)


def _gqa_attn(q, kv_sets, tq):
    b, t, width = q.shape
    n_blocks = width // LANES
    kvh = kv_sets[0][0].shape[1]
    blocks_per_kv = n_blocks // kvh
    in_specs = [pl.BlockSpec((1, tq, LANES), lambda bi, p, i: (bi, i, p))]
    args = [q]
    for k, v in kv_sets:
        s = k.shape[2]
        spec = pl.BlockSpec((1, 1, s, LANES), lambda bi, p, i: (bi, p // blocks_per_kv, 0, 0))
        in_specs += [spec, spec]
        args += [k, v]
    return pl.pallas_call(
        functools.partial(_gqa_attn_kernel, n_sets=len(kv_sets)),
        out_shape=jax.ShapeDtypeStruct((b, t, width), MXU),
        grid=(b, n_blocks, t // tq),
        in_specs=in_specs,
        out_specs=pl.BlockSpec((1, tq, LANES), lambda bi, p, i: (bi, i, p)),
        compiler_params=_cparams(("parallel", "parallel", "arbitrary")),
        name="gqa_attn",
    )(*args)


def _rwkv_prep_kernel(p_ref, up_ref, dn_ref, mu_ref, lw_w_ref, la_w_ref, lg_w_ref, w0_ref, a0_ref, kk_ref, ka_ref,
                      rk_ref, r_o, v_o, kk_o, lw_o, kd_o, bb_o, bonus_o, gate_o, xm_scr, *, is_ctx, dim):
    tm = p_ref.shape[1]
    n_all = p_ref.shape[2] // LANES
    n_dim = dim // LANES
    i = pl.program_id(1)
    n_i = pl.num_programs(1)
    row = lax.broadcasted_iota(jnp.int32, (tm, LANES), 0)
    cls = _lane((tm, LANES)) % 4
    if is_ctx:
        first = row == 0
        last = row == tm - 1
    else:
        first = (row % GRID_W) == 0
        last = (row % GRID_W) == GRID_W - 1
        up_ok = jnp.logical_or(row >= GRID_W, i > 0)
        dn_ok = jnp.logical_or(row < tm - GRID_W, i < n_i - 1)

    for j in range(n_all):
        sl = slice(j * LANES, (j + 1) * LANES)
        p = p_ref[0, :, sl]
        prev = jnp.where(first, 0.0, pltpu.roll(p, 1, 0))
        nxt = jnp.where(last, 0.0, pltpu.roll(p, tm - 1, 0))
        if is_ctx:
            sh = jnp.where(cls % 2 == 0, prev, nxt)
        else:
            up = jnp.concatenate([up_ref[0, :, sl], p[:tm - GRID_W]], axis=0)
            dn = jnp.concatenate([p[GRID_W:], dn_ref[0, :, sl]], axis=0)
            up = jnp.where(up_ok, up, 0.0)
            dn = jnp.where(dn_ok, dn, 0.0)
            sh = jnp.where(cls == 0, prev, jnp.where(cls == 1, nxt, jnp.where(cls == 2, up, dn)))
        xm_scr[:, sl] = p + mu_ref[:, sl] * (sh - p)

    ones = _head_ones()
    x_w = jnp.tanh(xm_scr[:, 3 * dim:3 * dim + LANES]).astype(MXU)
    x_a = xm_scr[:, 3 * dim + LANES:3 * dim + 2 * LANES].astype(MXU)
    x_g = _sigmoid(xm_scr[:, 3 * dim + 2 * LANES:3 * dim + 4 * LANES]).astype(MXU)
    for j in range(n_dim):
        sl = slice(j * LANES, (j + 1) * LANES)
        r = xm_scr[:, sl]
        k = xm_scr[:, dim + j * LANES:dim + (j + 1) * LANES]
        v = xm_scr[:, 2 * dim + j * LANES:2 * dim + (j + 1) * LANES]
        kkr = k * kk_ref[:, sl]
        kk = kkr * lax.rsqrt(jnp.maximum(_headsum(kkr * kkr, ones), 1e-12))
        r_o[0, :, sl] = r
        v_o[0, :, sl] = v
        kk_o[0, :, sl] = kk
        gate_o[0, :, sl] = jnp.dot(x_g, lg_w_ref[:, sl], preferred_element_type=F32)
        bonus = jnp.zeros_like(r)
        for d in range(2):
            dsl = slice(d * dim + j * LANES, d * dim + (j + 1) * LANES)
            z = w0_ref[:, dsl] + jnp.dot(x_w, lw_w_ref[:, dsl], preferred_element_type=F32)
            softplus = jnp.maximum(-z, 0.0) + jnp.log(1.0 + jnp.exp(-jnp.abs(z)))
            lw_o[d, 0, :, sl] = -jnp.exp(-softplus - 0.5)
            iclr = _sigmoid(a0_ref[:, dsl] + jnp.dot(x_a, la_w_ref[:, dsl], preferred_element_type=F32))
            kd = k * (1.0 + (iclr - 1.0) * ka_ref[:, sl])
            kd_o[d, 0, :, sl] = kd
            bb_o[d, 0, :, sl] = kk * iclr
            bonus = bonus + _headsum(r * kd * rk_ref[:, sl], ones) * v
        bonus_o[0, :, sl] = bonus


def _rwkv_prep(p, mu, lw_w, la_w, lg_w, w0, a0, k_k, k_a, r_k, dim, tm, is_ctx):
    b, t, width = p.shape
    if is_ctx:
        assert t == tm
    else:
        assert tm % GRID_W == 0 and t % tm == 0
    hb = tm // GRID_W
    n_halo = t // GRID_W
    kern = functools.partial(_rwkv_prep_kernel, is_ctx=is_ctx, dim=dim)
    vec = lambda n: pl.BlockSpec((1, n), lambda bi, i: (0, 0))
    mat = lambda k, n: pl.BlockSpec((k, n), lambda bi, i: (0, 0))
    one = jax.ShapeDtypeStruct((b, t, dim), F32)
    two = jax.ShapeDtypeStruct((2, b, t, dim), F32)
    one_spec = pl.BlockSpec((1, tm, dim), lambda bi, i: (bi, i, 0))
    two_spec = pl.BlockSpec((2, 1, tm, dim), lambda bi, i: (0, bi, i, 0))
    return pl.pallas_call(
        kern,
        out_shape=(one, one, one, two, two, two, one, one),
        grid=(b, t // tm),
        in_specs=[pl.BlockSpec((1, tm, width), lambda bi, i: (bi, i, 0)),
                  pl.BlockSpec((1, GRID_W, width), lambda bi, i: (bi, jnp.maximum(i * hb - 1, 0), 0)),
                  pl.BlockSpec((1, GRID_W, width), lambda bi, i: (bi, jnp.minimum((i + 1) * hb, n_halo - 1), 0)),
                  vec(width), mat(LANES, 2 * dim), mat(LANES, 2 * dim), mat(2 * LANES, dim),
                  vec(2 * dim), vec(2 * dim), vec(dim), vec(dim), vec(dim)],
        out_specs=(one_spec, one_spec, one_spec, two_spec, two_spec, two_spec, one_spec, one_spec),
        scratch_shapes=[pltpu.VMEM((tm, width), F32)],
        compiler_params=_cparams(("parallel", "parallel")),
        name="rwkv_prep_ctx" if is_ctx else "rwkv_prep",
    )(p, p, p, mu, lw_w, la_w, lg_w, w0, a0, k_k, k_a, r_k)


def _scan_chunk_terms(r, v, kk, lw, kd, bb, sgn):
    blocks = range(len(r))
    c = r[0].shape[0]
    n = 2 * c
    ri = lax.broadcasted_iota(jnp.int32, (n, n), 0)
    ci = lax.broadcasted_iota(jnp.int32, (n, n), 1)
    same = (ri // c) == (ci // c)
    dt = ((ri % c) - (ci % c)) * sgn
    strict = jnp.logical_and(same, dt > 0)
    incl = jnp.logical_and(same, dt >= 0)
    eye = ri == ci
    head_lanes = (ri // c) == (ci // HEAD_DIM)
    zero = jnp.zeros((n, n), F32)
    ti = lax.broadcasted_iota(jnp.int32, (c, c), 0)
    tj = lax.broadcasted_iota(jnp.int32, (c, c), 1)
    tri = jnp.where((ti - tj) * sgn >= 0, 1.0, 0.0)

    def stack(x):
        return jnp.concatenate([x, x], axis=0)

    def own(x):
        return jnp.where(head_lanes, stack(x), zero)

    cum = [_mm_exact_lhs(tri, lw[p]) for p in blocks]
    tot = [jnp.sum(lw[p], axis=0, keepdims=True) for p in blocks]
    e_inv = [jnp.exp(-cum[p]) for p in blocks]
    e_fin = [jnp.exp(tot[p] - cum[p]) for p in blocks]
    at = [own(-kk[p] * jnp.exp(cum[p] - lw[p])) for p in blocks]
    rt = [own(r[p] * jnp.exp(cum[p])) for p in blocks]
    bt = [stack(bb[p] * e_inv[p]) for p in blocks]
    kt = [stack(kd[p] * e_inv[p]) for p in blocks]
    bhat = [own(bb[p] * e_fin[p]) for p in blocks]
    khat = [own(kd[p] * e_fin[p]) for p in blocks]
    vbd = [own(v[p]) for p in blocks]

    full = [_mm_nt(jnp.concatenate([at[p], rt[p]], axis=0), jnp.concatenate([bt[p], kt[p]], axis=0)) for p in blocks]
    a_ab = [jnp.where(strict, full[p][:n, :n], zero) for p in blocks]
    a_ak = [jnp.where(strict, full[p][:n, n:], zero) for p in blocks]
    a_rb = [jnp.where(incl, full[p][n:, :n], zero) for p in blocks]
    a_rk = [jnp.where(incl, full[p][n:, n:], zero) for p in blocks]

    ident = jnp.where(eye, 1.0, 0.0)
    inv = [ident + a_ab[p] for p in blocks]
    power = a_ab
    for _ in range(int(np.log2(c)) - 1):
        power = [_mm(power[p], power[p]) for p in blocks]
        inv = [inv[p] + _mm(inv[p], power[p]) for p in blocks]

    xv = [_mm(jnp.concatenate([a_ak[p], a_rk[p]], axis=0), vbd[p]) for p in blocks]
    w12 = [_mm(inv[p], jnp.concatenate([at[p], xv[p][:n]], axis=1)) for p in blocks]
    yw = [_mm(a_rb[p], w12[p]) for p in blocks]
    mn = [_mm_tn(bhat[p], w12[p]) for p in blocks]
    py = [rt[p] + yw[p][:, :n] for p in blocks]
    yl = [yw[p][:, n:] + xv[p][n:] for p in blocks]
    m = [jnp.where(eye, jnp.exp(tot[p]), zero) + mn[p][:, :n] for p in blocks]
    nn = [mn[p][:, n:] + _mm_tn(khat[p], vbd[p]) for p in blocks]
    return py, yl, m, nn


def _scan_kernel(r_ref, v_ref, kk_ref, lw_ref, kd_ref, bb_ref, h0_ref, y_ref, ht_ref, h_scr, *, pairs, chunks):
    d = pl.program_id(0)
    ci = pl.program_id(2)
    sgn = 1 - 2 * d
    c = SCAN_CHUNK
    n = 2 * c

    @pl.when(ci == 0)
    def _():
        h_scr[...] = h0_ref[0, 0]

    rows = [pl.ds(pl.multiple_of((d * (chunks - 1) + sgn * s) * c, c), c) for s in range(chunks)]
    lanes = [slice(j * LANES, (j + 1) * LANES) for j in range(pairs)]
    probs = [(rw, ln) for rw in rows for ln in lanes]
    py, yl, m, nn = _scan_chunk_terms(
        [r_ref[0, rw, ln] for rw, ln in probs], [v_ref[0, rw, ln] for rw, ln in probs],
        [kk_ref[0, rw, ln] for rw, ln in probs], [lw_ref[0, 0, rw, ln] for rw, ln in probs],
        [kd_ref[0, 0, rw, ln] for rw, ln in probs], [bb_ref[0, 0, rw, ln] for rw, ln in probs], sgn)

    h = [h_scr[j] for j in range(pairs)]
    for s in range(chunks):
        idx = [s * pairs + j for j in range(pairs)]
        yh = [_mm(jnp.concatenate([py[i], m[i]], axis=0), h[j]) for j, i in enumerate(idx)]
        for j, i in enumerate(idx):
            y2 = yh[j][:n] + yl[i]
            y_ref[0, 0, rows[s], lanes[j]] = y2[:c] + y2[c:]
        h = [yh[j][n:] + nn[i] for j, i in enumerate(idx)]
    for j in range(pairs):
        h_scr[j] = h[j]

    @pl.when(ci == pl.num_programs(2) - 1)
    def _():
        ht_ref[0, 0] = h_scr[...]


def _rwkv_scan(r, v, kk, lw, kd, bb, h0):
    b, t, dim = r.shape
    c = SCAN_CHUNK * SCAN_STEP_CHUNKS
    assert t % c == 0
    nch = t // c
    width = SCAN_PAIRS * LANES
    nblk = dim // width

    def chunk(d, ci):
        return ci + d * (nch - 1 - 2 * ci)

    one = pl.BlockSpec((1, c, width), lambda d, g, ci: (g // nblk, chunk(d, ci), g % nblk))
    two = pl.BlockSpec((1, 1, c, width), lambda d, g, ci: (d, g // nblk, chunk(d, ci), g % nblk))
    st = pl.BlockSpec((1, 1, SCAN_PAIRS, LANES, LANES), lambda d, g, ci: (d, g // nblk, g % nblk, 0, 0))
    return pl.pallas_call(
        functools.partial(_scan_kernel, pairs=SCAN_PAIRS, chunks=SCAN_STEP_CHUNKS),
        out_shape=(jax.ShapeDtypeStruct((2, b, t, dim), F32), jax.ShapeDtypeStruct(h0.shape, F32)),
        grid=(2, b * nblk, nch),
        in_specs=[one, one, one, two, two, two, st],
        out_specs=(two, st),
        scratch_shapes=[pltpu.VMEM((SCAN_PAIRS, LANES, LANES), F32)],
        compiler_params=_cparams(("parallel", "parallel", "arbitrary")),
        name="rwkv_scan",
    )(r, v, kk, lw, kd, bb, h0)


def _rwkv_finish_kernel(y_ref, bonus_ref, gate_ref, g_ref, b_ref, o_ref):
    ones = _head_ones()
    for j in range(o_ref.shape[2] // LANES):
        sl = slice(j * LANES, (j + 1) * LANES)
        y = y_ref[0, 0, :, sl] + y_ref[1, 0, :, sl]
        mu = _headsum(y, ones) * (1.0 / HEAD_DIM)
        dlt = y - mu
        var = _headsum(dlt * dlt, ones) * (1.0 / HEAD_DIM)
        yn = dlt * lax.rsqrt(var + LNX_EPS)
        o_ref[0, :, sl] = ((yn * g_ref[:, sl] + b_ref[:, sl] + bonus_ref[0, :, sl]) * gate_ref[0, :, sl]).astype(o_ref.dtype)


def _rwkv_finish(y, bonus, gate, lnx_g, lnx_b, tm):
    _, b, t, dim = y.shape
    one_spec = pl.BlockSpec((1, tm, dim), lambda bi, i: (bi, i, 0))
    vec = pl.BlockSpec((1, dim), lambda bi, i: (0, 0))
    return pl.pallas_call(
        _rwkv_finish_kernel,
        out_shape=jax.ShapeDtypeStruct((b, t, dim), MXU),
        grid=(b, t // tm),
        in_specs=[pl.BlockSpec((2, 1, tm, dim), lambda bi, i: (0, bi, i, 0)), one_spec, one_spec, vec, vec],
        out_specs=one_spec,
        compiler_params=_cparams(("parallel", "parallel")),
        name="rwkv_finish",
    )(y, bonus, gate, lnx_g.reshape(1, dim), lnx_b.reshape(1, dim))


def _outproj_kernel(*refs, n_in):
    x_ref, gate_ref = refs[0], refs[1]
    o_ref = refs[-1]
    acc = None
    for i in range(n_in):
        part = jnp.dot(refs[2 + 2 * i][0], refs[3 + 2 * i][...], preferred_element_type=F32)
        acc = part if acc is None else acc + part
    o_ref[0] = x_ref[0] + gate_ref[0] * acc


def _outproj(x, gate, parts, tm):
    b, t, d = x.shape
    in_specs = [pl.BlockSpec((1, tm, d), lambda bi, i: (bi, i, 0)),
                pl.BlockSpec((1, 1, d), lambda bi, i: (bi, 0, 0))]
    args = [x, gate]
    for a, w in parts:
        k = w.shape[0]
        in_specs += [pl.BlockSpec((1, tm, k), lambda bi, i: (bi, i, 0)), pl.BlockSpec((k, d), lambda bi, i: (0, 0))]
        args += [a, w]
    return pl.pallas_call(
        functools.partial(_outproj_kernel, n_in=len(parts)),
        out_shape=jax.ShapeDtypeStruct((b, t, d), F32),
        grid=(b, t // tm),
        in_specs=in_specs,
        out_specs=pl.BlockSpec((1, tm, d), lambda bi, i: (bi, i, 0)),
        compiler_params=_cparams(("parallel", "parallel")),
        name="outproj",
    )(*args)


def _mlp_kernel(x_ref, g_ref, sh_ref, sc_ref, gate_ref, w1_ref, w2_ref, fg_ref, o_ref, a_scr, acc_scr, *, final_norm):
    f = pl.program_id(2)

    @pl.when(f == 0)
    def _():
        a_scr[...] = _norm_mod(x_ref[0], g_ref[...], sh_ref[0], sc_ref[0]).astype(a_scr.dtype)
        acc_scr[...] = jnp.zeros_like(acc_scr)

    h = jnp.dot(a_scr[...], w1_ref[...], preferred_element_type=F32)
    h = jnp.square(jnp.maximum(h, 0.0))
    acc_scr[...] += jnp.dot(h.astype(MXU), w2_ref[...], preferred_element_type=F32)

    @pl.when(f == pl.num_programs(2) - 1)
    def _():
        y = x_ref[0] + gate_ref[0] * acc_scr[...]
        if final_norm:
            y = y * lax.rsqrt(jnp.mean(y * y, axis=-1, keepdims=True) + NORM_EPS) * fg_ref[...]
        o_ref[0] = y


def _mlp(x, g, shift, scale, gate, w1, w2, final_g, tm, final_norm):
    b, t, d = x.shape
    ff = w1.shape[1]
    tf = _col_tile(ff, MLP_MAX_HIDDEN)
    vec3 = pl.BlockSpec((1, 1, d), lambda bi, i, f: (bi, 0, 0))
    vec2 = pl.BlockSpec((1, d), lambda bi, i, f: (0, 0))
    return pl.pallas_call(
        functools.partial(_mlp_kernel, final_norm=final_norm),
        out_shape=jax.ShapeDtypeStruct((b, t, d), F32),
        grid=(b, t // tm, ff // tf),
        in_specs=[pl.BlockSpec((1, tm, d), lambda bi, i, f: (bi, i, 0)), vec2, vec3, vec3, vec3,
                  pl.BlockSpec((d, tf), lambda bi, i, f: (0, f)),
                  pl.BlockSpec((tf, d), lambda bi, i, f: (f, 0)), vec2],
        out_specs=pl.BlockSpec((1, tm, d), lambda bi, i, f: (bi, i, 0)),
        scratch_shapes=[pltpu.VMEM((tm, d), MXU), pltpu.VMEM((tm, d), F32)],
        compiler_params=_cparams(("parallel", "parallel", "arbitrary")),
        name="mlp",
    )(x, g.reshape(1, d), shift, scale, gate, w1, w2, final_g.reshape(1, d))


NA_GROUP_OFFSETS = (0, NA_ROWS // 2, NA_ROWS)
NA_STRIP_PAD = NA_WIN - NA_ROWS
NA_STRIP_BLOCKS = 2 * NA_WIN


def _na_window_lo(typ, local_row):
    return (0, local_row, NA_WIN - NA_ROWS)[typ]


def _na_kernel(q_ref, k_ref, v_ref, kc_ref, vc_ref, tc_ref, o_ref, bias_scr, *, n_win_rows):
    g = pl.program_id(2)
    n_g = pl.num_programs(2)
    n_keys = n_win_rows * GRID_W

    @pl.when(jnp.logical_and(pl.program_id(1) == 0, g == 0))
    def _():
        key_row = _lane((GRID_W, n_keys)) // GRID_W
        for typ in range(3):
            for half in range(HEADS_PER_BLOCK):
                for lr in range(NA_GROUP):
                    rho = NA_GROUP_OFFSETS[typ] + lr
                    off = (NA_WIN - 1 - rho) * GRID_W
                    strip = tc_ref[0, half, :, off:off + n_keys]
                    lo = _na_window_lo(typ, lr)
                    ok = jnp.logical_and(key_row >= lo, key_row < lo + NA_ROWS)
                    r0 = (half * NA_GROUP + lr) * GRID_W
                    bias_scr[typ, r0:r0 + GRID_W, :] = jnp.where(ok, strip, MASK_VALUE)

    win_row = jnp.clip(g * NA_GROUP - NA_ROWS // 2, 0, n_g * NA_GROUP - n_win_rows)
    typ = jnp.where(g == 0, 0, jnp.where(g == n_g - 1, 2, 1))
    start = pl.multiple_of(win_row * GRID_W, GRID_W)
    chunks = [(lambda: k_ref[0, pl.ds(start, n_keys), :], lambda: v_ref[0, pl.ds(start, n_keys), :],
               lambda h: bias_scr[typ, pl.ds(h * NA_GROUP * GRID_W, NA_GROUP * GRID_W), :])]
    chunks += _key_chunks(kc_ref, vc_ref, (0,), kc_ref.shape[1], ATTN_KEY_CHUNK)
    o_ref[0] = _softmax_pv(q_ref[0], chunks).astype(o_ref.dtype)


def _na_col_table(rpb, n_heads):
    col = np.arange(GRID_W)
    c0 = np.clip(col - NA_COLS // 2, 0, GRID_W - NA_COLS)
    col_ok = (col[None, :] >= c0[:, None]) & (col[None, :] < c0[:, None] + NA_COLS)
    dcol = col[None, :] - col[:, None] + NA_COLS - 1
    onehot = (dcol[None] == np.arange(2 * NA_COLS - 1)[:, None, None]) & col_ok[None]
    tc = jnp.einsum("hrd,dqk->hqrk", rpb * LOG2E, jnp.asarray(onehot, F32), precision=lax.Precision.HIGHEST)
    tc = jnp.where(col_ok[None, :, None, :], tc, MASK_VALUE)
    n_dr = 2 * NA_ROWS - 1
    tc = jnp.pad(tc, ((0, 0), (0, 0), (NA_STRIP_PAD, NA_STRIP_BLOCKS - NA_STRIP_PAD - n_dr), (0, 0)),
                 constant_values=MASK_VALUE)
    return tc.reshape(n_heads // 2, HEADS_PER_BLOCK, GRID_W, NA_STRIP_BLOCKS * GRID_W)


def _na_attn(qkv, kvc, col_table, width):
    b, t, _ = qkv.shape
    ctx = kvc.shape[1]
    nb = width // LANES
    tq = NA_GROUP * GRID_W
    rows = t // GRID_W
    assert rows % NA_GROUP == 0 and rows >= NA_WIN
    return pl.pallas_call(
        functools.partial(_na_kernel, n_win_rows=NA_WIN),
        out_shape=jax.ShapeDtypeStruct((b, t, width), MXU),
        grid=(nb, b, rows // NA_GROUP),
        in_specs=[pl.BlockSpec((1, tq, LANES), lambda p, bi, g: (bi, g, p)),
                  pl.BlockSpec((1, t, LANES), lambda p, bi, g: (bi, 0, nb + p)),
                  pl.BlockSpec((1, t, LANES), lambda p, bi, g: (bi, 0, 2 * nb + p)),
                  pl.BlockSpec((1, ctx, LANES), lambda p, bi, g: (bi, 0, p)),
                  pl.BlockSpec((1, ctx, LANES), lambda p, bi, g: (bi, 0, nb + p)),
                  pl.BlockSpec((1, HEADS_PER_BLOCK, GRID_W, NA_STRIP_BLOCKS * GRID_W), lambda p, bi, g: (p, 0, 0, 0))],
        out_specs=pl.BlockSpec((1, tq, LANES), lambda p, bi, g: (bi, g, p)),
        scratch_shapes=[pltpu.VMEM((3, HEADS_PER_BLOCK * tq, NA_WIN * GRID_W), F32)],
        compiler_params=_cparams(("arbitrary", "arbitrary", "arbitrary")),
        name="na_attn",
    )(qkv, qkv, qkv, kvc, kvc, col_table)


def _rope_tables(t):
    pos = jnp.arange(t, dtype=jnp.int32)
    row = (pos // GRID_W).astype(F32)
    col = (pos % GRID_W).astype(F32)
    pairs = HEAD_DIM // 4
    inv = ROPE_THETA ** (-jnp.arange(pairs, dtype=F32) / pairs)
    ang = jnp.concatenate([row[:, None] * inv, col[:, None] * inv], axis=-1)
    cos, sin = jnp.cos(ang), jnp.sin(ang)
    cos_h = jnp.concatenate([cos, cos], axis=-1)
    sin_h = jnp.concatenate([-sin, sin], axis=-1)
    return jnp.tile(cos_h, (1, HEADS_PER_BLOCK)), jnp.tile(sin_h, (1, HEADS_PER_BLOCK))


def _block_diag2(top, bottom):
    z_t = jnp.zeros_like(top)
    z_b = jnp.zeros_like(bottom)
    return jnp.concatenate([jnp.concatenate([top, z_t], axis=1), jnp.concatenate([z_b, bottom], axis=1)], axis=0)


def kernel(x, c, ctx, c_ctx, l0_norm1, l0_norm2, l0_ada_w, l0_ada_b, l0_w_in, l0_shift_mu, l0_w0_f, l0_w0_b, l0_ww2_f, l0_ww2_b, l0_a0_f, l0_a0_b, l0_wa2_f, l0_wa2_b, l0_wg2, l0_k_k, l0_k_a, l0_r_k, l0_lnx_g, l0_lnx_b, l0_q_norm, l0_k_norm, l0_w_out, l0_mlp_w1, l0_mlp_w2, l1_norm1, l1_norm2, l1_ada_w, l1_ada_b, l1_w_qkv, l1_rpb, l1_w_out, l1_mlp_w1, l1_mlp_w2, final_norm):
    b, t, d = x.shape
    n_ctx = ctx.shape[1]
    dim = l0_w0_f.shape[0]
    q_width = l0_w_out.shape[0] - dim
    gqa_cols = l0_w_in.shape[1] - l0_shift_mu.shape[0]
    kv_width = (gqa_cols - q_width) // 2
    rw_cols = l0_shift_mu.shape[0]
    rw_pad = 3 * dim + 4 * LANES
    assert rw_cols <= rw_pad and 3 * dim + 2 * LANES == rw_cols - l0_wg2.shape[0]
    tm = min(512, t)
    tm_c = n_ctx

    cc = jnp.zeros((8, d), F32).at[:b].set(c).at[b].set(c_ctx)

    def modulation(ada_w, ada_b):
        mod = _ada_mod(cc, ada_w, ada_b)
        lat = mod[:b].reshape(b, 6, 1, d)
        cx = jnp.broadcast_to(mod[b].reshape(1, 6, 1, d), (b, 6, 1, d))
        return [lat[:, i] for i in range(6)], [cx[:, i] for i in range(6)]

    mod_l, mod_c = modulation(l0_ada_w, l0_ada_b)

    w_gqa = l0_w_in[:, :gqa_cols].astype(MXU)
    w_rw = jnp.pad(l0_w_in[:, gqa_cols:], ((0, 0), (0, rw_pad - rw_cols))).astype(MXU)
    mu = jnp.pad(l0_shift_mu, (0, rw_pad - rw_cols)).reshape(1, rw_pad)
    lw_w = _block_diag2(l0_ww2_f, l0_ww2_b).astype(MXU)
    la_w = _block_diag2(l0_wa2_f, l0_wa2_b).astype(MXU)
    lg_w = jnp.pad(l0_wg2, ((0, 2 * LANES - l0_wg2.shape[0]), (0, 0))).astype(MXU)
    w0 = jnp.concatenate([l0_w0_f, l0_w0_b]).reshape(1, 2 * dim)
    a0 = jnp.concatenate([l0_a0_f, l0_a0_b]).reshape(1, 2 * dim)
    k_k = l0_k_k.reshape(1, dim)
    k_a = l0_k_a.reshape(1, dim)
    r_k = l0_r_k.reshape(1, dim)
    qg = jnp.tile(l0_q_norm, HEADS_PER_BLOCK).reshape(1, LANES)
    kg = jnp.tile(l0_k_norm, HEADS_PER_BLOCK).reshape(1, LANES)
    cos_l, sin_l = _rope_tables(t)
    cos_c, sin_c = jnp.ones((n_ctx, LANES), F32), jnp.zeros((n_ctx, LANES), F32)
    w_out_gqa = l0_w_out[:q_width].astype(MXU)
    w_out_rw = l0_w_out[q_width:].astype(MXU)

    def half_layer0(xs, mod, tm_, cos, sin, is_ctx):
        pg = _inproj(xs, l0_norm1, mod[0], mod[1], w_gqa, tm_, F32)
        pr = _inproj(xs, l0_norm1, mod[0], mod[1], w_rw, tm_, F32)
        q, kd, vd = _gqa_prep(pg, cos, sin, qg, kg, q_width, kv_width, tm_)
        prep = _rwkv_prep(pr, mu, lw_w, la_w, lg_w, w0, a0, k_k, k_a, r_k, dim, 256, is_ctx)
        return q, kd, vd, prep

    q_c, kd_c, vd_c, prep_c = half_layer0(ctx, mod_c, tm_c, cos_c, sin_c, True)
    q_l, kd_l, vd_l, prep_l = half_layer0(x, mod_l, tm, cos_l, sin_l, False)

    o_gqa_l = _gqa_attn(q_l, [(kd_l, vd_l), (kd_c, vd_c)], 256)
    o_gqa_c = _gqa_attn(q_c, [(kd_c, vd_c)], n_ctx)

    h0 = jnp.zeros((2, b, dim // LANES, LANES, LANES), F32)
    y_c, h_c = _rwkv_scan(*prep_c[:6], h0)
    y_l, _ = _rwkv_scan(*prep_l[:6], h_c)
    o_rw_l = _rwkv_finish(y_l, prep_l[6], prep_l[7], l0_lnx_g, l0_lnx_b, tm)
    o_rw_c = _rwkv_finish(y_c, prep_c[6], prep_c[7], l0_lnx_g, l0_lnx_b, tm_c)

    w1 = l0_mlp_w1.astype(MXU)
    w2 = l0_mlp_w2.astype(MXU)
    x = _outproj(x, mod_l[2], [(o_gqa_l, w_out_gqa), (o_rw_l, w_out_rw)], tm)
    x = _mlp(x, l0_norm2, mod_l[3], mod_l[4], mod_l[5], w1, w2, final_norm, tm, False)
    ctx = _outproj(ctx, mod_c[2], [(o_gqa_c, w_out_gqa), (o_rw_c, w_out_rw)], tm_c)
    ctx = _mlp(ctx, l0_norm2, mod_c[3], mod_c[4], mod_c[5], w1, w2, final_norm, tm_c, False)

    mod_l, mod_c = modulation(l1_ada_w, l1_ada_b)
    width = l1_w_out.shape[0]
    n_heads = width // HEAD_DIM
    scale = jnp.concatenate([jnp.full((width,), ATTN_SCALE, F32), jnp.ones((2 * width,), F32)])
    w_qkv = (l1_w_qkv * scale).astype(MXU)
    qkv = _inproj(x, l1_norm1, mod_l[0], mod_l[1], w_qkv, tm, MXU)
    kvc = _inproj(ctx, l1_norm1, mod_c[0], mod_c[1], w_qkv[:, width:], tm_c, MXU)
    o_na = _na_attn(qkv, kvc, _na_col_table(l1_rpb, n_heads), width)
    x = _outproj(x, mod_l[2], [(o_na, l1_w_out.astype(MXU))], tm)
    x = _mlp(x, l1_norm2, mod_l[3], mod_l[4], mod_l[5], l1_mlp_w1.astype(MXU), l1_mlp_w2.astype(MXU),
             final_norm, tm, True)
    return x
```

```python
import functools

import jax
import jax.numpy as jnp
import numpy as np
from jax import lax
from jax.experimental import pallas as pl
from jax.experimental.pallas import tpu as pltpu

F32 = jnp.float32
MXU = jnp.bfloat16

LANES = 128
HEAD_DIM = 64
HEADS_PER_BLOCK = LANES // HEAD_DIM
GRID_W = 64
NORM_EPS = 1e-6
LNX_EPS = 64e-5
ROPE_THETA = 10000.0
NA_ROWS = 8
NA_COLS = 16
NA_GROUP = 4
NA_WIN = NA_GROUP + NA_ROWS
NA_STEP_BLOCKS = 2
MASK_VALUE = -1e30
LOG2E = float(np.log2(np.e))
ATTN_SCALE = HEAD_DIM ** -0.5 * LOG2E
ATTN_KEY_CHUNK = 1024
INPROJ_MAX_COLS = 2048
MLP_MAX_HIDDEN = 1024
SCAN_CHUNK = 64
SCAN_PAIRS = 8
SCAN_STEP_CHUNKS = 2
VMEM_LIMIT = 56 * 1024 * 1024


def _cparams(sem):
    return pltpu.CompilerParams(dimension_semantics=sem, vmem_limit_bytes=VMEM_LIMIT)


def _mm(a, b):
    return jnp.dot(a.astype(MXU), b.astype(MXU), preferred_element_type=F32)


def _mm_nt(a, b):
    return lax.dot_general(a.astype(MXU), b.astype(MXU), (((1,), (1,)), ((), ())),
                           preferred_element_type=F32)


def _mm_tn(a, b):
    return lax.dot_general(a.astype(MXU), b.astype(MXU), (((0,), (0,)), ((), ())),
                           preferred_element_type=F32)


def _split3(x):
    hi = x.astype(MXU)
    r1 = x - hi.astype(F32)
    mid = r1.astype(MXU)
    lo = (r1 - mid.astype(F32)).astype(MXU)
    return hi, mid, lo


def _mm_exact_lhs(m, x):
    hi, mid, lo = _split3(x)
    mb = m.astype(MXU)
    return (jnp.dot(mb, hi, preferred_element_type=F32) + jnp.dot(mb, mid, preferred_element_type=F32)
            + jnp.dot(mb, lo, preferred_element_type=F32))


def _head_ones():
    r = lax.broadcasted_iota(jnp.int32, (LANES, LANES), 0) // HEAD_DIM
    c = lax.broadcasted_iota(jnp.int32, (LANES, LANES), 1) // HEAD_DIM
    return jnp.where(r == c, 1.0, 0.0).astype(MXU)


def _headsum(x, ones):
    hi, mid, lo = _split3(x)
    return (jnp.dot(hi, ones, preferred_element_type=F32) + jnp.dot(mid, ones, preferred_element_type=F32)
            + jnp.dot(lo, ones, preferred_element_type=F32))


def _lane(shape):
    return lax.broadcasted_iota(jnp.int32, shape, len(shape) - 1)


def _sigmoid(x):
    return 1.0 / (1.0 + jnp.exp(-x))


def _ada_kernel(c_ref, w_ref, b_ref, o_ref):
    c = c_ref[...]
    a = c * _sigmoid(c)
    o_ref[...] = _mm(a, w_ref[...]) + b_ref[...]


def _ada_mod(cc, w, bias):
    d, n = w.shape
    tn = 1024
    return pl.pallas_call(
        _ada_kernel,
        out_shape=jax.ShapeDtypeStruct((cc.shape[0], n), F32),
        grid=(n // tn,),
        in_specs=[pl.BlockSpec((cc.shape[0], d), lambda j: (0, 0)),
                  pl.BlockSpec((d, tn), lambda j: (0, j)),
                  pl.BlockSpec((1, tn), lambda j: (0, j))],
        out_specs=pl.BlockSpec((cc.shape[0], tn), lambda j: (0, j)),
        compiler_params=_cparams(("arbitrary",)),
        name="ada_mod",
    )(cc, w, bias.reshape(1, n))


def _norm_mod(x, g, shift, scale):
    y = x * lax.rsqrt(jnp.mean(x * x, axis=-1, keepdims=True) + NORM_EPS)
    return (y * g) * (1.0 + scale) + shift


def _inproj_kernel(x_ref, g_ref, sh_ref, sc_ref, w_ref, o_ref, a_scr):
    @pl.when(pl.program_id(2) == 0)
    def _():
        a_scr[...] = _norm_mod(x_ref[0], g_ref[...], sh_ref[0], sc_ref[0]).astype(a_scr.dtype)

    o_ref[0] = jnp.dot(a_scr[...], w_ref[...], preferred_element_type=F32).astype(o_ref.dtype)


def _col_tile(n, cap):
    return max(tn for tn in range(LANES, cap + 1, LANES) if n % tn == 0)


def _inproj(x, g, shift, scale, w, tm, out_dtype):
    b, t, d = x.shape
    n = w.shape[1]
    tn = _col_tile(n, INPROJ_MAX_COLS)
    return pl.pallas_call(
        _inproj_kernel,
        out_shape=jax.ShapeDtypeStruct((b, t, n), out_dtype),
        grid=(b, t // tm, n // tn),
        in_specs=[pl.BlockSpec((1, tm, d), lambda bi, i, j: (bi, i, 0)),
                  pl.BlockSpec((1, d), lambda bi, i, j: (0, 0)),
                  pl.BlockSpec((1, 1, d), lambda bi, i, j: (bi, 0, 0)),
                  pl.BlockSpec((1, 1, d), lambda bi, i, j: (bi, 0, 0)),
                  pl.BlockSpec((d, tn), lambda bi, i, j: (0, j))],
        out_specs=pl.BlockSpec((1, tm, tn), lambda bi, i, j: (bi, i, j)),
        scratch_shapes=[pltpu.VMEM((tm, d), MXU)],
        compiler_params=_cparams(("parallel", "parallel", "arbitrary")),
        name="inproj",
    )(x, g.reshape(1, d), shift, scale, w)


def _swap_half_heads(x):
    first = (_lane(x.shape) % HEAD_DIM) < HEAD_DIM // 2
    return jnp.where(first, pltpu.roll(x, LANES - HEAD_DIM // 2, 1), pltpu.roll(x, HEAD_DIM // 2, 1))


def _gqa_prep_kernel(q_ref, kv_ref, cos_ref, sin_ref, qg_ref, kg_ref, qo_ref, ko_ref, vo_ref, *, n_q, n_kv):
    ones = _head_ones()
    cos = cos_ref[...]
    sin = sin_ref[...]
    first_head = _lane(cos.shape) < HEAD_DIM

    def norm_rope(x, g):
        ms = _headsum(x * x, ones) * (1.0 / HEAD_DIM)
        y = x * lax.rsqrt(ms + NORM_EPS) * g
        return y * cos + _swap_half_heads(y) * sin

    def dup(x, half):
        rolled = pltpu.roll(x, HEAD_DIM, 1)
        return jnp.where(first_head, x, rolled) if half == 0 else jnp.where(first_head, rolled, x)

    for j in range(n_q):
        x = q_ref[0, :, j * LANES:(j + 1) * LANES]
        qo_ref[0, :, j * LANES:(j + 1) * LANES] = (norm_rope(x, qg_ref[...]) * ATTN_SCALE).astype(qo_ref.dtype)
    for j in range(n_kv):
        k = norm_rope(kv_ref[0, :, j * LANES:(j + 1) * LANES], kg_ref[...])
        v = kv_ref[0, :, (n_kv + j) * LANES:(n_kv + j + 1) * LANES]
        for half in range(HEADS_PER_BLOCK):
            ko_ref[0, HEADS_PER_BLOCK * j + half] = dup(k, half).astype(ko_ref.dtype)
            vo_ref[0, HEADS_PER_BLOCK * j + half] = dup(v, half).astype(vo_ref.dtype)


def _gqa_prep(p, cos, sin, qg, kg, q_width, kv_width, tm):
    b, t, _ = p.shape
    n_q = q_width // LANES
    n_kv = kv_width // LANES
    kvh = kv_width // HEAD_DIM
    kern = functools.partial(_gqa_prep_kernel, n_q=n_q, n_kv=n_kv)
    kv_spec = pl.BlockSpec((1, kvh, tm, LANES), lambda bi, i: (bi, 0, i, 0))
    return pl.pallas_call(
        kern,
        out_shape=(jax.ShapeDtypeStruct((b, t, q_width), MXU),
                   jax.ShapeDtypeStruct((b, kvh, t, LANES), MXU),
                   jax.ShapeDtypeStruct((b, kvh, t, LANES), MXU)),
        grid=(b, t // tm),
        in_specs=[pl.BlockSpec((1, tm, q_width), lambda bi, i: (bi, i, 0)),
                  pl.BlockSpec((1, tm, 2 * kv_width), lambda bi, i: (bi, i, q_width // (2 * kv_width))),
                  pl.BlockSpec((tm, LANES), lambda bi, i: (i, 0)),
                  pl.BlockSpec((tm, LANES), lambda bi, i: (i, 0)),
                  pl.BlockSpec((1, LANES), lambda bi, i: (0, 0)),
                  pl.BlockSpec((1, LANES), lambda bi, i: (0, 0))],
        out_specs=(pl.BlockSpec((1, tm, q_width), lambda bi, i: (bi, i, 0)), kv_spec, kv_spec),
        compiler_params=_cparams(("parallel", "parallel")),
        name="gqa_prep",
    )(p, p, cos, sin, qg, kg)


def _softmax_pv(q_blocks, chunks):
    first_head = _lane(q_blocks[0].shape) < HEAD_DIM
    zero = jnp.zeros_like(q_blocks[0])
    n_heads = HEADS_PER_BLOCK * len(q_blocks)
    s = [[None] * len(chunks) for _ in range(n_heads)]
    m = [None] * n_heads
    l = [None] * n_heads
    acc = [None] * n_heads
    for stage in range(n_heads + 1):
        ha, hb = stage, stage - 1
        if ha < n_heads:
            q = q_blocks[ha // HEADS_PER_BLOCK]
            qa = jnp.where(first_head, q, zero) if ha % HEADS_PER_BLOCK == 0 else jnp.where(first_head, zero, q)
        for c, (load_k, load_v, load_bias) in enumerate(chunks):
            if ha < n_heads:
                sc = _mm_nt(qa, load_k(ha // HEADS_PER_BLOCK))
                if load_bias is not None:
                    sc = sc + load_bias(ha)
                s[ha][c] = sc
                mc = jnp.max(sc, axis=-1, keepdims=True)
                m[ha] = mc if m[ha] is None else jnp.maximum(m[ha], mc)
            if hb >= 0:
                p = jnp.exp2(s[hb][c] - m[hb])
                s[hb][c] = None
                lc = jnp.sum(p, axis=-1, keepdims=True)
                pv = _mm(p, load_v(hb // HEADS_PER_BLOCK))
                l[hb] = lc if l[hb] is None else l[hb] + lc
                acc[hb] = pv if acc[hb] is None else acc[hb] + pv
    return [jnp.where(first_head, acc[2 * j] / l[2 * j], acc[2 * j + 1] / l[2 * j + 1])
            for j in range(len(q_blocks))]


def _key_chunks(k_ref, v_ref, idx, n_keys, chunk):
    out = []
    for start in range(0, n_keys, chunk):
        size = min(chunk, n_keys - start)
        out.append((lambda blk, s=start, z=size: k_ref[idx + (pl.ds(s, z), slice(None))],
                    lambda blk, s=start, z=size: v_ref[idx + (pl.ds(s, z), slice(None))], None))
    return out


def _gqa_attn_kernel(q_ref, *refs, n_sets):
    o_ref = refs[-1]
    chunks = []
    for i in range(n_sets):
        chunks += _key_chunks(refs[2 * i], refs[2 * i + 1], (0, 0), refs[2 * i].shape[2], ATTN_KEY_CHUNK)
    n_blocks = q_ref.shape[2] // LANES
    outs = _softmax_pv([q_ref[0, :, j * LANES:(j + 1) * LANES] for j in range(n_blocks)], chunks)
    for j, o in enumerate(outs):
        o_ref[0, :, j * LANES:(j + 1) * LANES] = o.astype(o_ref.dtype)


def _gqa_attn(q, kv_sets, tq):
    b, t, width = q.shape
    n_blocks = width // LANES
    kvh = kv_sets[0][0].shape[1]
    group_width = (n_blocks // kvh) * LANES
    in_specs = [pl.BlockSpec((1, tq, group_width), lambda bi, g, i: (bi, i, g))]
    args = [q]
    for k, v in kv_sets:
        s = k.shape[2]
        spec = pl.BlockSpec((1, 1, s, LANES), lambda bi, g, i: (bi, g, 0, 0))
        in_specs += [spec, spec]
        args += [k, v]
    return pl.pallas_call(
        functools.partial(_gqa_attn_kernel, n_sets=len(kv_sets)),
        out_shape=jax.ShapeDtypeStruct((b, t, width), MXU),
        grid=(b, kvh, t // tq),
        in_specs=in_specs,
        out_specs=pl.BlockSpec((1, tq, group_width), lambda bi, g, i: (bi, i, g)),
        compiler_params=_cparams(("parallel", "parallel", "arbitrary")),
        name="gqa_attn",
    )(*args)


def _rwkv_prep_kernel(p_ref, up_ref, dn_ref, mu_ref, lw_w_ref, la_w_ref, lg_w_ref, w0_ref, a0_ref, kk_ref, ka_ref,
                      rk_ref, r_o, v_o, kk_o, lw_o, kd_o, bb_o, bonus_o, gate_o, xm_scr, *, is_ctx, dim):
    tm = p_ref.shape[1]
    n_all = p_ref.shape[2] // LANES
    n_dim = dim // LANES
    i = pl.program_id(1)
    n_i = pl.num_programs(1)
    row = lax.broadcasted_iota(jnp.int32, (tm, LANES), 0)
    cls = _lane((tm, LANES)) % 4
    if is_ctx:
        first = row == 0
        last = row == tm - 1
    else:
        first = (row % GRID_W) == 0
        last = (row % GRID_W) == GRID_W - 1
        up_ok = jnp.logical_or(row >= GRID_W, i > 0)
        dn_ok = jnp.logical_or(row < tm - GRID_W, i < n_i - 1)

    for j in range(n_all):
        sl = slice(j * LANES, (j + 1) * LANES)
        p = p_ref[0, :, sl]
        prev = jnp.where(first, 0.0, pltpu.roll(p, 1, 0))
        nxt = jnp.where(last, 0.0, pltpu.roll(p, tm - 1, 0))
        if is_ctx:
            sh = jnp.where(cls % 2 == 0, prev, nxt)
        else:
            up = jnp.concatenate([up_ref[0, :, sl], p[:tm - GRID_W]], axis=0)
            dn = jnp.concatenate([p[GRID_W:], dn_ref[0, :, sl]], axis=0)
            up = jnp.where(up_ok, up, 0.0)
            dn = jnp.where(dn_ok, dn, 0.0)
            sh = jnp.where(cls == 0, prev, jnp.where(cls == 1, nxt, jnp.where(cls == 2, up, dn)))
        xm_scr[:, sl] = p + mu_ref[:, sl] * (sh - p)

    ones = _head_ones()
    x_w = jnp.tanh(xm_scr[:, 3 * dim:3 * dim + LANES]).astype(MXU)
    x_a = xm_scr[:, 3 * dim + LANES:3 * dim + 2 * LANES].astype(MXU)
    x_g = _sigmoid(xm_scr[:, 3 * dim + 2 * LANES:3 * dim + 4 * LANES]).astype(MXU)
    for j in range(n_dim):
        sl = slice(j * LANES, (j + 1) * LANES)
        r = xm_scr[:, sl]
        k = xm_scr[:, dim + j * LANES:dim + (j + 1) * LANES]
        v = xm_scr[:, 2 * dim + j * LANES:2 * dim + (j + 1) * LANES]
        kkr = k * kk_ref[:, sl]
        kk = kkr * lax.rsqrt(jnp.maximum(_headsum(kkr * kkr, ones), 1e-12))
        r_o[0, :, sl] = r
        v_o[0, :, sl] = v
        kk_o[0, :, sl] = kk
        gate_o[0, :, sl] = jnp.dot(x_g, lg_w_ref[:, sl], preferred_element_type=F32)
        bonus = jnp.zeros_like(r)
        for d in range(2):
            dsl = slice(d * dim + j * LANES, d * dim + (j + 1) * LANES)
            z = w0_ref[:, dsl] + jnp.dot(x_w, lw_w_ref[:, dsl], preferred_element_type=F32)
            softplus = jnp.maximum(-z, 0.0) + jnp.log(1.0 + jnp.exp(-jnp.abs(z)))
            lw_o[d, 0, :, sl] = -jnp.exp(-softplus - 0.5)
            iclr = _sigmoid(a0_ref[:, dsl] + jnp.dot(x_a, la_w_ref[:, dsl], preferred_element_type=F32))
            kd = k * (1.0 + (iclr - 1.0) * ka_ref[:, sl])
            kd_o[d, 0, :, sl] = kd
            bb_o[d, 0, :, sl] = kk * iclr
            bonus = bonus + _headsum(r * kd * rk_ref[:, sl], ones) * v
        bonus_o[0, :, sl] = bonus


def _rwkv_prep(p, mu, lw_w, la_w, lg_w, w0, a0, k_k, k_a, r_k, dim, tm, is_ctx):
    b, t, width = p.shape
    if is_ctx:
        assert t == tm
    else:
        assert tm % GRID_W == 0 and t % tm == 0
    hb = tm // GRID_W
    n_halo = t // GRID_W
    kern = functools.partial(_rwkv_prep_kernel, is_ctx=is_ctx, dim=dim)
    vec = lambda n: pl.BlockSpec((1, n), lambda bi, i: (0, 0))
    mat = lambda k, n: pl.BlockSpec((k, n), lambda bi, i: (0, 0))
    one = jax.ShapeDtypeStruct((b, t, dim), F32)
    two = jax.ShapeDtypeStruct((2, b, t, dim), F32)
    one_spec = pl.BlockSpec((1, tm, dim), lambda bi, i: (bi, i, 0))
    two_spec = pl.BlockSpec((2, 1, tm, dim), lambda bi, i: (0, bi, i, 0))
    return pl.pallas_call(
        kern,
        out_shape=(one, one, one, two, two, two, one, one),
        grid=(b, t // tm),
        in_specs=[pl.BlockSpec((1, tm, width), lambda bi, i: (bi, i, 0)),
                  pl.BlockSpec((1, GRID_W, width), lambda bi, i: (bi, jnp.maximum(i * hb - 1, 0), 0)),
                  pl.BlockSpec((1, GRID_W, width), lambda bi, i: (bi, jnp.minimum((i + 1) * hb, n_halo - 1), 0)),
                  vec(width), mat(LANES, 2 * dim), mat(LANES, 2 * dim), mat(2 * LANES, dim),
                  vec(2 * dim), vec(2 * dim), vec(dim), vec(dim), vec(dim)],
        out_specs=(one_spec, one_spec, one_spec, two_spec, two_spec, two_spec, one_spec, one_spec),
        scratch_shapes=[pltpu.VMEM((tm, width), F32)],
        compiler_params=_cparams(("parallel", "parallel")),
        name="rwkv_prep_ctx" if is_ctx else "rwkv_prep",
    )(p, p, p, mu, lw_w, la_w, lg_w, w0, a0, k_k, k_a, r_k)


def _scan_chunk_terms(r, v, kk, lw, kd, bb, sgn):
    blocks = range(len(r))
    c = r[0].shape[0]
    n = 2 * c
    ri = lax.broadcasted_iota(jnp.int32, (n, n), 0)
    ci = lax.broadcasted_iota(jnp.int32, (n, n), 1)
    same = (ri // c) == (ci // c)
    dt = ((ri % c) - (ci % c)) * sgn
    strict = jnp.logical_and(same, dt > 0)
    incl = jnp.logical_and(same, dt >= 0)
    eye = ri == ci
    head_lanes = (ri // c) == (ci // HEAD_DIM)
    zero = jnp.zeros((n, n), F32)
    ti = lax.broadcasted_iota(jnp.int32, (c, c), 0)
    tj = lax.broadcasted_iota(jnp.int32, (c, c), 1)
    tri = jnp.where((ti - tj) * sgn >= 0, 1.0, 0.0)

    def stack(x):
        return jnp.concatenate([x, x], axis=0)

    def own(x):
        return jnp.where(head_lanes, stack(x), zero)

    cum = [_mm_exact_lhs(tri, lw[p]) for p in blocks]
    tot = [jnp.sum(lw[p], axis=0, keepdims=True) for p in blocks]
    e_inv = [jnp.exp(-cum[p]) for p in blocks]
    e_fin = [jnp.exp(tot[p] - cum[p]) for p in blocks]
    at = [own(-kk[p] * jnp.exp(cum[p] - lw[p])) for p in blocks]
    rt = [own(r[p] * jnp.exp(cum[p])) for p in blocks]
    bt = [stack(bb[p] * e_inv[p]) for p in blocks]
    kt = [stack(kd[p] * e_inv[p]) for p in blocks]
    bhat = [own(bb[p] * e_fin[p]) for p in blocks]
    khat = [own(kd[p] * e_fin[p]) for p in blocks]
    vbd = [own(v[p]) for p in blocks]

    full = [_mm_nt(jnp.concatenate([at[p], rt[p]], axis=0), jnp.concatenate([bt[p], kt[p]], axis=0)) for p in blocks]
    a_ab = [jnp.where(strict, full[p][:n, :n], zero) for p in blocks]
    a_ak = [jnp.where(strict, full[p][:n, n:], zero) for p in blocks]
    a_rb = [jnp.where(incl, full[p][n:, :n], zero) for p in blocks]
    a_rk = [jnp.where(incl, full[p][n:, n:], zero) for p in blocks]

    ident = jnp.where(eye, 1.0, 0.0)
    inv = [ident + a_ab[p] for p in blocks]
    power = a_ab
    for _ in range(int(np.log2(c)) - 1):
        power = [_mm(power[p], power[p]) for p in blocks]
        inv = [inv[p] + _mm(inv[p], power[p]) for p in blocks]

    xv = [_mm(jnp.concatenate([a_ak[p], a_rk[p]], axis=0), vbd[p]) for p in blocks]
    w12 = [_mm(inv[p], jnp.concatenate([at[p], xv[p][:n]], axis=1)) for p in blocks]
    yw = [_mm(a_rb[p], w12[p]) for p in blocks]
    mn = [_mm_tn(bhat[p], w12[p]) for p in blocks]
    py = [rt[p] + yw[p][:, :n] for p in blocks]
    yl = [yw[p][:, n:] + xv[p][n:] for p in blocks]
    m = [jnp.where(eye, jnp.exp(tot[p]), zero) + mn[p][:, :n] for p in blocks]
    nn = [mn[p][:, n:] + _mm_tn(khat[p], vbd[p]) for p in blocks]
    return py, yl, m, nn


def _scan_kernel(r_ref, v_ref, kk_ref, lw_ref, kd_ref, bb_ref, h0_ref, y_ref, ht_ref, h_scr, *, pairs, chunks):
    d = pl.program_id(0)
    ci = pl.program_id(2)
    sgn = 1 - 2 * d
    c = SCAN_CHUNK
    n = 2 * c

    @pl.when(ci == 0)
    def _():
        h_scr[...] = h0_ref[0, 0]

    rows = [pl.ds(pl.multiple_of((d * (chunks - 1) + sgn * s) * c, c), c) for s in range(chunks)]
    lanes = [slice(j * LANES, (j + 1) * LANES) for j in range(pairs)]
    probs = [(rw, ln) for rw in rows for ln in lanes]
    py, yl, m, nn = _scan_chunk_terms(
        [r_ref[0, rw, ln] for rw, ln in probs], [v_ref[0, rw, ln] for rw, ln in probs],
        [kk_ref[0, rw, ln] for rw, ln in probs], [lw_ref[0, 0, rw, ln] for rw, ln in probs],
        [kd_ref[0, 0, rw, ln] for rw, ln in probs], [bb_ref[0, 0, rw, ln] for rw, ln in probs], sgn)

    h = [h_scr[j] for j in range(pairs)]
    for s in range(chunks):
        idx = [s * pairs + j for j in range(pairs)]
        yh = [_mm(jnp.concatenate([py[i], m[i]], axis=0), h[j]) for j, i in enumerate(idx)]
        for j, i in enumerate(idx):
            y2 = yh[j][:n] + yl[i]
            y_ref[0, 0, rows[s], lanes[j]] = y2[:c] + y2[c:]
        h = [yh[j][n:] + nn[i] for j, i in enumerate(idx)]
    for j in range(pairs):
        h_scr[j] = h[j]

    @pl.when(ci == pl.num_programs(2) - 1)
    def _():
        ht_ref[0, 0] = h_scr[...]


def _rwkv_scan(r, v, kk, lw, kd, bb, h0):
    b, t, dim = r.shape
    c = SCAN_CHUNK * SCAN_STEP_CHUNKS
    assert t % c == 0
    nch = t // c
    width = SCAN_PAIRS * LANES
    nblk = dim // width

    def chunk(d, ci):
        return ci + d * (nch - 1 - 2 * ci)

    one = pl.BlockSpec((1, c, width), lambda d, g, ci: (g // nblk, chunk(d, ci), g % nblk))
    two = pl.BlockSpec((1, 1, c, width), lambda d, g, ci: (d, g // nblk, chunk(d, ci), g % nblk))
    st = pl.BlockSpec((1, 1, SCAN_PAIRS, LANES, LANES), lambda d, g, ci: (d, g // nblk, g % nblk, 0, 0))
    return pl.pallas_call(
        functools.partial(_scan_kernel, pairs=SCAN_PAIRS, chunks=SCAN_STEP_CHUNKS),
        out_shape=(jax.ShapeDtypeStruct((2, b, t, dim), F32), jax.ShapeDtypeStruct(h0.shape, F32)),
        grid=(2, b * nblk, nch),
        in_specs=[one, one, one, two, two, two, st],
        out_specs=(two, st),
        scratch_shapes=[pltpu.VMEM((SCAN_PAIRS, LANES, LANES), F32)],
        compiler_params=_cparams(("parallel", "parallel", "arbitrary")),
        name="rwkv_scan",
    )(r, v, kk, lw, kd, bb, h0)


def _rwkv_finish_kernel(y_ref, bonus_ref, gate_ref, g_ref, b_ref, o_ref):
    ones = _head_ones()
    for j in range(o_ref.shape[2] // LANES):
        sl = slice(j * LANES, (j + 1) * LANES)
        y = y_ref[0, 0, :, sl] + y_ref[1, 0, :, sl]
        mu = _headsum(y, ones) * (1.0 / HEAD_DIM)
        dlt = y - mu
        var = _headsum(dlt * dlt, ones) * (1.0 / HEAD_DIM)
        yn = dlt * lax.rsqrt(var + LNX_EPS)
        o_ref[0, :, sl] = ((yn * g_ref[:, sl] + b_ref[:, sl] + bonus_ref[0, :, sl]) * gate_ref[0, :, sl]).astype(o_ref.dtype)


def _rwkv_finish(y, bonus, gate, lnx_g, lnx_b, tm):
    _, b, t, dim = y.shape
    one_spec = pl.BlockSpec((1, tm, dim), lambda bi, i: (bi, i, 0))
    vec = pl.BlockSpec((1, dim), lambda bi, i: (0, 0))
    return pl.pallas_call(
        _rwkv_finish_kernel,
        out_shape=jax.ShapeDtypeStruct((b, t, dim), MXU),
        grid=(b, t // tm),
        in_specs=[pl.BlockSpec((2, 1, tm, dim), lambda bi, i: (0, bi, i, 0)), one_spec, one_spec, vec, vec],
        out_specs=one_spec,
        compiler_params=_cparams(("parallel", "parallel")),
        name="rwkv_finish",
    )(y, bonus, gate, lnx_g.reshape(1, dim), lnx_b.reshape(1, dim))


def _outproj_kernel(*refs, n_in):
    x_ref, gate_ref = refs[0], refs[1]
    o_ref = refs[-1]
    acc = None
    for i in range(n_in):
        part = jnp.dot(refs[2 + 2 * i][0], refs[3 + 2 * i][...], preferred_element_type=F32)
        acc = part if acc is None else acc + part
    o_ref[0] = x_ref[0] + gate_ref[0] * acc


def _outproj(x, gate, parts, tm):
    b, t, d = x.shape
    in_specs = [pl.BlockSpec((1, tm, d), lambda bi, i: (bi, i, 0)),
                pl.BlockSpec((1, 1, d), lambda bi, i: (bi, 0, 0))]
    args = [x, gate]
    for a, w in parts:
        k = w.shape[0]
        in_specs += [pl.BlockSpec((1, tm, k), lambda bi, i: (bi, i, 0)), pl.BlockSpec((k, d), lambda bi, i: (0, 0))]
        args += [a, w]
    return pl.pallas_call(
        functools.partial(_outproj_kernel, n_in=len(parts)),
        out_shape=jax.ShapeDtypeStruct((b, t, d), F32),
        grid=(b, t // tm),
        in_specs=in_specs,
        out_specs=pl.BlockSpec((1, tm, d), lambda bi, i: (bi, i, 0)),
        compiler_params=_cparams(("parallel", "parallel")),
        name="outproj",
    )(*args)


def _mlp_kernel(x_ref, g_ref, sh_ref, sc_ref, gate_ref, w1_ref, w2_ref, fg_ref, o_ref, a_scr, acc_scr, *, final_norm):
    f = pl.program_id(2)

    @pl.when(f == 0)
    def _():
        a_scr[...] = _norm_mod(x_ref[0], g_ref[...], sh_ref[0], sc_ref[0]).astype(a_scr.dtype)
        acc_scr[...] = jnp.zeros_like(acc_scr)

    h = jnp.dot(a_scr[...], w1_ref[...], preferred_element_type=F32)
    h = jnp.square(jnp.maximum(h, 0.0))
    acc_scr[...] += jnp.dot(h.astype(MXU), w2_ref[...], preferred_element_type=F32)

    @pl.when(f == pl.num_programs(2) - 1)
    def _():
        y = x_ref[0] + gate_ref[0] * acc_scr[...]
        if final_norm:
            y = y * lax.rsqrt(jnp.mean(y * y, axis=-1, keepdims=True) + NORM_EPS) * fg_ref[...]
        o_ref[0] = y


def _mlp(x, g, shift, scale, gate, w1, w2, final_g, tm, final_norm):
    b, t, d = x.shape
    ff = w1.shape[1]
    tf = _col_tile(ff, MLP_MAX_HIDDEN)
    vec3 = pl.BlockSpec((1, 1, d), lambda bi, i, f: (bi, 0, 0))
    vec2 = pl.BlockSpec((1, d), lambda bi, i, f: (0, 0))
    return pl.pallas_call(
        functools.partial(_mlp_kernel, final_norm=final_norm),
        out_shape=jax.ShapeDtypeStruct((b, t, d), F32),
        grid=(b, t // tm, ff // tf),
        in_specs=[pl.BlockSpec((1, tm, d), lambda bi, i, f: (bi, i, 0)), vec2, vec3, vec3, vec3,
                  pl.BlockSpec((d, tf), lambda bi, i, f: (0, f)),
                  pl.BlockSpec((tf, d), lambda bi, i, f: (f, 0)), vec2],
        out_specs=pl.BlockSpec((1, tm, d), lambda bi, i, f: (bi, i, 0)),
        scratch_shapes=[pltpu.VMEM((tm, d), MXU), pltpu.VMEM((tm, d), F32)],
        compiler_params=_cparams(("parallel", "parallel", "arbitrary")),
        name="mlp",
    )(x, g.reshape(1, d), shift, scale, gate, w1, w2, final_g.reshape(1, d))


NA_GROUP_OFFSETS = (0, NA_ROWS // 2, NA_ROWS)
NA_STRIP_PAD = NA_WIN - NA_ROWS
NA_STRIP_BLOCKS = 2 * NA_WIN


def _na_window_lo(typ, local_row):
    return (0, local_row, NA_WIN - NA_ROWS)[typ]


def _na_kernel(q_ref, k_ref, v_ref, kc_ref, vc_ref, tc_ref, o_ref, bias_scr, *, n_win_rows):
    g = pl.program_id(2)
    n_g = pl.num_programs(2)
    n_keys = n_win_rows * GRID_W

    n_blocks = q_ref.shape[2] // LANES
    tq = NA_GROUP * GRID_W

    @pl.when(jnp.logical_and(pl.program_id(1) == 0, g == 0))
    def _():
        key_row = _lane((GRID_W, n_keys)) // GRID_W
        for blk in range(n_blocks):
            for typ in range(3):
                for half in range(HEADS_PER_BLOCK):
                    for lr in range(NA_GROUP):
                        rho = NA_GROUP_OFFSETS[typ] + lr
                        off = (NA_WIN - 1 - rho) * GRID_W
                        strip = tc_ref[blk, half, :, off:off + n_keys]
                        lo = _na_window_lo(typ, lr)
                        ok = jnp.logical_and(key_row >= lo, key_row < lo + NA_ROWS)
                        r0 = (half * NA_GROUP + lr) * GRID_W
                        bias_scr[blk, typ, r0:r0 + GRID_W, :] = jnp.where(ok, strip, MASK_VALUE)

    win_row = jnp.clip(g * NA_GROUP - NA_ROWS // 2, 0, n_g * NA_GROUP - n_win_rows)
    typ = jnp.where(g == 0, 0, jnp.where(g == n_g - 1, 2, 1))
    start = pl.multiple_of(win_row * GRID_W, GRID_W)
    lanes = lambda blk: slice(blk * LANES, (blk + 1) * LANES)
    chunks = [(lambda blk: k_ref[0, pl.ds(start, n_keys), lanes(blk)],
               lambda blk: v_ref[0, pl.ds(start, n_keys), lanes(blk)],
               lambda h: bias_scr[h // HEADS_PER_BLOCK, typ, pl.ds((h % HEADS_PER_BLOCK) * tq, tq), :]),
              (lambda blk: kc_ref[0, :, lanes(blk)], lambda blk: vc_ref[0, :, lanes(blk)], None)]
    outs = _softmax_pv([q_ref[0, :, lanes(blk)] for blk in range(n_blocks)], chunks)
    for blk, o in enumerate(outs):
        o_ref[0, :, lanes(blk)] = o.astype(o_ref.dtype)


def _na_col_table(rpb, n_heads):
    col = np.arange(GRID_W)
    c0 = np.clip(col - NA_COLS // 2, 0, GRID_W - NA_COLS)
    col_ok = (col[None, :] >= c0[:, None]) & (col[None, :] < c0[:, None] + NA_COLS)
    dcol = col[None, :] - col[:, None] + NA_COLS - 1
    onehot = (dcol[None] == np.arange(2 * NA_COLS - 1)[:, None, None]) & col_ok[None]
    tc = jnp.einsum("hrd,dqk->hqrk", rpb * LOG2E, jnp.asarray(onehot, F32), precision=lax.Precision.HIGHEST)
    tc = jnp.where(col_ok[None, :, None, :], tc, MASK_VALUE)
    n_dr = 2 * NA_ROWS - 1
    tc = jnp.pad(tc, ((0, 0), (0, 0), (NA_STRIP_PAD, NA_STRIP_BLOCKS - NA_STRIP_PAD - n_dr), (0, 0)),
                 constant_values=MASK_VALUE)
    return tc.reshape(n_heads // 2, HEADS_PER_BLOCK, GRID_W, NA_STRIP_BLOCKS * GRID_W)


def _na_attn(qkv, kvc, col_table, width):
    b, t, _ = qkv.shape
    ctx = kvc.shape[1]
    bw = NA_STEP_BLOCKS * LANES
    nb = width // bw
    tq = NA_GROUP * GRID_W
    rows = t // GRID_W
    assert rows % NA_GROUP == 0 and rows >= NA_WIN and width % bw == 0
    return pl.pallas_call(
        functools.partial(_na_kernel, n_win_rows=NA_WIN),
        out_shape=jax.ShapeDtypeStruct((b, t, width), MXU),
        grid=(nb, b, rows // NA_GROUP),
        in_specs=[pl.BlockSpec((1, tq, bw), lambda p, bi, g: (bi, g, p)),
                  pl.BlockSpec((1, t, bw), lambda p, bi, g: (bi, 0, nb + p)),
                  pl.BlockSpec((1, t, bw), lambda p, bi, g: (bi, 0, 2 * nb + p)),
                  pl.BlockSpec((1, ctx, bw), lambda p, bi, g: (bi, 0, p)),
                  pl.BlockSpec((1, ctx, bw), lambda p, bi, g: (bi, 0, nb + p)),
                  pl.BlockSpec((NA_STEP_BLOCKS, HEADS_PER_BLOCK, GRID_W, NA_STRIP_BLOCKS * GRID_W),
                               lambda p, bi, g: (p, 0, 0, 0))],
        out_specs=pl.BlockSpec((1, tq, bw), lambda p, bi, g: (bi, g, p)),
        scratch_shapes=[pltpu.VMEM((NA_STEP_BLOCKS, 3, HEADS_PER_BLOCK * tq, NA_WIN * GRID_W), F32)],
        compiler_params=_cparams(("arbitrary", "arbitrary", "arbitrary")),
        name="na_attn",
    )(qkv, qkv, qkv, kvc, kvc, col_table)


def _rope_tables(t):
    pos = jnp.arange(t, dtype=jnp.int32)
    row = (pos // GRID_W).astype(F32)
    col = (pos % GRID_W).astype(F32)
    pairs = HEAD_DIM // 4
    inv = ROPE_THETA ** (-jnp.arange(pairs, dtype=F32) / pairs)
    ang = jnp.concatenate([row[:, None] * inv, col[:, None] * inv], axis=-1)
    cos, sin = jnp.cos(ang), jnp.sin(ang)
    cos_h = jnp.concatenate([cos, cos], axis=-1)
    sin_h = jnp.concatenate([-sin, sin], axis=-1)
    return jnp.tile(cos_h, (1, HEADS_PER_BLOCK)), jnp.tile(sin_h, (1, HEADS_PER_BLOCK))


def _block_diag2(top, bottom):
    z_t = jnp.zeros_like(top)
    z_b = jnp.zeros_like(bottom)
    return jnp.concatenate([jnp.concatenate([top, z_t], axis=1), jnp.concatenate([z_b, bottom], axis=1)], axis=0)


def kernel(x, c, ctx, c_ctx, l0_norm1, l0_norm2, l0_ada_w, l0_ada_b, l0_w_in, l0_shift_mu, l0_w0_f, l0_w0_b, l0_ww2_f, l0_ww2_b, l0_a0_f, l0_a0_b, l0_wa2_f, l0_wa2_b, l0_wg2, l0_k_k, l0_k_a, l0_r_k, l0_lnx_g, l0_lnx_b, l0_q_norm, l0_k_norm, l0_w_out, l0_mlp_w1, l0_mlp_w2, l1_norm1, l1_norm2, l1_ada_w, l1_ada_b, l1_w_qkv, l1_rpb, l1_w_out, l1_mlp_w1, l1_mlp_w2, final_norm):
    b, t, d = x.shape
    n_ctx = ctx.shape[1]
    dim = l0_w0_f.shape[0]
    q_width = l0_w_out.shape[0] - dim
    gqa_cols = l0_w_in.shape[1] - l0_shift_mu.shape[0]
    kv_width = (gqa_cols - q_width) // 2
    rw_cols = l0_shift_mu.shape[0]
    rw_pad = 3 * dim + 4 * LANES
    assert rw_cols <= rw_pad and 3 * dim + 2 * LANES == rw_cols - l0_wg2.shape[0]
    tm = min(512, t)
    tm_c = n_ctx

    cc = jnp.zeros((8, d), F32).at[:b].set(c).at[b].set(c_ctx)

    def modulation(ada_w, ada_b):
        mod = _ada_mod(cc, ada_w, ada_b)
        lat = mod[:b].reshape(b, 6, 1, d)
        cx = jnp.broadcast_to(mod[b].reshape(1, 6, 1, d), (b, 6, 1, d))
        return [lat[:, i] for i in range(6)], [cx[:, i] for i in range(6)]

    mod_l, mod_c = modulation(l0_ada_w, l0_ada_b)

    w_gqa = l0_w_in[:, :gqa_cols].astype(MXU)
    w_rw = jnp.pad(l0_w_in[:, gqa_cols:], ((0, 0), (0, rw_pad - rw_cols))).astype(MXU)
    mu = jnp.pad(l0_shift_mu, (0, rw_pad - rw_cols)).reshape(1, rw_pad)
    lw_w = _block_diag2(l0_ww2_f, l0_ww2_b).astype(MXU)
    la_w = _block_diag2(l0_wa2_f, l0_wa2_b).astype(MXU)
    lg_w = jnp.pad(l0_wg2, ((0, 2 * LANES - l0_wg2.shape[0]), (0, 0))).astype(MXU)
    w0 = jnp.concatenate([l0_w0_f, l0_w0_b]).reshape(1, 2 * dim)
    a0 = jnp.concatenate([l0_a0_f, l0_a0_b]).reshape(1, 2 * dim)
    k_k = l0_k_k.reshape(1, dim)
    k_a = l0_k_a.reshape(1, dim)
    r_k = l0_r_k.reshape(1, dim)
    qg = jnp.tile(l0_q_norm, HEADS_PER_BLOCK).reshape(1, LANES)
    kg = jnp.tile(l0_k_norm, HEADS_PER_BLOCK).reshape(1, LANES)
    cos_l, sin_l = _rope_tables(t)
    cos_c, sin_c = jnp.ones((n_ctx, LANES), F32), jnp.zeros((n_ctx, LANES), F32)
    w_out_gqa = l0_w_out[:q_width].astype(MXU)
    w_out_rw = l0_w_out[q_width:].astype(MXU)

    def half_layer0(xs, mod, tm_, cos, sin, is_ctx):
        pg = _inproj(xs, l0_norm1, mod[0], mod[1], w_gqa, tm_, F32)
        pr = _inproj(xs, l0_norm1, mod[0], mod[1], w_rw, tm_, F32)
        q, kd, vd = _gqa_prep(pg, cos, sin, qg, kg, q_width, kv_width, tm_)
        prep = _rwkv_prep(pr, mu, lw_w, la_w, lg_w, w0, a0, k_k, k_a, r_k, dim, 256, is_ctx)
        return q, kd, vd, prep

    q_c, kd_c, vd_c, prep_c = half_layer0(ctx, mod_c, tm_c, cos_c, sin_c, True)
    q_l, kd_l, vd_l, prep_l = half_layer0(x, mod_l, tm, cos_l, sin_l, False)

    o_gqa_l = _gqa_attn(q_l, [(kd_l, vd_l), (kd_c, vd_c)], 256)
    o_gqa_c = _gqa_attn(q_c, [(kd_c, vd_c)], n_ctx)

    h0 = jnp.zeros((2, b, dim // LANES, LANES, LANES), F32)
    y_c, h_c = _rwkv_scan(*prep_c[:6], h0)
    y_l, _ = _rwkv_scan(*prep_l[:6], h_c)
    o_rw_l = _rwkv_finish(y_l, prep_l[6], prep_l[7], l0_lnx_g, l0_lnx_b, tm)
    o_rw_c = _rwkv_finish(y_c, prep_c[6], prep_c[7], l0_lnx_g, l0_lnx_b, tm_c)

    w1 = l0_mlp_w1.astype(MXU)
    w2 = l0_mlp_w2.astype(MXU)
    x = _outproj(x, mod_l[2], [(o_gqa_l, w_out_gqa), (o_rw_l, w_out_rw)], tm)
    x = _mlp(x, l0_norm2, mod_l[3], mod_l[4], mod_l[5], w1, w2, final_norm, tm, False)
    ctx = _outproj(ctx, mod_c[2], [(o_gqa_c, w_out_gqa), (o_rw_c, w_out_rw)], tm_c)
    ctx = _mlp(ctx, l0_norm2, mod_c[3], mod_c[4], mod_c[5], w1, w2, final_norm, tm_c, False)

    mod_l, mod_c = modulation(l1_ada_w, l1_ada_b)
    width = l1_w_out.shape[0]
    n_heads = width // HEAD_DIM
    scale = jnp.concatenate([jnp.full((width,), ATTN_SCALE, F32), jnp.ones((2 * width,), F32)])
    w_qkv = (l1_w_qkv * scale).astype(MXU)
    qkv = _inproj(x, l1_norm1, mod_l[0], mod_l[1], w_qkv, tm, MXU)
    kvc = _inproj(ctx, l1_norm1, mod_c[0], mod_c[1], w_qkv[:, width:], tm_c, MXU)
    o_na = _na_attn(qkv, kvc, _na_col_table(l1_rpb, n_heads), width)
    x = _outproj(x, mod_l[2], [(o_na, l1_w_out.astype(MXU))], tm)
    x = _mlp(x, l1_norm2, mod_l[3], mod_l[4], mod_l[5], l1_mlp_w1.astype(MXU), l1_mlp_w2.astype(MXU),
             final_norm, tm, True)
    return x
```

```python
import functools

import jax
import jax.numpy as jnp
import numpy as np
from jax import lax
from jax.experimental import pallas as pl
from jax.experimental.pallas import tpu as pltpu

F32 = jnp.float32
MXU = jnp.bfloat16

LANES = 128
HEAD_DIM = 64
HEADS_PER_BLOCK = LANES // HEAD_DIM
GRID_W = 64
NORM_EPS = 1e-6
LNX_EPS = 64e-5
ROPE_THETA = 10000.0
NA_ROWS = 8
NA_COLS = 16
NA_GROUP = 4
NA_WIN = NA_GROUP + NA_ROWS
NA_STEP_BLOCKS = 4
GQA_STEP_KV_HEADS = 1
GQA_QUERY_TILE = 512
MASK_VALUE = -1e30
LOG2E = float(np.log2(np.e))
ATTN_SCALE = HEAD_DIM ** -0.5 * LOG2E
ATTN_KEY_CHUNK = 1024
INPROJ_MAX_COLS = 2048
INPROJ_RESIDENT_BYTES = 16 * 1024 * 1024
INPROJ_VMEM_BUDGET = 46 * 1024 * 1024
MLP_MAX_HIDDEN = 1024
SCAN_CHUNK = 64
SCAN_PAIRS = 8
SCAN_STEP_CHUNKS = 2
VMEM_LIMIT = 56 * 1024 * 1024


def _cparams(sem):
    return pltpu.CompilerParams(dimension_semantics=sem, vmem_limit_bytes=VMEM_LIMIT)


def _mm(a, b):
    return jnp.dot(a.astype(MXU), b.astype(MXU), preferred_element_type=F32)


def _mm_nt(a, b):
    return lax.dot_general(a.astype(MXU), b.astype(MXU), (((1,), (1,)), ((), ())),
                           preferred_element_type=F32)


def _mm_tn(a, b):
    return lax.dot_general(a.astype(MXU), b.astype(MXU), (((0,), (0,)), ((), ())),
                           preferred_element_type=F32)


def _split3(x):
    hi = x.astype(MXU)
    r1 = x - hi.astype(F32)
    mid = r1.astype(MXU)
    lo = (r1 - mid.astype(F32)).astype(MXU)
    return hi, mid, lo


def _head_ones():
    r = lax.broadcasted_iota(jnp.int32, (LANES, LANES), 0) // HEAD_DIM
    c = lax.broadcasted_iota(jnp.int32, (LANES, LANES), 1) // HEAD_DIM
    return jnp.where(r == c, 1.0, 0.0).astype(MXU)


def _headsum(x, ones):
    hi, mid, lo = _split3(x)
    return (jnp.dot(hi, ones, preferred_element_type=F32) + jnp.dot(mid, ones, preferred_element_type=F32)
            + jnp.dot(lo, ones, preferred_element_type=F32))


def _lane(shape):
    return lax.broadcasted_iota(jnp.int32, shape, len(shape) - 1)


def _sigmoid(x):
    return 1.0 / (1.0 + jnp.exp(-x))


def _ada_kernel(c_ref, w_ref, b_ref, o_ref):
    c = c_ref[...]
    a = c * _sigmoid(c)
    o_ref[...] = _mm(a, w_ref[...]) + b_ref[...]


def _ada_mod(cc, w, bias):
    d, n = w.shape
    tn = 1024
    return pl.pallas_call(
        _ada_kernel,
        out_shape=jax.ShapeDtypeStruct((cc.shape[0], n), F32),
        grid=(n // tn,),
        in_specs=[pl.BlockSpec((cc.shape[0], d), lambda j: (0, 0)),
                  pl.BlockSpec((d, tn), lambda j: (0, j)),
                  pl.BlockSpec((1, tn), lambda j: (0, j))],
        out_specs=pl.BlockSpec((cc.shape[0], tn), lambda j: (0, j)),
        compiler_params=_cparams(("arbitrary",)),
        name="ada_mod",
    )(cc, w, bias.reshape(1, n))


def _norm_mod(x, g, shift, scale):
    y = x * lax.rsqrt(jnp.mean(x * x, axis=-1, keepdims=True) + NORM_EPS)
    return (y * g) * (1.0 + scale) + shift


def _inproj_kernel(x_ref, g_ref, sh_ref, sc_ref, w_ref, o_ref, a_scr):
    @pl.when(pl.program_id(2) == 0)
    def _():
        a_scr[...] = _norm_mod(x_ref[0], g_ref[...], sh_ref[0], sc_ref[0]).astype(a_scr.dtype)

    o_ref[0] = jnp.dot(a_scr[...], w_ref[...], preferred_element_type=F32).astype(o_ref.dtype)


def _col_tile(n, cap):
    return max(tn for tn in range(LANES, cap + 1, LANES) if n % tn == 0)


def _inproj(x, g, shift, scale, w, tm, out_dtype):
    b, t, d = x.shape
    n = w.shape[1]
    tn = n if d * n * w.dtype.itemsize <= INPROJ_RESIDENT_BYTES else _col_tile(n, INPROJ_MAX_COLS)
    out_bytes = jnp.dtype(out_dtype).itemsize

    def vmem(tm_):
        return 2 * (tm_ * d * 4 + d * tn * w.dtype.itemsize + tm_ * tn * out_bytes) + tm_ * d * w.dtype.itemsize

    while vmem(tm) > INPROJ_VMEM_BUDGET and tm % 16 == 0:
        tm //= 2
    return pl.pallas_call(
        _inproj_kernel,
        out_shape=jax.ShapeDtypeStruct((b, t, n), out_dtype),
        grid=(b, t // tm, n // tn),
        in_specs=[pl.BlockSpec((1, tm, d), lambda bi, i, j: (bi, i, 0)),
                  pl.BlockSpec((1, d), lambda bi, i, j: (0, 0)),
                  pl.BlockSpec((1, 1, d), lambda bi, i, j: (bi, 0, 0)),
                  pl.BlockSpec((1, 1, d), lambda bi, i, j: (bi, 0, 0)),
                  pl.BlockSpec((d, tn), lambda bi, i, j: (0, j))],
        out_specs=pl.BlockSpec((1, tm, tn), lambda bi, i, j: (bi, i, j)),
        scratch_shapes=[pltpu.VMEM((tm, d), MXU)],
        compiler_params=_cparams(("parallel", "parallel", "arbitrary")),
        name="inproj",
    )(x, g.reshape(1, d), shift, scale, w)


def _swap_half_heads(x):
    first = (_lane(x.shape) % HEAD_DIM) < HEAD_DIM // 2
    return jnp.where(first, pltpu.roll(x, LANES - HEAD_DIM // 2, 1), pltpu.roll(x, HEAD_DIM // 2, 1))


def _gqa_prep_kernel(q_ref, kv_ref, cos_ref, sin_ref, qg_ref, kg_ref, qo_ref, ko_ref, vo_ref, *, n_q, n_kv):
    ones = _head_ones()
    cos = cos_ref[...]
    sin = sin_ref[...]
    first_head = _lane(cos.shape) < HEAD_DIM

    def norm_rope(x, g):
        ms = _headsum(x * x, ones) * (1.0 / HEAD_DIM)
        y = x * lax.rsqrt(ms + NORM_EPS) * g
        return y * cos + _swap_half_heads(y) * sin

    def dup(x, half):
        rolled = pltpu.roll(x, HEAD_DIM, 1)
        return jnp.where(first_head, x, rolled) if half == 0 else jnp.where(first_head, rolled, x)

    for j in range(n_q):
        x = q_ref[0, :, j * LANES:(j + 1) * LANES]
        qo_ref[0, :, j * LANES:(j + 1) * LANES] = (norm_rope(x, qg_ref[...]) * ATTN_SCALE).astype(qo_ref.dtype)
    for j in range(n_kv):
        k = norm_rope(kv_ref[0, :, j * LANES:(j + 1) * LANES], kg_ref[...])
        v = kv_ref[0, :, (n_kv + j) * LANES:(n_kv + j + 1) * LANES]
        for half in range(HEADS_PER_BLOCK):
            ko_ref[0, HEADS_PER_BLOCK * j + half] = dup(k, half).astype(ko_ref.dtype)
            vo_ref[0, HEADS_PER_BLOCK * j + half] = dup(v, half).astype(vo_ref.dtype)


def _gqa_prep(p, cos, sin, qg, kg, q_width, kv_width, tm):
    b, t, _ = p.shape
    n_q = q_width // LANES
    n_kv = kv_width // LANES
    kvh = kv_width // HEAD_DIM
    kern = functools.partial(_gqa_prep_kernel, n_q=n_q, n_kv=n_kv)
    kv_spec = pl.BlockSpec((1, kvh, tm, LANES), lambda bi, i: (bi, 0, i, 0))
    return pl.pallas_call(
        kern,
        out_shape=(jax.ShapeDtypeStruct((b, t, q_width), MXU),
                   jax.ShapeDtypeStruct((b, kvh, t, LANES), MXU),
                   jax.ShapeDtypeStruct((b, kvh, t, LANES), MXU)),
        grid=(b, t // tm),
        in_specs=[pl.BlockSpec((1, tm, q_width), lambda bi, i: (bi, i, 0)),
                  pl.BlockSpec((1, tm, 2 * kv_width), lambda bi, i: (bi, i, q_width // (2 * kv_width))),
                  pl.BlockSpec((tm, LANES), lambda bi, i: (i, 0)),
                  pl.BlockSpec((tm, LANES), lambda bi, i: (i, 0)),
                  pl.BlockSpec((1, LANES), lambda bi, i: (0, 0)),
                  pl.BlockSpec((1, LANES), lambda bi, i: (0, 0))],
        out_specs=(pl.BlockSpec((1, tm, q_width), lambda bi, i: (bi, i, 0)), kv_spec, kv_spec),
        compiler_params=_cparams(("parallel", "parallel")),
        name="gqa_prep",
    )(p, p, cos, sin, qg, kg)


def _softmax_pv(q_blocks, chunks):
    first_head = _lane(q_blocks[0].shape) < HEAD_DIM
    zero = jnp.zeros_like(q_blocks[0])
    n_heads = HEADS_PER_BLOCK * len(q_blocks)
    s = [[None] * len(chunks) for _ in range(n_heads)]
    m = [None] * n_heads
    l = [None] * n_heads
    acc = [None] * n_heads
    for stage in range(n_heads + 1):
        ha, hb = stage, stage - 1
        if ha < n_heads:
            q = q_blocks[ha // HEADS_PER_BLOCK]
            qa = jnp.where(first_head, q, zero) if ha % HEADS_PER_BLOCK == 0 else jnp.where(first_head, zero, q)
        for c, (load_k, load_v, load_bias) in enumerate(chunks):
            if ha < n_heads:
                sc = _mm_nt(qa, load_k(ha // HEADS_PER_BLOCK))
                if load_bias is not None:
                    sc = sc + load_bias(ha)
                s[ha][c] = sc
                mc = jnp.max(sc, axis=-1, keepdims=True)
                m[ha] = mc if m[ha] is None else jnp.maximum(m[ha], mc)
            if hb >= 0:
                p = jnp.exp2(s[hb][c] - m[hb])
                s[hb][c] = None
                lc = jnp.sum(p, axis=-1, keepdims=True)
                pv = _mm(p, load_v(hb // HEADS_PER_BLOCK))
                l[hb] = lc if l[hb] is None else l[hb] + lc
                acc[hb] = pv if acc[hb] is None else acc[hb] + pv
    return [jnp.where(first_head, acc[2 * j] / l[2 * j], acc[2 * j + 1] / l[2 * j + 1])
            for j in range(len(q_blocks))]


def _gqa_attn_kernel(q_ref, *refs, n_sets):
    o_ref = refs[-1]
    n_blocks = q_ref.shape[2] // LANES
    blocks_per_kv = n_blocks // refs[0].shape[1]
    chunks = []
    for i in range(n_sets):
        k_ref, v_ref = refs[2 * i], refs[2 * i + 1]
        n_keys = k_ref.shape[2]
        for start in range(0, n_keys, ATTN_KEY_CHUNK):
            size = min(ATTN_KEY_CHUNK, n_keys - start)
            chunks.append((lambda blk, r=k_ref, s=start, z=size: r[0, blk // blocks_per_kv, pl.ds(s, z), :],
                           lambda blk, r=v_ref, s=start, z=size: r[0, blk // blocks_per_kv, pl.ds(s, z), :], None))
    outs = _softmax_pv([q_ref[0, :, j * LANES:(j + 1) * LANES] for j in range(n_blocks)], chunks)
    for j, o in enumerate(outs):
        o_ref[0, :, j * LANES:(j + 1) * LANES] = o.astype(o_ref.dtype)


def _gqa_attn(q, kv_sets, tq):
    b, t, width = q.shape
    kvh = kv_sets[0][0].shape[1]
    kv_step = min(GQA_STEP_KV_HEADS, kvh)
    group_width = (width // kvh) * kv_step
    in_specs = [pl.BlockSpec((1, tq, group_width), lambda bi, g, i: (bi, i, g))]
    args = [q]
    for k, v in kv_sets:
        s = k.shape[2]
        spec = pl.BlockSpec((1, kv_step, s, LANES), lambda bi, g, i: (bi, g, 0, 0))
        in_specs += [spec, spec]
        args += [k, v]
    return pl.pallas_call(
        functools.partial(_gqa_attn_kernel, n_sets=len(kv_sets)),
        out_shape=jax.ShapeDtypeStruct((b, t, width), MXU),
        grid=(b, kvh // kv_step, t // tq),
        in_specs=in_specs,
        out_specs=pl.BlockSpec((1, tq, group_width), lambda bi, g, i: (bi, i, g)),
        compiler_params=_cparams(("parallel", "parallel", "arbitrary")),
        name="gqa_attn",
    )(*args)


def _rwkv_prep_kernel(p_ref, up_ref, dn_ref, mu_ref, lw_w_ref, la_w_ref, lg_w_ref, w0_ref, a0_ref, kk_ref, ka_ref,
                      rk_ref, r_o, v_o, kk_o, lw_o, kd_o, bb_o, bonus_o, gate_o, xm_scr, *, is_ctx, dim):
    tm = p_ref.shape[1]
    n_all = p_ref.shape[2] // LANES
    n_dim = dim // LANES
    i = pl.program_id(1)
    n_i = pl.num_programs(1)
    row = lax.broadcasted_iota(jnp.int32, (tm, LANES), 0)
    cls = _lane((tm, LANES)) % 4
    if is_ctx:
        first = row == 0
        last = row == tm - 1
    else:
        first = (row % GRID_W) == 0
        last = (row % GRID_W) == GRID_W - 1
        up_ok = jnp.logical_or(row >= GRID_W, i > 0)
        dn_ok = jnp.logical_or(row < tm - GRID_W, i < n_i - 1)

    for j in range(n_all):
        sl = slice(j * LANES, (j + 1) * LANES)
        p = p_ref[0, :, sl]
        prev = jnp.where(first, 0.0, pltpu.roll(p, 1, 0))
        nxt = jnp.where(last, 0.0, pltpu.roll(p, tm - 1, 0))
        if is_ctx:
            sh = jnp.where(cls % 2 == 0, prev, nxt)
        else:
            up = jnp.concatenate([up_ref[0, :, sl], p[:tm - GRID_W]], axis=0)
            dn = jnp.concatenate([p[GRID_W:], dn_ref[0, :, sl]], axis=0)
            up = jnp.where(up_ok, up, 0.0)
            dn = jnp.where(dn_ok, dn, 0.0)
            sh = jnp.where(cls == 0, prev, jnp.where(cls == 1, nxt, jnp.where(cls == 2, up, dn)))
        xm_scr[:, sl] = p + mu_ref[:, sl] * (sh - p)

    ones = _head_ones()
    x_w = jnp.tanh(xm_scr[:, 3 * dim:3 * dim + LANES]).astype(MXU)
    x_a = xm_scr[:, 3 * dim + LANES:3 * dim + 2 * LANES].astype(MXU)
    x_g = _sigmoid(xm_scr[:, 3 * dim + 2 * LANES:3 * dim + 4 * LANES]).astype(MXU)
    for j in range(n_dim):
        sl = slice(j * LANES, (j + 1) * LANES)
        r = xm_scr[:, sl]
        k = xm_scr[:, dim + j * LANES:dim + (j + 1) * LANES]
        v = xm_scr[:, 2 * dim + j * LANES:2 * dim + (j + 1) * LANES]
        kkr = k * kk_ref[:, sl]
        kk = kkr * lax.rsqrt(jnp.maximum(_headsum(kkr * kkr, ones), 1e-12))
        r_o[0, :, sl] = r
        v_o[0, :, sl] = v
        kk_o[0, :, sl] = kk
        gate_o[0, :, sl] = jnp.dot(x_g, lg_w_ref[:, sl], preferred_element_type=F32)
        bonus = jnp.zeros_like(r)
        for d in range(2):
            dsl = slice(d * dim + j * LANES, d * dim + (j + 1) * LANES)
            z = w0_ref[:, dsl] + jnp.dot(x_w, lw_w_ref[:, dsl], preferred_element_type=F32)
            softplus = jnp.maximum(-z, 0.0) + jnp.log(1.0 + jnp.exp(-jnp.abs(z)))
            lw_o[d, 0, :, sl] = -jnp.exp(-softplus - 0.5)
            iclr = _sigmoid(a0_ref[:, dsl] + jnp.dot(x_a, la_w_ref[:, dsl], preferred_element_type=F32))
            kd = k * (1.0 + (iclr - 1.0) * ka_ref[:, sl])
            kd_o[d, 0, :, sl] = kd
            bb_o[d, 0, :, sl] = kk * iclr
            bonus = bonus + _headsum(r * kd * rk_ref[:, sl], ones) * v
        bonus_o[0, :, sl] = bonus


def _rwkv_prep(p, mu, lw_w, la_w, lg_w, w0, a0, k_k, k_a, r_k, dim, tm, is_ctx):
    b, t, width = p.shape
    if is_ctx:
        assert t == tm
    else:
        assert tm % GRID_W == 0 and t % tm == 0
    hb = tm // GRID_W
    n_halo = t // GRID_W
    kern = functools.partial(_rwkv_prep_kernel, is_ctx=is_ctx, dim=dim)
    vec = lambda n: pl.BlockSpec((1, n), lambda bi, i: (0, 0))
    mat = lambda k, n: pl.BlockSpec((k, n), lambda bi, i: (0, 0))
    one = jax.ShapeDtypeStruct((b, t, dim), F32)
    two = jax.ShapeDtypeStruct((2, b, t, dim), F32)
    one_spec = pl.BlockSpec((1, tm, dim), lambda bi, i: (bi, i, 0))
    two_spec = pl.BlockSpec((2, 1, tm, dim), lambda bi, i: (0, bi, i, 0))
    return pl.pallas_call(
        kern,
        out_shape=(one, one, one, two, two, two, one, one),
        grid=(b, t // tm),
        in_specs=[pl.BlockSpec((1, tm, width), lambda bi, i: (bi, i, 0)),
                  pl.BlockSpec((1, GRID_W, width), lambda bi, i: (bi, jnp.maximum(i * hb - 1, 0), 0)),
                  pl.BlockSpec((1, GRID_W, width), lambda bi, i: (bi, jnp.minimum((i + 1) * hb, n_halo - 1), 0)),
                  vec(width), mat(LANES, 2 * dim), mat(LANES, 2 * dim), mat(2 * LANES, dim),
                  vec(2 * dim), vec(2 * dim), vec(dim), vec(dim), vec(dim)],
        out_specs=(one_spec, one_spec, one_spec, two_spec, two_spec, two_spec, one_spec, one_spec),
        scratch_shapes=[pltpu.VMEM((tm, width), F32)],
        compiler_params=_cparams(("parallel", "parallel")),
        name="rwkv_prep_ctx" if is_ctx else "rwkv_prep",
    )(p, p, p, mu, lw_w, la_w, lg_w, w0, a0, k_k, k_a, r_k)


def _scan_chunk_terms(r, v, kk, lw, kd, bb, sgn):
    blocks = range(len(r))
    c = r[0].shape[0]
    n = 2 * c
    ri = lax.broadcasted_iota(jnp.int32, (n, n), 0)
    ci = lax.broadcasted_iota(jnp.int32, (n, n), 1)
    same = (ri // c) == (ci // c)
    dt = ((ri % c) - (ci % c)) * sgn
    strict = jnp.logical_and(same, dt > 0)
    incl = jnp.logical_and(same, dt >= 0)
    eye = ri == ci
    head_lanes = (ri // c) == (ci // HEAD_DIM)
    zero = jnp.zeros((n, n), F32)
    time = lax.broadcasted_iota(jnp.int32, (c, LANES), 0)
    sgn_f = sgn.astype(F32)
    rev_f = 0.5 - 0.5 * sgn_f

    def stack(x):
        return jnp.concatenate([x, x], axis=0)

    def own(x):
        return jnp.where(head_lanes, stack(x), zero)

    def prefix(x):
        step = 1
        while step < c:
            x = x + jnp.where(time >= step, pltpu.roll(x, step, 0), 0.0)
            step *= 2
        return x

    tot = [jnp.sum(lw[p], axis=0, keepdims=True) for p in blocks]
    pre = [prefix(lw[p]) for p in blocks]
    cum = [rev_f * (tot[p] + lw[p]) + sgn_f * pre[p] for p in blocks]
    e_inv = [jnp.exp(-cum[p]) for p in blocks]
    e_fin = [jnp.exp(tot[p] - cum[p]) for p in blocks]
    at = [own(-kk[p] * jnp.exp(cum[p] - lw[p])) for p in blocks]
    rt = [own(r[p] * jnp.exp(cum[p])) for p in blocks]
    bt = [stack(bb[p] * e_inv[p]) for p in blocks]
    kt = [stack(kd[p] * e_inv[p]) for p in blocks]
    bhat = [own(bb[p] * e_fin[p]) for p in blocks]
    khat = [own(kd[p] * e_fin[p]) for p in blocks]
    vbd = [own(v[p]) for p in blocks]

    full = [_mm_nt(jnp.concatenate([at[p], rt[p]], axis=0), jnp.concatenate([bt[p], kt[p]], axis=0)) for p in blocks]
    a_ab = [jnp.where(strict, full[p][:n, :n], zero) for p in blocks]
    a_ak = [jnp.where(strict, full[p][:n, n:], zero) for p in blocks]
    a_rb = [jnp.where(incl, full[p][n:, :n], zero) for p in blocks]
    a_rk = [jnp.where(incl, full[p][n:, n:], zero) for p in blocks]

    ident = jnp.where(eye, 1.0, 0.0)
    inv = [ident + a_ab[p] for p in blocks]
    power = a_ab
    for _ in range(int(np.log2(c)) - 1):
        power = [_mm(power[p], power[p]) for p in blocks]
        inv = [inv[p] + _mm(inv[p], power[p]) for p in blocks]

    xv = [_mm(jnp.concatenate([a_ak[p], a_rk[p]], axis=0), vbd[p]) for p in blocks]
    w12 = [_mm(inv[p], jnp.concatenate([at[p], xv[p][:n]], axis=1)) for p in blocks]
    yw = [_mm(a_rb[p], w12[p]) for p in blocks]
    mn = [_mm_tn(bhat[p], w12[p]) for p in blocks]
    py = [rt[p] + yw[p][:, :n] for p in blocks]
    yl = [yw[p][:, n:] + xv[p][n:] for p in blocks]
    m = [jnp.where(eye, jnp.exp(tot[p]), zero) + mn[p][:, :n] for p in blocks]
    nn = [mn[p][:, n:] + _mm_tn(khat[p], vbd[p]) for p in blocks]
    return py, yl, m, nn


def _scan_kernel(r_ref, v_ref, kk_ref, lw_ref, kd_ref, bb_ref, h0_ref, y_ref, ht_ref, h_scr, *, pairs, chunks):
    d = pl.program_id(0)
    ci = pl.program_id(2)
    sgn = 1 - 2 * d
    c = SCAN_CHUNK
    n = 2 * c

    @pl.when(ci == 0)
    def _():
        h_scr[...] = h0_ref[0, 0]

    rows = [pl.ds(pl.multiple_of((d * (chunks - 1) + sgn * s) * c, c), c) for s in range(chunks)]
    lanes = [slice(j * LANES, (j + 1) * LANES) for j in range(pairs)]
    probs = [(rw, ln) for rw in rows for ln in lanes]
    py, yl, m, nn = _scan_chunk_terms(
        [r_ref[0, rw, ln] for rw, ln in probs], [v_ref[0, rw, ln] for rw, ln in probs],
        [kk_ref[0, rw, ln] for rw, ln in probs], [lw_ref[0, 0, rw, ln] for rw, ln in probs],
        [kd_ref[0, 0, rw, ln] for rw, ln in probs], [bb_ref[0, 0, rw, ln] for rw, ln in probs], sgn)

    h = [h_scr[j] for j in range(pairs)]
    for s in range(chunks):
        idx = [s * pairs + j for j in range(pairs)]
        yh = [_mm(jnp.concatenate([py[i], m[i]], axis=0), h[j]) for j, i in enumerate(idx)]
        for j, i in enumerate(idx):
            y2 = yh[j][:n] + yl[i]
            y_ref[0, 0, rows[s], lanes[j]] = y2[:c] + y2[c:]
        h = [yh[j][n:] + nn[i] for j, i in enumerate(idx)]
    for j in range(pairs):
        h_scr[j] = h[j]

    @pl.when(ci == pl.num_programs(2) - 1)
    def _():
        ht_ref[0, 0] = h_scr[...]


def _rwkv_scan(r, v, kk, lw, kd, bb, h0):
    b, t, dim = r.shape
    c = SCAN_CHUNK * SCAN_STEP_CHUNKS
    assert t % c == 0
    nch = t // c
    width = SCAN_PAIRS * LANES
    nblk = dim // width

    def chunk(d, ci):
        return ci + d * (nch - 1 - 2 * ci)

    one = pl.BlockSpec((1, c, width), lambda d, g, ci: (g // nblk, chunk(d, ci), g % nblk))
    two = pl.BlockSpec((1, 1, c, width), lambda d, g, ci: (d, g // nblk, chunk(d, ci), g % nblk))
    st = pl.BlockSpec((1, 1, SCAN_PAIRS, LANES, LANES), lambda d, g, ci: (d, g // nblk, g % nblk, 0, 0))
    return pl.pallas_call(
        functools.partial(_scan_kernel, pairs=SCAN_PAIRS, chunks=SCAN_STEP_CHUNKS),
        out_shape=(jax.ShapeDtypeStruct((2, b, t, dim), F32), jax.ShapeDtypeStruct(h0.shape, F32)),
        grid=(2, b * nblk, nch),
        in_specs=[one, one, one, two, two, two, st],
        out_specs=(two, st),
        scratch_shapes=[pltpu.VMEM((SCAN_PAIRS, LANES, LANES), F32)],
        compiler_params=_cparams(("parallel", "parallel", "arbitrary")),
        name="rwkv_scan",
    )(r, v, kk, lw, kd, bb, h0)


def _rwkv_finish_kernel(y_ref, bonus_ref, gate_ref, g_ref, b_ref, o_ref):
    ones = _head_ones()
    for j in range(o_ref.shape[2] // LANES):
        sl = slice(j * LANES, (j + 1) * LANES)
        y = y_ref[0, 0, :, sl] + y_ref[1, 0, :, sl]
        mu = _headsum(y, ones) * (1.0 / HEAD_DIM)
        dlt = y - mu
        var = _headsum(dlt * dlt, ones) * (1.0 / HEAD_DIM)
        yn = dlt * lax.rsqrt(var + LNX_EPS)
        o_ref[0, :, sl] = ((yn * g_ref[:, sl] + b_ref[:, sl] + bonus_ref[0, :, sl]) * gate_ref[0, :, sl]).astype(o_ref.dtype)


def _rwkv_finish(y, bonus, gate, lnx_g, lnx_b, tm):
    _, b, t, dim = y.shape
    one_spec = pl.BlockSpec((1, tm, dim), lambda bi, i: (bi, i, 0))
    vec = pl.BlockSpec((1, dim), lambda bi, i: (0, 0))
    return pl.pallas_call(
        _rwkv_finish_kernel,
        out_shape=jax.ShapeDtypeStruct((b, t, dim), MXU),
        grid=(b, t // tm),
        in_specs=[pl.BlockSpec((2, 1, tm, dim), lambda bi, i: (0, bi, i, 0)), one_spec, one_spec, vec, vec],
        out_specs=one_spec,
        compiler_params=_cparams(("parallel", "parallel")),
        name="rwkv_finish",
    )(y, bonus, gate, lnx_g.reshape(1, dim), lnx_b.reshape(1, dim))


def _outproj_kernel(*refs, n_in):
    x_ref, gate_ref = refs[0], refs[1]
    o_ref = refs[-1]
    acc = None
    for i in range(n_in):
        part = jnp.dot(refs[2 + 2 * i][0], refs[3 + 2 * i][...], preferred_element_type=F32)
        acc = part if acc is None else acc + part
    o_ref[0] = x_ref[0] + gate_ref[0] * acc


def _outproj(x, gate, parts, tm):
    b, t, d = x.shape
    in_specs = [pl.BlockSpec((1, tm, d), lambda bi, i: (bi, i, 0)),
                pl.BlockSpec((1, 1, d), lambda bi, i: (bi, 0, 0))]
    args = [x, gate]
    for a, w in parts:
        k = w.shape[0]
        in_specs += [pl.BlockSpec((1, tm, k), lambda bi, i: (bi, i, 0)), pl.BlockSpec((k, d), lambda bi, i: (0, 0))]
        args += [a, w]
    return pl.pallas_call(
        functools.partial(_outproj_kernel, n_in=len(parts)),
        out_shape=jax.ShapeDtypeStruct((b, t, d), F32),
        grid=(b, t // tm),
        in_specs=in_specs,
        out_specs=pl.BlockSpec((1, tm, d), lambda bi, i: (bi, i, 0)),
        compiler_params=_cparams(("parallel", "parallel")),
        name="outproj",
    )(*args)


def _mlp_kernel(x_ref, g_ref, sh_ref, sc_ref, gate_ref, w1_ref, w2_ref, fg_ref, o_ref, a_scr, acc_scr, *, final_norm):
    f = pl.program_id(2)

    @pl.when(f == 0)
    def _():
        a_scr[...] = _norm_mod(x_ref[0], g_ref[...], sh_ref[0], sc_ref[0]).astype(a_scr.dtype)
        acc_scr[...] = jnp.zeros_like(acc_scr)

    h = jnp.dot(a_scr[...], w1_ref[...], preferred_element_type=F32)
    h = jnp.square(jnp.maximum(h, 0.0))
    acc_scr[...] += jnp.dot(h.astype(MXU), w2_ref[...], preferred_element_type=F32)

    @pl.when(f == pl.num_programs(2) - 1)
    def _():
        y = x_ref[0] + gate_ref[0] * acc_scr[...]
        if final_norm:
            y = y * lax.rsqrt(jnp.mean(y * y, axis=-1, keepdims=True) + NORM_EPS) * fg_ref[...]
        o_ref[0] = y


def _mlp(x, g, shift, scale, gate, w1, w2, final_g, tm, final_norm):
    b, t, d = x.shape
    ff = w1.shape[1]
    tf = _col_tile(ff, MLP_MAX_HIDDEN)
    vec3 = pl.BlockSpec((1, 1, d), lambda bi, i, f: (bi, 0, 0))
    vec2 = pl.BlockSpec((1, d), lambda bi, i, f: (0, 0))
    return pl.pallas_call(
        functools.partial(_mlp_kernel, final_norm=final_norm),
        out_shape=jax.ShapeDtypeStruct((b, t, d), F32),
        grid=(b, t // tm, ff // tf),
        in_specs=[pl.BlockSpec((1, tm, d), lambda bi, i, f: (bi, i, 0)), vec2, vec3, vec3, vec3,
                  pl.BlockSpec((d, tf), lambda bi, i, f: (0, f)),
                  pl.BlockSpec((tf, d), lambda bi, i, f: (f, 0)), vec2],
        out_specs=pl.BlockSpec((1, tm, d), lambda bi, i, f: (bi, i, 0)),
        scratch_shapes=[pltpu.VMEM((tm, d), MXU), pltpu.VMEM((tm, d), F32)],
        compiler_params=_cparams(("parallel", "parallel", "arbitrary")),
        name="mlp",
    )(x, g.reshape(1, d), shift, scale, gate, w1, w2, final_g.reshape(1, d))


NA_GROUP_OFFSETS = (0, NA_ROWS // 2, NA_ROWS)
NA_STRIP_PAD = NA_WIN - NA_ROWS
NA_STRIP_BLOCKS = 2 * NA_WIN


def _na_window_lo(typ, local_row):
    return (0, local_row, NA_WIN - NA_ROWS)[typ]


def _na_kernel(q_ref, k_ref, v_ref, kc_ref, vc_ref, tc_ref, o_ref, bias_scr, *, n_win_rows):
    g = pl.program_id(2)
    n_g = pl.num_programs(2)
    n_keys = n_win_rows * GRID_W

    n_blocks = q_ref.shape[2] // LANES
    tq = NA_GROUP * GRID_W

    @pl.when(jnp.logical_and(pl.program_id(1) == 0, g == 0))
    def _():
        key_row = _lane((GRID_W, n_keys)) // GRID_W
        for blk in range(n_blocks):
            for typ in range(3):
                for half in range(HEADS_PER_BLOCK):
                    for lr in range(NA_GROUP):
                        rho = NA_GROUP_OFFSETS[typ] + lr
                        off = (NA_WIN - 1 - rho) * GRID_W
                        strip = tc_ref[blk, half, :, off:off + n_keys]
                        lo = _na_window_lo(typ, lr)
                        ok = jnp.logical_and(key_row >= lo, key_row < lo + NA_ROWS)
                        r0 = (half * NA_GROUP + lr) * GRID_W
                        bias_scr[blk, typ, r0:r0 + GRID_W, :] = jnp.where(ok, strip, MASK_VALUE)

    win_row = jnp.clip(g * NA_GROUP - NA_ROWS // 2, 0, n_g * NA_GROUP - n_win_rows)
    typ = jnp.where(g == 0, 0, jnp.where(g == n_g - 1, 2, 1))
    start = pl.multiple_of(win_row * GRID_W, GRID_W)
    lanes = lambda blk: slice(blk * LANES, (blk + 1) * LANES)
    chunks = [(lambda blk: k_ref[0, pl.ds(start, n_keys), lanes(blk)],
               lambda blk: v_ref[0, pl.ds(start, n_keys), lanes(blk)],
               lambda h: bias_scr[h // HEADS_PER_BLOCK, typ, pl.ds((h % HEADS_PER_BLOCK) * tq, tq), :]),
              (lambda blk: kc_ref[0, :, lanes(blk)], lambda blk: vc_ref[0, :, lanes(blk)], None)]
    outs = _softmax_pv([q_ref[0, :, lanes(blk)] for blk in range(n_blocks)], chunks)
    for blk, o in enumerate(outs):
        o_ref[0, :, lanes(blk)] = o.astype(o_ref.dtype)


def _na_col_table(rpb, n_heads):
    col = np.arange(GRID_W)
    c0 = np.clip(col - NA_COLS // 2, 0, GRID_W - NA_COLS)
    col_ok = (col[None, :] >= c0[:, None]) & (col[None, :] < c0[:, None] + NA_COLS)
    dcol = col[None, :] - col[:, None] + NA_COLS - 1
    onehot = (dcol[None] == np.arange(2 * NA_COLS - 1)[:, None, None]) & col_ok[None]
    tc = jnp.einsum("hrd,dqk->hqrk", rpb * LOG2E, jnp.asarray(onehot, F32), precision=lax.Precision.HIGHEST)
    tc = jnp.where(col_ok[None, :, None, :], tc, MASK_VALUE)
    n_dr = 2 * NA_ROWS - 1
    tc = jnp.pad(tc, ((0, 0), (0, 0), (NA_STRIP_PAD, NA_STRIP_BLOCKS - NA_STRIP_PAD - n_dr), (0, 0)),
                 constant_values=MASK_VALUE)
    return tc.reshape(n_heads // 2, HEADS_PER_BLOCK, GRID_W, NA_STRIP_BLOCKS * GRID_W)


def _na_attn(qkv, kvc, col_table, width):
    b, t, _ = qkv.shape
    ctx = kvc.shape[1]
    bw = NA_STEP_BLOCKS * LANES
    nb = width // bw
    tq = NA_GROUP * GRID_W
    rows = t // GRID_W
    assert rows % NA_GROUP == 0 and rows >= NA_WIN and width % bw == 0
    return pl.pallas_call(
        functools.partial(_na_kernel, n_win_rows=NA_WIN),
        out_shape=jax.ShapeDtypeStruct((b, t, width), MXU),
        grid=(nb, b, rows // NA_GROUP),
        in_specs=[pl.BlockSpec((1, tq, bw), lambda p, bi, g: (bi, g, p)),
                  pl.BlockSpec((1, t, bw), lambda p, bi, g: (bi, 0, nb + p)),
                  pl.BlockSpec((1, t, bw), lambda p, bi, g: (bi, 0, 2 * nb + p)),
                  pl.BlockSpec((1, ctx, bw), lambda p, bi, g: (bi, 0, p)),
                  pl.BlockSpec((1, ctx, bw), lambda p, bi, g: (bi, 0, nb + p)),
                  pl.BlockSpec((NA_STEP_BLOCKS, HEADS_PER_BLOCK, GRID_W, NA_STRIP_BLOCKS * GRID_W),
                               lambda p, bi, g: (p, 0, 0, 0))],
        out_specs=pl.BlockSpec((1, tq, bw), lambda p, bi, g: (bi, g, p)),
        scratch_shapes=[pltpu.VMEM((NA_STEP_BLOCKS, 3, HEADS_PER_BLOCK * tq, NA_WIN * GRID_W), F32)],
        compiler_params=_cparams(("arbitrary", "arbitrary", "arbitrary")),
        name="na_attn",
    )(qkv, qkv, qkv, kvc, kvc, col_table)


def _rope_tables(t):
    pos = jnp.arange(t, dtype=jnp.int32)
    row = (pos // GRID_W).astype(F32)
    col = (pos % GRID_W).astype(F32)
    pairs = HEAD_DIM // 4
    inv = ROPE_THETA ** (-jnp.arange(pairs, dtype=F32) / pairs)
    ang = jnp.concatenate([row[:, None] * inv, col[:, None] * inv], axis=-1)
    cos, sin = jnp.cos(ang), jnp.sin(ang)
    cos_h = jnp.concatenate([cos, cos], axis=-1)
    sin_h = jnp.concatenate([-sin, sin], axis=-1)
    return jnp.tile(cos_h, (1, HEADS_PER_BLOCK)), jnp.tile(sin_h, (1, HEADS_PER_BLOCK))


def _block_diag2(top, bottom):
    z_t = jnp.zeros_like(top)
    z_b = jnp.zeros_like(bottom)
    return jnp.concatenate([jnp.concatenate([top, z_t], axis=1), jnp.concatenate([z_b, bottom], axis=1)], axis=0)


def kernel(x, c, ctx, c_ctx, l0_norm1, l0_norm2, l0_ada_w, l0_ada_b, l0_w_in, l0_shift_mu, l0_w0_f, l0_w0_b, l0_ww2_f, l0_ww2_b, l0_a0_f, l0_a0_b, l0_wa2_f, l0_wa2_b, l0_wg2, l0_k_k, l0_k_a, l0_r_k, l0_lnx_g, l0_lnx_b, l0_q_norm, l0_k_norm, l0_w_out, l0_mlp_w1, l0_mlp_w2, l1_norm1, l1_norm2, l1_ada_w, l1_ada_b, l1_w_qkv, l1_rpb, l1_w_out, l1_mlp_w1, l1_mlp_w2, final_norm):
    b, t, d = x.shape
    n_ctx = ctx.shape[1]
    dim = l0_w0_f.shape[0]
    q_width = l0_w_out.shape[0] - dim
    gqa_cols = l0_w_in.shape[1] - l0_shift_mu.shape[0]
    kv_width = (gqa_cols - q_width) // 2
    rw_cols = l0_shift_mu.shape[0]
    rw_pad = 3 * dim + 4 * LANES
    assert rw_cols <= rw_pad and 3 * dim + 2 * LANES == rw_cols - l0_wg2.shape[0]
    tm = min(512, t)
    tm_c = n_ctx

    cc = jnp.zeros((8, d), F32).at[:b].set(c).at[b].set(c_ctx)

    def modulation(ada_w, ada_b):
        mod = _ada_mod(cc, ada_w, ada_b)
        lat = mod[:b].reshape(b, 6, 1, d)
        cx = jnp.broadcast_to(mod[b].reshape(1, 6, 1, d), (b, 6, 1, d))
        return [lat[:, i] for i in range(6)], [cx[:, i] for i in range(6)]

    mod_l, mod_c = modulation(l0_ada_w, l0_ada_b)

    w_gqa = l0_w_in[:, :gqa_cols].astype(MXU)
    w_rw = jnp.pad(l0_w_in[:, gqa_cols:], ((0, 0), (0, rw_pad - rw_cols))).astype(MXU)
    mu = jnp.pad(l0_shift_mu, (0, rw_pad - rw_cols)).reshape(1, rw_pad)
    lw_w = _block_diag2(l0_ww2_f, l0_ww2_b).astype(MXU)
    la_w = _block_diag2(l0_wa2_f, l0_wa2_b).astype(MXU)
    lg_w = jnp.pad(l0_wg2, ((0, 2 * LANES - l0_wg2.shape[0]), (0, 0))).astype(MXU)
    w0 = jnp.concatenate([l0_w0_f, l0_w0_b]).reshape(1, 2 * dim)
    a0 = jnp.concatenate([l0_a0_f, l0_a0_b]).reshape(1, 2 * dim)
    k_k = l0_k_k.reshape(1, dim)
    k_a = l0_k_a.reshape(1, dim)
    r_k = l0_r_k.reshape(1, dim)
    qg = jnp.tile(l0_q_norm, HEADS_PER_BLOCK).reshape(1, LANES)
    kg = jnp.tile(l0_k_norm, HEADS_PER_BLOCK).reshape(1, LANES)
    cos_l, sin_l = _rope_tables(t)
    cos_c, sin_c = jnp.ones((n_ctx, LANES), F32), jnp.zeros((n_ctx, LANES), F32)
    w_out_gqa = l0_w_out[:q_width].astype(MXU)
    w_out_rw = l0_w_out[q_width:].astype(MXU)

    def half_layer0(xs, mod, tm_, cos, sin, is_ctx):
        pg = _inproj(xs, l0_norm1, mod[0], mod[1], w_gqa, tm_, F32)
        pr = _inproj(xs, l0_norm1, mod[0], mod[1], w_rw, tm_, F32)
        q, kd, vd = _gqa_prep(pg, cos, sin, qg, kg, q_width, kv_width, tm_)
        prep = _rwkv_prep(pr, mu, lw_w, la_w, lg_w, w0, a0, k_k, k_a, r_k, dim, 256, is_ctx)
        return q, kd, vd, prep

    q_c, kd_c, vd_c, prep_c = half_layer0(ctx, mod_c, tm_c, cos_c, sin_c, True)
    q_l, kd_l, vd_l, prep_l = half_layer0(x, mod_l, tm, cos_l, sin_l, False)

    o_gqa_l = _gqa_attn(q_l, [(kd_l, vd_l), (kd_c, vd_c)], min(GQA_QUERY_TILE, t))
    o_gqa_c = _gqa_attn(q_c, [(kd_c, vd_c)], n_ctx)

    h0 = jnp.zeros((2, b, dim // LANES, LANES, LANES), F32)
    y_c, h_c = _rwkv_scan(*prep_c[:6], h0)
    y_l, _ = _rwkv_scan(*prep_l[:6], h_c)
    o_rw_l = _rwkv_finish(y_l, prep_l[6], prep_l[7], l0_lnx_g, l0_lnx_b, tm)
    o_rw_c = _rwkv_finish(y_c, prep_c[6], prep_c[7], l0_lnx_g, l0_lnx_b, tm_c)

    w1 = l0_mlp_w1.astype(MXU)
    w2 = l0_mlp_w2.astype(MXU)
    x = _outproj(x, mod_l[2], [(o_gqa_l, w_out_gqa), (o_rw_l, w_out_rw)], tm)
    x = _mlp(x, l0_norm2, mod_l[3], mod_l[4], mod_l[5], w1, w2, final_norm, tm, False)
    ctx = _outproj(ctx, mod_c[2], [(o_gqa_c, w_out_gqa), (o_rw_c, w_out_rw)], tm_c)
    ctx = _mlp(ctx, l0_norm2, mod_c[3], mod_c[4], mod_c[5], w1, w2, final_norm, tm_c, False)

    mod_l, mod_c = modulation(l1_ada_w, l1_ada_b)
    width = l1_w_out.shape[0]
    n_heads = width // HEAD_DIM
    scale = jnp.concatenate([jnp.full((width,), ATTN_SCALE, F32), jnp.ones((2 * width,), F32)])
    w_qkv = (l1_w_qkv * scale).astype(MXU)
    qkv = _inproj(x, l1_norm1, mod_l[0], mod_l[1], w_qkv, min(2 * tm, t), MXU)
    kvc = _inproj(ctx, l1_norm1, mod_c[0], mod_c[1], w_qkv[:, width:], tm_c, MXU)
    o_na = _na_attn(qkv, kvc, _na_col_table(l1_rpb, n_heads), width)
    x = _outproj(x, mod_l[2], [(o_na, l1_w_out.astype(MXU))], tm)
    x = _mlp(x, l1_norm2, mod_l[3], mod_l[4], mod_l[5], l1_mlp_w1.astype(MXU), l1_mlp_w2.astype(MXU),
             final_norm, tm, True)
    return x
```

```python
import functools

import jax
import jax.numpy as jnp
import numpy as np
from jax import lax
from jax.experimental import pallas as pl
from jax.experimental.pallas import tpu as pltpu

F32 = jnp.float32
MXU = jnp.bfloat16

LANES = 128
HEAD_DIM = 64
HEADS_PER_BLOCK = LANES // HEAD_DIM
GRID_W = 64
NORM_EPS = 1e-6
LNX_EPS = 64e-5
ROPE_THETA = 10000.0
NA_ROWS = 8
NA_COLS = 16
NA_GROUP = 4
NA_WIN = NA_GROUP + NA_ROWS
NA_STEP_BLOCKS = 4
GQA_STEP_KV_HEADS = 1
GQA_QUERY_TILE = 512
MASK_VALUE = -1e30
LOG2E = float(np.log2(np.e))
ATTN_SCALE = HEAD_DIM ** -0.5 * LOG2E
ATTN_KEY_CHUNK = 1024
INPROJ_MAX_COLS = 2048
INPROJ_RESIDENT_BYTES = 16 * 1024 * 1024
INPROJ_VMEM_BUDGET = 46 * 1024 * 1024
MLP_MAX_HIDDEN = 1024
MLP_ROW_TILE = 512
SCAN_CHUNK = 64
SCAN_PAIRS = 8
SCAN_STEP_CHUNKS = 2
VMEM_LIMIT = 56 * 1024 * 1024


def _cparams(sem):
    return pltpu.CompilerParams(dimension_semantics=sem, vmem_limit_bytes=VMEM_LIMIT)


def _mm(a, b):
    return jnp.dot(a.astype(MXU), b.astype(MXU), preferred_element_type=F32)


def _mm_nt(a, b):
    return lax.dot_general(a.astype(MXU), b.astype(MXU), (((1,), (1,)), ((), ())),
                           preferred_element_type=F32)


def _mm_tn(a, b):
    return lax.dot_general(a.astype(MXU), b.astype(MXU), (((0,), (0,)), ((), ())),
                           preferred_element_type=F32)


def _split3(x):
    hi = x.astype(MXU)
    r1 = x - hi.astype(F32)
    mid = r1.astype(MXU)
    lo = (r1 - mid.astype(F32)).astype(MXU)
    return hi, mid, lo


def _head_ones():
    r = lax.broadcasted_iota(jnp.int32, (LANES, LANES), 0) // HEAD_DIM
    c = lax.broadcasted_iota(jnp.int32, (LANES, LANES), 1) // HEAD_DIM
    return jnp.where(r == c, 1.0, 0.0).astype(MXU)


def _headsum(x, ones):
    hi, mid, lo = _split3(x)
    return (jnp.dot(hi, ones, preferred_element_type=F32) + jnp.dot(mid, ones, preferred_element_type=F32)
            + jnp.dot(lo, ones, preferred_element_type=F32))


def _lane(shape):
    return lax.broadcasted_iota(jnp.int32, shape, len(shape) - 1)


def _sigmoid(x):
    return 1.0 / (1.0 + jnp.exp(-x))


def _ada_kernel(c_ref, w_ref, b_ref, o_ref):
    c = c_ref[...]
    a = c * _sigmoid(c)
    o_ref[...] = _mm(a, w_ref[...]) + b_ref[...]


def _ada_mod(cc, w, bias):
    d, n = w.shape
    tn = 1024
    return pl.pallas_call(
        _ada_kernel,
        out_shape=jax.ShapeDtypeStruct((cc.shape[0], n), F32),
        grid=(n // tn,),
        in_specs=[pl.BlockSpec((cc.shape[0], d), lambda j: (0, 0)),
                  pl.BlockSpec((d, tn), lambda j: (0, j)),
                  pl.BlockSpec((1, tn), lambda j: (0, j))],
        out_specs=pl.BlockSpec((cc.shape[0], tn), lambda j: (0, j)),
        compiler_params=_cparams(("arbitrary",)),
        name="ada_mod",
    )(cc, w, bias.reshape(1, n))


def _norm_mod(x, g, shift, scale):
    y = x * lax.rsqrt(jnp.mean(x * x, axis=-1, keepdims=True) + NORM_EPS)
    return (y * g) * (1.0 + scale) + shift


def _inproj_kernel(x_ref, g_ref, sh_ref, sc_ref, w_ref, o_ref, a_scr):
    @pl.when(pl.program_id(2) == 0)
    def _():
        a_scr[...] = _norm_mod(x_ref[0], g_ref[...], sh_ref[0], sc_ref[0]).astype(a_scr.dtype)

    o_ref[0] = jnp.dot(a_scr[...], w_ref[...], preferred_element_type=F32).astype(o_ref.dtype)


def _col_tile(n, cap):
    return max(tn for tn in range(LANES, cap + 1, LANES) if n % tn == 0)


def _inproj(x, g, shift, scale, w, tm, out_dtype):
    b, t, d = x.shape
    n = w.shape[1]
    tn = n if d * n * w.dtype.itemsize <= INPROJ_RESIDENT_BYTES else _col_tile(n, INPROJ_MAX_COLS)
    out_bytes = jnp.dtype(out_dtype).itemsize

    def vmem(tm_):
        return 2 * (tm_ * d * 4 + d * tn * w.dtype.itemsize + tm_ * tn * out_bytes) + tm_ * d * w.dtype.itemsize

    while vmem(tm) > INPROJ_VMEM_BUDGET and tm % 16 == 0:
        tm //= 2
    return pl.pallas_call(
        _inproj_kernel,
        out_shape=jax.ShapeDtypeStruct((b, t, n), out_dtype),
        grid=(b, t // tm, n // tn),
        in_specs=[pl.BlockSpec((1, tm, d), lambda bi, i, j: (bi, i, 0)),
                  pl.BlockSpec((1, d), lambda bi, i, j: (0, 0)),
                  pl.BlockSpec((1, 1, d), lambda bi, i, j: (bi, 0, 0)),
                  pl.BlockSpec((1, 1, d), lambda bi, i, j: (bi, 0, 0)),
                  pl.BlockSpec((d, tn), lambda bi, i, j: (0, j))],
        out_specs=pl.BlockSpec((1, tm, tn), lambda bi, i, j: (bi, i, j)),
        scratch_shapes=[pltpu.VMEM((tm, d), MXU)],
        compiler_params=_cparams(("parallel", "parallel", "arbitrary")),
        name="inproj",
    )(x, g.reshape(1, d), shift, scale, w)


def _swap_half_heads(x):
    first = (_lane(x.shape) % HEAD_DIM) < HEAD_DIM // 2
    return jnp.where(first, pltpu.roll(x, LANES - HEAD_DIM // 2, 1), pltpu.roll(x, HEAD_DIM // 2, 1))


def _gqa_prep_kernel(q_ref, kv_ref, cos_ref, sin_ref, qg_ref, kg_ref, qo_ref, ko_ref, vo_ref, *, n_q, n_kv):
    ones = _head_ones()
    cos = cos_ref[...]
    sin = sin_ref[...]
    first_head = _lane(cos.shape) < HEAD_DIM

    def norm_rope(x, g):
        ms = _headsum(x * x, ones) * (1.0 / HEAD_DIM)
        y = x * lax.rsqrt(ms + NORM_EPS) * g
        return y * cos + _swap_half_heads(y) * sin

    def dup(x, half):
        rolled = pltpu.roll(x, HEAD_DIM, 1)
        return jnp.where(first_head, x, rolled) if half == 0 else jnp.where(first_head, rolled, x)

    for j in range(n_q):
        x = q_ref[0, :, j * LANES:(j + 1) * LANES]
        qo_ref[0, :, j * LANES:(j + 1) * LANES] = (norm_rope(x, qg_ref[...]) * ATTN_SCALE).astype(qo_ref.dtype)
    for j in range(n_kv):
        k = norm_rope(kv_ref[0, :, j * LANES:(j + 1) * LANES], kg_ref[...])
        vt = kv_ref[0, :, (n_kv + j) * LANES:(n_kv + j + 1) * LANES].T
        for half in range(HEADS_PER_BLOCK):
            ko_ref[0, HEADS_PER_BLOCK * j + half] = dup(k, half).astype(ko_ref.dtype)
            vo_ref[0, HEADS_PER_BLOCK * j + half] = vt[half * HEAD_DIM:(half + 1) * HEAD_DIM].astype(vo_ref.dtype)


def _gqa_prep(p, cos, sin, qg, kg, q_width, kv_width, tm):
    b, t, _ = p.shape
    n_q = q_width // LANES
    n_kv = kv_width // LANES
    kvh = kv_width // HEAD_DIM
    kern = functools.partial(_gqa_prep_kernel, n_q=n_q, n_kv=n_kv)
    kv_spec = pl.BlockSpec((1, kvh, tm, LANES), lambda bi, i: (bi, 0, i, 0))
    vt_spec = pl.BlockSpec((1, kvh, HEAD_DIM, tm), lambda bi, i: (bi, 0, 0, i))
    return pl.pallas_call(
        kern,
        out_shape=(jax.ShapeDtypeStruct((b, t, q_width), MXU),
                   jax.ShapeDtypeStruct((b, kvh, t, LANES), MXU),
                   jax.ShapeDtypeStruct((b, kvh, HEAD_DIM, t), MXU)),
        grid=(b, t // tm),
        in_specs=[pl.BlockSpec((1, tm, q_width), lambda bi, i: (bi, i, 0)),
                  pl.BlockSpec((1, tm, 2 * kv_width), lambda bi, i: (bi, i, q_width // (2 * kv_width))),
                  pl.BlockSpec((tm, LANES), lambda bi, i: (i, 0)),
                  pl.BlockSpec((tm, LANES), lambda bi, i: (i, 0)),
                  pl.BlockSpec((1, LANES), lambda bi, i: (0, 0)),
                  pl.BlockSpec((1, LANES), lambda bi, i: (0, 0))],
        out_specs=(pl.BlockSpec((1, tm, q_width), lambda bi, i: (bi, i, 0)), kv_spec, vt_spec),
        compiler_params=_cparams(("parallel", "parallel")),
        name="gqa_prep",
    )(p, p, cos, sin, qg, kg)


def _softmax_pv_t(q_blocks, chunks):
    first_head = _lane(q_blocks[0].shape) < HEAD_DIM
    zero = jnp.zeros_like(q_blocks[0])
    n_heads = HEADS_PER_BLOCK * len(q_blocks)
    s = [[None] * len(chunks) for _ in range(n_heads)]
    m = [None] * n_heads
    l = [None] * n_heads
    acc = [None] * n_heads
    for stage in range(n_heads + 1):
        ha, hb = stage, stage - 1
        if ha < n_heads:
            q = q_blocks[ha // HEADS_PER_BLOCK]
            qa = jnp.where(first_head, q, zero) if ha % HEADS_PER_BLOCK == 0 else jnp.where(first_head, zero, q)
        for c, (load_k, load_vt, load_bias) in enumerate(chunks):
            if ha < n_heads:
                sc = _mm_nt(load_k(ha // HEADS_PER_BLOCK), qa)
                if load_bias is not None:
                    sc = sc + load_bias(ha)
                s[ha][c] = sc
                mc = jnp.max(sc, axis=0, keepdims=True)
                m[ha] = mc if m[ha] is None else jnp.maximum(m[ha], mc)
            if hb >= 0:
                p = jnp.exp2(s[hb][c] - m[hb])
                s[hb][c] = None
                lc = jnp.sum(p, axis=0, keepdims=True)
                pv = _mm(load_vt(hb), p)
                l[hb] = lc if l[hb] is None else l[hb] + lc
                acc[hb] = pv if acc[hb] is None else acc[hb] + pv
    outs = []
    for j in range(len(q_blocks)):
        o_t = jnp.concatenate([acc[2 * j] / l[2 * j], acc[2 * j + 1] / l[2 * j + 1]], axis=0)
        outs.append(o_t.T)
    return outs


def _gqa_attn_kernel(q_ref, *refs, n_sets):
    o_ref = refs[-1]
    n_blocks = q_ref.shape[2] // LANES
    heads_per_kv = HEADS_PER_BLOCK * n_blocks // refs[0].shape[1]
    chunks = []
    for i in range(n_sets):
        k_ref, vt_ref = refs[2 * i], refs[2 * i + 1]
        n_keys = k_ref.shape[2]
        for start in range(0, n_keys, ATTN_KEY_CHUNK):
            size = min(ATTN_KEY_CHUNK, n_keys - start)
            chunks.append((lambda blk, r=k_ref, s=start, z=size: r[0, HEADS_PER_BLOCK * blk // heads_per_kv, pl.ds(s, z), :],
                           lambda h, r=vt_ref, s=start, z=size: r[0, h // heads_per_kv, :, pl.ds(s, z)], None))
    outs = _softmax_pv_t([q_ref[0, :, j * LANES:(j + 1) * LANES] for j in range(n_blocks)], chunks)
    for j, o in enumerate(outs):
        o_ref[0, :, j * LANES:(j + 1) * LANES] = o.astype(o_ref.dtype)


def _gqa_attn(q, kv_sets, tq):
    b, t, width = q.shape
    kvh = kv_sets[0][0].shape[1]
    kv_step = min(GQA_STEP_KV_HEADS, kvh)
    group_width = (width // kvh) * kv_step
    in_specs = [pl.BlockSpec((1, tq, group_width), lambda bi, g, i: (bi, i, g))]
    args = [q]
    for k, v in kv_sets:
        s = k.shape[2]
        in_specs += [pl.BlockSpec((1, kv_step, s, LANES), lambda bi, g, i: (bi, g, 0, 0)),
                     pl.BlockSpec((1, kv_step, HEAD_DIM, s), lambda bi, g, i: (bi, g, 0, 0))]
        args += [k, v]
    return pl.pallas_call(
        functools.partial(_gqa_attn_kernel, n_sets=len(kv_sets)),
        out_shape=jax.ShapeDtypeStruct((b, t, width), MXU),
        grid=(b, kvh // kv_step, t // tq),
        in_specs=in_specs,
        out_specs=pl.BlockSpec((1, tq, group_width), lambda bi, g, i: (bi, i, g)),
        compiler_params=_cparams(("parallel", "parallel", "arbitrary")),
        name="gqa_attn",
    )(*args)


def _rwkv_prep_kernel(p_ref, up_ref, dn_ref, mu_ref, lw_w_ref, la_w_ref, lg_w_ref, w0_ref, a0_ref, kk_ref, ka_ref,
                      rk_ref, r_o, v_o, kk_o, lw_o, kd_o, bb_o, bonus_o, gate_o, xm_scr, *, is_ctx, dim):
    tm = p_ref.shape[1]
    n_all = p_ref.shape[2] // LANES
    n_dim = dim // LANES
    i = pl.program_id(1)
    n_i = pl.num_programs(1)
    row = lax.broadcasted_iota(jnp.int32, (tm, LANES), 0)
    cls = _lane((tm, LANES)) % 4
    if is_ctx:
        first = row == 0
        last = row == tm - 1
    else:
        first = (row % GRID_W) == 0
        last = (row % GRID_W) == GRID_W - 1
        up_ok = jnp.logical_or(row >= GRID_W, i > 0)
        dn_ok = jnp.logical_or(row < tm - GRID_W, i < n_i - 1)

    for j in range(n_all):
        sl = slice(j * LANES, (j + 1) * LANES)
        p = p_ref[0, :, sl]
        prev = jnp.where(first, 0.0, pltpu.roll(p, 1, 0))
        nxt = jnp.where(last, 0.0, pltpu.roll(p, tm - 1, 0))
        if is_ctx:
            sh = jnp.where(cls % 2 == 0, prev, nxt)
        else:
            up = jnp.concatenate([up_ref[0, :, sl], p[:tm - GRID_W]], axis=0)
            dn = jnp.concatenate([p[GRID_W:], dn_ref[0, :, sl]], axis=0)
            up = jnp.where(up_ok, up, 0.0)
            dn = jnp.where(dn_ok, dn, 0.0)
            sh = jnp.where(cls == 0, prev, jnp.where(cls == 1, nxt, jnp.where(cls == 2, up, dn)))
        xm_scr[:, sl] = p + mu_ref[:, sl] * (sh - p)

    ones = _head_ones()
    x_w = jnp.tanh(xm_scr[:, 3 * dim:3 * dim + LANES]).astype(MXU)
    x_a = xm_scr[:, 3 * dim + LANES:3 * dim + 2 * LANES].astype(MXU)
    x_g = _sigmoid(xm_scr[:, 3 * dim + 2 * LANES:3 * dim + 4 * LANES]).astype(MXU)
    for j in range(n_dim):
        sl = slice(j * LANES, (j + 1) * LANES)
        r = xm_scr[:, sl]
        k = xm_scr[:, dim + j * LANES:dim + (j + 1) * LANES]
        v = xm_scr[:, 2 * dim + j * LANES:2 * dim + (j + 1) * LANES]
        kkr = k * kk_ref[:, sl]
        kk = kkr * lax.rsqrt(jnp.maximum(_headsum(kkr * kkr, ones), 1e-12))
        r_o[0, :, sl] = r
        v_o[0, :, sl] = v
        kk_o[0, :, sl] = kk
        gate_o[0, :, sl] = jnp.dot(x_g, lg_w_ref[:, sl], preferred_element_type=F32)
        bonus = jnp.zeros_like(r)
        for d in range(2):
            dsl = slice(d * dim + j * LANES, d * dim + (j + 1) * LANES)
            z = w0_ref[:, dsl] + jnp.dot(x_w, lw_w_ref[:, dsl], preferred_element_type=F32)
            softplus = jnp.maximum(-z, 0.0) + jnp.log(1.0 + jnp.exp(-jnp.abs(z)))
            lw_o[d, 0, :, sl] = -jnp.exp(-softplus - 0.5)
            iclr = _sigmoid(a0_ref[:, dsl] + jnp.dot(x_a, la_w_ref[:, dsl], preferred_element_type=F32))
            kd = k * (1.0 + (iclr - 1.0) * ka_ref[:, sl])
            kd_o[d, 0, :, sl] = kd
            bb_o[d, 0, :, sl] = kk * iclr
            bonus = bonus + _headsum(r * kd * rk_ref[:, sl], ones) * v
        bonus_o[0, :, sl] = bonus


def _rwkv_prep(p, mu, lw_w, la_w, lg_w, w0, a0, k_k, k_a, r_k, dim, tm, is_ctx):
    b, t, width = p.shape
    if is_ctx:
        assert t == tm
    else:
        assert tm % GRID_W == 0 and t % tm == 0
    hb = tm // GRID_W
    n_halo = t // GRID_W
    kern = functools.partial(_rwkv_prep_kernel, is_ctx=is_ctx, dim=dim)
    vec = lambda n: pl.BlockSpec((1, n), lambda bi, i: (0, 0))
    mat = lambda k, n: pl.BlockSpec((k, n), lambda bi, i: (0, 0))
    one = jax.ShapeDtypeStruct((b, t, dim), F32)
    two = jax.ShapeDtypeStruct((2, b, t, dim), F32)
    one_spec = pl.BlockSpec((1, tm, dim), lambda bi, i: (bi, i, 0))
    two_spec = pl.BlockSpec((2, 1, tm, dim), lambda bi, i: (0, bi, i, 0))
    return pl.pallas_call(
        kern,
        out_shape=(one, one, one, two, two, two, one, one),
        grid=(b, t // tm),
        in_specs=[pl.BlockSpec((1, tm, width), lambda bi, i: (bi, i, 0)),
                  pl.BlockSpec((1, GRID_W, width), lambda bi, i: (bi, jnp.maximum(i * hb - 1, 0), 0)),
                  pl.BlockSpec((1, GRID_W, width), lambda bi, i: (bi, jnp.minimum((i + 1) * hb, n_halo - 1), 0)),
                  vec(width), mat(LANES, 2 * dim), mat(LANES, 2 * dim), mat(2 * LANES, dim),
                  vec(2 * dim), vec(2 * dim), vec(dim), vec(dim), vec(dim)],
        out_specs=(one_spec, one_spec, one_spec, two_spec, two_spec, two_spec, one_spec, one_spec),
        scratch_shapes=[pltpu.VMEM((tm, width), F32)],
        compiler_params=_cparams(("parallel", "parallel")),
        name="rwkv_prep_ctx" if is_ctx else "rwkv_prep",
    )(p, p, p, mu, lw_w, la_w, lg_w, w0, a0, k_k, k_a, r_k)


def _scan_chunk_terms(r, v, kk, lw, kd, bb, sgn):
    blocks = range(len(r))
    c = r[0].shape[0]
    n = 2 * c
    ri = lax.broadcasted_iota(jnp.int32, (n, n), 0)
    ci = lax.broadcasted_iota(jnp.int32, (n, n), 1)
    same = (ri // c) == (ci // c)
    dt = ((ri % c) - (ci % c)) * sgn
    strict = jnp.logical_and(same, dt > 0)
    incl = jnp.logical_and(same, dt >= 0)
    eye = ri == ci
    head_lanes = (ri // c) == (ci // HEAD_DIM)
    zero = jnp.zeros((n, n), F32)
    time = lax.broadcasted_iota(jnp.int32, (c, LANES), 0)
    sgn_f = sgn.astype(F32)
    rev_f = 0.5 - 0.5 * sgn_f

    def stack(x):
        return jnp.concatenate([x, x], axis=0)

    def own(x):
        return jnp.where(head_lanes, stack(x), zero)

    def prefix(x):
        step = 1
        while step < c:
            x = x + jnp.where(time >= step, pltpu.roll(x, step, 0), 0.0)
            step *= 2
        return x

    tot = [jnp.sum(lw[p], axis=0, keepdims=True) for p in blocks]
    pre = [prefix(lw[p]) for p in blocks]
    cum = [rev_f * (tot[p] + lw[p]) + sgn_f * pre[p] for p in blocks]
    e_inv = [jnp.exp(-cum[p]) for p in blocks]
    e_fin = [jnp.exp(tot[p] - cum[p]) for p in blocks]
    at = [own(-kk[p] * jnp.exp(cum[p] - lw[p])) for p in blocks]
    rt = [own(r[p] * jnp.exp(cum[p])) for p in blocks]
    bt = [stack(bb[p] * e_inv[p]) for p in blocks]
    kt = [stack(kd[p] * e_inv[p]) for p in blocks]
    bhat = [own(bb[p] * e_fin[p]) for p in blocks]
    khat = [own(kd[p] * e_fin[p]) for p in blocks]
    vbd = [own(v[p]) for p in blocks]

    full = [_mm_nt(jnp.concatenate([at[p], rt[p]], axis=0), jnp.concatenate([bt[p], kt[p]], axis=0)) for p in blocks]
    a_ab = [jnp.where(strict, full[p][:n, :n], zero) for p in blocks]
    a_ak = [jnp.where(strict, full[p][:n, n:], zero) for p in blocks]
    a_rb = [jnp.where(incl, full[p][n:, :n], zero) for p in blocks]
    a_rk = [jnp.where(incl, full[p][n:, n:], zero) for p in blocks]

    ident = jnp.where(eye, 1.0, 0.0)
    inv = [ident + a_ab[p] for p in blocks]
    power = a_ab
    for _ in range(int(np.log2(c)) - 1):
        power = [_mm(power[p], power[p]) for p in blocks]
        inv = [inv[p] + _mm(inv[p], power[p]) for p in blocks]

    xv = [_mm(jnp.concatenate([a_ak[p], a_rk[p]], axis=0), vbd[p]) for p in blocks]
    w12 = [_mm(inv[p], jnp.concatenate([at[p], xv[p][:n]], axis=1)) for p in blocks]
    yw = [_mm(a_rb[p], w12[p]) for p in blocks]
    mn = [_mm_tn(bhat[p], w12[p]) for p in blocks]
    py = [rt[p] + yw[p][:, :n] for p in blocks]
    yl = [yw[p][:, n:] + xv[p][n:] for p in blocks]
    m = [jnp.where(eye, jnp.exp(tot[p]), zero) + mn[p][:, :n] for p in blocks]
    nn = [mn[p][:, n:] + _mm_tn(khat[p], vbd[p]) for p in blocks]
    return py, yl, m, nn


def _scan_kernel(r_ref, v_ref, kk_ref, lw_ref, kd_ref, bb_ref, h0_ref, y_ref, ht_ref, h_scr, *, pairs, chunks):
    d = pl.program_id(0)
    ci = pl.program_id(2)
    sgn = 1 - 2 * d
    c = SCAN_CHUNK
    n = 2 * c

    @pl.when(ci == 0)
    def _():
        h_scr[...] = h0_ref[0, 0]

    rows = [pl.ds(pl.multiple_of((d * (chunks - 1) + sgn * s) * c, c), c) for s in range(chunks)]
    lanes = [slice(j * LANES, (j + 1) * LANES) for j in range(pairs)]
    probs = [(rw, ln) for rw in rows for ln in lanes]
    py, yl, m, nn = _scan_chunk_terms(
        [r_ref[0, rw, ln] for rw, ln in probs], [v_ref[0, rw, ln] for rw, ln in probs],
        [kk_ref[0, rw, ln] for rw, ln in probs], [lw_ref[0, 0, rw, ln] for rw, ln in probs],
        [kd_ref[0, 0, rw, ln] for rw, ln in probs], [bb_ref[0, 0, rw, ln] for rw, ln in probs], sgn)

    h = [h_scr[j] for j in range(pairs)]
    for s in range(chunks):
        idx = [s * pairs + j for j in range(pairs)]
        yh = [_mm(jnp.concatenate([py[i], m[i]], axis=0), h[j]) for j, i in enumerate(idx)]
        for j, i in enumerate(idx):
            y2 = yh[j][:n] + yl[i]
            y_ref[0, 0, rows[s], lanes[j]] = y2[:c] + y2[c:]
        h = [yh[j][n:] + nn[i] for j, i in enumerate(idx)]
    for j in range(pairs):
        h_scr[j] = h[j]

    @pl.when(ci == pl.num_programs(2) - 1)
    def _():
        ht_ref[0, 0] = h_scr[...]


def _rwkv_scan(r, v, kk, lw, kd, bb, h0):
    b, t, dim = r.shape
    c = SCAN_CHUNK * SCAN_STEP_CHUNKS
    assert t % c == 0
    nch = t // c
    width = SCAN_PAIRS * LANES
    nblk = dim // width

    def chunk(d, ci):
        return ci + d * (nch - 1 - 2 * ci)

    one = pl.BlockSpec((1, c, width), lambda d, g, ci: (g // nblk, chunk(d, ci), g % nblk))
    two = pl.BlockSpec((1, 1, c, width), lambda d, g, ci: (d, g // nblk, chunk(d, ci), g % nblk))
    st = pl.BlockSpec((1, 1, SCAN_PAIRS, LANES, LANES), lambda d, g, ci: (d, g // nblk, g % nblk, 0, 0))
    return pl.pallas_call(
        functools.partial(_scan_kernel, pairs=SCAN_PAIRS, chunks=SCAN_STEP_CHUNKS),
        out_shape=(jax.ShapeDtypeStruct((2, b, t, dim), F32), jax.ShapeDtypeStruct(h0.shape, F32)),
        grid=(2, b * nblk, nch),
        in_specs=[one, one, one, two, two, two, st],
        out_specs=(two, st),
        scratch_shapes=[pltpu.VMEM((SCAN_PAIRS, LANES, LANES), F32)],
        compiler_params=_cparams(("parallel", "parallel", "arbitrary")),
        name="rwkv_scan",
    )(r, v, kk, lw, kd, bb, h0)


def _rwkv_finish_kernel(y_ref, bonus_ref, gate_ref, g_ref, b_ref, o_ref):
    ones = _head_ones()
    for j in range(o_ref.shape[2] // LANES):
        sl = slice(j * LANES, (j + 1) * LANES)
        y = y_ref[0, 0, :, sl] + y_ref[1, 0, :, sl]
        mu = _headsum(y, ones) * (1.0 / HEAD_DIM)
        dlt = y - mu
        var = _headsum(dlt * dlt, ones) * (1.0 / HEAD_DIM)
        yn = dlt * lax.rsqrt(var + LNX_EPS)
        o_ref[0, :, sl] = ((yn * g_ref[:, sl] + b_ref[:, sl] + bonus_ref[0, :, sl]) * gate_ref[0, :, sl]).astype(o_ref.dtype)


def _rwkv_finish(y, bonus, gate, lnx_g, lnx_b, tm):
    _, b, t, dim = y.shape
    one_spec = pl.BlockSpec((1, tm, dim), lambda bi, i: (bi, i, 0))
    vec = pl.BlockSpec((1, dim), lambda bi, i: (0, 0))
    return pl.pallas_call(
        _rwkv_finish_kernel,
        out_shape=jax.ShapeDtypeStruct((b, t, dim), MXU),
        grid=(b, t // tm),
        in_specs=[pl.BlockSpec((2, 1, tm, dim), lambda bi, i: (0, bi, i, 0)), one_spec, one_spec, vec, vec],
        out_specs=one_spec,
        compiler_params=_cparams(("parallel", "parallel")),
        name="rwkv_finish",
    )(y, bonus, gate, lnx_g.reshape(1, dim), lnx_b.reshape(1, dim))


def _outproj_kernel(*refs, n_in):
    x_ref, gate_ref = refs[0], refs[1]
    o_ref = refs[-1]
    acc = None
    for i in range(n_in):
        part = jnp.dot(refs[2 + 2 * i][0], refs[3 + 2 * i][...], preferred_element_type=F32)
        acc = part if acc is None else acc + part
    o_ref[0] = x_ref[0] + gate_ref[0] * acc


def _outproj(x, gate, parts, tm):
    b, t, d = x.shape
    in_specs = [pl.BlockSpec((1, tm, d), lambda bi, i: (bi, i, 0)),
                pl.BlockSpec((1, 1, d), lambda bi, i: (bi, 0, 0))]
    args = [x, gate]
    for a, w in parts:
        k = w.shape[0]
        in_specs += [pl.BlockSpec((1, tm, k), lambda bi, i: (bi, i, 0)), pl.BlockSpec((k, d), lambda bi, i: (0, 0))]
        args += [a, w]
    return pl.pallas_call(
        functools.partial(_outproj_kernel, n_in=len(parts)),
        out_shape=jax.ShapeDtypeStruct((b, t, d), F32),
        grid=(b, t // tm),
        in_specs=in_specs,
        out_specs=pl.BlockSpec((1, tm, d), lambda bi, i: (bi, i, 0)),
        compiler_params=_cparams(("parallel", "parallel")),
        name="outproj",
    )(*args)


def _mlp_kernel(x_ref, g_ref, sh_ref, sc_ref, gate_ref, w1_ref, w2_ref, fg_ref, o_ref, a_scr, *, final_norm):
    f = pl.program_id(2)

    @pl.when(f == 0)
    def _():
        a_scr[...] = _norm_mod(x_ref[0], g_ref[...], sh_ref[0], sc_ref[0]).astype(a_scr.dtype)
        o_ref[...] = jnp.zeros_like(o_ref)

    h = jnp.dot(a_scr[...], w1_ref[...], preferred_element_type=F32)
    h = jnp.square(jnp.maximum(h, 0.0))
    o_ref[0] += jnp.dot(h.astype(MXU), w2_ref[...], preferred_element_type=F32)

    @pl.when(f == pl.num_programs(2) - 1)
    def _():
        y = x_ref[0] + gate_ref[0] * o_ref[0]
        if final_norm:
            y = y * lax.rsqrt(jnp.mean(y * y, axis=-1, keepdims=True) + NORM_EPS) * fg_ref[...]
        o_ref[0] = y


def _mlp(x, g, shift, scale, gate, w1, w2, final_g, tm, final_norm):
    b, t, d = x.shape
    ff = w1.shape[1]
    tf = _col_tile(ff, MLP_MAX_HIDDEN)
    vec3 = pl.BlockSpec((1, 1, d), lambda bi, i, f: (bi, 0, 0))
    vec2 = pl.BlockSpec((1, d), lambda bi, i, f: (0, 0))
    return pl.pallas_call(
        functools.partial(_mlp_kernel, final_norm=final_norm),
        out_shape=jax.ShapeDtypeStruct((b, t, d), F32),
        grid=(b, t // tm, ff // tf),
        in_specs=[pl.BlockSpec((1, tm, d), lambda bi, i, f: (bi, i, 0)), vec2, vec3, vec3, vec3,
                  pl.BlockSpec((d, tf), lambda bi, i, f: (0, f)),
                  pl.BlockSpec((tf, d), lambda bi, i, f: (f, 0)), vec2],
        out_specs=pl.BlockSpec((1, tm, d), lambda bi, i, f: (bi, i, 0)),
        scratch_shapes=[pltpu.VMEM((tm, d), MXU)],
        compiler_params=_cparams(("parallel", "parallel", "arbitrary")),
        name="mlp",
    )(x, g.reshape(1, d), shift, scale, gate, w1, w2, final_g.reshape(1, d))


NA_GROUP_OFFSETS = (0, NA_ROWS // 2, NA_ROWS)
NA_MAX_DROW = NA_WIN - 1 + NA_ROWS - 1
NA_STRIP_BLOCKS = 2 * NA_WIN
NA_KEY_BLOCK = NA_GROUP * GRID_W


def _na_window_lo(typ, local_row):
    return (0, local_row, NA_WIN - NA_ROWS)[typ]


def _na_kernel(q_ref, k_ref, vt_ref, kc_ref, vct_ref, tc_ref, o_ref, bias_scr):
    g = pl.program_id(2)
    n_g = pl.num_programs(2)
    n_keys = NA_WIN * GRID_W
    n_blocks = q_ref.shape[2] // LANES
    tq = NA_GROUP * GRID_W

    @pl.when(jnp.logical_and(pl.program_id(1) == 0, g == 0))
    def _():
        query_row = _lane((GRID_W, tq)) // GRID_W
        for blk in range(n_blocks):
            for typ in range(3):
                for half in range(HEADS_PER_BLOCK):
                    for kr in range(NA_WIN):
                        off = (NA_WIN - 1 - kr + NA_GROUP_OFFSETS[typ]) * GRID_W
                        strip = tc_ref[blk, half, :, off:off + tq]
                        seen = [query_row == lr for lr in range(NA_GROUP)
                                if _na_window_lo(typ, lr) <= kr < _na_window_lo(typ, lr) + NA_ROWS]
                        r0 = half * n_keys + kr * GRID_W
                        if seen:
                            strip = jnp.where(functools.reduce(jnp.logical_or, seen), strip, MASK_VALUE)
                        else:
                            strip = jnp.full_like(strip, MASK_VALUE)
                        bias_scr[blk, typ, r0:r0 + GRID_W, :] = strip

    win_row = jnp.clip(g * NA_GROUP - NA_ROWS // 2, 0, n_g * NA_GROUP - NA_WIN)
    win_blk = win_row // NA_GROUP
    typ = jnp.where(g == 0, 0, jnp.where(g == n_g - 1, 2, 1))
    lanes = lambda blk: slice(blk * LANES, (blk + 1) * LANES)
    dims = lambda h: slice(h * HEAD_DIM, (h + 1) * HEAD_DIM)
    chunks = []
    for i in range(n_keys // NA_KEY_BLOCK):
        chunks.append((
            lambda blk, i=i: k_ref[0, pl.ds(pl.multiple_of((win_blk + i) * NA_KEY_BLOCK, NA_KEY_BLOCK), NA_KEY_BLOCK),
                                   lanes(blk)],
            lambda h, i=i: vt_ref[0, win_blk + i, dims(h), :],
            lambda h, i=i: bias_scr[h // HEADS_PER_BLOCK, typ,
                                    pl.ds((h % HEADS_PER_BLOCK) * n_keys + i * NA_KEY_BLOCK, NA_KEY_BLOCK), :]))
    chunks.append((lambda blk: kc_ref[0, :, lanes(blk)], lambda h: vct_ref[0, dims(h), :], None))
    outs = _softmax_pv_t([q_ref[0, :, lanes(blk)] for blk in range(n_blocks)], chunks)
    for blk, o in enumerate(outs):
        o_ref[0, :, lanes(blk)] = o.astype(o_ref.dtype)


def _na_col_table(rpb, n_heads):
    col = np.arange(GRID_W)
    c0 = np.clip(col - NA_COLS // 2, 0, GRID_W - NA_COLS)
    col_ok = (col[None, :] >= c0[:, None]) & (col[None, :] < c0[:, None] + NA_COLS)
    dcol = col[None, :] - col[:, None] + NA_COLS - 1
    onehot = (dcol[None] == np.arange(2 * NA_COLS - 1)[:, None, None]) & col_ok[None]
    tc = jnp.einsum("hrd,dqk->hkrq", rpb * LOG2E, jnp.asarray(onehot, F32), precision=lax.Precision.HIGHEST)
    tc = jnp.where(col_ok.T[None, :, None, :], tc, MASK_VALUE)
    n_dr = 2 * NA_ROWS - 1
    lead = NA_MAX_DROW - (n_dr - 1)
    tc = jnp.pad(tc[:, :, ::-1, :], ((0, 0), (0, 0), (lead, NA_STRIP_BLOCKS - lead - n_dr), (0, 0)),
                 constant_values=MASK_VALUE)
    return tc.reshape(n_heads // 2, HEADS_PER_BLOCK, GRID_W, NA_STRIP_BLOCKS * GRID_W)


def _transpose_kernel(x_ref, o_ref):
    for j in range(x_ref.shape[2] // LANES):
        x = x_ref[0, :, j * LANES:(j + 1) * LANES].astype(F32)
        o_ref[0, 0, j * LANES:(j + 1) * LANES, :] = x.T.astype(o_ref.dtype)


def _transpose_blocks(x, col0, width, rows):
    b, t, _ = x.shape
    return pl.pallas_call(
        _transpose_kernel,
        out_shape=jax.ShapeDtypeStruct((b, t // rows, width, rows), x.dtype),
        grid=(b, t // rows),
        in_specs=[pl.BlockSpec((1, rows, width), lambda bi, i: (bi, i, col0 // width))],
        out_specs=pl.BlockSpec((1, 1, width, rows), lambda bi, i: (bi, i, 0, 0)),
        compiler_params=_cparams(("parallel", "parallel")),
        name="transpose_blocks",
    )(x)


def _na_attn(qkv, kvc, col_table, width):
    b, t, _ = qkv.shape
    ctx = kvc.shape[1]
    bw = NA_STEP_BLOCKS * LANES
    nb = width // bw
    tq = NA_GROUP * GRID_W
    rows = t // GRID_W
    assert rows % NA_GROUP == 0 and rows >= NA_WIN and width % bw == 0 and NA_WIN % NA_GROUP == 0
    vt = _transpose_blocks(qkv, 2 * width, width, NA_KEY_BLOCK)
    vct = _transpose_blocks(kvc, width, width, ctx)[:, 0]
    return pl.pallas_call(
        _na_kernel,
        out_shape=jax.ShapeDtypeStruct((b, t, width), MXU),
        grid=(nb, b, rows // NA_GROUP),
        in_specs=[pl.BlockSpec((1, tq, bw), lambda p, bi, g: (bi, g, p)),
                  pl.BlockSpec((1, t, bw), lambda p, bi, g: (bi, 0, nb + p)),
                  pl.BlockSpec((1, t // NA_KEY_BLOCK, bw, NA_KEY_BLOCK), lambda p, bi, g: (bi, 0, p, 0)),
                  pl.BlockSpec((1, ctx, bw), lambda p, bi, g: (bi, 0, p)),
                  pl.BlockSpec((1, bw, ctx), lambda p, bi, g: (bi, p, 0)),
                  pl.BlockSpec((NA_STEP_BLOCKS, HEADS_PER_BLOCK, GRID_W, NA_STRIP_BLOCKS * GRID_W),
                               lambda p, bi, g: (p, 0, 0, 0))],
        out_specs=pl.BlockSpec((1, tq, bw), lambda p, bi, g: (bi, g, p)),
        scratch_shapes=[pltpu.VMEM((NA_STEP_BLOCKS, 3, HEADS_PER_BLOCK * NA_WIN * GRID_W, tq), F32)],
        compiler_params=_cparams(("arbitrary", "arbitrary", "arbitrary")),
        name="na_attn",
    )(qkv, qkv, vt, kvc, vct, col_table)


def _rope_tables(t):
    pos = jnp.arange(t, dtype=jnp.int32)
    row = (pos // GRID_W).astype(F32)
    col = (pos % GRID_W).astype(F32)
    pairs = HEAD_DIM // 4
    inv = ROPE_THETA ** (-jnp.arange(pairs, dtype=F32) / pairs)
    ang = jnp.concatenate([row[:, None] * inv, col[:, None] * inv], axis=-1)
    cos, sin = jnp.cos(ang), jnp.sin(ang)
    cos_h = jnp.concatenate([cos, cos], axis=-1)
    sin_h = jnp.concatenate([-sin, sin], axis=-1)
    return jnp.tile(cos_h, (1, HEADS_PER_BLOCK)), jnp.tile(sin_h, (1, HEADS_PER_BLOCK))


def _block_diag2(top, bottom):
    z_t = jnp.zeros_like(top)
    z_b = jnp.zeros_like(bottom)
    return jnp.concatenate([jnp.concatenate([top, z_t], axis=1), jnp.concatenate([z_b, bottom], axis=1)], axis=0)


def kernel(x, c, ctx, c_ctx, l0_norm1, l0_norm2, l0_ada_w, l0_ada_b, l0_w_in, l0_shift_mu, l0_w0_f, l0_w0_b, l0_ww2_f, l0_ww2_b, l0_a0_f, l0_a0_b, l0_wa2_f, l0_wa2_b, l0_wg2, l0_k_k, l0_k_a, l0_r_k, l0_lnx_g, l0_lnx_b, l0_q_norm, l0_k_norm, l0_w_out, l0_mlp_w1, l0_mlp_w2, l1_norm1, l1_norm2, l1_ada_w, l1_ada_b, l1_w_qkv, l1_rpb, l1_w_out, l1_mlp_w1, l1_mlp_w2, final_norm):
    b, t, d = x.shape
    n_ctx = ctx.shape[1]
    dim = l0_w0_f.shape[0]
    q_width = l0_w_out.shape[0] - dim
    gqa_cols = l0_w_in.shape[1] - l0_shift_mu.shape[0]
    kv_width = (gqa_cols - q_width) // 2
    rw_cols = l0_shift_mu.shape[0]
    rw_pad = 3 * dim + 4 * LANES
    assert rw_cols <= rw_pad and 3 * dim + 2 * LANES == rw_cols - l0_wg2.shape[0]
    tm = min(512, t)
    tm_c = n_ctx

    cc = jnp.zeros((8, d), F32).at[:b].set(c).at[b].set(c_ctx)

    def modulation(ada_w, ada_b):
        mod = _ada_mod(cc, ada_w, ada_b)
        lat = mod[:b].reshape(b, 6, 1, d)
        cx = jnp.broadcast_to(mod[b].reshape(1, 6, 1, d), (b, 6, 1, d))
        return [lat[:, i] for i in range(6)], [cx[:, i] for i in range(6)]

    mod_l, mod_c = modulation(l0_ada_w, l0_ada_b)

    w_gqa = l0_w_in[:, :gqa_cols].astype(MXU)
    w_rw = jnp.pad(l0_w_in[:, gqa_cols:], ((0, 0), (0, rw_pad - rw_cols))).astype(MXU)
    mu = jnp.pad(l0_shift_mu, (0, rw_pad - rw_cols)).reshape(1, rw_pad)
    lw_w = _block_diag2(l0_ww2_f, l0_ww2_b).astype(MXU)
    la_w = _block_diag2(l0_wa2_f, l0_wa2_b).astype(MXU)
    lg_w = jnp.pad(l0_wg2, ((0, 2 * LANES - l0_wg2.shape[0]), (0, 0))).astype(MXU)
    w0 = jnp.concatenate([l0_w0_f, l0_w0_b]).reshape(1, 2 * dim)
    a0 = jnp.concatenate([l0_a0_f, l0_a0_b]).reshape(1, 2 * dim)
    k_k = l0_k_k.reshape(1, dim)
    k_a = l0_k_a.reshape(1, dim)
    r_k = l0_r_k.reshape(1, dim)
    qg = jnp.tile(l0_q_norm, HEADS_PER_BLOCK).reshape(1, LANES)
    kg = jnp.tile(l0_k_norm, HEADS_PER_BLOCK).reshape(1, LANES)
    cos_l, sin_l = _rope_tables(t)
    cos_c, sin_c = jnp.ones((n_ctx, LANES), F32), jnp.zeros((n_ctx, LANES), F32)
    w_out_gqa = l0_w_out[:q_width].astype(MXU)
    w_out_rw = l0_w_out[q_width:].astype(MXU)

    def half_layer0(xs, mod, tm_, cos, sin, is_ctx):
        pg = _inproj(xs, l0_norm1, mod[0], mod[1], w_gqa, tm_, F32)
        pr = _inproj(xs, l0_norm1, mod[0], mod[1], w_rw, tm_, F32)
        q, kd, vd = _gqa_prep(pg, cos, sin, qg, kg, q_width, kv_width, tm_)
        prep = _rwkv_prep(pr, mu, lw_w, la_w, lg_w, w0, a0, k_k, k_a, r_k, dim, 256, is_ctx)
        return q, kd, vd, prep

    q_c, kd_c, vd_c, prep_c = half_layer0(ctx, mod_c, tm_c, cos_c, sin_c, True)
    q_l, kd_l, vd_l, prep_l = half_layer0(x, mod_l, tm, cos_l, sin_l, False)

    o_gqa_l = _gqa_attn(q_l, [(kd_l, vd_l), (kd_c, vd_c)], min(GQA_QUERY_TILE, t))
    o_gqa_c = _gqa_attn(q_c, [(kd_c, vd_c)], n_ctx)

    h0 = jnp.zeros((2, b, dim // LANES, LANES, LANES), F32)
    y_c, h_c = _rwkv_scan(*prep_c[:6], h0)
    y_l, _ = _rwkv_scan(*prep_l[:6], h_c)
    o_rw_l = _rwkv_finish(y_l, prep_l[6], prep_l[7], l0_lnx_g, l0_lnx_b, tm)
    o_rw_c = _rwkv_finish(y_c, prep_c[6], prep_c[7], l0_lnx_g, l0_lnx_b, tm_c)

    w1 = l0_mlp_w1.astype(MXU)
    w2 = l0_mlp_w2.astype(MXU)
    x = _outproj(x, mod_l[2], [(o_gqa_l, w_out_gqa), (o_rw_l, w_out_rw)], tm)
    x = _mlp(x, l0_norm2, mod_l[3], mod_l[4], mod_l[5], w1, w2, final_norm, min(MLP_ROW_TILE, t), False)
    ctx = _outproj(ctx, mod_c[2], [(o_gqa_c, w_out_gqa), (o_rw_c, w_out_rw)], tm_c)
    ctx = _mlp(ctx, l0_norm2, mod_c[3], mod_c[4], mod_c[5], w1, w2, final_norm, tm_c, False)

    mod_l, mod_c = modulation(l1_ada_w, l1_ada_b)
    width = l1_w_out.shape[0]
    n_heads = width // HEAD_DIM
    scale = jnp.concatenate([jnp.full((width,), ATTN_SCALE, F32), jnp.ones((2 * width,), F32)])
    w_qkv = (l1_w_qkv * scale).astype(MXU)
    qkv = _inproj(x, l1_norm1, mod_l[0], mod_l[1], w_qkv, min(2 * tm, t), MXU)
    kvc = _inproj(ctx, l1_norm1, mod_c[0], mod_c[1], w_qkv[:, width:], tm_c, MXU)
    o_na = _na_attn(qkv, kvc, _na_col_table(l1_rpb, n_heads), width)
    x = _outproj(x, mod_l[2], [(o_na, l1_w_out.astype(MXU))], tm)
    x = _mlp(x, l1_norm2, mod_l[3], mod_l[4], mod_l[5], l1_mlp_w1.astype(MXU), l1_mlp_w2.astype(MXU),
             final_norm, min(MLP_ROW_TILE, t), True)
    return x
```

```python
import functools

import jax
import jax.numpy as jnp
import numpy as np
from jax import lax
from jax.experimental import pallas as pl
from jax.experimental.pallas import tpu as pltpu

F32 = jnp.float32
MXU = jnp.bfloat16

LANES = 128
HEAD_DIM = 64
HEADS_PER_BLOCK = LANES // HEAD_DIM
GRID_W = 64
NORM_EPS = 1e-6
LNX_EPS = 64e-5
ROPE_THETA = 10000.0
NA_ROWS = 8
NA_COLS = 16
NA_GROUP = 4
NA_WIN = NA_GROUP + NA_ROWS
NA_STEP_BLOCKS = 4
GQA_STEP_KV_HEADS = 1
GQA_QUERY_TILE = 512
MASK_VALUE = -1e30
LOG2E = float(np.log2(np.e))
ATTN_SCALE = HEAD_DIM ** -0.5 * LOG2E
ATTN_KEY_CHUNK = 2048
INPROJ_MAX_COLS = 2048
INPROJ_RESIDENT_BYTES = 16 * 1024 * 1024
INPROJ_VMEM_BUDGET = 46 * 1024 * 1024
MLP_MAX_HIDDEN = 1024
MLP_ROW_TILE = 512
SCAN_CHUNK = 64
SCAN_PAIRS = 8
SCAN_STEP_CHUNKS = 2
VMEM_LIMIT = 56 * 1024 * 1024


def _cparams(sem):
    return pltpu.CompilerParams(dimension_semantics=sem, vmem_limit_bytes=VMEM_LIMIT)


def _mm(a, b):
    return jnp.dot(a.astype(MXU), b.astype(MXU), preferred_element_type=F32)


def _mm_nt(a, b):
    return lax.dot_general(a.astype(MXU), b.astype(MXU), (((1,), (1,)), ((), ())),
                           preferred_element_type=F32)


def _mm_tn(a, b):
    return lax.dot_general(a.astype(MXU), b.astype(MXU), (((0,), (0,)), ((), ())),
                           preferred_element_type=F32)


def _split3(x):
    hi = x.astype(MXU)
    r1 = x - hi.astype(F32)
    mid = r1.astype(MXU)
    lo = (r1 - mid.astype(F32)).astype(MXU)
    return hi, mid, lo


def _head_ones():
    r = lax.broadcasted_iota(jnp.int32, (LANES, LANES), 0) // HEAD_DIM
    c = lax.broadcasted_iota(jnp.int32, (LANES, LANES), 1) // HEAD_DIM
    return jnp.where(r == c, 1.0, 0.0).astype(MXU)


def _headsum(x, ones):
    hi, mid, lo = _split3(x)
    return (jnp.dot(hi, ones, preferred_element_type=F32) + jnp.dot(mid, ones, preferred_element_type=F32)
            + jnp.dot(lo, ones, preferred_element_type=F32))


def _lane(shape):
    return lax.broadcasted_iota(jnp.int32, shape, len(shape) - 1)


def _sigmoid(x):
    return 1.0 / (1.0 + jnp.exp(-x))


def _ada_kernel(c_ref, w_ref, b_ref, o_ref):
    c = c_ref[...]
    a = c * _sigmoid(c)
    o_ref[...] = _mm(a, w_ref[...]) + b_ref[...]


def _ada_mod(cc, w, bias):
    d, n = w.shape
    tn = 1024
    return pl.pallas_call(
        _ada_kernel,
        out_shape=jax.ShapeDtypeStruct((cc.shape[0], n), F32),
        grid=(n // tn,),
        in_specs=[pl.BlockSpec((cc.shape[0], d), lambda j: (0, 0)),
                  pl.BlockSpec((d, tn), lambda j: (0, j)),
                  pl.BlockSpec((1, tn), lambda j: (0, j))],
        out_specs=pl.BlockSpec((cc.shape[0], tn), lambda j: (0, j)),
        compiler_params=_cparams(("arbitrary",)),
        name="ada_mod",
    )(cc, w, bias.reshape(1, n))


def _norm_mod(x, g, shift, scale):
    y = x * lax.rsqrt(jnp.mean(x * x, axis=-1, keepdims=True) + NORM_EPS)
    return (y * g) * (1.0 + scale) + shift


def _inproj_kernel(x_ref, g_ref, sh_ref, sc_ref, w_ref, o_ref, a_scr):
    @pl.when(pl.program_id(2) == 0)
    def _():
        a_scr[...] = _norm_mod(x_ref[0], g_ref[...], sh_ref[0], sc_ref[0]).astype(a_scr.dtype)

    o_ref[0] = jnp.dot(a_scr[...], w_ref[...], preferred_element_type=F32).astype(o_ref.dtype)


def _col_tile(n, cap):
    return max(tn for tn in range(LANES, cap + 1, LANES) if n % tn == 0)


def _inproj(x, g, shift, scale, w, tm, out_dtype):
    b, t, d = x.shape
    n = w.shape[1]
    tn = n if d * n * w.dtype.itemsize <= INPROJ_RESIDENT_BYTES else _col_tile(n, INPROJ_MAX_COLS)
    out_bytes = jnp.dtype(out_dtype).itemsize

    def vmem(tm_):
        return 2 * (tm_ * d * 4 + d * tn * w.dtype.itemsize + tm_ * tn * out_bytes) + tm_ * d * w.dtype.itemsize

    while vmem(tm) > INPROJ_VMEM_BUDGET and tm % 16 == 0:
        tm //= 2
    return pl.pallas_call(
        _inproj_kernel,
        out_shape=jax.ShapeDtypeStruct((b, t, n), out_dtype),
        grid=(b, t // tm, n // tn),
        in_specs=[pl.BlockSpec((1, tm, d), lambda bi, i, j: (bi, i, 0)),
                  pl.BlockSpec((1, d), lambda bi, i, j: (0, 0)),
                  pl.BlockSpec((1, 1, d), lambda bi, i, j: (bi, 0, 0)),
                  pl.BlockSpec((1, 1, d), lambda bi, i, j: (bi, 0, 0)),
                  pl.BlockSpec((d, tn), lambda bi, i, j: (0, j))],
        out_specs=pl.BlockSpec((1, tm, tn), lambda bi, i, j: (bi, i, j)),
        scratch_shapes=[pltpu.VMEM((tm, d), MXU)],
        compiler_params=_cparams(("parallel", "parallel", "arbitrary")),
        name="inproj",
    )(x, g.reshape(1, d), shift, scale, w)


def _swap_half_heads(x):
    first = (_lane(x.shape) % HEAD_DIM) < HEAD_DIM // 2
    return jnp.where(first, pltpu.roll(x, LANES - HEAD_DIM // 2, 1), pltpu.roll(x, HEAD_DIM // 2, 1))


def _gqa_prep_kernel(q_ref, kv_ref, cos_ref, sin_ref, qg_ref, kg_ref, qo_ref, ko_ref, vo_ref, *, n_q, n_kv):
    ones = _head_ones()
    cos = cos_ref[...]
    sin = sin_ref[...]
    first_head = _lane(cos.shape) < HEAD_DIM

    def norm_rope(x, g):
        ms = _headsum(x * x, ones) * (1.0 / HEAD_DIM)
        y = x * lax.rsqrt(ms + NORM_EPS) * g
        return y * cos + _swap_half_heads(y) * sin

    def dup(x, half):
        rolled = pltpu.roll(x, HEAD_DIM, 1)
        return jnp.where(first_head, x, rolled) if half == 0 else jnp.where(first_head, rolled, x)

    for j in range(n_q):
        x = q_ref[0, :, j * LANES:(j + 1) * LANES]
        qo_ref[0, :, j * LANES:(j + 1) * LANES] = (norm_rope(x, qg_ref[...]) * ATTN_SCALE).astype(qo_ref.dtype)
    for j in range(n_kv):
        k = norm_rope(kv_ref[0, :, j * LANES:(j + 1) * LANES], kg_ref[...])
        vt = kv_ref[0, :, (n_kv + j) * LANES:(n_kv + j + 1) * LANES].T
        for half in range(HEADS_PER_BLOCK):
            ko_ref[0, HEADS_PER_BLOCK * j + half] = dup(k, half).astype(ko_ref.dtype)
            vo_ref[0, HEADS_PER_BLOCK * j + half] = vt[half * HEAD_DIM:(half + 1) * HEAD_DIM].astype(vo_ref.dtype)


def _gqa_prep(p, cos, sin, qg, kg, q_width, kv_width, tm):
    b, t, _ = p.shape
    n_q = q_width // LANES
    n_kv = kv_width // LANES
    kvh = kv_width // HEAD_DIM
    kern = functools.partial(_gqa_prep_kernel, n_q=n_q, n_kv=n_kv)
    kv_spec = pl.BlockSpec((1, kvh, tm, LANES), lambda bi, i: (bi, 0, i, 0))
    vt_spec = pl.BlockSpec((1, kvh, HEAD_DIM, tm), lambda bi, i: (bi, 0, 0, i))
    return pl.pallas_call(
        kern,
        out_shape=(jax.ShapeDtypeStruct((b, t, q_width), MXU),
                   jax.ShapeDtypeStruct((b, kvh, t, LANES), MXU),
                   jax.ShapeDtypeStruct((b, kvh, HEAD_DIM, t), MXU)),
        grid=(b, t // tm),
        in_specs=[pl.BlockSpec((1, tm, q_width), lambda bi, i: (bi, i, 0)),
                  pl.BlockSpec((1, tm, 2 * kv_width), lambda bi, i: (bi, i, q_width // (2 * kv_width))),
                  pl.BlockSpec((tm, LANES), lambda bi, i: (i, 0)),
                  pl.BlockSpec((tm, LANES), lambda bi, i: (i, 0)),
                  pl.BlockSpec((1, LANES), lambda bi, i: (0, 0)),
                  pl.BlockSpec((1, LANES), lambda bi, i: (0, 0))],
        out_specs=(pl.BlockSpec((1, tm, q_width), lambda bi, i: (bi, i, 0)), kv_spec, vt_spec),
        compiler_params=_cparams(("parallel", "parallel")),
        name="gqa_prep",
    )(p, p, cos, sin, qg, kg)


def _softmax_pv(q_blocks, chunks, keys_on_rows):
    first_head = _lane(q_blocks[0].shape) < HEAD_DIM
    zero = jnp.zeros_like(q_blocks[0])
    n_heads = HEADS_PER_BLOCK * len(q_blocks)
    axis = 0 if keys_on_rows else 1
    s = [[None] * len(chunks) for _ in range(n_heads)]
    m = [None] * n_heads
    l = [None] * n_heads
    acc = [None] * n_heads
    for stage in range(n_heads + 1):
        ha, hb = stage, stage - 1
        if ha < n_heads:
            q = q_blocks[ha // HEADS_PER_BLOCK]
            qa = jnp.where(first_head, q, zero) if ha % HEADS_PER_BLOCK == 0 else jnp.where(first_head, zero, q)
        for c, (load_k, load_v, load_bias) in enumerate(chunks):
            if ha < n_heads:
                k = load_k(ha // HEADS_PER_BLOCK)
                sc = _mm_nt(k, qa) if keys_on_rows else _mm_nt(qa, k)
                if load_bias is not None:
                    sc = sc + load_bias(ha)
                s[ha][c] = sc
                mc = jnp.max(sc, axis=axis, keepdims=True)
                m[ha] = mc if m[ha] is None else jnp.maximum(m[ha], mc)
            if hb >= 0:
                p = jnp.exp2(s[hb][c] - m[hb])
                s[hb][c] = None
                lc = jnp.sum(p, axis=axis, keepdims=True)
                pv = _mm(load_v(hb), p) if keys_on_rows else _mm(p, load_v(hb))
                l[hb] = lc if l[hb] is None else l[hb] + lc
                acc[hb] = pv if acc[hb] is None else acc[hb] + pv
    o = [a / d for a, d in zip(acc, l)]
    if keys_on_rows:
        return [jnp.concatenate(o[2 * j:2 * j + 2], axis=0).T for j in range(len(q_blocks))]
    return [jnp.where(first_head, o[2 * j], o[2 * j + 1]) for j in range(len(q_blocks))]


def _gqa_attn_kernel(q_ref, *refs, n_sets):
    o_ref = refs[-1]
    n_blocks = q_ref.shape[2] // LANES
    heads_per_kv = HEADS_PER_BLOCK * n_blocks // refs[0].shape[1]
    chunks = []
    for i in range(n_sets):
        k_ref, vt_ref = refs[2 * i], refs[2 * i + 1]
        n_keys = k_ref.shape[2]
        for start in range(0, n_keys, ATTN_KEY_CHUNK):
            size = min(ATTN_KEY_CHUNK, n_keys - start)
            chunks.append((lambda blk, r=k_ref, s=start, z=size: r[0, HEADS_PER_BLOCK * blk // heads_per_kv, pl.ds(s, z), :],
                           lambda h, r=vt_ref, s=start, z=size: r[0, h // heads_per_kv, :, pl.ds(s, z)], None))
    outs = _softmax_pv([q_ref[0, :, j * LANES:(j + 1) * LANES] for j in range(n_blocks)], chunks, keys_on_rows=True)
    for j, o in enumerate(outs):
        o_ref[0, :, j * LANES:(j + 1) * LANES] = o.astype(o_ref.dtype)


def _gqa_attn(q, kv_sets, tq):
    b, t, width = q.shape
    kvh = kv_sets[0][0].shape[1]
    kv_step = min(GQA_STEP_KV_HEADS, kvh)
    group_width = (width // kvh) * kv_step
    in_specs = [pl.BlockSpec((1, tq, group_width), lambda bi, g, i: (bi, i, g))]
    args = [q]
    for k, v in kv_sets:
        s = k.shape[2]
        in_specs += [pl.BlockSpec((1, kv_step, s, LANES), lambda bi, g, i: (bi, g, 0, 0)),
                     pl.BlockSpec((1, kv_step, HEAD_DIM, s), lambda bi, g, i: (bi, g, 0, 0))]
        args += [k, v]
    return pl.pallas_call(
        functools.partial(_gqa_attn_kernel, n_sets=len(kv_sets)),
        out_shape=jax.ShapeDtypeStruct((b, t, width), MXU),
        grid=(b, kvh // kv_step, t // tq),
        in_specs=in_specs,
        out_specs=pl.BlockSpec((1, tq, group_width), lambda bi, g, i: (bi, i, g)),
        compiler_params=_cparams(("parallel", "parallel", "arbitrary")),
        name="gqa_attn",
    )(*args)


def _rwkv_prep_kernel(p_ref, up_ref, dn_ref, mu_ref, lw_w_ref, la_w_ref, lg_w_ref, w0_ref, a0_ref, kk_ref, ka_ref,
                      rk_ref, r_o, v_o, kk_o, lw_o, kd_o, bb_o, bonus_o, gate_o, xm_scr, *, is_ctx, dim):
    tm = p_ref.shape[1]
    n_all = p_ref.shape[2] // LANES
    n_dim = dim // LANES
    i = pl.program_id(1)
    n_i = pl.num_programs(1)
    row = lax.broadcasted_iota(jnp.int32, (tm, LANES), 0)
    cls = _lane((tm, LANES)) % 4
    if is_ctx:
        first = row == 0
        last = row == tm - 1
    else:
        first = (row % GRID_W) == 0
        last = (row % GRID_W) == GRID_W - 1
        up_ok = jnp.logical_or(row >= GRID_W, i > 0)
        dn_ok = jnp.logical_or(row < tm - GRID_W, i < n_i - 1)

    for j in range(n_all):
        sl = slice(j * LANES, (j + 1) * LANES)
        p = p_ref[0, :, sl]
        prev = jnp.where(first, 0.0, pltpu.roll(p, 1, 0))
        nxt = jnp.where(last, 0.0, pltpu.roll(p, tm - 1, 0))
        if is_ctx:
            sh = jnp.where(cls % 2 == 0, prev, nxt)
        else:
            up = jnp.concatenate([up_ref[0, :, sl], p[:tm - GRID_W]], axis=0)
            dn = jnp.concatenate([p[GRID_W:], dn_ref[0, :, sl]], axis=0)
            up = jnp.where(up_ok, up, 0.0)
            dn = jnp.where(dn_ok, dn, 0.0)
            sh = jnp.where(cls == 0, prev, jnp.where(cls == 1, nxt, jnp.where(cls == 2, up, dn)))
        xm_scr[:, sl] = p + mu_ref[:, sl] * (sh - p)

    ones = _head_ones()
    x_w = jnp.tanh(xm_scr[:, 3 * dim:3 * dim + LANES]).astype(MXU)
    x_a = xm_scr[:, 3 * dim + LANES:3 * dim + 2 * LANES].astype(MXU)
    x_g = _sigmoid(xm_scr[:, 3 * dim + 2 * LANES:3 * dim + 4 * LANES]).astype(MXU)
    for j in range(n_dim):
        sl = slice(j * LANES, (j + 1) * LANES)
        r = xm_scr[:, sl]
        k = xm_scr[:, dim + j * LANES:dim + (j + 1) * LANES]
        v = xm_scr[:, 2 * dim + j * LANES:2 * dim + (j + 1) * LANES]
        kkr = k * kk_ref[:, sl]
        kk = kkr * lax.rsqrt(jnp.maximum(_headsum(kkr * kkr, ones), 1e-12))
        r_o[0, :, sl] = r
        v_o[0, :, sl] = v
        kk_o[0, :, sl] = kk
        gate_o[0, :, sl] = jnp.dot(x_g, lg_w_ref[:, sl], preferred_element_type=F32)
        bonus = jnp.zeros_like(r)
        for d in range(2):
            dsl = slice(d * dim + j * LANES, d * dim + (j + 1) * LANES)
            z = w0_ref[:, dsl] + jnp.dot(x_w, lw_w_ref[:, dsl], preferred_element_type=F32)
            softplus = jnp.maximum(-z, 0.0) + jnp.log(1.0 + jnp.exp(-jnp.abs(z)))
            lw_o[d, 0, :, sl] = -jnp.exp(-softplus - 0.5)
            iclr = _sigmoid(a0_ref[:, dsl] + jnp.dot(x_a, la_w_ref[:, dsl], preferred_element_type=F32))
            kd = k * (1.0 + (iclr - 1.0) * ka_ref[:, sl])
            kd_o[d, 0, :, sl] = kd
            bb_o[d, 0, :, sl] = kk * iclr
            bonus = bonus + _headsum(r * kd * rk_ref[:, sl], ones) * v
        bonus_o[0, :, sl] = bonus


def _rwkv_prep(p, mu, lw_w, la_w, lg_w, w0, a0, k_k, k_a, r_k, dim, tm, is_ctx):
    b, t, width = p.shape
    if is_ctx:
        assert t == tm
    else:
        assert tm % GRID_W == 0 and t % tm == 0
    hb = tm // GRID_W
    n_halo = t // GRID_W
    kern = functools.partial(_rwkv_prep_kernel, is_ctx=is_ctx, dim=dim)
    vec = lambda n: pl.BlockSpec((1, n), lambda bi, i: (0, 0))
    mat = lambda k, n: pl.BlockSpec((k, n), lambda bi, i: (0, 0))
    one = jax.ShapeDtypeStruct((b, t, dim), F32)
    two = jax.ShapeDtypeStruct((2, b, t, dim), F32)
    one_spec = pl.BlockSpec((1, tm, dim), lambda bi, i: (bi, i, 0))
    two_spec = pl.BlockSpec((2, 1, tm, dim), lambda bi, i: (0, bi, i, 0))
    return pl.pallas_call(
        kern,
        out_shape=(one, one, one, two, two, two, one, one),
        grid=(b, t // tm),
        in_specs=[pl.BlockSpec((1, tm, width), lambda bi, i: (bi, i, 0)),
                  pl.BlockSpec((1, GRID_W, width), lambda bi, i: (bi, jnp.maximum(i * hb - 1, 0), 0)),
                  pl.BlockSpec((1, GRID_W, width), lambda bi, i: (bi, jnp.minimum((i + 1) * hb, n_halo - 1), 0)),
                  vec(width), mat(LANES, 2 * dim), mat(LANES, 2 * dim), mat(2 * LANES, dim),
                  vec(2 * dim), vec(2 * dim), vec(dim), vec(dim), vec(dim)],
        out_specs=(one_spec, one_spec, one_spec, two_spec, two_spec, two_spec, one_spec, one_spec),
        scratch_shapes=[pltpu.VMEM((tm, width), F32)],
        compiler_params=_cparams(("parallel", "parallel")),
        name="rwkv_prep_ctx" if is_ctx else "rwkv_prep",
    )(p, p, p, mu, lw_w, la_w, lg_w, w0, a0, k_k, k_a, r_k)


def _scan_chunk_terms(r, v, kk, lw, kd, bb, sgn):
    blocks = range(len(r))
    c = r[0].shape[0]
    n = 2 * c
    ri = lax.broadcasted_iota(jnp.int32, (n, n), 0)
    ci = lax.broadcasted_iota(jnp.int32, (n, n), 1)
    same = (ri // c) == (ci // c)
    dt = ((ri % c) - (ci % c)) * sgn
    strict = jnp.logical_and(same, dt > 0)
    incl = jnp.logical_and(same, dt >= 0)
    eye = ri == ci
    head_lanes = (ri // c) == (ci // HEAD_DIM)
    zero = jnp.zeros((n, n), F32)
    time = lax.broadcasted_iota(jnp.int32, (c, LANES), 0)
    sgn_f = sgn.astype(F32)
    rev_f = 0.5 - 0.5 * sgn_f

    def stack(x):
        return jnp.concatenate([x, x], axis=0)

    def own(x):
        return jnp.where(head_lanes, stack(x), zero)

    def prefix(x):
        step = 1
        while step < c:
            x = x + jnp.where(time >= step, pltpu.roll(x, step, 0), 0.0)
            step *= 2
        return x

    tot = [jnp.sum(lw[p], axis=0, keepdims=True) for p in blocks]
    pre = [prefix(lw[p]) for p in blocks]
    cum = [rev_f * (tot[p] + lw[p]) + sgn_f * pre[p] for p in blocks]
    e_inv = [jnp.exp(-cum[p]) for p in blocks]
    e_fin = [jnp.exp(tot[p] - cum[p]) for p in blocks]
    at = [own(-kk[p] * jnp.exp(cum[p] - lw[p])) for p in blocks]
    rt = [own(r[p] * jnp.exp(cum[p])) for p in blocks]
    bt = [stack(bb[p] * e_inv[p]) for p in blocks]
    kt = [stack(kd[p] * e_inv[p]) for p in blocks]
    bhat = [own(bb[p] * e_fin[p]) for p in blocks]
    khat = [own(kd[p] * e_fin[p]) for p in blocks]
    vbd = [own(v[p]) for p in blocks]

    full = [_mm_nt(jnp.concatenate([at[p], rt[p]], axis=0), jnp.concatenate([bt[p], kt[p]], axis=0)) for p in blocks]
    a_ab = [jnp.where(strict, full[p][:n, :n], zero) for p in blocks]
    a_ak = [jnp.where(strict, full[p][:n, n:], zero) for p in blocks]
    a_rb = [jnp.where(incl, full[p][n:, :n], zero) for p in blocks]
    a_rk = [jnp.where(incl, full[p][n:, n:], zero) for p in blocks]

    ident = jnp.where(eye, 1.0, 0.0)
    inv = [ident + a_ab[p] for p in blocks]
    power = a_ab
    for _ in range(int(np.log2(c)) - 1):
        power = [_mm(power[p], power[p]) for p in blocks]
        inv = [inv[p] + _mm(inv[p], power[p]) for p in blocks]

    xv = [_mm(jnp.concatenate([a_ak[p], a_rk[p]], axis=0), vbd[p]) for p in blocks]
    w12 = [_mm(inv[p], jnp.concatenate([at[p], xv[p][:n]], axis=1)) for p in blocks]
    yw = [_mm(a_rb[p], w12[p]) for p in blocks]
    mn = [_mm_tn(bhat[p], w12[p]) for p in blocks]
    py = [rt[p] + yw[p][:, :n] for p in blocks]
    yl = [yw[p][:, n:] + xv[p][n:] for p in blocks]
    m = [jnp.where(eye, jnp.exp(tot[p]), zero) + mn[p][:, :n] for p in blocks]
    nn = [mn[p][:, n:] + _mm_tn(khat[p], vbd[p]) for p in blocks]
    return py, yl, m, nn


def _scan_kernel(r_ref, v_ref, kk_ref, lw_ref, kd_ref, bb_ref, h0_ref, y_ref, ht_ref, h_scr, *, pairs, chunks):
    d = pl.program_id(0)
    ci = pl.program_id(2)
    sgn = 1 - 2 * d
    c = SCAN_CHUNK
    n = 2 * c

    @pl.when(ci == 0)
    def _():
        h_scr[...] = h0_ref[0, 0]

    rows = [pl.ds(pl.multiple_of((d * (chunks - 1) + sgn * s) * c, c), c) for s in range(chunks)]
    lanes = [slice(j * LANES, (j + 1) * LANES) for j in range(pairs)]
    probs = [(rw, ln) for rw in rows for ln in lanes]
    py, yl, m, nn = _scan_chunk_terms(
        [r_ref[0, rw, ln] for rw, ln in probs], [v_ref[0, rw, ln] for rw, ln in probs],
        [kk_ref[0, rw, ln] for rw, ln in probs], [lw_ref[0, 0, rw, ln] for rw, ln in probs],
        [kd_ref[0, 0, rw, ln] for rw, ln in probs], [bb_ref[0, 0, rw, ln] for rw, ln in probs], sgn)

    h = [h_scr[j] for j in range(pairs)]
    for s in range(chunks):
        idx = [s * pairs + j for j in range(pairs)]
        yh = [_mm(jnp.concatenate([py[i], m[i]], axis=0), h[j]) for j, i in enumerate(idx)]
        for j, i in enumerate(idx):
            y2 = yh[j][:n] + yl[i]
            y_ref[0, 0, rows[s], lanes[j]] = y2[:c] + y2[c:]
        h = [yh[j][n:] + nn[i] for j, i in enumerate(idx)]
    for j in range(pairs):
        h_scr[j] = h[j]

    @pl.when(ci == pl.num_programs(2) - 1)
    def _():
        ht_ref[0, 0] = h_scr[...]


def _rwkv_scan(r, v, kk, lw, kd, bb, h0):
    b, t, dim = r.shape
    c = SCAN_CHUNK * SCAN_STEP_CHUNKS
    assert t % c == 0
    nch = t // c
    width = SCAN_PAIRS * LANES
    nblk = dim // width

    def chunk(d, ci):
        return ci + d * (nch - 1 - 2 * ci)

    one = pl.BlockSpec((1, c, width), lambda d, g, ci: (g // nblk, chunk(d, ci), g % nblk))
    two = pl.BlockSpec((1, 1, c, width), lambda d, g, ci: (d, g // nblk, chunk(d, ci), g % nblk))
    st = pl.BlockSpec((1, 1, SCAN_PAIRS, LANES, LANES), lambda d, g, ci: (d, g // nblk, g % nblk, 0, 0))
    return pl.pallas_call(
        functools.partial(_scan_kernel, pairs=SCAN_PAIRS, chunks=SCAN_STEP_CHUNKS),
        out_shape=(jax.ShapeDtypeStruct((2, b, t, dim), F32), jax.ShapeDtypeStruct(h0.shape, F32)),
        grid=(2, b * nblk, nch),
        in_specs=[one, one, one, two, two, two, st],
        out_specs=(two, st),
        scratch_shapes=[pltpu.VMEM((SCAN_PAIRS, LANES, LANES), F32)],
        compiler_params=_cparams(("parallel", "parallel", "arbitrary")),
        name="rwkv_scan",
    )(r, v, kk, lw, kd, bb, h0)


def _rwkv_finish_kernel(y_ref, bonus_ref, gate_ref, g_ref, b_ref, o_ref):
    ones = _head_ones()
    for j in range(o_ref.shape[2] // LANES):
        sl = slice(j * LANES, (j + 1) * LANES)
        y = y_ref[0, 0, :, sl] + y_ref[1, 0, :, sl]
        mu = _headsum(y, ones) * (1.0 / HEAD_DIM)
        dlt = y - mu
        var = _headsum(dlt * dlt, ones) * (1.0 / HEAD_DIM)
        yn = dlt * lax.rsqrt(var + LNX_EPS)
        o_ref[0, :, sl] = ((yn * g_ref[:, sl] + b_ref[:, sl] + bonus_ref[0, :, sl]) * gate_ref[0, :, sl]).astype(o_ref.dtype)


def _rwkv_finish(y, bonus, gate, lnx_g, lnx_b, tm):
    _, b, t, dim = y.shape
    one_spec = pl.BlockSpec((1, tm, dim), lambda bi, i: (bi, i, 0))
    vec = pl.BlockSpec((1, dim), lambda bi, i: (0, 0))
    return pl.pallas_call(
        _rwkv_finish_kernel,
        out_shape=jax.ShapeDtypeStruct((b, t, dim), MXU),
        grid=(b, t // tm),
        in_specs=[pl.BlockSpec((2, 1, tm, dim), lambda bi, i: (0, bi, i, 0)), one_spec, one_spec, vec, vec],
        out_specs=one_spec,
        compiler_params=_cparams(("parallel", "parallel")),
        name="rwkv_finish",
    )(y, bonus, gate, lnx_g.reshape(1, dim), lnx_b.reshape(1, dim))


def _outproj_kernel(*refs, n_in):
    x_ref, gate_ref = refs[0], refs[1]
    o_ref = refs[-1]
    acc = None
    for i in range(n_in):
        part = jnp.dot(refs[2 + 2 * i][0], refs[3 + 2 * i][...], preferred_element_type=F32)
        acc = part if acc is None else acc + part
    o_ref[0] = x_ref[0] + gate_ref[0] * acc


def _outproj(x, gate, parts, tm):
    b, t, d = x.shape
    in_specs = [pl.BlockSpec((1, tm, d), lambda bi, i: (bi, i, 0)),
                pl.BlockSpec((1, 1, d), lambda bi, i: (bi, 0, 0))]
    args = [x, gate]
    for a, w in parts:
        k = w.shape[0]
        in_specs += [pl.BlockSpec((1, tm, k), lambda bi, i: (bi, i, 0)), pl.BlockSpec((k, d), lambda bi, i: (0, 0))]
        args += [a, w]
    return pl.pallas_call(
        functools.partial(_outproj_kernel, n_in=len(parts)),
        out_shape=jax.ShapeDtypeStruct((b, t, d), F32),
        grid=(b, t // tm),
        in_specs=in_specs,
        out_specs=pl.BlockSpec((1, tm, d), lambda bi, i: (bi, i, 0)),
        compiler_params=_cparams(("parallel", "parallel")),
        name="outproj",
    )(*args)


def _mlp_kernel(x_ref, g_ref, sh_ref, sc_ref, gate_ref, w1_ref, w2_ref, fg_ref, o_ref, a_scr, *, final_norm):
    f = pl.program_id(2)

    @pl.when(f == 0)
    def _():
        a_scr[...] = _norm_mod(x_ref[0], g_ref[...], sh_ref[0], sc_ref[0]).astype(a_scr.dtype)
        o_ref[...] = jnp.zeros_like(o_ref)

    h = jnp.dot(a_scr[...], w1_ref[...], preferred_element_type=F32)
    h = jnp.square(jnp.maximum(h, 0.0))
    o_ref[0] += jnp.dot(h.astype(MXU), w2_ref[...], preferred_element_type=F32)

    @pl.when(f == pl.num_programs(2) - 1)
    def _():
        y = x_ref[0] + gate_ref[0] * o_ref[0]
        if final_norm:
            y = y * lax.rsqrt(jnp.mean(y * y, axis=-1, keepdims=True) + NORM_EPS) * fg_ref[...]
        o_ref[0] = y


def _mlp(x, g, shift, scale, gate, w1, w2, final_g, tm, final_norm):
    b, t, d = x.shape
    ff = w1.shape[1]
    tf = _col_tile(ff, MLP_MAX_HIDDEN)
    vec3 = pl.BlockSpec((1, 1, d), lambda bi, i, f: (bi, 0, 0))
    vec2 = pl.BlockSpec((1, d), lambda bi, i, f: (0, 0))
    return pl.pallas_call(
        functools.partial(_mlp_kernel, final_norm=final_norm),
        out_shape=jax.ShapeDtypeStruct((b, t, d), F32),
        grid=(b, t // tm, ff // tf),
        in_specs=[pl.BlockSpec((1, tm, d), lambda bi, i, f: (bi, i, 0)), vec2, vec3, vec3, vec3,
                  pl.BlockSpec((d, tf), lambda bi, i, f: (0, f)),
                  pl.BlockSpec((tf, d), lambda bi, i, f: (f, 0)), vec2],
        out_specs=pl.BlockSpec((1, tm, d), lambda bi, i, f: (bi, i, 0)),
        scratch_shapes=[pltpu.VMEM((tm, d), MXU)],
        compiler_params=_cparams(("parallel", "parallel", "arbitrary")),
        name="mlp",
    )(x, g.reshape(1, d), shift, scale, gate, w1, w2, final_g.reshape(1, d))


NA_GROUP_OFFSETS = (0, NA_ROWS // 2, NA_ROWS)
NA_STRIP_PAD = NA_WIN - NA_ROWS
NA_STRIP_BLOCKS = 2 * NA_WIN


def _na_window_lo(typ, local_row):
    return (0, local_row, NA_WIN - NA_ROWS)[typ]


def _na_kernel(q_ref, k_ref, v_ref, kc_ref, vc_ref, tc_ref, o_ref, bias_scr):
    g = pl.program_id(2)
    n_g = pl.num_programs(2)
    n_keys = NA_WIN * GRID_W
    n_blocks = q_ref.shape[2] // LANES
    tq = NA_GROUP * GRID_W

    @pl.when(jnp.logical_and(pl.program_id(1) == 0, g == 0))
    def _():
        key_row = _lane((GRID_W, n_keys)) // GRID_W
        for blk in range(n_blocks):
            for typ in range(3):
                for half in range(HEADS_PER_BLOCK):
                    for lr in range(NA_GROUP):
                        rho = NA_GROUP_OFFSETS[typ] + lr
                        off = (NA_WIN - 1 - rho) * GRID_W
                        strip = tc_ref[blk, half, :, off:off + n_keys]
                        lo = _na_window_lo(typ, lr)
                        ok = jnp.logical_and(key_row >= lo, key_row < lo + NA_ROWS)
                        r0 = (half * NA_GROUP + lr) * GRID_W
                        bias_scr[blk, typ, r0:r0 + GRID_W, :] = jnp.where(ok, strip, MASK_VALUE)

    win_row = jnp.clip(g * NA_GROUP - NA_ROWS // 2, 0, n_g * NA_GROUP - NA_WIN)
    typ = jnp.where(g == 0, 0, jnp.where(g == n_g - 1, 2, 1))
    start = pl.multiple_of(win_row * GRID_W, GRID_W)
    lanes = lambda blk: slice(blk * LANES, (blk + 1) * LANES)
    block = lambda h: lanes(h // HEADS_PER_BLOCK)
    chunks = [(lambda blk: k_ref[0, pl.ds(start, n_keys), lanes(blk)],
               lambda h: v_ref[0, pl.ds(start, n_keys), block(h)],
               lambda h: bias_scr[h // HEADS_PER_BLOCK, typ, pl.ds((h % HEADS_PER_BLOCK) * tq, tq), :]),
              (lambda blk: kc_ref[0, :, lanes(blk)], lambda h: vc_ref[0, :, block(h)], None)]
    outs = _softmax_pv([q_ref[0, :, lanes(blk)] for blk in range(n_blocks)], chunks, keys_on_rows=False)
    for blk, o in enumerate(outs):
        o_ref[0, :, lanes(blk)] = o.astype(o_ref.dtype)


def _na_col_table(rpb, n_heads):
    col = np.arange(GRID_W)
    c0 = np.clip(col - NA_COLS // 2, 0, GRID_W - NA_COLS)
    col_ok = (col[None, :] >= c0[:, None]) & (col[None, :] < c0[:, None] + NA_COLS)
    dcol = col[None, :] - col[:, None] + NA_COLS - 1
    onehot = (dcol[None] == np.arange(2 * NA_COLS - 1)[:, None, None]) & col_ok[None]
    tc = jnp.einsum("hrd,dqk->hqrk", rpb * LOG2E, jnp.asarray(onehot, F32), precision=lax.Precision.HIGHEST)
    tc = jnp.where(col_ok[None, :, None, :], tc, MASK_VALUE)
    n_dr = 2 * NA_ROWS - 1
    tc = jnp.pad(tc, ((0, 0), (0, 0), (NA_STRIP_PAD, NA_STRIP_BLOCKS - NA_STRIP_PAD - n_dr), (0, 0)),
                 constant_values=MASK_VALUE)
    return tc.reshape(n_heads // 2, HEADS_PER_BLOCK, GRID_W, NA_STRIP_BLOCKS * GRID_W)


def _na_attn(qkv, kvc, col_table, width):
    b, t, _ = qkv.shape
    ctx = kvc.shape[1]
    bw = NA_STEP_BLOCKS * LANES
    nb = width // bw
    tq = NA_GROUP * GRID_W
    rows = t // GRID_W
    assert rows % NA_GROUP == 0 and rows >= NA_WIN and width % bw == 0
    return pl.pallas_call(
        _na_kernel,
        out_shape=jax.ShapeDtypeStruct((b, t, width), MXU),
        grid=(nb, b, rows // NA_GROUP),
        in_specs=[pl.BlockSpec((1, tq, bw), lambda p, bi, g: (bi, g, p)),
                  pl.BlockSpec((1, t, bw), lambda p, bi, g: (bi, 0, nb + p)),
                  pl.BlockSpec((1, t, bw), lambda p, bi, g: (bi, 0, 2 * nb + p)),
                  pl.BlockSpec((1, ctx, bw), lambda p, bi, g: (bi, 0, p)),
                  pl.BlockSpec((1, ctx, bw), lambda p, bi, g: (bi, 0, nb + p)),
                  pl.BlockSpec((NA_STEP_BLOCKS, HEADS_PER_BLOCK, GRID_W, NA_STRIP_BLOCKS * GRID_W),
                               lambda p, bi, g: (p, 0, 0, 0))],
        out_specs=pl.BlockSpec((1, tq, bw), lambda p, bi, g: (bi, g, p)),
        scratch_shapes=[pltpu.VMEM((NA_STEP_BLOCKS, 3, HEADS_PER_BLOCK * tq, NA_WIN * GRID_W), F32)],
        compiler_params=_cparams(("arbitrary", "arbitrary", "arbitrary")),
        name="na_attn",
    )(qkv, qkv, qkv, kvc, kvc, col_table)


def _rope_tables(t):
    pos = jnp.arange(t, dtype=jnp.int32)
    row = (pos // GRID_W).astype(F32)
    col = (pos % GRID_W).astype(F32)
    pairs = HEAD_DIM // 4
    inv = ROPE_THETA ** (-jnp.arange(pairs, dtype=F32) / pairs)
    ang = jnp.concatenate([row[:, None] * inv, col[:, None] * inv], axis=-1)
    cos, sin = jnp.cos(ang), jnp.sin(ang)
    cos_h = jnp.concatenate([cos, cos], axis=-1)
    sin_h = jnp.concatenate([-sin, sin], axis=-1)
    return jnp.tile(cos_h, (1, HEADS_PER_BLOCK)), jnp.tile(sin_h, (1, HEADS_PER_BLOCK))


def _block_diag2(top, bottom):
    z_t = jnp.zeros_like(top)
    z_b = jnp.zeros_like(bottom)
    return jnp.concatenate([jnp.concatenate([top, z_t], axis=1), jnp.concatenate([z_b, bottom], axis=1)], axis=0)


def kernel(x, c, ctx, c_ctx, l0_norm1, l0_norm2, l0_ada_w, l0_ada_b, l0_w_in, l0_shift_mu, l0_w0_f, l0_w0_b, l0_ww2_f, l0_ww2_b, l0_a0_f, l0_a0_b, l0_wa2_f, l0_wa2_b, l0_wg2, l0_k_k, l0_k_a, l0_r_k, l0_lnx_g, l0_lnx_b, l0_q_norm, l0_k_norm, l0_w_out, l0_mlp_w1, l0_mlp_w2, l1_norm1, l1_norm2, l1_ada_w, l1_ada_b, l1_w_qkv, l1_rpb, l1_w_out, l1_mlp_w1, l1_mlp_w2, final_norm):
    b, t, d = x.shape
    n_ctx = ctx.shape[1]
    dim = l0_w0_f.shape[0]
    q_width = l0_w_out.shape[0] - dim
    gqa_cols = l0_w_in.shape[1] - l0_shift_mu.shape[0]
    kv_width = (gqa_cols - q_width) // 2
    rw_cols = l0_shift_mu.shape[0]
    rw_pad = 3 * dim + 4 * LANES
    assert rw_cols <= rw_pad and 3 * dim + 2 * LANES == rw_cols - l0_wg2.shape[0]
    tm = min(512, t)
    tm_c = n_ctx

    cc = jnp.zeros((8, d), F32).at[:b].set(c).at[b].set(c_ctx)

    def modulation(ada_w, ada_b):
        mod = _ada_mod(cc, ada_w, ada_b)
        lat = mod[:b].reshape(b, 6, 1, d)
        cx = jnp.broadcast_to(mod[b].reshape(1, 6, 1, d), (b, 6, 1, d))
        return [lat[:, i] for i in range(6)], [cx[:, i] for i in range(6)]

    mod_l, mod_c = modulation(l0_ada_w, l0_ada_b)

    w_gqa = l0_w_in[:, :gqa_cols].astype(MXU)
    w_rw = jnp.pad(l0_w_in[:, gqa_cols:], ((0, 0), (0, rw_pad - rw_cols))).astype(MXU)
    mu = jnp.pad(l0_shift_mu, (0, rw_pad - rw_cols)).reshape(1, rw_pad)
    lw_w = _block_diag2(l0_ww2_f, l0_ww2_b).astype(MXU)
    la_w = _block_diag2(l0_wa2_f, l0_wa2_b).astype(MXU)
    lg_w = jnp.pad(l0_wg2, ((0, 2 * LANES - l0_wg2.shape[0]), (0, 0))).astype(MXU)
    w0 = jnp.concatenate([l0_w0_f, l0_w0_b]).reshape(1, 2 * dim)
    a0 = jnp.concatenate([l0_a0_f, l0_a0_b]).reshape(1, 2 * dim)
    k_k = l0_k_k.reshape(1, dim)
    k_a = l0_k_a.reshape(1, dim)
    r_k = l0_r_k.reshape(1, dim)
    qg = jnp.tile(l0_q_norm, HEADS_PER_BLOCK).reshape(1, LANES)
    kg = jnp.tile(l0_k_norm, HEADS_PER_BLOCK).reshape(1, LANES)
    cos_l, sin_l = _rope_tables(t)
    cos_c, sin_c = jnp.ones((n_ctx, LANES), F32), jnp.zeros((n_ctx, LANES), F32)
    w_out_gqa = l0_w_out[:q_width].astype(MXU)
    w_out_rw = l0_w_out[q_width:].astype(MXU)

    def half_layer0(xs, mod, tm_, cos, sin, is_ctx):
        pg = _inproj(xs, l0_norm1, mod[0], mod[1], w_gqa, tm_, F32)
        pr = _inproj(xs, l0_norm1, mod[0], mod[1], w_rw, tm_, F32)
        q, kd, vd = _gqa_prep(pg, cos, sin, qg, kg, q_width, kv_width, tm_)
        prep = _rwkv_prep(pr, mu, lw_w, la_w, lg_w, w0, a0, k_k, k_a, r_k, dim, 256, is_ctx)
        return q, kd, vd, prep

    q_c, kd_c, vd_c, prep_c = half_layer0(ctx, mod_c, tm_c, cos_c, sin_c, True)
    q_l, kd_l, vd_l, prep_l = half_layer0(x, mod_l, tm, cos_l, sin_l, False)

    o_gqa_l = _gqa_attn(q_l, [(kd_l, vd_l), (kd_c, vd_c)], min(GQA_QUERY_TILE, t))
    o_gqa_c = _gqa_attn(q_c, [(kd_c, vd_c)], n_ctx)

    h0 = jnp.zeros((2, b, dim // LANES, LANES, LANES), F32)
    y_c, h_c = _rwkv_scan(*prep_c[:6], h0)
    y_l, _ = _rwkv_scan(*prep_l[:6], h_c)
    o_rw_l = _rwkv_finish(y_l, prep_l[6], prep_l[7], l0_lnx_g, l0_lnx_b, tm)
    o_rw_c = _rwkv_finish(y_c, prep_c[6], prep_c[7], l0_lnx_g, l0_lnx_b, tm_c)

    w1 = l0_mlp_w1.astype(MXU)
    w2 = l0_mlp_w2.astype(MXU)
    x = _outproj(x, mod_l[2], [(o_gqa_l, w_out_gqa), (o_rw_l, w_out_rw)], tm)
    x = _mlp(x, l0_norm2, mod_l[3], mod_l[4], mod_l[5], w1, w2, final_norm, min(MLP_ROW_TILE, t), False)
    ctx = _outproj(ctx, mod_c[2], [(o_gqa_c, w_out_gqa), (o_rw_c, w_out_rw)], tm_c)
    ctx = _mlp(ctx, l0_norm2, mod_c[3], mod_c[4], mod_c[5], w1, w2, final_norm, tm_c, False)

    mod_l, mod_c = modulation(l1_ada_w, l1_ada_b)
    width = l1_w_out.shape[0]
    n_heads = width // HEAD_DIM
    scale = jnp.concatenate([jnp.full((width,), ATTN_SCALE, F32), jnp.ones((2 * width,), F32)])
    w_qkv = (l1_w_qkv * scale).astype(MXU)
    qkv = _inproj(x, l1_norm1, mod_l[0], mod_l[1], w_qkv, min(2 * tm, t), MXU)
    kvc = _inproj(ctx, l1_norm1, mod_c[0], mod_c[1], w_qkv[:, width:], tm_c, MXU)
    o_na = _na_attn(qkv, kvc, _na_col_table(l1_rpb, n_heads), width)
    x = _outproj(x, mod_l[2], [(o_na, l1_w_out.astype(MXU))], tm)
    x = _mlp(x, l1_norm2, mod_l[3], mod_l[4], mod_l[5], l1_mlp_w1.astype(MXU), l1_mlp_w2.astype(MXU),
             final_norm, min(MLP_ROW_TILE, t), True)
    return x
```

```python
import functools

import jax
import jax.numpy as jnp
import numpy as np
from jax import lax
from jax.experimental import pallas as pl
from jax.experimental.pallas import tpu as pltpu

F32 = jnp.float32
MXU = jnp.bfloat16

LANES = 128
HEAD_DIM = 64
HEADS_PER_BLOCK = LANES // HEAD_DIM
GRID_W = 64
NORM_EPS = 1e-6
LNX_EPS = 64e-5
ROPE_THETA = 10000.0
NA_ROWS = 8
NA_COLS = 16
NA_GROUP = 4
NA_WIN = NA_GROUP + NA_ROWS
NA_STEP_BLOCKS = 4
GQA_STEP_KV_HEADS = 2
GQA_QUERY_TILE = 512
MASK_VALUE = -1e30
LOG2E = float(np.log2(np.e))
ATTN_SCALE = HEAD_DIM ** -0.5 * LOG2E
ATTN_KEY_CHUNK = 2048
INPROJ_MAX_COLS = 2048
INPROJ_RESIDENT_BYTES = 16 * 1024 * 1024
INPROJ_VMEM_BUDGET = 46 * 1024 * 1024
MLP_MAX_HIDDEN = 1024
MLP_ROW_TILE = 512
SCAN_CHUNK = 64
SCAN_PAIRS = 8
SCAN_STEP_CHUNKS = 4
VMEM_LIMIT = 56 * 1024 * 1024


def _cparams(sem):
    return pltpu.CompilerParams(dimension_semantics=sem, vmem_limit_bytes=VMEM_LIMIT)


def _mm(a, b):
    return jnp.dot(a.astype(MXU), b.astype(MXU), preferred_element_type=F32)


def _mm_nt(a, b):
    return lax.dot_general(a.astype(MXU), b.astype(MXU), (((1,), (1,)), ((), ())),
                           preferred_element_type=F32)


def _mm_tn(a, b):
    return lax.dot_general(a.astype(MXU), b.astype(MXU), (((0,), (0,)), ((), ())),
                           preferred_element_type=F32)


def _split3(x):
    hi = x.astype(MXU)
    r1 = x - hi.astype(F32)
    mid = r1.astype(MXU)
    lo = (r1 - mid.astype(F32)).astype(MXU)
    return hi, mid, lo


def _head_ones():
    r = lax.broadcasted_iota(jnp.int32, (LANES, LANES), 0) // HEAD_DIM
    c = lax.broadcasted_iota(jnp.int32, (LANES, LANES), 1) // HEAD_DIM
    return jnp.where(r == c, 1.0, 0.0).astype(MXU)


def _headsum(x, ones):
    hi, mid, lo = _split3(x)
    return (jnp.dot(hi, ones, preferred_element_type=F32) + jnp.dot(mid, ones, preferred_element_type=F32)
            + jnp.dot(lo, ones, preferred_element_type=F32))


def _lane(shape):
    return lax.broadcasted_iota(jnp.int32, shape, len(shape) - 1)


def _sigmoid(x):
    return 1.0 / (1.0 + jnp.exp(-x))


def _ada_kernel(c_ref, w_ref, b_ref, o_ref):
    c = c_ref[...]
    a = c * _sigmoid(c)
    o_ref[...] = _mm(a, w_ref[...]) + b_ref[...]


def _ada_mod(cc, w, bias):
    d, n = w.shape
    tn = 1024
    return pl.pallas_call(
        _ada_kernel,
        out_shape=jax.ShapeDtypeStruct((cc.shape[0], n), F32),
        grid=(n // tn,),
        in_specs=[pl.BlockSpec((cc.shape[0], d), lambda j: (0, 0)),
                  pl.BlockSpec((d, tn), lambda j: (0, j)),
                  pl.BlockSpec((1, tn), lambda j: (0, j))],
        out_specs=pl.BlockSpec((cc.shape[0], tn), lambda j: (0, j)),
        compiler_params=_cparams(("arbitrary",)),
        name="ada_mod",
    )(cc, w, bias.reshape(1, n))


def _norm_mod(x, g, shift, scale):
    y = x * lax.rsqrt(jnp.mean(x * x, axis=-1, keepdims=True) + NORM_EPS)
    return (y * g) * (1.0 + scale) + shift


def _inproj_kernel(x_ref, g_ref, sh_ref, sc_ref, w_ref, o_ref, a_scr):
    @pl.when(pl.program_id(2) == 0)
    def _():
        a_scr[...] = _norm_mod(x_ref[0], g_ref[...], sh_ref[0], sc_ref[0]).astype(a_scr.dtype)

    o_ref[0] = jnp.dot(a_scr[...], w_ref[...], preferred_element_type=F32).astype(o_ref.dtype)


def _col_tile(n, cap):
    return max(tn for tn in range(LANES, cap + 1, LANES) if n % tn == 0)


def _inproj(x, g, shift, scale, w, tm, out_dtype):
    b, t, d = x.shape
    n = w.shape[1]
    tn = n if d * n * w.dtype.itemsize <= INPROJ_RESIDENT_BYTES else _col_tile(n, INPROJ_MAX_COLS)
    out_bytes = jnp.dtype(out_dtype).itemsize

    def vmem(tm_):
        return 2 * (tm_ * d * 4 + d * tn * w.dtype.itemsize + tm_ * tn * out_bytes) + tm_ * d * w.dtype.itemsize

    while vmem(tm) > INPROJ_VMEM_BUDGET and tm % 16 == 0:
        tm //= 2
    return pl.pallas_call(
        _inproj_kernel,
        out_shape=jax.ShapeDtypeStruct((b, t, n), out_dtype),
        grid=(b, t // tm, n // tn),
        in_specs=[pl.BlockSpec((1, tm, d), lambda bi, i, j: (bi, i, 0)),
                  pl.BlockSpec((1, d), lambda bi, i, j: (0, 0)),
                  pl.BlockSpec((1, 1, d), lambda bi, i, j: (bi, 0, 0)),
                  pl.BlockSpec((1, 1, d), lambda bi, i, j: (bi, 0, 0)),
                  pl.BlockSpec((d, tn), lambda bi, i, j: (0, j))],
        out_specs=pl.BlockSpec((1, tm, tn), lambda bi, i, j: (bi, i, j)),
        scratch_shapes=[pltpu.VMEM((tm, d), MXU)],
        compiler_params=_cparams(("parallel", "parallel", "arbitrary")),
        name="inproj",
    )(x, g.reshape(1, d), shift, scale, w)


def _swap_half_heads(x):
    first = (_lane(x.shape) % HEAD_DIM) < HEAD_DIM // 2
    return jnp.where(first, pltpu.roll(x, LANES - HEAD_DIM // 2, 1), pltpu.roll(x, HEAD_DIM // 2, 1))


def _gqa_prep_kernel(q_ref, kv_ref, cos_ref, sin_ref, qg_ref, kg_ref, qo_ref, ko_ref, vo_ref, *, n_q, n_kv):
    ones = _head_ones()
    cos = cos_ref[...]
    sin = sin_ref[...]
    first_head = _lane(cos.shape) < HEAD_DIM

    def norm_rope(x, g):
        ms = _headsum(x * x, ones) * (1.0 / HEAD_DIM)
        y = x * lax.rsqrt(ms + NORM_EPS) * g
        return y * cos + _swap_half_heads(y) * sin

    def dup(x, half):
        rolled = pltpu.roll(x, HEAD_DIM, 1)
        return jnp.where(first_head, x, rolled) if half == 0 else jnp.where(first_head, rolled, x)

    for j in range(n_q):
        x = q_ref[0, :, j * LANES:(j + 1) * LANES]
        qo_ref[0, :, j * LANES:(j + 1) * LANES] = (norm_rope(x, qg_ref[...]) * ATTN_SCALE).astype(qo_ref.dtype)
    for j in range(n_kv):
        k = norm_rope(kv_ref[0, :, j * LANES:(j + 1) * LANES], kg_ref[...])
        vt = kv_ref[0, :, (n_kv + j) * LANES:(n_kv + j + 1) * LANES].T
        for half in range(HEADS_PER_BLOCK):
            ko_ref[0, HEADS_PER_BLOCK * j + half] = dup(k, half).astype(ko_ref.dtype)
            vo_ref[0, HEADS_PER_BLOCK * j + half] = vt[half * HEAD_DIM:(half + 1) * HEAD_DIM].astype(vo_ref.dtype)


def _gqa_prep(p, cos, sin, qg, kg, q_width, kv_width, tm):
    b, t, _ = p.shape
    n_q = q_width // LANES
    n_kv = kv_width // LANES
    kvh = kv_width // HEAD_DIM
    kern = functools.partial(_gqa_prep_kernel, n_q=n_q, n_kv=n_kv)
    kv_spec = pl.BlockSpec((1, kvh, tm, LANES), lambda bi, i: (bi, 0, i, 0))
    vt_spec = pl.BlockSpec((1, kvh, HEAD_DIM, tm), lambda bi, i: (bi, 0, 0, i))
    return pl.pallas_call(
        kern,
        out_shape=(jax.ShapeDtypeStruct((b, t, q_width), MXU),
                   jax.ShapeDtypeStruct((b, kvh, t, LANES), MXU),
                   jax.ShapeDtypeStruct((b, kvh, HEAD_DIM, t), MXU)),
        grid=(b, t // tm),
        in_specs=[pl.BlockSpec((1, tm, q_width), lambda bi, i: (bi, i, 0)),
                  pl.BlockSpec((1, tm, 2 * kv_width), lambda bi, i: (bi, i, q_width // (2 * kv_width))),
                  pl.BlockSpec((tm, LANES), lambda bi, i: (i, 0)),
                  pl.BlockSpec((tm, LANES), lambda bi, i: (i, 0)),
                  pl.BlockSpec((1, LANES), lambda bi, i: (0, 0)),
                  pl.BlockSpec((1, LANES), lambda bi, i: (0, 0))],
        out_specs=(pl.BlockSpec((1, tm, q_width), lambda bi, i: (bi, i, 0)), kv_spec, vt_spec),
        compiler_params=_cparams(("parallel", "parallel")),
        name="gqa_prep",
    )(p, p, cos, sin, qg, kg)


def _softmax_pv(q_blocks, chunks, keys_on_rows):
    first_head = _lane(q_blocks[0].shape) < HEAD_DIM
    zero = jnp.zeros_like(q_blocks[0])
    n_heads = HEADS_PER_BLOCK * len(q_blocks)
    axis = 0 if keys_on_rows else 1
    s = [[None] * len(chunks) for _ in range(n_heads)]
    m = [None] * n_heads
    l = [None] * n_heads
    acc = [None] * n_heads
    for stage in range(n_heads + 1):
        ha, hb = stage, stage - 1
        if ha < n_heads:
            q = q_blocks[ha // HEADS_PER_BLOCK]
            qa = jnp.where(first_head, q, zero) if ha % HEADS_PER_BLOCK == 0 else jnp.where(first_head, zero, q)
        for c, (load_k, load_v, load_bias) in enumerate(chunks):
            if ha < n_heads:
                k = load_k(ha // HEADS_PER_BLOCK)
                sc = _mm_nt(k, qa) if keys_on_rows else _mm_nt(qa, k)
                if load_bias is not None:
                    sc = sc + load_bias(ha)
                s[ha][c] = sc
                mc = jnp.max(sc, axis=axis, keepdims=True)
                m[ha] = mc if m[ha] is None else jnp.maximum(m[ha], mc)
            if hb >= 0:
                p = jnp.exp2(s[hb][c] - m[hb])
                s[hb][c] = None
                lc = jnp.sum(p, axis=axis, keepdims=True)
                pv = _mm(load_v(hb), p) if keys_on_rows else _mm(p, load_v(hb))
                l[hb] = lc if l[hb] is None else l[hb] + lc
                acc[hb] = pv if acc[hb] is None else acc[hb] + pv
    o = [a / d for a, d in zip(acc, l)]
    if keys_on_rows:
        return [jnp.concatenate(o[2 * j:2 * j + 2], axis=0).T for j in range(len(q_blocks))]
    return [jnp.where(first_head, o[2 * j], o[2 * j + 1]) for j in range(len(q_blocks))]


def _gqa_attn_kernel(q_ref, *refs, n_sets):
    o_ref = refs[-1]
    n_blocks = q_ref.shape[2] // LANES
    heads_per_kv = HEADS_PER_BLOCK * n_blocks // refs[0].shape[1]
    chunks = []
    for i in range(n_sets):
        k_ref, vt_ref = refs[2 * i], refs[2 * i + 1]
        n_keys = k_ref.shape[2]
        for start in range(0, n_keys, ATTN_KEY_CHUNK):
            size = min(ATTN_KEY_CHUNK, n_keys - start)
            chunks.append((lambda blk, r=k_ref, s=start, z=size: r[0, HEADS_PER_BLOCK * blk // heads_per_kv, pl.ds(s, z), :],
                           lambda h, r=vt_ref, s=start, z=size: r[0, h // heads_per_kv, :, pl.ds(s, z)], None))
    outs = _softmax_pv([q_ref[0, :, j * LANES:(j + 1) * LANES] for j in range(n_blocks)], chunks, keys_on_rows=True)
    for j, o in enumerate(outs):
        o_ref[0, :, j * LANES:(j + 1) * LANES] = o.astype(o_ref.dtype)


def _gqa_attn(q, kv_sets, tq):
    b, t, width = q.shape
    kvh = kv_sets[0][0].shape[1]
    kv_step = min(GQA_STEP_KV_HEADS, kvh)
    group_width = (width // kvh) * kv_step
    in_specs = [pl.BlockSpec((1, tq, group_width), lambda bi, g, i: (bi, i, g))]
    args = [q]
    for k, v in kv_sets:
        s = k.shape[2]
        in_specs += [pl.BlockSpec((1, kv_step, s, LANES), lambda bi, g, i: (bi, g, 0, 0)),
                     pl.BlockSpec((1, kv_step, HEAD_DIM, s), lambda bi, g, i: (bi, g, 0, 0))]
        args += [k, v]
    return pl.pallas_call(
        functools.partial(_gqa_attn_kernel, n_sets=len(kv_sets)),
        out_shape=jax.ShapeDtypeStruct((b, t, width), MXU),
        grid=(b, kvh // kv_step, t // tq),
        in_specs=in_specs,
        out_specs=pl.BlockSpec((1, tq, group_width), lambda bi, g, i: (bi, i, g)),
        compiler_params=_cparams(("parallel", "parallel", "arbitrary")),
        name="gqa_attn",
    )(*args)


def _rwkv_prep_kernel(p_ref, up_ref, dn_ref, mu_ref, lw_w_ref, la_w_ref, lg_w_ref, w0_ref, a0_ref, kk_ref, ka_ref,
                      rk_ref, r_o, v_o, kk_o, lw_o, kd_o, bb_o, bonus_o, gate_o, xm_scr, *, is_ctx, dim):
    tm = p_ref.shape[1]
    n_all = p_ref.shape[2] // LANES
    n_dim = dim // LANES
    i = pl.program_id(1)
    n_i = pl.num_programs(1)
    row = lax.broadcasted_iota(jnp.int32, (tm, LANES), 0)
    cls = _lane((tm, LANES)) % 4
    if is_ctx:
        first = row == 0
        last = row == tm - 1
    else:
        first = (row % GRID_W) == 0
        last = (row % GRID_W) == GRID_W - 1
        up_ok = jnp.logical_or(row >= GRID_W, i > 0)
        dn_ok = jnp.logical_or(row < tm - GRID_W, i < n_i - 1)

    for j in range(n_all):
        sl = slice(j * LANES, (j + 1) * LANES)
        p = p_ref[0, :, sl]
        prev = jnp.where(first, 0.0, pltpu.roll(p, 1, 0))
        nxt = jnp.where(last, 0.0, pltpu.roll(p, tm - 1, 0))
        if is_ctx:
            sh = jnp.where(cls % 2 == 0, prev, nxt)
        else:
            up = jnp.concatenate([up_ref[0, :, sl], p[:tm - GRID_W]], axis=0)
            dn = jnp.concatenate([p[GRID_W:], dn_ref[0, :, sl]], axis=0)
            up = jnp.where(up_ok, up, 0.0)
            dn = jnp.where(dn_ok, dn, 0.0)
            sh = jnp.where(cls == 0, prev, jnp.where(cls == 1, nxt, jnp.where(cls == 2, up, dn)))
        xm_scr[:, sl] = p + mu_ref[:, sl] * (sh - p)

    ones = _head_ones()
    x_w = jnp.tanh(xm_scr[:, 3 * dim:3 * dim + LANES]).astype(MXU)
    x_a = xm_scr[:, 3 * dim + LANES:3 * dim + 2 * LANES].astype(MXU)
    x_g = _sigmoid(xm_scr[:, 3 * dim + 2 * LANES:3 * dim + 4 * LANES]).astype(MXU)
    for j in range(n_dim):
        sl = slice(j * LANES, (j + 1) * LANES)
        r = xm_scr[:, sl]
        k = xm_scr[:, dim + j * LANES:dim + (j + 1) * LANES]
        v = xm_scr[:, 2 * dim + j * LANES:2 * dim + (j + 1) * LANES]
        kkr = k * kk_ref[:, sl]
        kk = kkr * lax.rsqrt(jnp.maximum(_headsum(kkr * kkr, ones), 1e-12))
        r_o[0, :, sl] = r
        v_o[0, :, sl] = v
        kk_o[0, :, sl] = kk
        gate_o[0, :, sl] = jnp.dot(x_g, lg_w_ref[:, sl], preferred_element_type=F32)
        bonus = jnp.zeros_like(r)
        for d in range(2):
            dsl = slice(d * dim + j * LANES, d * dim + (j + 1) * LANES)
            z = w0_ref[:, dsl] + jnp.dot(x_w, lw_w_ref[:, dsl], preferred_element_type=F32)
            softplus = jnp.maximum(-z, 0.0) + jnp.log(1.0 + jnp.exp(-jnp.abs(z)))
            lw_o[d, 0, :, sl] = -jnp.exp(-softplus - 0.5)
            iclr = _sigmoid(a0_ref[:, dsl] + jnp.dot(x_a, la_w_ref[:, dsl], preferred_element_type=F32))
            kd = k * (1.0 + (iclr - 1.0) * ka_ref[:, sl])
            kd_o[d, 0, :, sl] = kd
            bb_o[d, 0, :, sl] = kk * iclr
            bonus = bonus + _headsum(r * kd * rk_ref[:, sl], ones) * v
        bonus_o[0, :, sl] = bonus


def _rwkv_prep(p, mu, lw_w, la_w, lg_w, w0, a0, k_k, k_a, r_k, dim, tm, is_ctx):
    b, t, width = p.shape
    if is_ctx:
        assert t == tm
    else:
        assert tm % GRID_W == 0 and t % tm == 0
    hb = tm // GRID_W
    n_halo = t // GRID_W
    kern = functools.partial(_rwkv_prep_kernel, is_ctx=is_ctx, dim=dim)
    vec = lambda n: pl.BlockSpec((1, n), lambda bi, i: (0, 0))
    mat = lambda k, n: pl.BlockSpec((k, n), lambda bi, i: (0, 0))
    one = jax.ShapeDtypeStruct((b, t, dim), F32)
    two = jax.ShapeDtypeStruct((2, b, t, dim), F32)
    one_spec = pl.BlockSpec((1, tm, dim), lambda bi, i: (bi, i, 0))
    two_spec = pl.BlockSpec((2, 1, tm, dim), lambda bi, i: (0, bi, i, 0))
    return pl.pallas_call(
        kern,
        out_shape=(one, one, one, two, two, two, one, one),
        grid=(b, t // tm),
        in_specs=[pl.BlockSpec((1, tm, width), lambda bi, i: (bi, i, 0)),
                  pl.BlockSpec((1, GRID_W, width), lambda bi, i: (bi, jnp.maximum(i * hb - 1, 0), 0)),
                  pl.BlockSpec((1, GRID_W, width), lambda bi, i: (bi, jnp.minimum((i + 1) * hb, n_halo - 1), 0)),
                  vec(width), mat(LANES, 2 * dim), mat(LANES, 2 * dim), mat(2 * LANES, dim),
                  vec(2 * dim), vec(2 * dim), vec(dim), vec(dim), vec(dim)],
        out_specs=(one_spec, one_spec, one_spec, two_spec, two_spec, two_spec, one_spec, one_spec),
        scratch_shapes=[pltpu.VMEM((tm, width), F32)],
        compiler_params=_cparams(("parallel", "parallel")),
        name="rwkv_prep_ctx" if is_ctx else "rwkv_prep",
    )(p, p, p, mu, lw_w, la_w, lg_w, w0, a0, k_k, k_a, r_k)


def _scan_chunk_terms(r, v, kk, lw, kd, bb, sgn):
    blocks = range(len(r))
    c = r[0].shape[0]
    n = 2 * c
    ri = lax.broadcasted_iota(jnp.int32, (n, n), 0)
    ci = lax.broadcasted_iota(jnp.int32, (n, n), 1)
    same = (ri // c) == (ci // c)
    dt = ((ri % c) - (ci % c)) * sgn
    strict = jnp.logical_and(same, dt > 0)
    incl = jnp.logical_and(same, dt >= 0)
    eye = ri == ci
    head_lanes = (ri // c) == (ci // HEAD_DIM)
    zero = jnp.zeros((n, n), F32)
    time = lax.broadcasted_iota(jnp.int32, (c, LANES), 0)
    sgn_f = sgn.astype(F32)
    rev_f = 0.5 - 0.5 * sgn_f

    def stack(x):
        return jnp.concatenate([x, x], axis=0)

    def own(x):
        return jnp.where(head_lanes, stack(x), zero)

    def prefix(x):
        step = 1
        while step < c:
            x = x + jnp.where(time >= step, pltpu.roll(x, step, 0), 0.0)
            step *= 2
        return x

    tot = [jnp.sum(lw[p], axis=0, keepdims=True) for p in blocks]
    pre = [prefix(lw[p]) for p in blocks]
    cum = [rev_f * (tot[p] + lw[p]) + sgn_f * pre[p] for p in blocks]
    e_inv = [jnp.exp(-cum[p]) for p in blocks]
    e_fin = [jnp.exp(tot[p] - cum[p]) for p in blocks]
    at = [own(-kk[p] * jnp.exp(cum[p] - lw[p])) for p in blocks]
    rt = [own(r[p] * jnp.exp(cum[p])) for p in blocks]
    bt = [stack(bb[p] * e_inv[p]) for p in blocks]
    kt = [stack(kd[p] * e_inv[p]) for p in blocks]
    bhat = [own(bb[p] * e_fin[p]) for p in blocks]
    khat = [own(kd[p] * e_fin[p]) for p in blocks]
    vbd = [own(v[p]) for p in blocks]

    full = [_mm_nt(jnp.concatenate([at[p], rt[p]], axis=0), jnp.concatenate([bt[p], kt[p]], axis=0)) for p in blocks]
    a_ab = [jnp.where(strict, full[p][:n, :n], zero) for p in blocks]
    a_ak = [jnp.where(strict, full[p][:n, n:], zero) for p in blocks]
    a_rb = [jnp.where(incl, full[p][n:, :n], zero) for p in blocks]
    a_rk = [jnp.where(incl, full[p][n:, n:], zero) for p in blocks]

    ident = jnp.where(eye, 1.0, 0.0)
    inv = [ident + a_ab[p] for p in blocks]
    power = a_ab
    for _ in range(int(np.log2(c)) - 1):
        power = [_mm(power[p], power[p]) for p in blocks]
        inv = [inv[p] + _mm(inv[p], power[p]) for p in blocks]

    xv = [_mm(jnp.concatenate([a_ak[p], a_rk[p]], axis=0), vbd[p]) for p in blocks]
    w12 = [_mm(inv[p], jnp.concatenate([at[p], xv[p][:n]], axis=1)) for p in blocks]
    yw = [_mm(a_rb[p], w12[p]) for p in blocks]
    mn = [_mm_tn(bhat[p], w12[p]) for p in blocks]
    py = [rt[p] + yw[p][:, :n] for p in blocks]
    yl = [yw[p][:, n:] + xv[p][n:] for p in blocks]
    m = [jnp.where(eye, jnp.exp(tot[p]), zero) + mn[p][:, :n] for p in blocks]
    nn = [mn[p][:, n:] + _mm_tn(khat[p], vbd[p]) for p in blocks]
    return py, yl, m, nn


def _scan_kernel(r_ref, v_ref, kk_ref, lw_ref, kd_ref, bb_ref, h0_ref, y_ref, ht_ref, h_scr, *, pairs, chunks):
    d = pl.program_id(0)
    ci = pl.program_id(2)
    sgn = 1 - 2 * d
    c = SCAN_CHUNK
    n = 2 * c

    @pl.when(ci == 0)
    def _():
        h_scr[...] = h0_ref[0, 0]

    rows = [pl.ds(pl.multiple_of((d * (chunks - 1) + sgn * s) * c, c), c) for s in range(chunks)]
    lanes = [slice(j * LANES, (j + 1) * LANES) for j in range(pairs)]
    probs = [(rw, ln) for rw in rows for ln in lanes]
    py, yl, m, nn = _scan_chunk_terms(
        [r_ref[0, rw, ln] for rw, ln in probs], [v_ref[0, rw, ln] for rw, ln in probs],
        [kk_ref[0, rw, ln] for rw, ln in probs], [lw_ref[0, 0, rw, ln] for rw, ln in probs],
        [kd_ref[0, 0, rw, ln] for rw, ln in probs], [bb_ref[0, 0, rw, ln] for rw, ln in probs], sgn)

    h = [h_scr[j] for j in range(pairs)]
    for s in range(chunks):
        idx = [s * pairs + j for j in range(pairs)]
        yh = [_mm(jnp.concatenate([py[i], m[i]], axis=0), h[j]) for j, i in enumerate(idx)]
        for j, i in enumerate(idx):
            y2 = yh[j][:n] + yl[i]
            y_ref[0, 0, rows[s], lanes[j]] = y2[:c] + y2[c:]
        h = [yh[j][n:] + nn[i] for j, i in enumerate(idx)]
    for j in range(pairs):
        h_scr[j] = h[j]

    @pl.when(ci == pl.num_programs(2) - 1)
    def _():
        ht_ref[0, 0] = h_scr[...]


def _rwkv_scan(r, v, kk, lw, kd, bb, h0):
    b, t, dim = r.shape
    c = SCAN_CHUNK * SCAN_STEP_CHUNKS
    assert t % c == 0
    nch = t // c
    width = SCAN_PAIRS * LANES
    nblk = dim // width

    def chunk(d, ci):
        return ci + d * (nch - 1 - 2 * ci)

    one = pl.BlockSpec((1, c, width), lambda d, g, ci: (g // nblk, chunk(d, ci), g % nblk))
    two = pl.BlockSpec((1, 1, c, width), lambda d, g, ci: (d, g // nblk, chunk(d, ci), g % nblk))
    st = pl.BlockSpec((1, 1, SCAN_PAIRS, LANES, LANES), lambda d, g, ci: (d, g // nblk, g % nblk, 0, 0))
    return pl.pallas_call(
        functools.partial(_scan_kernel, pairs=SCAN_PAIRS, chunks=SCAN_STEP_CHUNKS),
        out_shape=(jax.ShapeDtypeStruct((2, b, t, dim), F32), jax.ShapeDtypeStruct(h0.shape, F32)),
        grid=(2, b * nblk, nch),
        in_specs=[one, one, one, two, two, two, st],
        out_specs=(two, st),
        scratch_shapes=[pltpu.VMEM((SCAN_PAIRS, LANES, LANES), F32)],
        compiler_params=_cparams(("parallel", "parallel", "arbitrary")),
        name="rwkv_scan",
    )(r, v, kk, lw, kd, bb, h0)


def _rwkv_finish_kernel(y_ref, bonus_ref, gate_ref, g_ref, b_ref, o_ref):
    ones = _head_ones()
    for j in range(o_ref.shape[2] // LANES):
        sl = slice(j * LANES, (j + 1) * LANES)
        y = y_ref[0, 0, :, sl] + y_ref[1, 0, :, sl]
        mu = _headsum(y, ones) * (1.0 / HEAD_DIM)
        dlt = y - mu
        var = _headsum(dlt * dlt, ones) * (1.0 / HEAD_DIM)
        yn = dlt * lax.rsqrt(var + LNX_EPS)
        o_ref[0, :, sl] = ((yn * g_ref[:, sl] + b_ref[:, sl] + bonus_ref[0, :, sl]) * gate_ref[0, :, sl]).astype(o_ref.dtype)


def _rwkv_finish(y, bonus, gate, lnx_g, lnx_b, tm):
    _, b, t, dim = y.shape
    one_spec = pl.BlockSpec((1, tm, dim), lambda bi, i: (bi, i, 0))
    vec = pl.BlockSpec((1, dim), lambda bi, i: (0, 0))
    return pl.pallas_call(
        _rwkv_finish_kernel,
        out_shape=jax.ShapeDtypeStruct((b, t, dim), MXU),
        grid=(b, t // tm),
        in_specs=[pl.BlockSpec((2, 1, tm, dim), lambda bi, i: (0, bi, i, 0)), one_spec, one_spec, vec, vec],
        out_specs=one_spec,
        compiler_params=_cparams(("parallel", "parallel")),
        name="rwkv_finish",
    )(y, bonus, gate, lnx_g.reshape(1, dim), lnx_b.reshape(1, dim))


def _outproj_kernel(*refs, n_in):
    x_ref, gate_ref = refs[0], refs[1]
    o_ref = refs[-1]
    acc = None
    for i in range(n_in):
        part = jnp.dot(refs[2 + 2 * i][0], refs[3 + 2 * i][...], preferred_element_type=F32)
        acc = part if acc is None else acc + part
    o_ref[0] = x_ref[0] + gate_ref[0] * acc


def _outproj(x, gate, parts, tm):
    b, t, d = x.shape
    in_specs = [pl.BlockSpec((1, tm, d), lambda bi, i: (bi, i, 0)),
                pl.BlockSpec((1, 1, d), lambda bi, i: (bi, 0, 0))]
    args = [x, gate]
    for a, w in parts:
        k = w.shape[0]
        in_specs += [pl.BlockSpec((1, tm, k), lambda bi, i: (bi, i, 0)), pl.BlockSpec((k, d), lambda bi, i: (0, 0))]
        args += [a, w]
    return pl.pallas_call(
        functools.partial(_outproj_kernel, n_in=len(parts)),
        out_shape=jax.ShapeDtypeStruct((b, t, d), F32),
        grid=(b, t // tm),
        in_specs=in_specs,
        out_specs=pl.BlockSpec((1, tm, d), lambda bi, i: (bi, i, 0)),
        compiler_params=_cparams(("parallel", "parallel")),
        name="outproj",
    )(*args)


def _mlp_kernel(x_ref, g_ref, sh_ref, sc_ref, gate_ref, w1_ref, w2_ref, fg_ref, o_ref, a_scr, *, final_norm):
    f = pl.program_id(2)

    @pl.when(f == 0)
    def _():
        a_scr[...] = _norm_mod(x_ref[0], g_ref[...], sh_ref[0], sc_ref[0]).astype(a_scr.dtype)
        o_ref[...] = jnp.zeros_like(o_ref)

    h = jnp.dot(a_scr[...], w1_ref[...], preferred_element_type=F32)
    h = jnp.square(jnp.maximum(h, 0.0))
    o_ref[0] += jnp.dot(h.astype(MXU), w2_ref[...], preferred_element_type=F32)

    @pl.when(f == pl.num_programs(2) - 1)
    def _():
        y = x_ref[0] + gate_ref[0] * o_ref[0]
        if final_norm:
            y = y * lax.rsqrt(jnp.mean(y * y, axis=-1, keepdims=True) + NORM_EPS) * fg_ref[...]
        o_ref[0] = y


def _mlp(x, g, shift, scale, gate, w1, w2, final_g, tm, final_norm):
    b, t, d = x.shape
    ff = w1.shape[1]
    tf = _col_tile(ff, MLP_MAX_HIDDEN)
    vec3 = pl.BlockSpec((1, 1, d), lambda bi, i, f: (bi, 0, 0))
    vec2 = pl.BlockSpec((1, d), lambda bi, i, f: (0, 0))
    return pl.pallas_call(
        functools.partial(_mlp_kernel, final_norm=final_norm),
        out_shape=jax.ShapeDtypeStruct((b, t, d), F32),
        grid=(b, t // tm, ff // tf),
        in_specs=[pl.BlockSpec((1, tm, d), lambda bi, i, f: (bi, i, 0)), vec2, vec3, vec3, vec3,
                  pl.BlockSpec((d, tf), lambda bi, i, f: (0, f)),
                  pl.BlockSpec((tf, d), lambda bi, i, f: (f, 0)), vec2],
        out_specs=pl.BlockSpec((1, tm, d), lambda bi, i, f: (bi, i, 0)),
        scratch_shapes=[pltpu.VMEM((tm, d), MXU)],
        compiler_params=_cparams(("parallel", "parallel", "arbitrary")),
        name="mlp",
    )(x, g.reshape(1, d), shift, scale, gate, w1, w2, final_g.reshape(1, d))


NA_GROUP_OFFSETS = (0, NA_ROWS // 2, NA_ROWS)
NA_STRIP_PAD = NA_WIN - NA_ROWS
NA_STRIP_BLOCKS = 2 * NA_WIN


def _na_window_lo(typ, local_row):
    return (0, local_row, NA_WIN - NA_ROWS)[typ]


def _na_kernel(q_ref, k_ref, v_ref, kc_ref, vc_ref, tc_ref, o_ref, bias_scr):
    g = pl.program_id(2)
    n_g = pl.num_programs(2)
    n_keys = NA_WIN * GRID_W
    n_blocks = q_ref.shape[2] // LANES
    tq = NA_GROUP * GRID_W

    @pl.when(jnp.logical_and(pl.program_id(1) == 0, g == 0))
    def _():
        key_row = _lane((GRID_W, n_keys)) // GRID_W
        for blk in range(n_blocks):
            for typ in range(3):
                for half in range(HEADS_PER_BLOCK):
                    for lr in range(NA_GROUP):
                        rho = NA_GROUP_OFFSETS[typ] + lr
                        off = (NA_WIN - 1 - rho) * GRID_W
                        strip = tc_ref[blk, half, :, off:off + n_keys]
                        lo = _na_window_lo(typ, lr)
                        ok = jnp.logical_and(key_row >= lo, key_row < lo + NA_ROWS)
                        r0 = (half * NA_GROUP + lr) * GRID_W
                        bias_scr[blk, typ, r0:r0 + GRID_W, :] = jnp.where(ok, strip, MASK_VALUE)

    win_row = jnp.clip(g * NA_GROUP - NA_ROWS // 2, 0, n_g * NA_GROUP - NA_WIN)
    typ = jnp.where(g == 0, 0, jnp.where(g == n_g - 1, 2, 1))
    start = pl.multiple_of(win_row * GRID_W, GRID_W)
    lanes = lambda blk: slice(blk * LANES, (blk + 1) * LANES)
    block = lambda h: lanes(h // HEADS_PER_BLOCK)
    chunks = [(lambda blk: k_ref[0, pl.ds(start, n_keys), lanes(blk)],
               lambda h: v_ref[0, pl.ds(start, n_keys), block(h)],
               lambda h: bias_scr[h // HEADS_PER_BLOCK, typ, pl.ds((h % HEADS_PER_BLOCK) * tq, tq), :]),
              (lambda blk: kc_ref[0, :, lanes(blk)], lambda h: vc_ref[0, :, block(h)], None)]
    outs = _softmax_pv([q_ref[0, :, lanes(blk)] for blk in range(n_blocks)], chunks, keys_on_rows=False)
    for blk, o in enumerate(outs):
        o_ref[0, :, lanes(blk)] = o.astype(o_ref.dtype)


def _na_col_table(rpb, n_heads):
    col = np.arange(GRID_W)
    c0 = np.clip(col - NA_COLS // 2, 0, GRID_W - NA_COLS)
    col_ok = (col[None, :] >= c0[:, None]) & (col[None, :] < c0[:, None] + NA_COLS)
    dcol = col[None, :] - col[:, None] + NA_COLS - 1
    onehot = (dcol[None] == np.arange(2 * NA_COLS - 1)[:, None, None]) & col_ok[None]
    tc = jnp.einsum("hrd,dqk->hqrk", rpb * LOG2E, jnp.asarray(onehot, F32), precision=lax.Precision.HIGHEST)
    tc = jnp.where(col_ok[None, :, None, :], tc, MASK_VALUE)
    n_dr = 2 * NA_ROWS - 1
    tc = jnp.pad(tc, ((0, 0), (0, 0), (NA_STRIP_PAD, NA_STRIP_BLOCKS - NA_STRIP_PAD - n_dr), (0, 0)),
                 constant_values=MASK_VALUE)
    return tc.reshape(n_heads // 2, HEADS_PER_BLOCK, GRID_W, NA_STRIP_BLOCKS * GRID_W)


def _na_attn(qkv, kvc, col_table, width):
    b, t, _ = qkv.shape
    ctx = kvc.shape[1]
    bw = NA_STEP_BLOCKS * LANES
    nb = width // bw
    tq = NA_GROUP * GRID_W
    rows = t // GRID_W
    assert rows % NA_GROUP == 0 and rows >= NA_WIN and width % bw == 0
    return pl.pallas_call(
        _na_kernel,
        out_shape=jax.ShapeDtypeStruct((b, t, width), MXU),
        grid=(nb, b, rows // NA_GROUP),
        in_specs=[pl.BlockSpec((1, tq, bw), lambda p, bi, g: (bi, g, p)),
                  pl.BlockSpec((1, t, bw), lambda p, bi, g: (bi, 0, nb + p)),
                  pl.BlockSpec((1, t, bw), lambda p, bi, g: (bi, 0, 2 * nb + p)),
                  pl.BlockSpec((1, ctx, bw), lambda p, bi, g: (bi, 0, p)),
                  pl.BlockSpec((1, ctx, bw), lambda p, bi, g: (bi, 0, nb + p)),
                  pl.BlockSpec((NA_STEP_BLOCKS, HEADS_PER_BLOCK, GRID_W, NA_STRIP_BLOCKS * GRID_W),
                               lambda p, bi, g: (p, 0, 0, 0))],
        out_specs=pl.BlockSpec((1, tq, bw), lambda p, bi, g: (bi, g, p)),
        scratch_shapes=[pltpu.VMEM((NA_STEP_BLOCKS, 3, HEADS_PER_BLOCK * tq, NA_WIN * GRID_W), F32)],
        compiler_params=_cparams(("arbitrary", "arbitrary", "arbitrary")),
        name="na_attn",
    )(qkv, qkv, qkv, kvc, kvc, col_table)


def _rope_tables(t):
    pos = jnp.arange(t, dtype=jnp.int32)
    row = (pos // GRID_W).astype(F32)
    col = (pos % GRID_W).astype(F32)
    pairs = HEAD_DIM // 4
    inv = ROPE_THETA ** (-jnp.arange(pairs, dtype=F32) / pairs)
    ang = jnp.concatenate([row[:, None] * inv, col[:, None] * inv], axis=-1)
    cos, sin = jnp.cos(ang), jnp.sin(ang)
    cos_h = jnp.concatenate([cos, cos], axis=-1)
    sin_h = jnp.concatenate([-sin, sin], axis=-1)
    return jnp.tile(cos_h, (1, HEADS_PER_BLOCK)), jnp.tile(sin_h, (1, HEADS_PER_BLOCK))


def _block_diag2(top, bottom):
    z_t = jnp.zeros_like(top)
    z_b = jnp.zeros_like(bottom)
    return jnp.concatenate([jnp.concatenate([top, z_t], axis=1), jnp.concatenate([z_b, bottom], axis=1)], axis=0)


def kernel(x, c, ctx, c_ctx, l0_norm1, l0_norm2, l0_ada_w, l0_ada_b, l0_w_in, l0_shift_mu, l0_w0_f, l0_w0_b, l0_ww2_f, l0_ww2_b, l0_a0_f, l0_a0_b, l0_wa2_f, l0_wa2_b, l0_wg2, l0_k_k, l0_k_a, l0_r_k, l0_lnx_g, l0_lnx_b, l0_q_norm, l0_k_norm, l0_w_out, l0_mlp_w1, l0_mlp_w2, l1_norm1, l1_norm2, l1_ada_w, l1_ada_b, l1_w_qkv, l1_rpb, l1_w_out, l1_mlp_w1, l1_mlp_w2, final_norm):
    b, t, d = x.shape
    n_ctx = ctx.shape[1]
    dim = l0_w0_f.shape[0]
    q_width = l0_w_out.shape[0] - dim
    gqa_cols = l0_w_in.shape[1] - l0_shift_mu.shape[0]
    kv_width = (gqa_cols - q_width) // 2
    rw_cols = l0_shift_mu.shape[0]
    rw_pad = 3 * dim + 4 * LANES
    assert rw_cols <= rw_pad and 3 * dim + 2 * LANES == rw_cols - l0_wg2.shape[0]
    tm = min(512, t)
    tm_c = n_ctx

    cc = jnp.zeros((8, d), F32).at[:b].set(c).at[b].set(c_ctx)

    def modulation(ada_w, ada_b):
        mod = _ada_mod(cc, ada_w, ada_b)
        lat = mod[:b].reshape(b, 6, 1, d)
        cx = jnp.broadcast_to(mod[b].reshape(1, 6, 1, d), (b, 6, 1, d))
        return [lat[:, i] for i in range(6)], [cx[:, i] for i in range(6)]

    mod_l, mod_c = modulation(l0_ada_w, l0_ada_b)

    w_gqa = l0_w_in[:, :gqa_cols].astype(MXU)
    w_rw = jnp.pad(l0_w_in[:, gqa_cols:], ((0, 0), (0, rw_pad - rw_cols))).astype(MXU)
    mu = jnp.pad(l0_shift_mu, (0, rw_pad - rw_cols)).reshape(1, rw_pad)
    lw_w = _block_diag2(l0_ww2_f, l0_ww2_b).astype(MXU)
    la_w = _block_diag2(l0_wa2_f, l0_wa2_b).astype(MXU)
    lg_w = jnp.pad(l0_wg2, ((0, 2 * LANES - l0_wg2.shape[0]), (0, 0))).astype(MXU)
    w0 = jnp.concatenate([l0_w0_f, l0_w0_b]).reshape(1, 2 * dim)
    a0 = jnp.concatenate([l0_a0_f, l0_a0_b]).reshape(1, 2 * dim)
    k_k = l0_k_k.reshape(1, dim)
    k_a = l0_k_a.reshape(1, dim)
    r_k = l0_r_k.reshape(1, dim)
    qg = jnp.tile(l0_q_norm, HEADS_PER_BLOCK).reshape(1, LANES)
    kg = jnp.tile(l0_k_norm, HEADS_PER_BLOCK).reshape(1, LANES)
    cos_l, sin_l = _rope_tables(t)
    cos_c, sin_c = jnp.ones((n_ctx, LANES), F32), jnp.zeros((n_ctx, LANES), F32)
    w_out_gqa = l0_w_out[:q_width].astype(MXU)
    w_out_rw = l0_w_out[q_width:].astype(MXU)

    def half_layer0(xs, mod, tm_, cos, sin, is_ctx):
        pg = _inproj(xs, l0_norm1, mod[0], mod[1], w_gqa, tm_, F32)
        pr = _inproj(xs, l0_norm1, mod[0], mod[1], w_rw, tm_, F32)
        q, kd, vd = _gqa_prep(pg, cos, sin, qg, kg, q_width, kv_width, tm_)
        prep = _rwkv_prep(pr, mu, lw_w, la_w, lg_w, w0, a0, k_k, k_a, r_k, dim, 256, is_ctx)
        return q, kd, vd, prep

    q_c, kd_c, vd_c, prep_c = half_layer0(ctx, mod_c, tm_c, cos_c, sin_c, True)
    q_l, kd_l, vd_l, prep_l = half_layer0(x, mod_l, tm, cos_l, sin_l, False)

    o_gqa_l = _gqa_attn(q_l, [(kd_l, vd_l), (kd_c, vd_c)], min(GQA_QUERY_TILE, t))
    o_gqa_c = _gqa_attn(q_c, [(kd_c, vd_c)], n_ctx)

    h0 = jnp.zeros((2, b, dim // LANES, LANES, LANES), F32)
    y_c, h_c = _rwkv_scan(*prep_c[:6], h0)
    y_l, _ = _rwkv_scan(*prep_l[:6], h_c)
    o_rw_l = _rwkv_finish(y_l, prep_l[6], prep_l[7], l0_lnx_g, l0_lnx_b, tm)
    o_rw_c = _rwkv_finish(y_c, prep_c[6], prep_c[7], l0_lnx_g, l0_lnx_b, tm_c)

    w1 = l0_mlp_w1.astype(MXU)
    w2 = l0_mlp_w2.astype(MXU)
    x = _outproj(x, mod_l[2], [(o_gqa_l, w_out_gqa), (o_rw_l, w_out_rw)], tm)
    x = _mlp(x, l0_norm2, mod_l[3], mod_l[4], mod_l[5], w1, w2, final_norm, min(MLP_ROW_TILE, t), False)
    ctx = _outproj(ctx, mod_c[2], [(o_gqa_c, w_out_gqa), (o_rw_c, w_out_rw)], tm_c)
    ctx = _mlp(ctx.reshape(1, b * n_ctx, d), l0_norm2, mod_c[3][:1], mod_c[4][:1], mod_c[5][:1], w1, w2, final_norm,
               _col_tile(b * n_ctx, MLP_ROW_TILE), False).reshape(b, n_ctx, d)

    mod_l, mod_c = modulation(l1_ada_w, l1_ada_b)
    width = l1_w_out.shape[0]
    n_heads = width // HEAD_DIM
    scale = jnp.concatenate([jnp.full((width,), ATTN_SCALE, F32), jnp.ones((2 * width,), F32)])
    w_qkv = (l1_w_qkv * scale).astype(MXU)
    qkv = _inproj(x, l1_norm1, mod_l[0], mod_l[1], w_qkv, min(2 * tm, t), MXU)
    kvc = _inproj(ctx, l1_norm1, mod_c[0], mod_c[1], w_qkv[:, width:], tm_c, MXU)
    o_na = _na_attn(qkv, kvc, _na_col_table(l1_rpb, n_heads), width)
    x = _outproj(x, mod_l[2], [(o_na, l1_w_out.astype(MXU))], tm)
    x = _mlp(x, l1_norm2, mod_l[3], mod_l[4], mod_l[5], l1_mlp_w1.astype(MXU), l1_mlp_w2.astype(MXU),
             final_norm, min(MLP_ROW_TILE, t), True)
    return x
```

```python
import functools

import jax
import jax.numpy as jnp
import numpy as np
from jax import lax
from jax.experimental import pallas as pl
from jax.experimental.pallas import tpu as pltpu

F32 = jnp.float32
MXU = jnp.bfloat16

LANES = 128
HEAD_DIM = 64
HEADS_PER_BLOCK = LANES // HEAD_DIM
GRID_W = 64
NORM_EPS = 1e-6
LNX_EPS = 64e-5
ROPE_THETA = 10000.0
NA_ROWS = 8
NA_COLS = 16
NA_GROUP = 4
NA_WIN = NA_GROUP + NA_ROWS
NA_STEP_BLOCKS = 4
GQA_STEP_KV_HEADS = 2
GQA_QUERY_TILE = 512
MASK_VALUE = -1e30
LOG2E = float(np.log2(np.e))
ATTN_SCALE = HEAD_DIM ** -0.5 * LOG2E
ATTN_KEY_CHUNK = 2048
INPROJ_MAX_COLS = 2048
INPROJ_RESIDENT_BYTES = 16 * 1024 * 1024
INPROJ_VMEM_BUDGET = 46 * 1024 * 1024
MLP_MAX_HIDDEN = 1024
MLP_ROW_TILE = 512
SCAN_CHUNK = 64
SCAN_PAIRS = 8
SCAN_STEP_CHUNKS = 4
VMEM_LIMIT = 56 * 1024 * 1024


def _cparams(sem):
    return pltpu.CompilerParams(dimension_semantics=sem, vmem_limit_bytes=VMEM_LIMIT)


def _mm(a, b):
    return jnp.dot(a.astype(MXU), b.astype(MXU), preferred_element_type=F32)


def _mm_nt(a, b):
    return lax.dot_general(a.astype(MXU), b.astype(MXU), (((1,), (1,)), ((), ())),
                           preferred_element_type=F32)


def _mm_tn(a, b):
    return lax.dot_general(a.astype(MXU), b.astype(MXU), (((0,), (0,)), ((), ())),
                           preferred_element_type=F32)


def _split3(x):
    hi = x.astype(MXU)
    r1 = x - hi.astype(F32)
    mid = r1.astype(MXU)
    lo = (r1 - mid.astype(F32)).astype(MXU)
    return hi, mid, lo


def _head_ones():
    r = lax.broadcasted_iota(jnp.int32, (LANES, LANES), 0) // HEAD_DIM
    c = lax.broadcasted_iota(jnp.int32, (LANES, LANES), 1) // HEAD_DIM
    return jnp.where(r == c, 1.0, 0.0).astype(MXU)


def _headsum(x, ones):
    hi, mid, lo = _split3(x)
    return (jnp.dot(hi, ones, preferred_element_type=F32) + jnp.dot(mid, ones, preferred_element_type=F32)
            + jnp.dot(lo, ones, preferred_element_type=F32))


def _lane(shape):
    return lax.broadcasted_iota(jnp.int32, shape, len(shape) - 1)


def _sigmoid(x):
    return 1.0 / (1.0 + jnp.exp(-x))


def _ada_kernel(c_ref, w_ref, b_ref, o_ref):
    c = c_ref[...]
    a = c * _sigmoid(c)
    o_ref[...] = _mm(a, w_ref[...]) + b_ref[...]


def _ada_mod(cc, w, bias):
    d, n = w.shape
    tn = 1024
    return pl.pallas_call(
        _ada_kernel,
        out_shape=jax.ShapeDtypeStruct((cc.shape[0], n), F32),
        grid=(n // tn,),
        in_specs=[pl.BlockSpec((cc.shape[0], d), lambda j: (0, 0)),
                  pl.BlockSpec((d, tn), lambda j: (0, j)),
                  pl.BlockSpec((1, tn), lambda j: (0, j))],
        out_specs=pl.BlockSpec((cc.shape[0], tn), lambda j: (0, j)),
        compiler_params=_cparams(("arbitrary",)),
        name="ada_mod",
    )(cc, w, bias.reshape(1, n))


def _norm_mod(x, g, shift, scale):
    y = x * lax.rsqrt(jnp.mean(x * x, axis=-1, keepdims=True) + NORM_EPS)
    return (y * g) * (1.0 + scale) + shift


def _inproj_kernel(x_ref, g_ref, sh_ref, sc_ref, w_ref, o_ref, a_scr):
    @pl.when(pl.program_id(2) == 0)
    def _():
        a_scr[...] = _norm_mod(x_ref[0], g_ref[...], sh_ref[0], sc_ref[0]).astype(a_scr.dtype)

    o_ref[0] = jnp.dot(a_scr[...], w_ref[...], preferred_element_type=F32).astype(o_ref.dtype)


def _col_tile(n, cap):
    return max(tn for tn in range(LANES, cap + 1, LANES) if n % tn == 0)


def _inproj(x, g, shift, scale, w, tm, out_dtype, col0=0, n=None):
    b, t, d = x.shape
    n = w.shape[1] if n is None else n
    if d * n * w.dtype.itemsize <= INPROJ_RESIDENT_BYTES and col0 % n == 0:
        tn = n
    else:
        tn = _col_tile(int(np.gcd(n, col0)) if col0 else n, INPROJ_MAX_COLS)
    j0 = col0 // tn
    out_bytes = jnp.dtype(out_dtype).itemsize

    def vmem(tm_):
        return 2 * (tm_ * d * 4 + d * tn * w.dtype.itemsize + tm_ * tn * out_bytes) + tm_ * d * w.dtype.itemsize

    while vmem(tm) > INPROJ_VMEM_BUDGET and tm % 16 == 0:
        tm //= 2
    return pl.pallas_call(
        _inproj_kernel,
        out_shape=jax.ShapeDtypeStruct((b, t, n), out_dtype),
        grid=(b, t // tm, n // tn),
        in_specs=[pl.BlockSpec((1, tm, d), lambda bi, i, j: (bi, i, 0)),
                  pl.BlockSpec((1, d), lambda bi, i, j: (0, 0)),
                  pl.BlockSpec((1, 1, d), lambda bi, i, j: (bi, 0, 0)),
                  pl.BlockSpec((1, 1, d), lambda bi, i, j: (bi, 0, 0)),
                  pl.BlockSpec((d, tn), lambda bi, i, j: (0, j0 + j))],
        out_specs=pl.BlockSpec((1, tm, tn), lambda bi, i, j: (bi, i, j)),
        scratch_shapes=[pltpu.VMEM((tm, d), MXU)],
        compiler_params=_cparams(("parallel", "parallel", "arbitrary")),
        name="inproj",
    )(x, g.reshape(1, d), shift, scale, w)


def _swap_half_heads(x):
    first = (_lane(x.shape) % HEAD_DIM) < HEAD_DIM // 2
    return jnp.where(first, pltpu.roll(x, LANES - HEAD_DIM // 2, 1), pltpu.roll(x, HEAD_DIM // 2, 1))


def _gqa_prep_kernel(q_ref, kv_ref, cos_ref, sin_ref, qg_ref, kg_ref, qo_ref, ko_ref, vo_ref, *, n_q, n_kv):
    ones = _head_ones()
    cos = cos_ref[...]
    sin = sin_ref[...]
    first_head = _lane(cos.shape) < HEAD_DIM

    def norm_rope(x, g):
        ms = _headsum(x * x, ones) * (1.0 / HEAD_DIM)
        y = x * lax.rsqrt(ms + NORM_EPS) * g
        return y * cos + _swap_half_heads(y) * sin

    def dup(x, half):
        rolled = pltpu.roll(x, HEAD_DIM, 1)
        return jnp.where(first_head, x, rolled) if half == 0 else jnp.where(first_head, rolled, x)

    for j in range(n_q):
        x = q_ref[0, :, j * LANES:(j + 1) * LANES]
        qo_ref[0, :, j * LANES:(j + 1) * LANES] = (norm_rope(x, qg_ref[...]) * ATTN_SCALE).astype(qo_ref.dtype)
    for j in range(n_kv):
        k = norm_rope(kv_ref[0, :, j * LANES:(j + 1) * LANES], kg_ref[...])
        vt = kv_ref[0, :, (n_kv + j) * LANES:(n_kv + j + 1) * LANES].T
        for half in range(HEADS_PER_BLOCK):
            ko_ref[0, HEADS_PER_BLOCK * j + half] = dup(k, half).astype(ko_ref.dtype)
            vo_ref[0, HEADS_PER_BLOCK * j + half] = vt[half * HEAD_DIM:(half + 1) * HEAD_DIM].astype(vo_ref.dtype)


def _gqa_prep(p, cos, sin, qg, kg, q_width, kv_width, tm):
    b, t, _ = p.shape
    n_q = q_width // LANES
    n_kv = kv_width // LANES
    kvh = kv_width // HEAD_DIM
    kern = functools.partial(_gqa_prep_kernel, n_q=n_q, n_kv=n_kv)
    kv_spec = pl.BlockSpec((1, kvh, tm, LANES), lambda bi, i: (bi, 0, i, 0))
    vt_spec = pl.BlockSpec((1, kvh, HEAD_DIM, tm), lambda bi, i: (bi, 0, 0, i))
    return pl.pallas_call(
        kern,
        out_shape=(jax.ShapeDtypeStruct((b, t, q_width), MXU),
                   jax.ShapeDtypeStruct((b, kvh, t, LANES), MXU),
                   jax.ShapeDtypeStruct((b, kvh, HEAD_DIM, t), MXU)),
        grid=(b, t // tm),
        in_specs=[pl.BlockSpec((1, tm, q_width), lambda bi, i: (bi, i, 0)),
                  pl.BlockSpec((1, tm, 2 * kv_width), lambda bi, i: (bi, i, q_width // (2 * kv_width))),
                  pl.BlockSpec((tm, LANES), lambda bi, i: (i, 0)),
                  pl.BlockSpec((tm, LANES), lambda bi, i: (i, 0)),
                  pl.BlockSpec((1, LANES), lambda bi, i: (0, 0)),
                  pl.BlockSpec((1, LANES), lambda bi, i: (0, 0))],
        out_specs=(pl.BlockSpec((1, tm, q_width), lambda bi, i: (bi, i, 0)), kv_spec, vt_spec),
        compiler_params=_cparams(("parallel", "parallel")),
        name="gqa_prep",
    )(p, p, cos, sin, qg, kg)


def _softmax_pv(q_blocks, chunks, keys_on_rows):
    first_head = _lane(q_blocks[0].shape) < HEAD_DIM
    zero = jnp.zeros_like(q_blocks[0])
    n_heads = HEADS_PER_BLOCK * len(q_blocks)
    axis = 0 if keys_on_rows else 1
    s = [[None] * len(chunks) for _ in range(n_heads)]
    m = [None] * n_heads
    l = [None] * n_heads
    acc = [None] * n_heads
    for stage in range(n_heads + 1):
        ha, hb = stage, stage - 1
        if ha < n_heads:
            q = q_blocks[ha // HEADS_PER_BLOCK]
            qa = jnp.where(first_head, q, zero) if ha % HEADS_PER_BLOCK == 0 else jnp.where(first_head, zero, q)
        for c, (load_k, load_v, load_bias) in enumerate(chunks):
            if ha < n_heads:
                k = load_k(ha // HEADS_PER_BLOCK)
                sc = _mm_nt(k, qa) if keys_on_rows else _mm_nt(qa, k)
                if load_bias is not None:
                    sc = sc + load_bias(ha)
                s[ha][c] = sc
                mc = jnp.max(sc, axis=axis, keepdims=True)
                m[ha] = mc if m[ha] is None else jnp.maximum(m[ha], mc)
            if hb >= 0:
                p = jnp.exp2(s[hb][c] - m[hb])
                s[hb][c] = None
                lc = jnp.sum(p, axis=axis, keepdims=True)
                pv = _mm(load_v(hb), p) if keys_on_rows else _mm(p, load_v(hb))
                l[hb] = lc if l[hb] is None else l[hb] + lc
                acc[hb] = pv if acc[hb] is None else acc[hb] + pv
    o = [a / d for a, d in zip(acc, l)]
    if keys_on_rows:
        return [jnp.concatenate(o[2 * j:2 * j + 2], axis=0).T for j in range(len(q_blocks))]
    return [jnp.where(first_head, o[2 * j], o[2 * j + 1]) for j in range(len(q_blocks))]


def _gqa_attn_kernel(q_ref, *refs, n_sets):
    o_ref = refs[-1]
    n_blocks = q_ref.shape[2] // LANES
    heads_per_kv = HEADS_PER_BLOCK * n_blocks // refs[0].shape[1]
    chunks = []
    for i in range(n_sets):
        k_ref, vt_ref = refs[2 * i], refs[2 * i + 1]
        n_keys = k_ref.shape[2]
        for start in range(0, n_keys, ATTN_KEY_CHUNK):
            size = min(ATTN_KEY_CHUNK, n_keys - start)
            chunks.append((lambda blk, r=k_ref, s=start, z=size: r[0, HEADS_PER_BLOCK * blk // heads_per_kv, pl.ds(s, z), :],
                           lambda h, r=vt_ref, s=start, z=size: r[0, h // heads_per_kv, :, pl.ds(s, z)], None))
    outs = _softmax_pv([q_ref[0, :, j * LANES:(j + 1) * LANES] for j in range(n_blocks)], chunks, keys_on_rows=True)
    for j, o in enumerate(outs):
        o_ref[0, :, j * LANES:(j + 1) * LANES] = o.astype(o_ref.dtype)


def _gqa_attn(q, kv_sets, tq):
    b, t, width = q.shape
    kvh = kv_sets[0][0].shape[1]
    kv_step = min(GQA_STEP_KV_HEADS, kvh)
    group_width = (width // kvh) * kv_step
    in_specs = [pl.BlockSpec((1, tq, group_width), lambda bi, g, i: (bi, i, g))]
    args = [q]
    for k, v in kv_sets:
        s = k.shape[2]
        in_specs += [pl.BlockSpec((1, kv_step, s, LANES), lambda bi, g, i: (bi, g, 0, 0)),
                     pl.BlockSpec((1, kv_step, HEAD_DIM, s), lambda bi, g, i: (bi, g, 0, 0))]
        args += [k, v]
    return pl.pallas_call(
        functools.partial(_gqa_attn_kernel, n_sets=len(kv_sets)),
        out_shape=jax.ShapeDtypeStruct((b, t, width), MXU),
        grid=(b, kvh // kv_step, t // tq),
        in_specs=in_specs,
        out_specs=pl.BlockSpec((1, tq, group_width), lambda bi, g, i: (bi, i, g)),
        compiler_params=_cparams(("parallel", "parallel", "arbitrary")),
        name="gqa_attn",
    )(*args)


def _rwkv_prep_kernel(p_ref, up_ref, dn_ref, mu_ref, lw_w_ref, la_w_ref, lg_w_ref, w0_ref, a0_ref, kk_ref, ka_ref,
                      rk_ref, r_o, v_o, kk_o, lw_o, kd_o, bb_o, bonus_o, gate_o, xm_scr, *, is_ctx, dim):
    tm = p_ref.shape[1]
    n_all = p_ref.shape[2] // LANES
    n_dim = dim // LANES
    i = pl.program_id(1)
    n_i = pl.num_programs(1)
    row = lax.broadcasted_iota(jnp.int32, (tm, LANES), 0)
    cls = _lane((tm, LANES)) % 4
    if is_ctx:
        first = row == 0
        last = row == tm - 1
    else:
        first = (row % GRID_W) == 0
        last = (row % GRID_W) == GRID_W - 1
        up_ok = jnp.logical_or(row >= GRID_W, i > 0)
        dn_ok = jnp.logical_or(row < tm - GRID_W, i < n_i - 1)

    for j in range(n_all):
        sl = slice(j * LANES, (j + 1) * LANES)
        p = p_ref[0, :, sl]
        prev = jnp.where(first, 0.0, pltpu.roll(p, 1, 0))
        nxt = jnp.where(last, 0.0, pltpu.roll(p, tm - 1, 0))
        if is_ctx:
            sh = jnp.where(cls % 2 == 0, prev, nxt)
        else:
            up = jnp.concatenate([up_ref[0, :, sl], p[:tm - GRID_W]], axis=0)
            dn = jnp.concatenate([p[GRID_W:], dn_ref[0, :, sl]], axis=0)
            up = jnp.where(up_ok, up, 0.0)
            dn = jnp.where(dn_ok, dn, 0.0)
            sh = jnp.where(cls == 0, prev, jnp.where(cls == 1, nxt, jnp.where(cls == 2, up, dn)))
        xm_scr[:, sl] = p + mu_ref[:, sl] * (sh - p)

    ones = _head_ones()
    x_w = jnp.tanh(xm_scr[:, 3 * dim:3 * dim + LANES]).astype(MXU)
    x_a = xm_scr[:, 3 * dim + LANES:3 * dim + 2 * LANES].astype(MXU)
    x_g = _sigmoid(xm_scr[:, 3 * dim + 2 * LANES:3 * dim + 4 * LANES]).astype(MXU)
    for j in range(n_dim):
        sl = slice(j * LANES, (j + 1) * LANES)
        r = xm_scr[:, sl]
        k = xm_scr[:, dim + j * LANES:dim + (j + 1) * LANES]
        v = xm_scr[:, 2 * dim + j * LANES:2 * dim + (j + 1) * LANES]
        kkr = k * kk_ref[:, sl]
        kk = kkr * lax.rsqrt(jnp.maximum(_headsum(kkr * kkr, ones), 1e-12))
        r_o[0, :, sl] = r
        v_o[0, :, sl] = v
        kk_o[0, :, sl] = kk
        gate_o[0, :, sl] = jnp.dot(x_g, lg_w_ref[:, sl], preferred_element_type=F32)
        bonus = jnp.zeros_like(r)
        for d in range(2):
            dsl = slice(d * dim + j * LANES, d * dim + (j + 1) * LANES)
            z = w0_ref[:, dsl] + jnp.dot(x_w, lw_w_ref[:, dsl], preferred_element_type=F32)
            softplus = jnp.maximum(-z, 0.0) + jnp.log(1.0 + jnp.exp(-jnp.abs(z)))
            lw_o[d, 0, :, sl] = -jnp.exp(-softplus - 0.5)
            iclr = _sigmoid(a0_ref[:, dsl] + jnp.dot(x_a, la_w_ref[:, dsl], preferred_element_type=F32))
            kd = k * (1.0 + (iclr - 1.0) * ka_ref[:, sl])
            kd_o[d, 0, :, sl] = kd
            bb_o[d, 0, :, sl] = kk * iclr
            bonus = bonus + _headsum(r * kd * rk_ref[:, sl], ones) * v
        bonus_o[0, :, sl] = bonus


def _rwkv_prep(p, mu, lw_w, la_w, lg_w, w0, a0, k_k, k_a, r_k, dim, tm, is_ctx):
    b, t, width = p.shape
    if is_ctx:
        assert t == tm
    else:
        assert tm % GRID_W == 0 and t % tm == 0
    hb = tm // GRID_W
    n_halo = t // GRID_W
    kern = functools.partial(_rwkv_prep_kernel, is_ctx=is_ctx, dim=dim)
    vec = lambda n: pl.BlockSpec((1, n), lambda bi, i: (0, 0))
    mat = lambda k, n: pl.BlockSpec((k, n), lambda bi, i: (0, 0))
    one = jax.ShapeDtypeStruct((b, t, dim), F32)
    two = jax.ShapeDtypeStruct((2, b, t, dim), F32)
    one_spec = pl.BlockSpec((1, tm, dim), lambda bi, i: (bi, i, 0))
    two_spec = pl.BlockSpec((2, 1, tm, dim), lambda bi, i: (0, bi, i, 0))
    return pl.pallas_call(
        kern,
        out_shape=(one, one, one, two, two, two, one, one),
        grid=(b, t // tm),
        in_specs=[pl.BlockSpec((1, tm, width), lambda bi, i: (bi, i, 0)),
                  pl.BlockSpec((1, GRID_W, width), lambda bi, i: (bi, jnp.maximum(i * hb - 1, 0), 0)),
                  pl.BlockSpec((1, GRID_W, width), lambda bi, i: (bi, jnp.minimum((i + 1) * hb, n_halo - 1), 0)),
                  vec(width), mat(LANES, 2 * dim), mat(LANES, 2 * dim), mat(2 * LANES, dim),
                  vec(2 * dim), vec(2 * dim), vec(dim), vec(dim), vec(dim)],
        out_specs=(one_spec, one_spec, one_spec, two_spec, two_spec, two_spec, one_spec, one_spec),
        scratch_shapes=[pltpu.VMEM((tm, width), F32)],
        compiler_params=_cparams(("parallel", "parallel")),
        name="rwkv_prep_ctx" if is_ctx else "rwkv_prep",
    )(p, p, p, mu, lw_w, la_w, lg_w, w0, a0, k_k, k_a, r_k)


def _scan_chunk_terms(r, v, kk, lw, kd, bb, sgn):
    blocks = range(len(r))
    c = r[0].shape[0]
    n = 2 * c
    ri = lax.broadcasted_iota(jnp.int32, (n, n), 0)
    ci = lax.broadcasted_iota(jnp.int32, (n, n), 1)
    same = (ri // c) == (ci // c)
    dt = ((ri % c) - (ci % c)) * sgn
    strict = jnp.logical_and(same, dt > 0)
    incl = jnp.logical_and(same, dt >= 0)
    eye = ri == ci
    head_lanes = (ri // c) == (ci // HEAD_DIM)
    zero = jnp.zeros((n, n), F32)
    time = lax.broadcasted_iota(jnp.int32, (c, LANES), 0)
    sgn_f = sgn.astype(F32)
    rev_f = 0.5 - 0.5 * sgn_f

    def stack(x):
        return jnp.concatenate([x, x], axis=0)

    def own(x):
        return jnp.where(head_lanes, stack(x), zero)

    def prefix(x):
        step = 1
        while step < c:
            x = x + jnp.where(time >= step, pltpu.roll(x, step, 0), 0.0)
            step *= 2
        return x

    tot = [jnp.sum(lw[p], axis=0, keepdims=True) for p in blocks]
    pre = [prefix(lw[p]) for p in blocks]
    cum = [rev_f * (tot[p] + lw[p]) + sgn_f * pre[p] for p in blocks]
    e_inv = [jnp.exp(-cum[p]) for p in blocks]
    e_fin = [jnp.exp(tot[p] - cum[p]) for p in blocks]
    at = [own(-kk[p] * jnp.exp(cum[p] - lw[p])) for p in blocks]
    rt = [own(r[p] * jnp.exp(cum[p])) for p in blocks]
    bt = [stack(bb[p] * e_inv[p]) for p in blocks]
    kt = [stack(kd[p] * e_inv[p]) for p in blocks]
    bhat = [own(bb[p] * e_fin[p]) for p in blocks]
    khat = [own(kd[p] * e_fin[p]) for p in blocks]
    vbd = [own(v[p]) for p in blocks]

    full = [_mm_nt(jnp.concatenate([at[p], rt[p]], axis=0), jnp.concatenate([bt[p], kt[p]], axis=0)) for p in blocks]
    a_ab = [jnp.where(strict, full[p][:n, :n], zero) for p in blocks]
    a_ak = [jnp.where(strict, full[p][:n, n:], zero) for p in blocks]
    a_rb = [jnp.where(incl, full[p][n:, :n], zero) for p in blocks]
    a_rk = [jnp.where(incl, full[p][n:, n:], zero) for p in blocks]

    ident = jnp.where(eye, 1.0, 0.0)
    inv = [ident + a_ab[p] for p in blocks]
    power = a_ab
    for _ in range(int(np.log2(c)) - 1):
        power = [_mm(power[p], power[p]) for p in blocks]
        inv = [inv[p] + _mm(inv[p], power[p]) for p in blocks]

    xv = [_mm(jnp.concatenate([a_ak[p], a_rk[p]], axis=0), vbd[p]) for p in blocks]
    w12 = [_mm(inv[p], jnp.concatenate([at[p], xv[p][:n]], axis=1)) for p in blocks]
    yw = [_mm(a_rb[p], w12[p]) for p in blocks]
    mn = [_mm_tn(bhat[p], w12[p]) for p in blocks]
    py = [rt[p] + yw[p][:, :n] for p in blocks]
    yl = [yw[p][:, n:] + xv[p][n:] for p in blocks]
    m = [jnp.where(eye, jnp.exp(tot[p]), zero) + mn[p][:, :n] for p in blocks]
    nn = [mn[p][:, n:] + _mm_tn(khat[p], vbd[p]) for p in blocks]
    return py, yl, m, nn


def _scan_kernel(r_ref, v_ref, kk_ref, lw_ref, kd_ref, bb_ref, h0_ref, y_ref, ht_ref, h_scr, *, pairs, chunks):
    d = pl.program_id(0)
    ci = pl.program_id(2)
    sgn = 1 - 2 * d
    c = SCAN_CHUNK
    n = 2 * c

    @pl.when(ci == 0)
    def _():
        h_scr[...] = h0_ref[0, 0]

    rows = [pl.ds(pl.multiple_of((d * (chunks - 1) + sgn * s) * c, c), c) for s in range(chunks)]
    lanes = [slice(j * LANES, (j + 1) * LANES) for j in range(pairs)]
    probs = [(rw, ln) for rw in rows for ln in lanes]
    py, yl, m, nn = _scan_chunk_terms(
        [r_ref[0, rw, ln] for rw, ln in probs], [v_ref[0, rw, ln] for rw, ln in probs],
        [kk_ref[0, rw, ln] for rw, ln in probs], [lw_ref[0, 0, rw, ln] for rw, ln in probs],
        [kd_ref[0, 0, rw, ln] for rw, ln in probs], [bb_ref[0, 0, rw, ln] for rw, ln in probs], sgn)

    h = [h_scr[j] for j in range(pairs)]
    for s in range(chunks):
        idx = [s * pairs + j for j in range(pairs)]
        yh = [_mm(jnp.concatenate([py[i], m[i]], axis=0), h[j]) for j, i in enumerate(idx)]
        for j, i in enumerate(idx):
            y2 = yh[j][:n] + yl[i]
            y_ref[0, 0, rows[s], lanes[j]] = y2[:c] + y2[c:]
        h = [yh[j][n:] + nn[i] for j, i in enumerate(idx)]
    for j in range(pairs):
        h_scr[j] = h[j]

    @pl.when(ci == pl.num_programs(2) - 1)
    def _():
        ht_ref[0, 0] = h_scr[...]


def _rwkv_scan(r, v, kk, lw, kd, bb, h0):
    b, t, dim = r.shape
    c = SCAN_CHUNK * SCAN_STEP_CHUNKS
    assert t % c == 0
    nch = t // c
    width = SCAN_PAIRS * LANES
    nblk = dim // width

    def chunk(d, ci):
        return ci + d * (nch - 1 - 2 * ci)

    one = pl.BlockSpec((1, c, width), lambda d, g, ci: (g // nblk, chunk(d, ci), g % nblk))
    two = pl.BlockSpec((1, 1, c, width), lambda d, g, ci: (d, g // nblk, chunk(d, ci), g % nblk))
    st = pl.BlockSpec((1, 1, SCAN_PAIRS, LANES, LANES), lambda d, g, ci: (d, g // nblk, g % nblk, 0, 0))
    return pl.pallas_call(
        functools.partial(_scan_kernel, pairs=SCAN_PAIRS, chunks=SCAN_STEP_CHUNKS),
        out_shape=(jax.ShapeDtypeStruct((2, b, t, dim), F32), jax.ShapeDtypeStruct(h0.shape, F32)),
        grid=(2, b * nblk, nch),
        in_specs=[one, one, one, two, two, two, st],
        out_specs=(two, st),
        scratch_shapes=[pltpu.VMEM((SCAN_PAIRS, LANES, LANES), F32)],
        compiler_params=_cparams(("parallel", "parallel", "arbitrary")),
        name="rwkv_scan",
    )(r, v, kk, lw, kd, bb, h0)


def _rwkv_finish_kernel(y_ref, bonus_ref, gate_ref, g_ref, b_ref, o_ref):
    ones = _head_ones()
    for j in range(o_ref.shape[2] // LANES):
        sl = slice(j * LANES, (j + 1) * LANES)
        y = y_ref[0, 0, :, sl] + y_ref[1, 0, :, sl]
        mu = _headsum(y, ones) * (1.0 / HEAD_DIM)
        dlt = y - mu
        var = _headsum(dlt * dlt, ones) * (1.0 / HEAD_DIM)
        yn = dlt * lax.rsqrt(var + LNX_EPS)
        o_ref[0, :, sl] = ((yn * g_ref[:, sl] + b_ref[:, sl] + bonus_ref[0, :, sl]) * gate_ref[0, :, sl]).astype(o_ref.dtype)


def _rwkv_finish(y, bonus, gate, lnx_g, lnx_b, tm):
    _, b, t, dim = y.shape
    one_spec = pl.BlockSpec((1, tm, dim), lambda bi, i: (bi, i, 0))
    vec = pl.BlockSpec((1, dim), lambda bi, i: (0, 0))
    return pl.pallas_call(
        _rwkv_finish_kernel,
        out_shape=jax.ShapeDtypeStruct((b, t, dim), MXU),
        grid=(b, t // tm),
        in_specs=[pl.BlockSpec((2, 1, tm, dim), lambda bi, i: (0, bi, i, 0)), one_spec, one_spec, vec, vec],
        out_specs=one_spec,
        compiler_params=_cparams(("parallel", "parallel")),
        name="rwkv_finish",
    )(y, bonus, gate, lnx_g.reshape(1, dim), lnx_b.reshape(1, dim))


def _outproj_kernel(*refs, n_in):
    x_ref, gate_ref = refs[0], refs[1]
    o_ref = refs[-1]
    acc = None
    for i in range(n_in):
        part = jnp.dot(refs[2 + 2 * i][0], refs[3 + 2 * i][...], preferred_element_type=F32)
        acc = part if acc is None else acc + part
    o_ref[0] = x_ref[0] + gate_ref[0] * acc


def _outproj(x, gate, parts, tm):
    b, t, d = x.shape
    in_specs = [pl.BlockSpec((1, tm, d), lambda bi, i: (bi, i, 0)),
                pl.BlockSpec((1, 1, d), lambda bi, i: (bi, 0, 0))]
    args = [x, gate]
    for a, w, row0 in parts:
        k = a.shape[2]
        assert row0 % k == 0
        in_specs += [pl.BlockSpec((1, tm, k), lambda bi, i: (bi, i, 0)),
                     pl.BlockSpec((k, d), lambda bi, i, r=row0 // k: (r, 0))]
        args += [a, w]
    return pl.pallas_call(
        functools.partial(_outproj_kernel, n_in=len(parts)),
        out_shape=jax.ShapeDtypeStruct((b, t, d), F32),
        grid=(b, t // tm),
        in_specs=in_specs,
        out_specs=pl.BlockSpec((1, tm, d), lambda bi, i: (bi, i, 0)),
        compiler_params=_cparams(("parallel", "parallel")),
        name="outproj",
    )(*args)


def _mlp_kernel(x_ref, g_ref, sh_ref, sc_ref, gate_ref, w1_ref, w2_ref, fg_ref, o_ref, a_scr, *, final_norm):
    f = pl.program_id(2)

    @pl.when(f == 0)
    def _():
        a_scr[...] = _norm_mod(x_ref[0], g_ref[...], sh_ref[0], sc_ref[0]).astype(a_scr.dtype)
        o_ref[...] = jnp.zeros_like(o_ref)

    h = jnp.dot(a_scr[...], w1_ref[...], preferred_element_type=F32)
    h = jnp.square(jnp.maximum(h, 0.0))
    o_ref[0] += jnp.dot(h.astype(MXU), w2_ref[...], preferred_element_type=F32)

    @pl.when(f == pl.num_programs(2) - 1)
    def _():
        y = x_ref[0] + gate_ref[0] * o_ref[0]
        if final_norm:
            y = y * lax.rsqrt(jnp.mean(y * y, axis=-1, keepdims=True) + NORM_EPS) * fg_ref[...]
        o_ref[0] = y


def _mlp(x, g, shift, scale, gate, w1, w2, final_g, tm, final_norm):
    b, t, d = x.shape
    ff = w1.shape[1]
    tf = _col_tile(ff, MLP_MAX_HIDDEN)
    vec3 = pl.BlockSpec((1, 1, d), lambda bi, i, f: (bi, 0, 0))
    vec2 = pl.BlockSpec((1, d), lambda bi, i, f: (0, 0))
    return pl.pallas_call(
        functools.partial(_mlp_kernel, final_norm=final_norm),
        out_shape=jax.ShapeDtypeStruct((b, t, d), F32),
        grid=(b, t // tm, ff // tf),
        in_specs=[pl.BlockSpec((1, tm, d), lambda bi, i, f: (bi, i, 0)), vec2, vec3, vec3, vec3,
                  pl.BlockSpec((d, tf), lambda bi, i, f: (0, f)),
                  pl.BlockSpec((tf, d), lambda bi, i, f: (f, 0)), vec2],
        out_specs=pl.BlockSpec((1, tm, d), lambda bi, i, f: (bi, i, 0)),
        scratch_shapes=[pltpu.VMEM((tm, d), MXU)],
        compiler_params=_cparams(("parallel", "parallel", "arbitrary")),
        name="mlp",
    )(x, g.reshape(1, d), shift, scale, gate, w1, w2, final_g.reshape(1, d))


NA_GROUP_OFFSETS = (0, NA_ROWS // 2, NA_ROWS)
NA_STRIP_PAD = NA_WIN - NA_ROWS
NA_STRIP_BLOCKS = 2 * NA_WIN


def _na_window_lo(typ, local_row):
    return (0, local_row, NA_WIN - NA_ROWS)[typ]


def _na_kernel(q_ref, k_ref, v_ref, kc_ref, vc_ref, tc_ref, o_ref, bias_scr):
    g = pl.program_id(2)
    n_g = pl.num_programs(2)
    n_keys = NA_WIN * GRID_W
    n_blocks = q_ref.shape[2] // LANES
    tq = NA_GROUP * GRID_W

    @pl.when(jnp.logical_and(pl.program_id(1) == 0, g == 0))
    def _():
        key_row = _lane((GRID_W, n_keys)) // GRID_W
        for blk in range(n_blocks):
            for typ in range(3):
                for half in range(HEADS_PER_BLOCK):
                    for lr in range(NA_GROUP):
                        rho = NA_GROUP_OFFSETS[typ] + lr
                        off = (NA_WIN - 1 - rho) * GRID_W
                        strip = tc_ref[blk, half, :, off:off + n_keys]
                        lo = _na_window_lo(typ, lr)
                        ok = jnp.logical_and(key_row >= lo, key_row < lo + NA_ROWS)
                        r0 = (half * NA_GROUP + lr) * GRID_W
                        bias_scr[blk, typ, r0:r0 + GRID_W, :] = jnp.where(ok, strip, MASK_VALUE)

    win_row = jnp.clip(g * NA_GROUP - NA_ROWS // 2, 0, n_g * NA_GROUP - NA_WIN)
    typ = jnp.where(g == 0, 0, jnp.where(g == n_g - 1, 2, 1))
    start = pl.multiple_of(win_row * GRID_W, GRID_W)
    lanes = lambda blk: slice(blk * LANES, (blk + 1) * LANES)
    block = lambda h: lanes(h // HEADS_PER_BLOCK)
    chunks = [(lambda blk: k_ref[0, pl.ds(start, n_keys), lanes(blk)],
               lambda h: v_ref[0, pl.ds(start, n_keys), block(h)],
               lambda h: bias_scr[h // HEADS_PER_BLOCK, typ, pl.ds((h % HEADS_PER_BLOCK) * tq, tq), :]),
              (lambda blk: kc_ref[0, :, lanes(blk)], lambda h: vc_ref[0, :, block(h)], None)]
    outs = _softmax_pv([q_ref[0, :, lanes(blk)] for blk in range(n_blocks)], chunks, keys_on_rows=False)
    for blk, o in enumerate(outs):
        o_ref[0, :, lanes(blk)] = o.astype(o_ref.dtype)


def _na_col_table(rpb, n_heads):
    col = np.arange(GRID_W)
    c0 = np.clip(col - NA_COLS // 2, 0, GRID_W - NA_COLS)
    col_ok = (col[None, :] >= c0[:, None]) & (col[None, :] < c0[:, None] + NA_COLS)
    dcol = col[None, :] - col[:, None] + NA_COLS - 1
    n_dr, n_dc = 2 * NA_ROWS - 1, 2 * NA_COLS - 1
    onehot = np.concatenate([(dcol[None] == np.arange(n_dc)[:, None, None]) & col_ok[None],
                             ~col_ok[None], np.ones((1, GRID_W, GRID_W), bool)])
    vals = jnp.concatenate([rpb * LOG2E, jnp.full((n_heads, n_dr, 1), MASK_VALUE, F32),
                            jnp.zeros((n_heads, n_dr, 1), F32)], axis=2)
    pad_rows = jnp.zeros((n_heads, 1, n_dc + 2), F32).at[:, :, n_dc + 1].set(MASK_VALUE)
    vals = jnp.concatenate([jnp.tile(pad_rows, (1, NA_STRIP_PAD, 1)), vals,
                            jnp.tile(pad_rows, (1, NA_STRIP_BLOCKS - NA_STRIP_PAD - n_dr, 1))], axis=1)
    tc = jnp.einsum("hrd,dqk->hqrk", vals, jnp.asarray(onehot, F32), precision=lax.Precision.HIGHEST)
    return tc.reshape(n_heads // 2, HEADS_PER_BLOCK, GRID_W, NA_STRIP_BLOCKS * GRID_W)


def _na_attn(qkv, kvc, col_table, width):
    b, t, _ = qkv.shape
    ctx = kvc.shape[1]
    bw = NA_STEP_BLOCKS * LANES
    nb = width // bw
    tq = NA_GROUP * GRID_W
    rows = t // GRID_W
    assert rows % NA_GROUP == 0 and rows >= NA_WIN and width % bw == 0
    return pl.pallas_call(
        _na_kernel,
        out_shape=jax.ShapeDtypeStruct((b, t, width), MXU),
        grid=(nb, b, rows // NA_GROUP),
        in_specs=[pl.BlockSpec((1, tq, bw), lambda p, bi, g: (bi, g, p)),
                  pl.BlockSpec((1, t, bw), lambda p, bi, g: (bi, 0, nb + p)),
                  pl.BlockSpec((1, t, bw), lambda p, bi, g: (bi, 0, 2 * nb + p)),
                  pl.BlockSpec((1, ctx, bw), lambda p, bi, g: (bi, 0, p)),
                  pl.BlockSpec((1, ctx, bw), lambda p, bi, g: (bi, 0, nb + p)),
                  pl.BlockSpec((NA_STEP_BLOCKS, HEADS_PER_BLOCK, GRID_W, NA_STRIP_BLOCKS * GRID_W),
                               lambda p, bi, g: (p, 0, 0, 0))],
        out_specs=pl.BlockSpec((1, tq, bw), lambda p, bi, g: (bi, g, p)),
        scratch_shapes=[pltpu.VMEM((NA_STEP_BLOCKS, 3, HEADS_PER_BLOCK * tq, NA_WIN * GRID_W), F32)],
        compiler_params=_cparams(("arbitrary", "arbitrary", "arbitrary")),
        name="na_attn",
    )(qkv, qkv, qkv, kvc, kvc, col_table)


def _rope_tables(t):
    pos = jnp.arange(t, dtype=jnp.int32)
    row = (pos // GRID_W).astype(F32)
    col = (pos % GRID_W).astype(F32)
    pairs = HEAD_DIM // 4
    inv = ROPE_THETA ** (-jnp.arange(pairs, dtype=F32) / pairs)
    ang = jnp.concatenate([row[:, None] * inv, col[:, None] * inv], axis=-1)
    cos, sin = jnp.cos(ang), jnp.sin(ang)
    cos_h = jnp.concatenate([cos, cos], axis=-1)
    sin_h = jnp.concatenate([-sin, sin], axis=-1)
    return jnp.tile(cos_h, (1, HEADS_PER_BLOCK)), jnp.tile(sin_h, (1, HEADS_PER_BLOCK))


def _block_diag2(top, bottom):
    z_t = jnp.zeros_like(top)
    z_b = jnp.zeros_like(bottom)
    return jnp.concatenate([jnp.concatenate([top, z_t], axis=1), jnp.concatenate([z_b, bottom], axis=1)], axis=0)


def kernel(x, c, ctx, c_ctx, l0_norm1, l0_norm2, l0_ada_w, l0_ada_b, l0_w_in, l0_shift_mu, l0_w0_f, l0_w0_b, l0_ww2_f, l0_ww2_b, l0_a0_f, l0_a0_b, l0_wa2_f, l0_wa2_b, l0_wg2, l0_k_k, l0_k_a, l0_r_k, l0_lnx_g, l0_lnx_b, l0_q_norm, l0_k_norm, l0_w_out, l0_mlp_w1, l0_mlp_w2, l1_norm1, l1_norm2, l1_ada_w, l1_ada_b, l1_w_qkv, l1_rpb, l1_w_out, l1_mlp_w1, l1_mlp_w2, final_norm):
    b, t, d = x.shape
    n_ctx = ctx.shape[1]
    dim = l0_w0_f.shape[0]
    q_width = l0_w_out.shape[0] - dim
    gqa_cols = l0_w_in.shape[1] - l0_shift_mu.shape[0]
    kv_width = (gqa_cols - q_width) // 2
    rw_cols = l0_shift_mu.shape[0]
    rw_pad = 3 * dim + 4 * LANES
    assert rw_cols <= rw_pad and 3 * dim + 2 * LANES == rw_cols - l0_wg2.shape[0]
    tm = min(512, t)
    tm_c = n_ctx

    cc = jnp.zeros((8, d), F32).at[:b].set(c).at[b].set(c_ctx)

    def modulation(ada_w, ada_b):
        mod = _ada_mod(cc, ada_w, ada_b)
        lat = mod[:b].reshape(b, 6, 1, d)
        cx = jnp.broadcast_to(mod[b].reshape(1, 6, 1, d), (b, 6, 1, d))
        return [lat[:, i] for i in range(6)], [cx[:, i] for i in range(6)]

    mod_l, mod_c = modulation(l0_ada_w, l0_ada_b)

    w_in = l0_w_in.astype(MXU)
    w_rw = jnp.pad(w_in[:, gqa_cols:], ((0, 0), (0, rw_pad - rw_cols)))
    mu = jnp.pad(l0_shift_mu, (0, rw_pad - rw_cols)).reshape(1, rw_pad)
    lw_w = _block_diag2(l0_ww2_f, l0_ww2_b).astype(MXU)
    la_w = _block_diag2(l0_wa2_f, l0_wa2_b).astype(MXU)
    lg_w = jnp.pad(l0_wg2, ((0, 2 * LANES - l0_wg2.shape[0]), (0, 0))).astype(MXU)
    w0 = jnp.concatenate([l0_w0_f, l0_w0_b]).reshape(1, 2 * dim)
    a0 = jnp.concatenate([l0_a0_f, l0_a0_b]).reshape(1, 2 * dim)
    k_k = l0_k_k.reshape(1, dim)
    k_a = l0_k_a.reshape(1, dim)
    r_k = l0_r_k.reshape(1, dim)
    qg = jnp.tile(l0_q_norm, HEADS_PER_BLOCK).reshape(1, LANES)
    kg = jnp.tile(l0_k_norm, HEADS_PER_BLOCK).reshape(1, LANES)
    cos_l, sin_l = _rope_tables(t)
    cos_c, sin_c = jnp.ones((n_ctx, LANES), F32), jnp.zeros((n_ctx, LANES), F32)
    w_out = l0_w_out.astype(MXU)

    def half_layer0(xs, mod, tm_, cos, sin, is_ctx):
        pg = _inproj(xs, l0_norm1, mod[0], mod[1], w_in, tm_, F32, col0=0, n=gqa_cols)
        pr = _inproj(xs, l0_norm1, mod[0], mod[1], w_rw, tm_, F32)
        q, kd, vd = _gqa_prep(pg, cos, sin, qg, kg, q_width, kv_width, tm_)
        prep = _rwkv_prep(pr, mu, lw_w, la_w, lg_w, w0, a0, k_k, k_a, r_k, dim, 256, is_ctx)
        return q, kd, vd, prep

    q_c, kd_c, vd_c, prep_c = half_layer0(ctx, mod_c, tm_c, cos_c, sin_c, True)
    q_l, kd_l, vd_l, prep_l = half_layer0(x, mod_l, tm, cos_l, sin_l, False)

    o_gqa_l = _gqa_attn(q_l, [(kd_l, vd_l), (kd_c, vd_c)], min(GQA_QUERY_TILE, t))
    o_gqa_c = _gqa_attn(q_c, [(kd_c, vd_c)], n_ctx)

    h0 = jnp.zeros((2, b, dim // LANES, LANES, LANES), F32)
    y_c, h_c = _rwkv_scan(*prep_c[:6], h0)
    y_l, _ = _rwkv_scan(*prep_l[:6], h_c)
    o_rw_l = _rwkv_finish(y_l, prep_l[6], prep_l[7], l0_lnx_g, l0_lnx_b, tm)
    o_rw_c = _rwkv_finish(y_c, prep_c[6], prep_c[7], l0_lnx_g, l0_lnx_b, tm_c)

    w1 = l0_mlp_w1.astype(MXU)
    w2 = l0_mlp_w2.astype(MXU)
    x = _outproj(x, mod_l[2], [(o_gqa_l, w_out, 0), (o_rw_l, w_out, q_width)], tm)
    x = _mlp(x, l0_norm2, mod_l[3], mod_l[4], mod_l[5], w1, w2, final_norm, min(MLP_ROW_TILE, t), False)
    ctx = _outproj(ctx, mod_c[2], [(o_gqa_c, w_out, 0), (o_rw_c, w_out, q_width)], tm_c)
    ctx = _mlp(ctx.reshape(1, b * n_ctx, d), l0_norm2, mod_c[3][:1], mod_c[4][:1], mod_c[5][:1], w1, w2, final_norm,
               _col_tile(b * n_ctx, MLP_ROW_TILE), False).reshape(b, n_ctx, d)

    mod_l, mod_c = modulation(l1_ada_w, l1_ada_b)
    width = l1_w_out.shape[0]
    n_heads = width // HEAD_DIM
    scale = jnp.concatenate([jnp.full((width,), ATTN_SCALE, F32), jnp.ones((2 * width,), F32)])
    w_qkv = (l1_w_qkv * scale).astype(MXU)
    qkv = _inproj(x, l1_norm1, mod_l[0], mod_l[1], w_qkv, min(2 * tm, t), MXU)
    kvc = _inproj(ctx, l1_norm1, mod_c[0], mod_c[1], w_qkv, tm_c, MXU, col0=width, n=2 * width)
    o_na = _na_attn(qkv, kvc, _na_col_table(l1_rpb, n_heads), width)
    x = _outproj(x, mod_l[2], [(o_na, l1_w_out.astype(MXU), 0)], tm)
    x = _mlp(x, l1_norm2, mod_l[3], mod_l[4], mod_l[5], l1_mlp_w1.astype(MXU), l1_mlp_w2.astype(MXU),
             final_norm, min(MLP_ROW_TILE, t), True)
    return x
```

```python
import functools

import jax
import jax.numpy as jnp
import numpy as np
from jax import lax
from jax.experimental import pallas as pl
from jax.experimental.pallas import tpu as pltpu

F32 = jnp.float32
MXU = jnp.bfloat16

LANES = 128
HEAD_DIM = 64
HEADS_PER_BLOCK = LANES // HEAD_DIM
GRID_W = 64
NORM_EPS = 1e-6
LNX_EPS = 64e-5
ROPE_THETA = 10000.0
NA_ROWS = 8
NA_COLS = 16
NA_GROUP = 4
NA_WIN = NA_GROUP + NA_ROWS
NA_STEP_BLOCKS = 4
GQA_STEP_KV_HEADS = 2
GQA_QUERY_TILE = 512
MASK_VALUE = -1e30
LOG2E = float(np.log2(np.e))
ATTN_SCALE = HEAD_DIM ** -0.5 * LOG2E
ATTN_KEY_CHUNK = 2048
INPROJ_MAX_COLS = 2048
INPROJ_RESIDENT_BYTES = 16 * 1024 * 1024
INPROJ_VMEM_BUDGET = 46 * 1024 * 1024
MLP_MAX_HIDDEN = 1024
MLP_ROW_TILE = 512
SCAN_CHUNK = 64
SCAN_PAIRS = 8
SCAN_STEP_CHUNKS = 4
VMEM_LIMIT = 56 * 1024 * 1024


def _cparams(sem):
    return pltpu.CompilerParams(dimension_semantics=sem, vmem_limit_bytes=VMEM_LIMIT)


def _mm(a, b):
    return jnp.dot(a.astype(MXU), b.astype(MXU), preferred_element_type=F32)


def _mm_nt(a, b):
    return lax.dot_general(a.astype(MXU), b.astype(MXU), (((1,), (1,)), ((), ())),
                           preferred_element_type=F32)


def _mm_tn(a, b):
    return lax.dot_general(a.astype(MXU), b.astype(MXU), (((0,), (0,)), ((), ())),
                           preferred_element_type=F32)


def _split3(x):
    hi = x.astype(MXU)
    r1 = x - hi.astype(F32)
    mid = r1.astype(MXU)
    lo = (r1 - mid.astype(F32)).astype(MXU)
    return hi, mid, lo


def _head_ones():
    r = lax.broadcasted_iota(jnp.int32, (LANES, LANES), 0) // HEAD_DIM
    c = lax.broadcasted_iota(jnp.int32, (LANES, LANES), 1) // HEAD_DIM
    return jnp.where(r == c, 1.0, 0.0).astype(MXU)


def _headsum(x, ones):
    hi, mid, lo = _split3(x)
    return (jnp.dot(hi, ones, preferred_element_type=F32) + jnp.dot(mid, ones, preferred_element_type=F32)
            + jnp.dot(lo, ones, preferred_element_type=F32))


def _lane(shape):
    return lax.broadcasted_iota(jnp.int32, shape, len(shape) - 1)


def _sigmoid(x):
    return 1.0 / (1.0 + jnp.exp(-x))


def _ada_kernel(c_ref, w_ref, b_ref, o_ref):
    c = c_ref[...]
    a = c * _sigmoid(c)
    o_ref[...] = _mm(a, w_ref[...]) + b_ref[...]


def _ada_mod(cc, w, bias):
    d, n = w.shape
    tn = 1024
    return pl.pallas_call(
        _ada_kernel,
        out_shape=jax.ShapeDtypeStruct((cc.shape[0], n), F32),
        grid=(n // tn,),
        in_specs=[pl.BlockSpec((cc.shape[0], d), lambda j: (0, 0)),
                  pl.BlockSpec((d, tn), lambda j: (0, j)),
                  pl.BlockSpec((1, tn), lambda j: (0, j))],
        out_specs=pl.BlockSpec((cc.shape[0], tn), lambda j: (0, j)),
        compiler_params=_cparams(("arbitrary",)),
        name="ada_mod",
    )(cc, w, bias.reshape(1, n))


def _norm_mod(x, g, shift, scale):
    y = x * lax.rsqrt(jnp.mean(x * x, axis=-1, keepdims=True) + NORM_EPS)
    return (y * g) * (1.0 + scale) + shift


def _inproj_kernel(x_ref, g_ref, sh_ref, sc_ref, w_ref, o_ref, a_scr):
    @pl.when(pl.program_id(2) == 0)
    def _():
        a_scr[...] = _norm_mod(x_ref[0], g_ref[...], sh_ref[0], sc_ref[0]).astype(a_scr.dtype)

    o_ref[0] = jnp.dot(a_scr[...], w_ref[...], preferred_element_type=F32).astype(o_ref.dtype)


def _col_tile(n, cap):
    return max(tn for tn in range(LANES, cap + 1, LANES) if n % tn == 0)


def _inproj(x, g, shift, scale, w, tm, out_dtype, col0=0, n=None):
    b, t, d = x.shape
    n = w.shape[1] if n is None else n
    if d * n * w.dtype.itemsize <= INPROJ_RESIDENT_BYTES and col0 % n == 0:
        tn = n
    else:
        tn = _col_tile(int(np.gcd(n, col0)) if col0 else n, INPROJ_MAX_COLS)
    j0 = col0 // tn
    out_bytes = jnp.dtype(out_dtype).itemsize

    def vmem(tm_):
        return 2 * (tm_ * d * 4 + d * tn * w.dtype.itemsize + tm_ * tn * out_bytes) + tm_ * d * w.dtype.itemsize

    while vmem(tm) > INPROJ_VMEM_BUDGET and tm % 16 == 0:
        tm //= 2
    return pl.pallas_call(
        _inproj_kernel,
        out_shape=jax.ShapeDtypeStruct((b, t, n), out_dtype),
        grid=(b, t // tm, n // tn),
        in_specs=[pl.BlockSpec((1, tm, d), lambda bi, i, j: (bi, i, 0)),
                  pl.BlockSpec((1, d), lambda bi, i, j: (0, 0)),
                  pl.BlockSpec((1, 1, d), lambda bi, i, j: (bi, 0, 0)),
                  pl.BlockSpec((1, 1, d), lambda bi, i, j: (bi, 0, 0)),
                  pl.BlockSpec((d, tn), lambda bi, i, j: (0, j0 + j))],
        out_specs=pl.BlockSpec((1, tm, tn), lambda bi, i, j: (bi, i, j)),
        scratch_shapes=[pltpu.VMEM((tm, d), MXU)],
        compiler_params=_cparams(("parallel", "parallel", "arbitrary")),
        name="inproj",
    )(x, g.reshape(1, d), shift, scale, w)


def _swap_half_heads(x):
    first = (_lane(x.shape) % HEAD_DIM) < HEAD_DIM // 2
    return jnp.where(first, pltpu.roll(x, LANES - HEAD_DIM // 2, 1), pltpu.roll(x, HEAD_DIM // 2, 1))


def _gqa_prep_kernel(q_ref, kv_ref, cos_ref, sin_ref, qg_ref, kg_ref, qo_ref, ko_ref, vo_ref, *, n_q, n_kv):
    ones = _head_ones()
    cos = cos_ref[...]
    sin = sin_ref[...]
    first_head = _lane(cos.shape) < HEAD_DIM

    def norm_rope(x, g):
        ms = _headsum(x * x, ones) * (1.0 / HEAD_DIM)
        y = x * lax.rsqrt(ms + NORM_EPS) * g
        return y * cos + _swap_half_heads(y) * sin

    def dup(x, half):
        rolled = pltpu.roll(x, HEAD_DIM, 1)
        return jnp.where(first_head, x, rolled) if half == 0 else jnp.where(first_head, rolled, x)

    for j in range(n_q):
        x = q_ref[0, :, j * LANES:(j + 1) * LANES]
        qo_ref[0, :, j * LANES:(j + 1) * LANES] = (norm_rope(x, qg_ref[...]) * ATTN_SCALE).astype(qo_ref.dtype)
    for j in range(n_kv):
        k = norm_rope(kv_ref[0, :, j * LANES:(j + 1) * LANES], kg_ref[...])
        vt = kv_ref[0, :, (n_kv + j) * LANES:(n_kv + j + 1) * LANES].T
        for half in range(HEADS_PER_BLOCK):
            ko_ref[0, HEADS_PER_BLOCK * j + half] = dup(k, half).astype(ko_ref.dtype)
            vo_ref[0, HEADS_PER_BLOCK * j + half] = vt[half * HEAD_DIM:(half + 1) * HEAD_DIM].astype(vo_ref.dtype)


def _gqa_prep(p, cos, sin, qg, kg, q_width, kv_width, tm):
    b, t, _ = p.shape
    n_q = q_width // LANES
    n_kv = kv_width // LANES
    kvh = kv_width // HEAD_DIM
    kern = functools.partial(_gqa_prep_kernel, n_q=n_q, n_kv=n_kv)
    kv_spec = pl.BlockSpec((1, kvh, tm, LANES), lambda bi, i: (bi, 0, i, 0))
    vt_spec = pl.BlockSpec((1, kvh, HEAD_DIM, tm), lambda bi, i: (bi, 0, 0, i))
    return pl.pallas_call(
        kern,
        out_shape=(jax.ShapeDtypeStruct((b, t, q_width), MXU),
                   jax.ShapeDtypeStruct((b, kvh, t, LANES), MXU),
                   jax.ShapeDtypeStruct((b, kvh, HEAD_DIM, t), MXU)),
        grid=(b, t // tm),
        in_specs=[pl.BlockSpec((1, tm, q_width), lambda bi, i: (bi, i, 0)),
                  pl.BlockSpec((1, tm, 2 * kv_width), lambda bi, i: (bi, i, q_width // (2 * kv_width))),
                  pl.BlockSpec((tm, LANES), lambda bi, i: (i, 0)),
                  pl.BlockSpec((tm, LANES), lambda bi, i: (i, 0)),
                  pl.BlockSpec((1, LANES), lambda bi, i: (0, 0)),
                  pl.BlockSpec((1, LANES), lambda bi, i: (0, 0))],
        out_specs=(pl.BlockSpec((1, tm, q_width), lambda bi, i: (bi, i, 0)), kv_spec, vt_spec),
        compiler_params=_cparams(("parallel", "parallel")),
        name="gqa_prep",
    )(p, p, cos, sin, qg, kg)


def _softmax_pv(q_blocks, chunks, keys_on_rows):
    first_head = _lane(q_blocks[0].shape) < HEAD_DIM
    zero = jnp.zeros_like(q_blocks[0])
    n_heads = HEADS_PER_BLOCK * len(q_blocks)
    axis = 0 if keys_on_rows else 1
    s = [[None] * len(chunks) for _ in range(n_heads)]
    m = [None] * n_heads
    l = [None] * n_heads
    acc = [None] * n_heads
    for stage in range(n_heads + 1):
        ha, hb = stage, stage - 1
        if ha < n_heads:
            q = q_blocks[ha // HEADS_PER_BLOCK]
            qa = jnp.where(first_head, q, zero) if ha % HEADS_PER_BLOCK == 0 else jnp.where(first_head, zero, q)
        for c, (load_k, load_v, load_bias) in enumerate(chunks):
            if ha < n_heads:
                k = load_k(ha // HEADS_PER_BLOCK)
                sc = _mm_nt(k, qa) if keys_on_rows else _mm_nt(qa, k)
                if load_bias is not None:
                    sc = sc + load_bias(ha)
                s[ha][c] = sc
                mc = jnp.max(sc, axis=axis, keepdims=True)
                m[ha] = mc if m[ha] is None else jnp.maximum(m[ha], mc)
            if hb >= 0:
                p = jnp.exp2(s[hb][c] - m[hb])
                s[hb][c] = None
                lc = jnp.sum(p, axis=axis, keepdims=True)
                pv = _mm(load_v(hb), p) if keys_on_rows else _mm(p, load_v(hb))
                l[hb] = lc if l[hb] is None else l[hb] + lc
                acc[hb] = pv if acc[hb] is None else acc[hb] + pv
    o = [a / d for a, d in zip(acc, l)]
    if keys_on_rows:
        return [jnp.concatenate(o[2 * j:2 * j + 2], axis=0).T for j in range(len(q_blocks))]
    return [jnp.where(first_head, o[2 * j], o[2 * j + 1]) for j in range(len(q_blocks))]


def _gqa_attn_kernel(q_ref, *refs, n_sets):
    o_ref = refs[-1]
    n_blocks = q_ref.shape[2] // LANES
    heads_per_kv = HEADS_PER_BLOCK * n_blocks // refs[0].shape[1]
    chunks = []
    for i in range(n_sets):
        k_ref, vt_ref = refs[2 * i], refs[2 * i + 1]
        n_keys = k_ref.shape[2]
        for start in range(0, n_keys, ATTN_KEY_CHUNK):
            size = min(ATTN_KEY_CHUNK, n_keys - start)
            chunks.append((lambda blk, r=k_ref, s=start, z=size: r[0, HEADS_PER_BLOCK * blk // heads_per_kv, pl.ds(s, z), :],
                           lambda h, r=vt_ref, s=start, z=size: r[0, h // heads_per_kv, :, pl.ds(s, z)], None))
    outs = _softmax_pv([q_ref[0, :, j * LANES:(j + 1) * LANES] for j in range(n_blocks)], chunks, keys_on_rows=True)
    for j, o in enumerate(outs):
        o_ref[0, :, j * LANES:(j + 1) * LANES] = o.astype(o_ref.dtype)


def _gqa_attn(q, kv_sets, tq):
    b, t, width = q.shape
    kvh = kv_sets[0][0].shape[1]
    kv_step = min(GQA_STEP_KV_HEADS, kvh)
    group_width = (width // kvh) * kv_step
    in_specs = [pl.BlockSpec((1, tq, group_width), lambda bi, g, i: (bi, i, g))]
    args = [q]
    for k, v in kv_sets:
        s = k.shape[2]
        in_specs += [pl.BlockSpec((1, kv_step, s, LANES), lambda bi, g, i: (bi, g, 0, 0)),
                     pl.BlockSpec((1, kv_step, HEAD_DIM, s), lambda bi, g, i: (bi, g, 0, 0))]
        args += [k, v]
    return pl.pallas_call(
        functools.partial(_gqa_attn_kernel, n_sets=len(kv_sets)),
        out_shape=jax.ShapeDtypeStruct((b, t, width), MXU),
        grid=(b, kvh // kv_step, t // tq),
        in_specs=in_specs,
        out_specs=pl.BlockSpec((1, tq, group_width), lambda bi, g, i: (bi, i, g)),
        compiler_params=_cparams(("parallel", "parallel", "arbitrary")),
        name="gqa_attn",
    )(*args)


def _rwkv_prep_kernel(p_ref, up_ref, dn_ref, mu_ref, lw_w_ref, la_w_ref, lg_w_ref, w0_ref, a0_ref, kk_ref, ka_ref,
                      rk_ref, r_o, v_o, kk_o, lw_o, kd_o, bb_o, bonus_o, gate_o, xm_scr, *, is_ctx, dim):
    tm = p_ref.shape[1]
    n_all = p_ref.shape[2] // LANES
    n_dim = dim // LANES
    i = pl.program_id(1)
    n_i = pl.num_programs(1)
    row = lax.broadcasted_iota(jnp.int32, (tm, LANES), 0)
    cls = _lane((tm, LANES)) % 4
    if is_ctx:
        first = row == 0
        last = row == tm - 1
    else:
        first = (row % GRID_W) == 0
        last = (row % GRID_W) == GRID_W - 1
        up_ok = jnp.logical_or(row >= GRID_W, i > 0)
        dn_ok = jnp.logical_or(row < tm - GRID_W, i < n_i - 1)

    for j in range(n_all):
        sl = slice(j * LANES, (j + 1) * LANES)
        p = p_ref[0, :, sl]
        prev = jnp.where(first, 0.0, pltpu.roll(p, 1, 0))
        nxt = jnp.where(last, 0.0, pltpu.roll(p, tm - 1, 0))
        if is_ctx:
            sh = jnp.where(cls % 2 == 0, prev, nxt)
        else:
            up = jnp.concatenate([up_ref[0, :, sl], p[:tm - GRID_W]], axis=0)
            dn = jnp.concatenate([p[GRID_W:], dn_ref[0, :, sl]], axis=0)
            up = jnp.where(up_ok, up, 0.0)
            dn = jnp.where(dn_ok, dn, 0.0)
            sh = jnp.where(cls == 0, prev, jnp.where(cls == 1, nxt, jnp.where(cls == 2, up, dn)))
        xm_scr[:, sl] = p + mu_ref[:, sl] * (sh - p)

    ones = _head_ones()
    x_w = jnp.tanh(xm_scr[:, 3 * dim:3 * dim + LANES]).astype(MXU)
    x_a = xm_scr[:, 3 * dim + LANES:3 * dim + 2 * LANES].astype(MXU)
    x_g = _sigmoid(xm_scr[:, 3 * dim + 2 * LANES:3 * dim + 4 * LANES]).astype(MXU)
    for j in range(n_dim):
        sl = slice(j * LANES, (j + 1) * LANES)
        r = xm_scr[:, sl]
        k = xm_scr[:, dim + j * LANES:dim + (j + 1) * LANES]
        v = xm_scr[:, 2 * dim + j * LANES:2 * dim + (j + 1) * LANES]
        kkr = k * kk_ref[:, sl]
        kk = kkr * lax.rsqrt(jnp.maximum(_headsum(kkr * kkr, ones), 1e-12))
        r_o[0, :, sl] = r
        v_o[0, :, sl] = v
        kk_o[0, :, sl] = kk
        gate_o[0, :, sl] = jnp.dot(x_g, lg_w_ref[:, sl], preferred_element_type=F32)
        bonus = jnp.zeros_like(r)
        for d in range(2):
            dsl = slice(d * dim + j * LANES, d * dim + (j + 1) * LANES)
            z = w0_ref[:, dsl] + jnp.dot(x_w, lw_w_ref[:, dsl], preferred_element_type=F32)
            softplus = jnp.maximum(-z, 0.0) + jnp.log(1.0 + jnp.exp(-jnp.abs(z)))
            lw_o[d, 0, :, sl] = -jnp.exp(-softplus - 0.5)
            iclr = _sigmoid(a0_ref[:, dsl] + jnp.dot(x_a, la_w_ref[:, dsl], preferred_element_type=F32))
            kd = k * (1.0 + (iclr - 1.0) * ka_ref[:, sl])
            kd_o[d, 0, :, sl] = kd
            bb_o[d, 0, :, sl] = kk * iclr
            bonus = bonus + _headsum(r * kd * rk_ref[:, sl], ones) * v
        bonus_o[0, :, sl] = bonus


def _rwkv_prep(p, mu, lw_w, la_w, lg_w, w0, a0, k_k, k_a, r_k, dim, tm, is_ctx):
    b, t, width = p.shape
    if is_ctx:
        assert t == tm
    else:
        assert tm % GRID_W == 0 and t % tm == 0
    hb = tm // GRID_W
    n_halo = t // GRID_W
    kern = functools.partial(_rwkv_prep_kernel, is_ctx=is_ctx, dim=dim)
    vec = lambda n: pl.BlockSpec((1, n), lambda bi, i: (0, 0))
    mat = lambda k, n: pl.BlockSpec((k, n), lambda bi, i: (0, 0))
    one = jax.ShapeDtypeStruct((b, t, dim), F32)
    two = jax.ShapeDtypeStruct((2, b, t, dim), F32)
    one_spec = pl.BlockSpec((1, tm, dim), lambda bi, i: (bi, i, 0))
    two_spec = pl.BlockSpec((2, 1, tm, dim), lambda bi, i: (0, bi, i, 0))
    return pl.pallas_call(
        kern,
        out_shape=(one, one, one, two, two, two, one, one),
        grid=(b, t // tm),
        in_specs=[pl.BlockSpec((1, tm, width), lambda bi, i: (bi, i, 0)),
                  pl.BlockSpec((1, GRID_W, width), lambda bi, i: (bi, jnp.maximum(i * hb - 1, 0), 0)),
                  pl.BlockSpec((1, GRID_W, width), lambda bi, i: (bi, jnp.minimum((i + 1) * hb, n_halo - 1), 0)),
                  vec(width), mat(LANES, 2 * dim), mat(LANES, 2 * dim), mat(2 * LANES, dim),
                  vec(2 * dim), vec(2 * dim), vec(dim), vec(dim), vec(dim)],
        out_specs=(one_spec, one_spec, one_spec, two_spec, two_spec, two_spec, one_spec, one_spec),
        scratch_shapes=[pltpu.VMEM((tm, width), F32)],
        compiler_params=_cparams(("parallel", "parallel")),
        name="rwkv_prep_ctx" if is_ctx else "rwkv_prep",
    )(p, p, p, mu, lw_w, la_w, lg_w, w0, a0, k_k, k_a, r_k)


def _scan_chunk_terms(r, v, kk, lw, kd, bb, sgn):
    blocks = range(len(r))
    c = r[0].shape[0]
    n = 2 * c
    ri = lax.broadcasted_iota(jnp.int32, (n, n), 0)
    ci = lax.broadcasted_iota(jnp.int32, (n, n), 1)
    same = (ri // c) == (ci // c)
    dt = ((ri % c) - (ci % c)) * sgn
    strict = jnp.logical_and(same, dt > 0)
    incl = jnp.logical_and(same, dt >= 0)
    eye = ri == ci
    head_lanes = (ri // c) == (ci // HEAD_DIM)
    zero = jnp.zeros((n, n), F32)
    time = lax.broadcasted_iota(jnp.int32, (c, LANES), 0)
    sgn_f = sgn.astype(F32)
    rev_f = 0.5 - 0.5 * sgn_f

    def stack(x):
        return jnp.concatenate([x, x], axis=0)

    def own(x):
        return jnp.where(head_lanes, stack(x), zero)

    def prefix(x):
        step = 1
        while step < c:
            x = x + jnp.where(time >= step, pltpu.roll(x, step, 0), 0.0)
            step *= 2
        return x

    tot = [jnp.sum(lw[p], axis=0, keepdims=True) for p in blocks]
    pre = [prefix(lw[p]) for p in blocks]
    cum = [rev_f * (tot[p] + lw[p]) + sgn_f * pre[p] for p in blocks]
    e_inv = [jnp.exp(-cum[p]) for p in blocks]
    e_fin = [jnp.exp(tot[p] - cum[p]) for p in blocks]
    at = [own(-kk[p] * jnp.exp(cum[p] - lw[p])) for p in blocks]
    rt = [own(r[p] * jnp.exp(cum[p])) for p in blocks]
    bt = [stack(bb[p] * e_inv[p]) for p in blocks]
    kt = [stack(kd[p] * e_inv[p]) for p in blocks]
    bhat = [own(bb[p] * e_fin[p]) for p in blocks]
    khat = [own(kd[p] * e_fin[p]) for p in blocks]
    vbd = [own(v[p]) for p in blocks]

    full = [_mm_nt(jnp.concatenate([at[p], rt[p]], axis=0), jnp.concatenate([bt[p], kt[p]], axis=0)) for p in blocks]
    a_ab = [jnp.where(strict, full[p][:n, :n], zero) for p in blocks]
    a_ak = [jnp.where(strict, full[p][:n, n:], zero) for p in blocks]
    a_rb = [jnp.where(incl, full[p][n:, :n], zero) for p in blocks]
    a_rk = [jnp.where(incl, full[p][n:, n:], zero) for p in blocks]

    ident = jnp.where(eye, 1.0, 0.0)
    inv = [ident + a_ab[p] for p in blocks]
    power = a_ab
    for _ in range(int(np.log2(c)) - 1):
        power = [_mm(power[p], power[p]) for p in blocks]
        inv = [inv[p] + _mm(inv[p], power[p]) for p in blocks]

    xv = [_mm(jnp.concatenate([a_ak[p], a_rk[p]], axis=0), vbd[p]) for p in blocks]
    w12 = [_mm(inv[p], jnp.concatenate([at[p], xv[p][:n]], axis=1)) for p in blocks]
    yw = [_mm(a_rb[p], w12[p]) for p in blocks]
    mn = [_mm_tn(bhat[p], w12[p]) for p in blocks]
    py = [rt[p] + yw[p][:, :n] for p in blocks]
    yl = [yw[p][:, n:] + xv[p][n:] for p in blocks]
    m = [jnp.where(eye, jnp.exp(tot[p]), zero) + mn[p][:, :n] for p in blocks]
    nn = [mn[p][:, n:] + _mm_tn(khat[p], vbd[p]) for p in blocks]
    return py, yl, m, nn


def _scan_kernel(r_ref, v_ref, kk_ref, lw_ref, kd_ref, bb_ref, h0_ref, y_ref, ht_ref, h_scr, *, pairs, chunks):
    d = pl.program_id(0)
    ci = pl.program_id(2)
    sgn = 1 - 2 * d
    c = SCAN_CHUNK
    n = 2 * c

    @pl.when(ci == 0)
    def _():
        h_scr[...] = h0_ref[0, 0]

    rows = [pl.ds(pl.multiple_of((d * (chunks - 1) + sgn * s) * c, c), c) for s in range(chunks)]
    lanes = [slice(j * LANES, (j + 1) * LANES) for j in range(pairs)]
    probs = [(rw, ln) for rw in rows for ln in lanes]
    py, yl, m, nn = _scan_chunk_terms(
        [r_ref[0, rw, ln] for rw, ln in probs], [v_ref[0, rw, ln] for rw, ln in probs],
        [kk_ref[0, rw, ln] for rw, ln in probs], [lw_ref[0, 0, rw, ln] for rw, ln in probs],
        [kd_ref[0, 0, rw, ln] for rw, ln in probs], [bb_ref[0, 0, rw, ln] for rw, ln in probs], sgn)

    h = [h_scr[j] for j in range(pairs)]
    for s in range(chunks):
        idx = [s * pairs + j for j in range(pairs)]
        yh = [_mm(jnp.concatenate([py[i], m[i]], axis=0), h[j]) for j, i in enumerate(idx)]
        for j, i in enumerate(idx):
            y2 = yh[j][:n] + yl[i]
            y_ref[0, 0, rows[s], lanes[j]] = y2[:c] + y2[c:]
        h = [yh[j][n:] + nn[i] for j, i in enumerate(idx)]
    for j in range(pairs):
        h_scr[j] = h[j]

    @pl.when(ci == pl.num_programs(2) - 1)
    def _():
        ht_ref[0, 0] = h_scr[...]


def _rwkv_scan(r, v, kk, lw, kd, bb, h0):
    b, t, dim = r.shape
    c = SCAN_CHUNK * SCAN_STEP_CHUNKS
    assert t % c == 0
    nch = t // c
    width = SCAN_PAIRS * LANES
    nblk = dim // width

    def chunk(d, ci):
        return ci + d * (nch - 1 - 2 * ci)

    one = pl.BlockSpec((1, c, width), lambda d, g, ci: (g // nblk, chunk(d, ci), g % nblk))
    two = pl.BlockSpec((1, 1, c, width), lambda d, g, ci: (d, g // nblk, chunk(d, ci), g % nblk))
    st = pl.BlockSpec((1, 1, SCAN_PAIRS, LANES, LANES), lambda d, g, ci: (d, g // nblk, g % nblk, 0, 0))
    return pl.pallas_call(
        functools.partial(_scan_kernel, pairs=SCAN_PAIRS, chunks=SCAN_STEP_CHUNKS),
        out_shape=(jax.ShapeDtypeStruct((2, b, t, dim), F32), jax.ShapeDtypeStruct(h0.shape, F32)),
        grid=(2, b * nblk, nch),
        in_specs=[one, one, one, two, two, two, st],
        out_specs=(two, st),
        scratch_shapes=[pltpu.VMEM((SCAN_PAIRS, LANES, LANES), F32)],
        compiler_params=_cparams(("parallel", "parallel", "arbitrary")),
        name="rwkv_scan",
    )(r, v, kk, lw, kd, bb, h0)


def _rwkv_finish_kernel(y_ref, bonus_ref, gate_ref, g_ref, b_ref, o_ref):
    ones = _head_ones()
    for j in range(o_ref.shape[2] // LANES):
        sl = slice(j * LANES, (j + 1) * LANES)
        y = y_ref[0, 0, :, sl] + y_ref[1, 0, :, sl]
        mu = _headsum(y, ones) * (1.0 / HEAD_DIM)
        dlt = y - mu
        var = _headsum(dlt * dlt, ones) * (1.0 / HEAD_DIM)
        yn = dlt * lax.rsqrt(var + LNX_EPS)
        o_ref[0, :, sl] = ((yn * g_ref[:, sl] + b_ref[:, sl] + bonus_ref[0, :, sl]) * gate_ref[0, :, sl]).astype(o_ref.dtype)


def _rwkv_finish(y, bonus, gate, lnx_g, lnx_b, tm):
    _, b, t, dim = y.shape
    one_spec = pl.BlockSpec((1, tm, dim), lambda bi, i: (bi, i, 0))
    vec = pl.BlockSpec((1, dim), lambda bi, i: (0, 0))
    return pl.pallas_call(
        _rwkv_finish_kernel,
        out_shape=jax.ShapeDtypeStruct((b, t, dim), MXU),
        grid=(b, t // tm),
        in_specs=[pl.BlockSpec((2, 1, tm, dim), lambda bi, i: (0, bi, i, 0)), one_spec, one_spec, vec, vec],
        out_specs=one_spec,
        compiler_params=_cparams(("parallel", "parallel")),
        name="rwkv_finish",
    )(y, bonus, gate, lnx_g.reshape(1, dim), lnx_b.reshape(1, dim))


def _outproj_kernel(*refs, n_in):
    x_ref, gate_ref = refs[0], refs[1]
    o_ref = refs[-1]
    acc = None
    for i in range(n_in):
        part = jnp.dot(refs[2 + 2 * i][0], refs[3 + 2 * i][...], preferred_element_type=F32)
        acc = part if acc is None else acc + part
    o_ref[0] = x_ref[0] + gate_ref[0] * acc


def _outproj(x, gate, parts, tm):
    b, t, d = x.shape
    in_specs = [pl.BlockSpec((1, tm, d), lambda bi, i: (bi, i, 0)),
                pl.BlockSpec((1, 1, d), lambda bi, i: (bi, 0, 0))]
    args = [x, gate]
    for a, w, row0 in parts:
        k = a.shape[2]
        assert row0 % k == 0
        in_specs += [pl.BlockSpec((1, tm, k), lambda bi, i: (bi, i, 0)),
                     pl.BlockSpec((k, d), lambda bi, i, r=row0 // k: (r, 0))]
        args += [a, w]
    return pl.pallas_call(
        functools.partial(_outproj_kernel, n_in=len(parts)),
        out_shape=jax.ShapeDtypeStruct((b, t, d), F32),
        grid=(b, t // tm),
        in_specs=in_specs,
        out_specs=pl.BlockSpec((1, tm, d), lambda bi, i: (bi, i, 0)),
        compiler_params=_cparams(("parallel", "parallel")),
        name="outproj",
    )(*args)


def _mlp_kernel(x_ref, g_ref, sh_ref, sc_ref, gate_ref, w1_ref, w2_ref, fg_ref, o_ref, a_scr, *, final_norm):
    f = pl.program_id(2)

    @pl.when(f == 0)
    def _():
        a_scr[...] = _norm_mod(x_ref[0], g_ref[...], sh_ref[0], sc_ref[0]).astype(a_scr.dtype)
        o_ref[...] = jnp.zeros_like(o_ref)

    h = jnp.dot(a_scr[...], w1_ref[...], preferred_element_type=F32)
    h = jnp.square(jnp.maximum(h, 0.0))
    o_ref[0] += jnp.dot(h.astype(MXU), w2_ref[...], preferred_element_type=F32)

    @pl.when(f == pl.num_programs(2) - 1)
    def _():
        y = x_ref[0] + gate_ref[0] * o_ref[0]
        if final_norm:
            y = y * lax.rsqrt(jnp.mean(y * y, axis=-1, keepdims=True) + NORM_EPS) * fg_ref[...]
        o_ref[0] = y


def _mlp(x, g, shift, scale, gate, w1, w2, final_g, tm, final_norm):
    b, t, d = x.shape
    ff = w1.shape[1]
    tf = _col_tile(ff, MLP_MAX_HIDDEN)
    vec3 = pl.BlockSpec((1, 1, d), lambda bi, i, f: (bi, 0, 0))
    vec2 = pl.BlockSpec((1, d), lambda bi, i, f: (0, 0))
    return pl.pallas_call(
        functools.partial(_mlp_kernel, final_norm=final_norm),
        out_shape=jax.ShapeDtypeStruct((b, t, d), F32),
        grid=(b, t // tm, ff // tf),
        in_specs=[pl.BlockSpec((1, tm, d), lambda bi, i, f: (bi, i, 0)), vec2, vec3, vec3, vec3,
                  pl.BlockSpec((d, tf), lambda bi, i, f: (0, f)),
                  pl.BlockSpec((tf, d), lambda bi, i, f: (f, 0)), vec2],
        out_specs=pl.BlockSpec((1, tm, d), lambda bi, i, f: (bi, i, 0)),
        scratch_shapes=[pltpu.VMEM((tm, d), MXU)],
        compiler_params=_cparams(("parallel", "parallel", "arbitrary")),
        name="mlp",
    )(x, g.reshape(1, d), shift, scale, gate, w1, w2, final_g.reshape(1, d))


NA_GROUP_OFFSETS = (0, NA_ROWS // 2, NA_ROWS)
NA_STRIP_PAD = NA_WIN - NA_ROWS
NA_STRIP_BLOCKS = 2 * NA_WIN


def _na_window_lo(typ, local_row):
    return (0, local_row, NA_WIN - NA_ROWS)[typ]


def _na_kernel(q_ref, k_ref, v_ref, kc_ref, vc_ref, tc_ref, o_ref, bias_scr):
    g = pl.program_id(2)
    n_g = pl.num_programs(2)
    n_keys = NA_WIN * GRID_W
    n_blocks = q_ref.shape[2] // LANES
    tq = NA_GROUP * GRID_W

    @pl.when(jnp.logical_and(pl.program_id(1) == 0, g == 0))
    def _():
        key_row = _lane((GRID_W, n_keys)) // GRID_W
        for blk in range(n_blocks):
            for typ in range(3):
                for half in range(HEADS_PER_BLOCK):
                    for lr in range(NA_GROUP):
                        rho = NA_GROUP_OFFSETS[typ] + lr
                        u0 = NA_WIN - 1 - rho
                        head = HEADS_PER_BLOCK * blk + half
                        strip = jnp.concatenate([tc_ref[head, u0 + kr] for kr in range(NA_WIN)], axis=1)
                        lo = _na_window_lo(typ, lr)
                        ok = jnp.logical_and(key_row >= lo, key_row < lo + NA_ROWS)
                        r0 = (half * NA_GROUP + lr) * GRID_W
                        bias_scr[blk, typ, r0:r0 + GRID_W, :] = jnp.where(ok, strip, MASK_VALUE)

    win_row = jnp.clip(g * NA_GROUP - NA_ROWS // 2, 0, n_g * NA_GROUP - NA_WIN)
    typ = jnp.where(g == 0, 0, jnp.where(g == n_g - 1, 2, 1))
    start = pl.multiple_of(win_row * GRID_W, GRID_W)
    lanes = lambda blk: slice(blk * LANES, (blk + 1) * LANES)
    block = lambda h: lanes(h // HEADS_PER_BLOCK)
    chunks = [(lambda blk: k_ref[0, pl.ds(start, n_keys), lanes(blk)],
               lambda h: v_ref[0, pl.ds(start, n_keys), block(h)],
               lambda h: bias_scr[h // HEADS_PER_BLOCK, typ, pl.ds((h % HEADS_PER_BLOCK) * tq, tq), :]),
              (lambda blk: kc_ref[0, :, lanes(blk)], lambda h: vc_ref[0, :, block(h)], None)]
    outs = _softmax_pv([q_ref[0, :, lanes(blk)] for blk in range(n_blocks)], chunks, keys_on_rows=False)
    for blk, o in enumerate(outs):
        o_ref[0, :, lanes(blk)] = o.astype(o_ref.dtype)


def _na_col_table(rpb, n_heads):
    col = np.arange(GRID_W)
    c0 = np.clip(col - NA_COLS // 2, 0, GRID_W - NA_COLS)
    col_ok = (col[None, :] >= c0[:, None]) & (col[None, :] < c0[:, None] + NA_COLS)
    dcol = col[None, :] - col[:, None] + NA_COLS - 1
    n_dr, n_dc = 2 * NA_ROWS - 1, 2 * NA_COLS - 1
    onehot = np.concatenate([(dcol[None] == np.arange(n_dc)[:, None, None]) & col_ok[None],
                             ~col_ok[None], np.ones((1, GRID_W, GRID_W), bool)])
    vals = jnp.concatenate([rpb * LOG2E, jnp.full((n_heads, n_dr, 1), MASK_VALUE, F32),
                            jnp.zeros((n_heads, n_dr, 1), F32)], axis=2)
    pad_rows = jnp.zeros((n_heads, 1, n_dc + 2), F32).at[:, :, n_dc + 1].set(MASK_VALUE)
    vals = jnp.concatenate([jnp.tile(pad_rows, (1, NA_STRIP_PAD, 1)), vals,
                            jnp.tile(pad_rows, (1, NA_STRIP_BLOCKS - NA_STRIP_PAD - n_dr, 1))], axis=1)
    return jnp.einsum("hrd,dqk->hrqk", vals, jnp.asarray(onehot, F32), precision=lax.Precision.HIGHEST)


def _na_attn(qkv, kvc, col_table, width):
    b, t, _ = qkv.shape
    ctx = kvc.shape[1]
    bw = NA_STEP_BLOCKS * LANES
    nb = width // bw
    tq = NA_GROUP * GRID_W
    rows = t // GRID_W
    assert rows % NA_GROUP == 0 and rows >= NA_WIN and width % bw == 0
    return pl.pallas_call(
        _na_kernel,
        out_shape=jax.ShapeDtypeStruct((b, t, width), MXU),
        grid=(nb, b, rows // NA_GROUP),
        in_specs=[pl.BlockSpec((1, tq, bw), lambda p, bi, g: (bi, g, p)),
                  pl.BlockSpec((1, t, bw), lambda p, bi, g: (bi, 0, nb + p)),
                  pl.BlockSpec((1, t, bw), lambda p, bi, g: (bi, 0, 2 * nb + p)),
                  pl.BlockSpec((1, ctx, bw), lambda p, bi, g: (bi, 0, p)),
                  pl.BlockSpec((1, ctx, bw), lambda p, bi, g: (bi, 0, nb + p)),
                  pl.BlockSpec((NA_STEP_BLOCKS * HEADS_PER_BLOCK, NA_STRIP_BLOCKS, GRID_W, GRID_W),
                               lambda p, bi, g: (p, 0, 0, 0))],
        out_specs=pl.BlockSpec((1, tq, bw), lambda p, bi, g: (bi, g, p)),
        scratch_shapes=[pltpu.VMEM((NA_STEP_BLOCKS, 3, HEADS_PER_BLOCK * tq, NA_WIN * GRID_W), F32)],
        compiler_params=_cparams(("arbitrary", "arbitrary", "arbitrary")),
        name="na_attn",
    )(qkv, qkv, qkv, kvc, kvc, col_table)


def _rope_tables(t):
    pos = jnp.arange(t, dtype=jnp.int32)
    row = (pos // GRID_W).astype(F32)
    col = (pos % GRID_W).astype(F32)
    pairs = HEAD_DIM // 4
    inv = ROPE_THETA ** (-jnp.arange(pairs, dtype=F32) / pairs)
    ang = jnp.concatenate([row[:, None] * inv, col[:, None] * inv], axis=-1)
    cos, sin = jnp.cos(ang), jnp.sin(ang)
    cos_h = jnp.concatenate([cos, cos], axis=-1)
    sin_h = jnp.concatenate([-sin, sin], axis=-1)
    return jnp.tile(cos_h, (1, HEADS_PER_BLOCK)), jnp.tile(sin_h, (1, HEADS_PER_BLOCK))


def _block_diag2(top, bottom):
    z_t = jnp.zeros_like(top)
    z_b = jnp.zeros_like(bottom)
    return jnp.concatenate([jnp.concatenate([top, z_t], axis=1), jnp.concatenate([z_b, bottom], axis=1)], axis=0)


def kernel(x, c, ctx, c_ctx, l0_norm1, l0_norm2, l0_ada_w, l0_ada_b, l0_w_in, l0_shift_mu, l0_w0_f, l0_w0_b, l0_ww2_f, l0_ww2_b, l0_a0_f, l0_a0_b, l0_wa2_f, l0_wa2_b, l0_wg2, l0_k_k, l0_k_a, l0_r_k, l0_lnx_g, l0_lnx_b, l0_q_norm, l0_k_norm, l0_w_out, l0_mlp_w1, l0_mlp_w2, l1_norm1, l1_norm2, l1_ada_w, l1_ada_b, l1_w_qkv, l1_rpb, l1_w_out, l1_mlp_w1, l1_mlp_w2, final_norm):
    b, t, d = x.shape
    n_ctx = ctx.shape[1]
    dim = l0_w0_f.shape[0]
    q_width = l0_w_out.shape[0] - dim
    gqa_cols = l0_w_in.shape[1] - l0_shift_mu.shape[0]
    kv_width = (gqa_cols - q_width) // 2
    rw_cols = l0_shift_mu.shape[0]
    rw_pad = 3 * dim + 4 * LANES
    assert rw_cols <= rw_pad and 3 * dim + 2 * LANES == rw_cols - l0_wg2.shape[0]
    tm = min(512, t)
    tm_c = n_ctx

    cc = jnp.zeros((8, d), F32).at[:b].set(c).at[b].set(c_ctx)

    def modulation(ada_w, ada_b):
        mod = _ada_mod(cc, ada_w, ada_b)
        lat = mod[:b].reshape(b, 6, 1, d)
        cx = jnp.broadcast_to(mod[b].reshape(1, 6, 1, d), (b, 6, 1, d))
        return [lat[:, i] for i in range(6)], [cx[:, i] for i in range(6)]

    mod_l, mod_c = modulation(l0_ada_w, l0_ada_b)

    w_in = jnp.pad(l0_w_in.astype(MXU), ((0, 0), (0, rw_pad - rw_cols)))
    w_rw = w_in[:, gqa_cols:]
    mu = jnp.pad(l0_shift_mu, (0, rw_pad - rw_cols)).reshape(1, rw_pad)
    lw_w = _block_diag2(l0_ww2_f, l0_ww2_b).astype(MXU)
    la_w = _block_diag2(l0_wa2_f, l0_wa2_b).astype(MXU)
    lg_w = jnp.pad(l0_wg2, ((0, 2 * LANES - l0_wg2.shape[0]), (0, 0))).astype(MXU)
    w0 = jnp.concatenate([l0_w0_f, l0_w0_b]).reshape(1, 2 * dim)
    a0 = jnp.concatenate([l0_a0_f, l0_a0_b]).reshape(1, 2 * dim)
    k_k = l0_k_k.reshape(1, dim)
    k_a = l0_k_a.reshape(1, dim)
    r_k = l0_r_k.reshape(1, dim)
    qg = jnp.tile(l0_q_norm, HEADS_PER_BLOCK).reshape(1, LANES)
    kg = jnp.tile(l0_k_norm, HEADS_PER_BLOCK).reshape(1, LANES)
    cos_l, sin_l = _rope_tables(t)
    cos_c, sin_c = jnp.ones((n_ctx, LANES), F32), jnp.zeros((n_ctx, LANES), F32)
    w_out = l0_w_out.astype(MXU)

    def half_layer0(xs, mod, tm_, cos, sin, is_ctx):
        pg = _inproj(xs, l0_norm1, mod[0], mod[1], w_in, tm_, F32, col0=0, n=gqa_cols)
        pr = _inproj(xs, l0_norm1, mod[0], mod[1], w_rw, tm_, F32)
        q, kd, vd = _gqa_prep(pg, cos, sin, qg, kg, q_width, kv_width, tm_)
        prep = _rwkv_prep(pr, mu, lw_w, la_w, lg_w, w0, a0, k_k, k_a, r_k, dim, 256, is_ctx)
        return q, kd, vd, prep

    q_c, kd_c, vd_c, prep_c = half_layer0(ctx, mod_c, tm_c, cos_c, sin_c, True)
    q_l, kd_l, vd_l, prep_l = half_layer0(x, mod_l, tm, cos_l, sin_l, False)

    o_gqa_l = _gqa_attn(q_l, [(kd_l, vd_l), (kd_c, vd_c)], min(GQA_QUERY_TILE, t))
    o_gqa_c = _gqa_attn(q_c, [(kd_c, vd_c)], n_ctx)

    h0 = jnp.zeros((2, b, dim // LANES, LANES, LANES), F32)
    y_c, h_c = _rwkv_scan(*prep_c[:6], h0)
    y_l, _ = _rwkv_scan(*prep_l[:6], h_c)
    o_rw_l = _rwkv_finish(y_l, prep_l[6], prep_l[7], l0_lnx_g, l0_lnx_b, tm)
    o_rw_c = _rwkv_finish(y_c, prep_c[6], prep_c[7], l0_lnx_g, l0_lnx_b, tm_c)

    w1 = l0_mlp_w1.astype(MXU)
    w2 = l0_mlp_w2.astype(MXU)
    x = _outproj(x, mod_l[2], [(o_gqa_l, w_out, 0), (o_rw_l, w_out, q_width)], tm)
    x = _mlp(x, l0_norm2, mod_l[3], mod_l[4], mod_l[5], w1, w2, final_norm, min(MLP_ROW_TILE, t), False)
    ctx = _outproj(ctx, mod_c[2], [(o_gqa_c, w_out, 0), (o_rw_c, w_out, q_width)], tm_c)
    ctx = _mlp(ctx.reshape(1, b * n_ctx, d), l0_norm2, mod_c[3][:1], mod_c[4][:1], mod_c[5][:1], w1, w2, final_norm,
               _col_tile(b * n_ctx, MLP_ROW_TILE), False).reshape(b, n_ctx, d)

    mod_l, mod_c = modulation(l1_ada_w, l1_ada_b)
    width = l1_w_out.shape[0]
    n_heads = width // HEAD_DIM
    scale = jnp.concatenate([jnp.full((width,), ATTN_SCALE, F32), jnp.ones((2 * width,), F32)])
    w_qkv = (l1_w_qkv * scale).astype(MXU)
    qkv = _inproj(x, l1_norm1, mod_l[0], mod_l[1], w_qkv, min(2 * tm, t), MXU)
    kvc = _inproj(ctx, l1_norm1, mod_c[0], mod_c[1], w_qkv, tm_c, MXU, col0=width, n=2 * width)
    o_na = _na_attn(qkv, kvc, _na_col_table(l1_rpb, n_heads), width)
    x = _outproj(x, mod_l[2], [(o_na, l1_w_out.astype(MXU), 0)], tm)
    x = _mlp(x, l1_norm2, mod_l[3], mod_l[4], mod_l[5], l1_mlp_w1.astype(MXU), l1_mlp_w2.astype(MXU),
             final_norm, min(MLP_ROW_TILE, t), True)
    return x
```

```python
import functools

import jax
import jax.numpy as jnp
import numpy as np
from jax import lax
from jax.experimental import pallas as pl
from jax.experimental.pallas import tpu as pltpu

F32 = jnp.float32
MXU = jnp.bfloat16

LANES = 128
SUBLANES = 8
ROW_TILE = 512
RWKV_PREP_ROWS = 256
ADA_MAX_COLS = 1024
HEAD_DIM = 64
HEADS_PER_BLOCK = LANES // HEAD_DIM
GRID_W = 64
NORM_EPS = 1e-6
LNX_EPS = 64e-5
ROPE_THETA = 10000.0
NA_ROWS = 8
NA_COLS = 16
NA_GROUP = 4
NA_WIN = NA_GROUP + NA_ROWS
NA_STEP_BLOCKS = 4
GQA_STEP_KV_HEADS = 4
GQA_QUERY_TILE = 512
MASK_VALUE = -1e30
LOG2E = float(np.log2(np.e))
ATTN_SCALE = HEAD_DIM ** -0.5 * LOG2E
ATTN_KEY_CHUNK = 2048
INPROJ_MAX_COLS = 2048
INPROJ_RESIDENT_BYTES = 16 * 1024 * 1024
INPROJ_VMEM_BUDGET = 46 * 1024 * 1024
MLP_MAX_HIDDEN = 1024
MLP_ROW_TILE = 512
SCAN_CHUNK = 64
SCAN_PAIRS = 8
SCAN_STEP_CHUNKS = 4
VMEM_LIMIT = 56 * 1024 * 1024


def _cparams(sem):
    return pltpu.CompilerParams(dimension_semantics=sem, vmem_limit_bytes=VMEM_LIMIT)


def _mm(a, b):
    return jnp.dot(a.astype(MXU), b.astype(MXU), preferred_element_type=F32)


def _mm_nt(a, b):
    return lax.dot_general(a.astype(MXU), b.astype(MXU), (((1,), (1,)), ((), ())),
                           preferred_element_type=F32)


def _mm_tn(a, b):
    return lax.dot_general(a.astype(MXU), b.astype(MXU), (((0,), (0,)), ((), ())),
                           preferred_element_type=F32)


def _split3(x):
    hi = x.astype(MXU)
    r1 = x - hi.astype(F32)
    mid = r1.astype(MXU)
    lo = (r1 - mid.astype(F32)).astype(MXU)
    return hi, mid, lo


def _head_ones():
    r = lax.broadcasted_iota(jnp.int32, (LANES, LANES), 0) // HEAD_DIM
    c = lax.broadcasted_iota(jnp.int32, (LANES, LANES), 1) // HEAD_DIM
    return jnp.where(r == c, 1.0, 0.0).astype(MXU)


def _headsum(x, ones):
    hi, mid, lo = _split3(x)
    return (jnp.dot(hi, ones, preferred_element_type=F32) + jnp.dot(mid, ones, preferred_element_type=F32)
            + jnp.dot(lo, ones, preferred_element_type=F32))


def _lane(shape):
    return lax.broadcasted_iota(jnp.int32, shape, len(shape) - 1)


def _sigmoid(x):
    return 1.0 / (1.0 + jnp.exp(-x))


def _ada_kernel(c_ref, w_ref, b_ref, o_ref):
    c = c_ref[...]
    a = c * _sigmoid(c)
    o_ref[...] = _mm(a, w_ref[...]) + b_ref[...]


def _ada_mod(cc, w, bias):
    d, n = w.shape
    tn = _col_tile(n, ADA_MAX_COLS)
    return pl.pallas_call(
        _ada_kernel,
        out_shape=jax.ShapeDtypeStruct((cc.shape[0], n), F32),
        grid=(n // tn,),
        in_specs=[pl.BlockSpec((cc.shape[0], d), lambda j: (0, 0)),
                  pl.BlockSpec((d, tn), lambda j: (0, j)),
                  pl.BlockSpec((1, tn), lambda j: (0, j))],
        out_specs=pl.BlockSpec((cc.shape[0], tn), lambda j: (0, j)),
        compiler_params=_cparams(("arbitrary",)),
        name="ada_mod",
    )(cc, w, bias.reshape(1, n))


def _norm_mod(x, g, shift, scale):
    y = x * lax.rsqrt(jnp.mean(x * x, axis=-1, keepdims=True) + NORM_EPS)
    return (y * g) * (1.0 + scale) + shift


def _inproj_kernel(x_ref, g_ref, sh_ref, sc_ref, w_ref, o_ref, a_scr):
    @pl.when(pl.program_id(2) == 0)
    def _():
        a_scr[...] = _norm_mod(x_ref[0], g_ref[...], sh_ref[0], sc_ref[0]).astype(a_scr.dtype)

    o_ref[0] = jnp.dot(a_scr[...], w_ref[...], preferred_element_type=F32).astype(o_ref.dtype)


def _col_tile(n, cap):
    return max(tn for tn in range(LANES, cap + 1, LANES) if n % tn == 0)


def _inproj(x, g, shift, scale, w, tm, out_dtype, col0=0, n=None):
    b, t, d = x.shape
    n = w.shape[1] if n is None else n
    if d * n * w.dtype.itemsize <= INPROJ_RESIDENT_BYTES and col0 % n == 0:
        tn = n
    else:
        tn = _col_tile(int(np.gcd(n, col0)) if col0 else n, INPROJ_MAX_COLS)
    j0 = col0 // tn
    out_bytes = jnp.dtype(out_dtype).itemsize

    def vmem(tm_):
        return 2 * (tm_ * d * 4 + d * tn * w.dtype.itemsize + tm_ * tn * out_bytes) + tm_ * d * w.dtype.itemsize

    while vmem(tm) > INPROJ_VMEM_BUDGET and tm % 16 == 0:
        tm //= 2
    return pl.pallas_call(
        _inproj_kernel,
        out_shape=jax.ShapeDtypeStruct((b, t, n), out_dtype),
        grid=(b, t // tm, n // tn),
        in_specs=[pl.BlockSpec((1, tm, d), lambda bi, i, j: (bi, i, 0)),
                  pl.BlockSpec((1, d), lambda bi, i, j: (0, 0)),
                  pl.BlockSpec((1, 1, d), lambda bi, i, j: (bi, 0, 0)),
                  pl.BlockSpec((1, 1, d), lambda bi, i, j: (bi, 0, 0)),
                  pl.BlockSpec((d, tn), lambda bi, i, j: (0, j0 + j))],
        out_specs=pl.BlockSpec((1, tm, tn), lambda bi, i, j: (bi, i, j)),
        scratch_shapes=[pltpu.VMEM((tm, d), MXU)],
        compiler_params=_cparams(("parallel", "parallel", "arbitrary")),
        name="inproj",
    )(x, g.reshape(1, d), shift, scale, w)


def _swap_half_heads(x):
    first = (_lane(x.shape) % HEAD_DIM) < HEAD_DIM // 2
    return jnp.where(first, pltpu.roll(x, LANES - HEAD_DIM // 2, 1), pltpu.roll(x, HEAD_DIM // 2, 1))


def _gqa_prep_kernel(q_ref, kv_ref, cos_ref, sin_ref, qg_ref, kg_ref, qo_ref, ko_ref, vo_ref, *, n_q, n_kv):
    ones = _head_ones()
    cos = cos_ref[...]
    sin = sin_ref[...]
    first_head = _lane(cos.shape) < HEAD_DIM

    def norm_rope(x, g):
        ms = _headsum(x * x, ones) * (1.0 / HEAD_DIM)
        y = x * lax.rsqrt(ms + NORM_EPS) * g
        return y * cos + _swap_half_heads(y) * sin

    def dup(x, half):
        rolled = pltpu.roll(x, HEAD_DIM, 1)
        return jnp.where(first_head, x, rolled) if half == 0 else jnp.where(first_head, rolled, x)

    for j in range(n_q):
        x = q_ref[0, :, j * LANES:(j + 1) * LANES]
        qo_ref[0, :, j * LANES:(j + 1) * LANES] = (norm_rope(x, qg_ref[...]) * ATTN_SCALE).astype(qo_ref.dtype)
    for j in range(n_kv):
        k = norm_rope(kv_ref[0, :, j * LANES:(j + 1) * LANES], kg_ref[...])
        vt = kv_ref[0, :, (n_kv + j) * LANES:(n_kv + j + 1) * LANES].T
        for half in range(HEADS_PER_BLOCK):
            ko_ref[0, HEADS_PER_BLOCK * j + half] = dup(k, half).astype(ko_ref.dtype)
            vo_ref[0, HEADS_PER_BLOCK * j + half] = vt[half * HEAD_DIM:(half + 1) * HEAD_DIM].astype(vo_ref.dtype)


def _gqa_prep(p, cos, sin, qg, kg, q_width, kv_width, tm):
    b, t, _ = p.shape
    n_q = q_width // LANES
    n_kv = kv_width // LANES
    kvh = kv_width // HEAD_DIM
    kern = functools.partial(_gqa_prep_kernel, n_q=n_q, n_kv=n_kv)
    kv_spec = pl.BlockSpec((1, kvh, tm, LANES), lambda bi, i: (bi, 0, i, 0))
    vt_spec = pl.BlockSpec((1, kvh, HEAD_DIM, tm), lambda bi, i: (bi, 0, 0, i))
    return pl.pallas_call(
        kern,
        out_shape=(jax.ShapeDtypeStruct((b, t, q_width), MXU),
                   jax.ShapeDtypeStruct((b, kvh, t, LANES), MXU),
                   jax.ShapeDtypeStruct((b, kvh, HEAD_DIM, t), MXU)),
        grid=(b, t // tm),
        in_specs=[pl.BlockSpec((1, tm, q_width), lambda bi, i: (bi, i, 0)),
                  pl.BlockSpec((1, tm, 2 * kv_width), lambda bi, i: (bi, i, q_width // (2 * kv_width))),
                  pl.BlockSpec((tm, LANES), lambda bi, i: (i, 0)),
                  pl.BlockSpec((tm, LANES), lambda bi, i: (i, 0)),
                  pl.BlockSpec((1, LANES), lambda bi, i: (0, 0)),
                  pl.BlockSpec((1, LANES), lambda bi, i: (0, 0))],
        out_specs=(pl.BlockSpec((1, tm, q_width), lambda bi, i: (bi, i, 0)), kv_spec, vt_spec),
        compiler_params=_cparams(("parallel", "parallel")),
        name="gqa_prep",
    )(p, p, cos, sin, qg, kg)


def _softmax_pv(q_blocks, chunks, keys_on_rows):
    first_head = _lane(q_blocks[0].shape) < HEAD_DIM
    zero = jnp.zeros_like(q_blocks[0])
    n_heads = HEADS_PER_BLOCK * len(q_blocks)
    axis = 0 if keys_on_rows else 1
    s = [[None] * len(chunks) for _ in range(n_heads)]
    m = [None] * n_heads
    l = [None] * n_heads
    acc = [None] * n_heads
    for stage in range(n_heads + 1):
        ha, hb = stage, stage - 1
        if ha < n_heads:
            q = q_blocks[ha // HEADS_PER_BLOCK]
            qa = jnp.where(first_head, q, zero) if ha % HEADS_PER_BLOCK == 0 else jnp.where(first_head, zero, q)
        for c, (load_k, load_v, load_bias) in enumerate(chunks):
            if ha < n_heads:
                k = load_k(ha // HEADS_PER_BLOCK)
                sc = _mm_nt(k, qa) if keys_on_rows else _mm_nt(qa, k)
                if load_bias is not None:
                    sc = sc + load_bias(ha)
                s[ha][c] = sc
                mc = jnp.max(sc, axis=axis, keepdims=True)
                m[ha] = mc if m[ha] is None else jnp.maximum(m[ha], mc)
            if hb >= 0:
                p = jnp.exp2(s[hb][c] - m[hb])
                s[hb][c] = None
                lc = jnp.sum(p, axis=axis, keepdims=True)
                pv = _mm(load_v(hb), p) if keys_on_rows else _mm(p, load_v(hb))
                l[hb] = lc if l[hb] is None else l[hb] + lc
                acc[hb] = pv if acc[hb] is None else acc[hb] + pv
    o = [a / d for a, d in zip(acc, l)]
    if keys_on_rows:
        return [jnp.concatenate(o[2 * j:2 * j + 2], axis=0).T for j in range(len(q_blocks))]
    return [jnp.where(first_head, o[2 * j], o[2 * j + 1]) for j in range(len(q_blocks))]


def _gqa_attn_kernel(q_ref, *refs, n_sets):
    o_ref = refs[-1]
    n_blocks = q_ref.shape[2] // LANES
    heads_per_kv = HEADS_PER_BLOCK * n_blocks // refs[0].shape[1]
    chunks = []
    for i in range(n_sets):
        k_ref, vt_ref = refs[2 * i], refs[2 * i + 1]
        n_keys = k_ref.shape[2]
        for start in range(0, n_keys, ATTN_KEY_CHUNK):
            size = min(ATTN_KEY_CHUNK, n_keys - start)
            chunks.append((lambda blk, r=k_ref, s=start, z=size: r[0, HEADS_PER_BLOCK * blk // heads_per_kv, pl.ds(s, z), :],
                           lambda h, r=vt_ref, s=start, z=size: r[0, h // heads_per_kv, :, pl.ds(s, z)], None))
    outs = _softmax_pv([q_ref[0, :, j * LANES:(j + 1) * LANES] for j in range(n_blocks)], chunks, keys_on_rows=True)
    for j, o in enumerate(outs):
        o_ref[0, :, j * LANES:(j + 1) * LANES] = o.astype(o_ref.dtype)


def _gqa_attn(q, kv_sets, tq):
    b, t, width = q.shape
    kvh = kv_sets[0][0].shape[1]
    kv_step = min(GQA_STEP_KV_HEADS, kvh)
    group_width = (width // kvh) * kv_step
    in_specs = [pl.BlockSpec((1, tq, group_width), lambda bi, g, i: (bi, i, g))]
    args = [q]
    for k, v in kv_sets:
        s = k.shape[2]
        in_specs += [pl.BlockSpec((1, kv_step, s, LANES), lambda bi, g, i: (bi, g, 0, 0)),
                     pl.BlockSpec((1, kv_step, HEAD_DIM, s), lambda bi, g, i: (bi, g, 0, 0))]
        args += [k, v]
    return pl.pallas_call(
        functools.partial(_gqa_attn_kernel, n_sets=len(kv_sets)),
        out_shape=jax.ShapeDtypeStruct((b, t, width), MXU),
        grid=(b, kvh // kv_step, t // tq),
        in_specs=in_specs,
        out_specs=pl.BlockSpec((1, tq, group_width), lambda bi, g, i: (bi, i, g)),
        compiler_params=_cparams(("parallel", "parallel", "arbitrary")),
        name="gqa_attn",
    )(*args)


def _rwkv_prep_kernel(p_ref, up_ref, dn_ref, mu_ref, lw_w_ref, la_w_ref, lg_w_ref, w0_ref, a0_ref, kk_ref, ka_ref,
                      rk_ref, r_o, v_o, kk_o, lw_o, kd_o, bb_o, bonus_o, gate_o, xm_scr, *, is_ctx, dim):
    tm = p_ref.shape[1]
    n_all = p_ref.shape[2] // LANES
    n_dim = dim // LANES
    i = pl.program_id(1)
    n_i = pl.num_programs(1)
    row = lax.broadcasted_iota(jnp.int32, (tm, LANES), 0)
    cls = _lane((tm, LANES)) % 4
    if is_ctx:
        first = row == 0
        last = row == tm - 1
    else:
        first = (row % GRID_W) == 0
        last = (row % GRID_W) == GRID_W - 1
        up_ok = jnp.logical_or(row >= GRID_W, i > 0)
        dn_ok = jnp.logical_or(row < tm - GRID_W, i < n_i - 1)

    for j in range(n_all):
        sl = slice(j * LANES, (j + 1) * LANES)
        p = p_ref[0, :, sl]
        prev = jnp.where(first, 0.0, pltpu.roll(p, 1, 0))
        nxt = jnp.where(last, 0.0, pltpu.roll(p, tm - 1, 0))
        if is_ctx:
            sh = jnp.where(cls % 2 == 0, prev, nxt)
        else:
            up = jnp.concatenate([up_ref[0, :, sl], p[:tm - GRID_W]], axis=0)
            dn = jnp.concatenate([p[GRID_W:], dn_ref[0, :, sl]], axis=0)
            up = jnp.where(up_ok, up, 0.0)
            dn = jnp.where(dn_ok, dn, 0.0)
            sh = jnp.where(cls == 0, prev, jnp.where(cls == 1, nxt, jnp.where(cls == 2, up, dn)))
        xm_scr[:, sl] = p + mu_ref[:, sl] * (sh - p)

    ones = _head_ones()
    x_w = jnp.tanh(xm_scr[:, 3 * dim:3 * dim + LANES]).astype(MXU)
    x_a = xm_scr[:, 3 * dim + LANES:3 * dim + 2 * LANES].astype(MXU)
    x_g = _sigmoid(xm_scr[:, 3 * dim + 2 * LANES:3 * dim + 4 * LANES]).astype(MXU)
    for j in range(n_dim):
        sl = slice(j * LANES, (j + 1) * LANES)
        r = xm_scr[:, sl]
        k = xm_scr[:, dim + j * LANES:dim + (j + 1) * LANES]
        v = xm_scr[:, 2 * dim + j * LANES:2 * dim + (j + 1) * LANES]
        kkr = k * kk_ref[:, sl]
        kk = kkr * lax.rsqrt(jnp.maximum(_headsum(kkr * kkr, ones), 1e-12))
        r_o[0, :, sl] = r
        v_o[0, :, sl] = v
        kk_o[0, :, sl] = kk
        gate_o[0, :, sl] = jnp.dot(x_g, lg_w_ref[:, sl], preferred_element_type=F32)
        bonus = jnp.zeros_like(r)
        for d in range(2):
            dsl = slice(d * dim + j * LANES, d * dim + (j + 1) * LANES)
            z = w0_ref[:, dsl] + jnp.dot(x_w, lw_w_ref[:, dsl], preferred_element_type=F32)
            softplus = jnp.maximum(-z, 0.0) + jnp.log(1.0 + jnp.exp(-jnp.abs(z)))
            lw_o[d, 0, :, sl] = -jnp.exp(-softplus - 0.5)
            iclr = _sigmoid(a0_ref[:, dsl] + jnp.dot(x_a, la_w_ref[:, dsl], preferred_element_type=F32))
            kd = k * (1.0 + (iclr - 1.0) * ka_ref[:, sl])
            kd_o[d, 0, :, sl] = kd
            bb_o[d, 0, :, sl] = kk * iclr
            bonus = bonus + _headsum(r * kd * rk_ref[:, sl], ones) * v
        bonus_o[0, :, sl] = bonus


def _rwkv_prep(p, mu, lw_w, la_w, lg_w, w0, a0, k_k, k_a, r_k, dim, tm, is_ctx):
    b, t, width = p.shape
    if is_ctx:
        assert t == tm
    else:
        assert tm % GRID_W == 0 and t % tm == 0
    hb = tm // GRID_W
    n_halo = t // GRID_W
    kern = functools.partial(_rwkv_prep_kernel, is_ctx=is_ctx, dim=dim)
    vec = lambda n: pl.BlockSpec((1, n), lambda bi, i: (0, 0))
    mat = lambda k, n: pl.BlockSpec((k, n), lambda bi, i: (0, 0))
    one = jax.ShapeDtypeStruct((b, t, dim), F32)
    two = jax.ShapeDtypeStruct((2, b, t, dim), F32)
    one_spec = pl.BlockSpec((1, tm, dim), lambda bi, i: (bi, i, 0))
    two_spec = pl.BlockSpec((2, 1, tm, dim), lambda bi, i: (0, bi, i, 0))
    return pl.pallas_call(
        kern,
        out_shape=(one, one, one, two, two, two, one, one),
        grid=(b, t // tm),
        in_specs=[pl.BlockSpec((1, tm, width), lambda bi, i: (bi, i, 0)),
                  pl.BlockSpec((1, GRID_W, width), lambda bi, i: (bi, jnp.maximum(i * hb - 1, 0), 0)),
                  pl.BlockSpec((1, GRID_W, width), lambda bi, i: (bi, jnp.minimum((i + 1) * hb, n_halo - 1), 0)),
                  vec(width), mat(LANES, 2 * dim), mat(LANES, 2 * dim), mat(2 * LANES, dim),
                  vec(2 * dim), vec(2 * dim), vec(dim), vec(dim), vec(dim)],
        out_specs=(one_spec, one_spec, one_spec, two_spec, two_spec, two_spec, one_spec, one_spec),
        scratch_shapes=[pltpu.VMEM((tm, width), F32)],
        compiler_params=_cparams(("parallel", "parallel")),
        name="rwkv_prep_ctx" if is_ctx else "rwkv_prep",
    )(p, p, p, mu, lw_w, la_w, lg_w, w0, a0, k_k, k_a, r_k)


def _scan_chunk_terms(r, v, kk, lw, kd, bb, sgn):
    blocks = range(len(r))
    c = r[0].shape[0]
    n = 2 * c
    ri = lax.broadcasted_iota(jnp.int32, (n, n), 0)
    ci = lax.broadcasted_iota(jnp.int32, (n, n), 1)
    same = (ri // c) == (ci // c)
    dt = ((ri % c) - (ci % c)) * sgn
    strict = jnp.logical_and(same, dt > 0)
    incl = jnp.logical_and(same, dt >= 0)
    eye = ri == ci
    head_lanes = (ri // c) == (ci // HEAD_DIM)
    zero = jnp.zeros((n, n), F32)
    time = lax.broadcasted_iota(jnp.int32, (c, LANES), 0)
    sgn_f = sgn.astype(F32)
    rev_f = 0.5 - 0.5 * sgn_f

    def stack(x):
        return jnp.concatenate([x, x], axis=0)

    def own(x):
        return jnp.where(head_lanes, stack(x), zero)

    def prefix(x):
        step = 1
        while step < c:
            x = x + jnp.where(time >= step, pltpu.roll(x, step, 0), 0.0)
            step *= 2
        return x

    tot = [jnp.sum(lw[p], axis=0, keepdims=True) for p in blocks]
    pre = [prefix(lw[p]) for p in blocks]
    cum = [rev_f * (tot[p] + lw[p]) + sgn_f * pre[p] for p in blocks]
    e_inv = [jnp.exp(-cum[p]) for p in blocks]
    e_fin = [jnp.exp(tot[p] - cum[p]) for p in blocks]
    at = [own(-kk[p] * jnp.exp(cum[p] - lw[p])) for p in blocks]
    rt = [own(r[p] * jnp.exp(cum[p])) for p in blocks]
    bt = [stack(bb[p] * e_inv[p]) for p in blocks]
    kt = [stack(kd[p] * e_inv[p]) for p in blocks]
    bhat = [own(bb[p] * e_fin[p]) for p in blocks]
    khat = [own(kd[p] * e_fin[p]) for p in blocks]
    vbd = [own(v[p]) for p in blocks]

    full = [_mm_nt(jnp.concatenate([at[p], rt[p]], axis=0), jnp.concatenate([bt[p], kt[p]], axis=0)) for p in blocks]
    a_ab = [jnp.where(strict, full[p][:n, :n], zero) for p in blocks]
    a_ak = [jnp.where(strict, full[p][:n, n:], zero) for p in blocks]
    a_rb = [jnp.where(incl, full[p][n:, :n], zero) for p in blocks]
    a_rk = [jnp.where(incl, full[p][n:, n:], zero) for p in blocks]

    ident = jnp.where(eye, 1.0, 0.0)
    inv = [ident + a_ab[p] for p in blocks]
    power = a_ab
    for _ in range(int(np.log2(c)) - 1):
        power = [_mm(power[p], power[p]) for p in blocks]
        inv = [inv[p] + _mm(inv[p], power[p]) for p in blocks]

    xv = [_mm(jnp.concatenate([a_ak[p], a_rk[p]], axis=0), vbd[p]) for p in blocks]
    w12 = [_mm(inv[p], jnp.concatenate([at[p], xv[p][:n]], axis=1)) for p in blocks]
    yw = [_mm(a_rb[p], w12[p]) for p in blocks]
    mn = [_mm_tn(bhat[p], w12[p]) for p in blocks]
    py = [rt[p] + yw[p][:, :n] for p in blocks]
    yl = [yw[p][:, n:] + xv[p][n:] for p in blocks]
    m = [jnp.where(eye, jnp.exp(tot[p]), zero) + mn[p][:, :n] for p in blocks]
    nn = [mn[p][:, n:] + _mm_tn(khat[p], vbd[p]) for p in blocks]
    return py, yl, m, nn


def _scan_kernel(r_ref, v_ref, kk_ref, lw_ref, kd_ref, bb_ref, h0_ref, y_ref, ht_ref, h_scr, *, pairs, chunks):
    d = pl.program_id(0)
    ci = pl.program_id(2)
    sgn = 1 - 2 * d
    c = SCAN_CHUNK
    n = 2 * c

    @pl.when(ci == 0)
    def _():
        h_scr[...] = h0_ref[0, 0]

    rows = [pl.ds(pl.multiple_of((d * (chunks - 1) + sgn * s) * c, c), c) for s in range(chunks)]
    lanes = [slice(j * LANES, (j + 1) * LANES) for j in range(pairs)]
    probs = [(rw, ln) for rw in rows for ln in lanes]
    py, yl, m, nn = _scan_chunk_terms(
        [r_ref[0, rw, ln] for rw, ln in probs], [v_ref[0, rw, ln] for rw, ln in probs],
        [kk_ref[0, rw, ln] for rw, ln in probs], [lw_ref[0, 0, rw, ln] for rw, ln in probs],
        [kd_ref[0, 0, rw, ln] for rw, ln in probs], [bb_ref[0, 0, rw, ln] for rw, ln in probs], sgn)

    h = [h_scr[j] for j in range(pairs)]
    for s in range(chunks):
        idx = [s * pairs + j for j in range(pairs)]
        yh = [_mm(jnp.concatenate([py[i], m[i]], axis=0), h[j]) for j, i in enumerate(idx)]
        for j, i in enumerate(idx):
            y2 = yh[j][:n] + yl[i]
            y_ref[0, 0, rows[s], lanes[j]] = y2[:c] + y2[c:]
        h = [yh[j][n:] + nn[i] for j, i in enumerate(idx)]
    for j in range(pairs):
        h_scr[j] = h[j]

    @pl.when(ci == pl.num_programs(2) - 1)
    def _():
        ht_ref[0, 0] = h_scr[...]


def _rwkv_scan(r, v, kk, lw, kd, bb, h0):
    b, t, dim = r.shape
    c = SCAN_CHUNK * SCAN_STEP_CHUNKS
    assert t % c == 0
    nch = t // c
    width = SCAN_PAIRS * LANES
    nblk = dim // width

    def chunk(d, ci):
        return ci + d * (nch - 1 - 2 * ci)

    one = pl.BlockSpec((1, c, width), lambda d, g, ci: (g // nblk, chunk(d, ci), g % nblk))
    two = pl.BlockSpec((1, 1, c, width), lambda d, g, ci: (d, g // nblk, chunk(d, ci), g % nblk))
    st = pl.BlockSpec((1, 1, SCAN_PAIRS, LANES, LANES), lambda d, g, ci: (d, g // nblk, g % nblk, 0, 0))
    return pl.pallas_call(
        functools.partial(_scan_kernel, pairs=SCAN_PAIRS, chunks=SCAN_STEP_CHUNKS),
        out_shape=(jax.ShapeDtypeStruct((2, b, t, dim), F32), jax.ShapeDtypeStruct(h0.shape, F32)),
        grid=(2, b * nblk, nch),
        in_specs=[one, one, one, two, two, two, st],
        out_specs=(two, st),
        scratch_shapes=[pltpu.VMEM((SCAN_PAIRS, LANES, LANES), F32)],
        compiler_params=_cparams(("parallel", "parallel", "arbitrary")),
        name="rwkv_scan",
    )(r, v, kk, lw, kd, bb, h0)


def _rwkv_finish_kernel(y_ref, bonus_ref, gate_ref, g_ref, b_ref, o_ref):
    ones = _head_ones()
    for j in range(o_ref.shape[2] // LANES):
        sl = slice(j * LANES, (j + 1) * LANES)
        y = y_ref[0, 0, :, sl] + y_ref[1, 0, :, sl]
        mu = _headsum(y, ones) * (1.0 / HEAD_DIM)
        dlt = y - mu
        var = _headsum(dlt * dlt, ones) * (1.0 / HEAD_DIM)
        yn = dlt * lax.rsqrt(var + LNX_EPS)
        o_ref[0, :, sl] = ((yn * g_ref[:, sl] + b_ref[:, sl] + bonus_ref[0, :, sl]) * gate_ref[0, :, sl]).astype(o_ref.dtype)


def _rwkv_finish(y, bonus, gate, lnx_g, lnx_b, tm):
    _, b, t, dim = y.shape
    one_spec = pl.BlockSpec((1, tm, dim), lambda bi, i: (bi, i, 0))
    vec = pl.BlockSpec((1, dim), lambda bi, i: (0, 0))
    return pl.pallas_call(
        _rwkv_finish_kernel,
        out_shape=jax.ShapeDtypeStruct((b, t, dim), MXU),
        grid=(b, t // tm),
        in_specs=[pl.BlockSpec((2, 1, tm, dim), lambda bi, i: (0, bi, i, 0)), one_spec, one_spec, vec, vec],
        out_specs=one_spec,
        compiler_params=_cparams(("parallel", "parallel")),
        name="rwkv_finish",
    )(y, bonus, gate, lnx_g.reshape(1, dim), lnx_b.reshape(1, dim))


def _outproj_kernel(*refs, n_in):
    x_ref, gate_ref = refs[0], refs[1]
    o_ref = refs[-1]
    acc = None
    for i in range(n_in):
        part = jnp.dot(refs[2 + 2 * i][0], refs[3 + 2 * i][...], preferred_element_type=F32)
        acc = part if acc is None else acc + part
    o_ref[0] = x_ref[0] + gate_ref[0] * acc


def _outproj(x, gate, parts, tm):
    b, t, d = x.shape
    in_specs = [pl.BlockSpec((1, tm, d), lambda bi, i: (bi, i, 0)),
                pl.BlockSpec((1, 1, d), lambda bi, i: (bi, 0, 0))]
    args = [x, gate]
    for a, w, row0 in parts:
        k = a.shape[2]
        assert row0 % k == 0
        in_specs += [pl.BlockSpec((1, tm, k), lambda bi, i: (bi, i, 0)),
                     pl.BlockSpec((k, d), lambda bi, i, r=row0 // k: (r, 0))]
        args += [a, w]
    return pl.pallas_call(
        functools.partial(_outproj_kernel, n_in=len(parts)),
        out_shape=jax.ShapeDtypeStruct((b, t, d), F32),
        grid=(b, t // tm),
        in_specs=in_specs,
        out_specs=pl.BlockSpec((1, tm, d), lambda bi, i: (bi, i, 0)),
        compiler_params=_cparams(("parallel", "parallel")),
        name="outproj",
    )(*args)


def _mlp_kernel(x_ref, g_ref, sh_ref, sc_ref, gate_ref, w1_ref, w2_ref, fg_ref, o_ref, a_scr, *, final_norm):
    f = pl.program_id(2)

    @pl.when(f == 0)
    def _():
        a_scr[...] = _norm_mod(x_ref[0], g_ref[...], sh_ref[0], sc_ref[0]).astype(a_scr.dtype)
        o_ref[...] = jnp.zeros_like(o_ref)

    h = jnp.dot(a_scr[...], w1_ref[...], preferred_element_type=F32)
    h = jnp.square(jnp.maximum(h, 0.0))
    o_ref[0] += jnp.dot(h.astype(MXU), w2_ref[...], preferred_element_type=F32)

    @pl.when(f == pl.num_programs(2) - 1)
    def _():
        y = x_ref[0] + gate_ref[0] * o_ref[0]
        if final_norm:
            y = y * lax.rsqrt(jnp.mean(y * y, axis=-1, keepdims=True) + NORM_EPS) * fg_ref[...]
        o_ref[0] = y


def _mlp(x, g, shift, scale, gate, w1, w2, final_g, tm, final_norm):
    b, t, d = x.shape
    ff = w1.shape[1]
    tf = _col_tile(ff, MLP_MAX_HIDDEN)
    vec3 = pl.BlockSpec((1, 1, d), lambda bi, i, f: (bi, 0, 0))
    vec2 = pl.BlockSpec((1, d), lambda bi, i, f: (0, 0))
    return pl.pallas_call(
        functools.partial(_mlp_kernel, final_norm=final_norm),
        out_shape=jax.ShapeDtypeStruct((b, t, d), F32),
        grid=(b, t // tm, ff // tf),
        in_specs=[pl.BlockSpec((1, tm, d), lambda bi, i, f: (bi, i, 0)), vec2, vec3, vec3, vec3,
                  pl.BlockSpec((d, tf), lambda bi, i, f: (0, f)),
                  pl.BlockSpec((tf, d), lambda bi, i, f: (f, 0)), vec2],
        out_specs=pl.BlockSpec((1, tm, d), lambda bi, i, f: (bi, i, 0)),
        scratch_shapes=[pltpu.VMEM((tm, d), MXU)],
        compiler_params=_cparams(("parallel", "parallel", "arbitrary")),
        name="mlp",
    )(x, g.reshape(1, d), shift, scale, gate, w1, w2, final_g.reshape(1, d))


NA_GROUP_OFFSETS = (0, NA_ROWS // 2, NA_ROWS)
NA_STRIP_PAD = NA_WIN - NA_ROWS
NA_STRIP_BLOCKS = 2 * NA_WIN


def _na_window_lo(typ, local_row):
    return (0, local_row, NA_WIN - NA_ROWS)[typ]


def _na_kernel(q_ref, k_ref, v_ref, kc_ref, vc_ref, tc_ref, o_ref, bias_scr):
    g = pl.program_id(2)
    n_g = pl.num_programs(2)
    n_keys = NA_WIN * GRID_W
    n_blocks = q_ref.shape[2] // LANES
    tq = NA_GROUP * GRID_W

    @pl.when(jnp.logical_and(pl.program_id(1) == 0, g == 0))
    def _():
        key_row = _lane((GRID_W, n_keys)) // GRID_W
        for blk in range(n_blocks):
            for typ in range(3):
                for half in range(HEADS_PER_BLOCK):
                    for lr in range(NA_GROUP):
                        rho = NA_GROUP_OFFSETS[typ] + lr
                        off = (NA_WIN - 1 - rho) * GRID_W
                        strip = tc_ref[blk, half, :, off:off + n_keys]
                        lo = _na_window_lo(typ, lr)
                        ok = jnp.logical_and(key_row >= lo, key_row < lo + NA_ROWS)
                        r0 = (half * NA_GROUP + lr) * GRID_W
                        bias_scr[blk, typ, r0:r0 + GRID_W, :] = jnp.where(ok, strip, MASK_VALUE)

    win_row = jnp.clip(g * NA_GROUP - NA_ROWS // 2, 0, n_g * NA_GROUP - NA_WIN)
    typ = jnp.where(g == 0, 0, jnp.where(g == n_g - 1, 2, 1))
    start = pl.multiple_of(win_row * GRID_W, GRID_W)
    lanes = lambda blk: slice(blk * LANES, (blk + 1) * LANES)
    block = lambda h: lanes(h // HEADS_PER_BLOCK)
    chunks = [(lambda blk: k_ref[0, pl.ds(start, n_keys), lanes(blk)],
               lambda h: v_ref[0, pl.ds(start, n_keys), block(h)],
               lambda h: bias_scr[h // HEADS_PER_BLOCK, typ, pl.ds((h % HEADS_PER_BLOCK) * tq, tq), :]),
              (lambda blk: kc_ref[0, :, lanes(blk)], lambda h: vc_ref[0, :, block(h)], None)]
    outs = _softmax_pv([q_ref[0, :, lanes(blk)] for blk in range(n_blocks)], chunks, keys_on_rows=False)
    for blk, o in enumerate(outs):
        o_ref[0, :, lanes(blk)] = o.astype(o_ref.dtype)


def _na_col_table(rpb, n_heads):
    col = np.arange(GRID_W)
    c0 = np.clip(col - NA_COLS // 2, 0, GRID_W - NA_COLS)
    col_ok = (col[None, :] >= c0[:, None]) & (col[None, :] < c0[:, None] + NA_COLS)
    dcol = col[None, :] - col[:, None] + NA_COLS - 1
    n_dr, n_dc = 2 * NA_ROWS - 1, 2 * NA_COLS - 1
    onehot = np.concatenate([(dcol[None] == np.arange(n_dc)[:, None, None]) & col_ok[None],
                             ~col_ok[None], np.ones((1, GRID_W, GRID_W), bool)])
    vals = jnp.concatenate([rpb * LOG2E, jnp.full((n_heads, n_dr, 1), MASK_VALUE, F32),
                            jnp.zeros((n_heads, n_dr, 1), F32)], axis=2)
    pad_rows = jnp.zeros((n_heads, 1, n_dc + 2), F32).at[:, :, n_dc + 1].set(MASK_VALUE)
    vals = jnp.concatenate([jnp.tile(pad_rows, (1, NA_STRIP_PAD, 1)), vals,
                            jnp.tile(pad_rows, (1, NA_STRIP_BLOCKS - NA_STRIP_PAD - n_dr, 1))], axis=1)
    tc = jnp.einsum("hrd,dqk->hqrk", vals, jnp.asarray(onehot, F32), precision=lax.Precision.HIGHEST)
    return tc.reshape(n_heads // 2, HEADS_PER_BLOCK, GRID_W, NA_STRIP_BLOCKS * GRID_W)


def _na_attn(qkv, kvc, col_table, width):
    b, t, _ = qkv.shape
    ctx = kvc.shape[1]
    bw = NA_STEP_BLOCKS * LANES
    nb = width // bw
    tq = NA_GROUP * GRID_W
    rows = t // GRID_W
    assert rows % NA_GROUP == 0 and rows >= NA_WIN and width % bw == 0
    return pl.pallas_call(
        _na_kernel,
        out_shape=jax.ShapeDtypeStruct((b, t, width), MXU),
        grid=(nb, b, rows // NA_GROUP),
        in_specs=[pl.BlockSpec((1, tq, bw), lambda p, bi, g: (bi, g, p)),
                  pl.BlockSpec((1, t, bw), lambda p, bi, g: (bi, 0, nb + p)),
                  pl.BlockSpec((1, t, bw), lambda p, bi, g: (bi, 0, 2 * nb + p)),
                  pl.BlockSpec((1, ctx, bw), lambda p, bi, g: (bi, 0, p)),
                  pl.BlockSpec((1, ctx, bw), lambda p, bi, g: (bi, 0, nb + p)),
                  pl.BlockSpec((NA_STEP_BLOCKS, HEADS_PER_BLOCK, GRID_W, NA_STRIP_BLOCKS * GRID_W),
                               lambda p, bi, g: (p, 0, 0, 0))],
        out_specs=pl.BlockSpec((1, tq, bw), lambda p, bi, g: (bi, g, p)),
        scratch_shapes=[pltpu.VMEM((NA_STEP_BLOCKS, 3, HEADS_PER_BLOCK * tq, NA_WIN * GRID_W), F32)],
        compiler_params=_cparams(("arbitrary", "arbitrary", "arbitrary")),
        name="na_attn",
    )(qkv, qkv, qkv, kvc, kvc, col_table)


def _rope_tables(t):
    pos = jnp.arange(t, dtype=jnp.int32)
    row = (pos // GRID_W).astype(F32)
    col = (pos % GRID_W).astype(F32)
    pairs = HEAD_DIM // 4
    inv = ROPE_THETA ** (-jnp.arange(pairs, dtype=F32) / pairs)
    ang = jnp.concatenate([row[:, None] * inv, col[:, None] * inv], axis=-1)
    cos, sin = jnp.cos(ang), jnp.sin(ang)
    cos_h = jnp.concatenate([cos, cos], axis=-1)
    sin_h = jnp.concatenate([-sin, sin], axis=-1)
    return jnp.tile(cos_h, (1, HEADS_PER_BLOCK)), jnp.tile(sin_h, (1, HEADS_PER_BLOCK))


def _block_diag2(top, bottom):
    z_t = jnp.zeros_like(top)
    z_b = jnp.zeros_like(bottom)
    return jnp.concatenate([jnp.concatenate([top, z_t], axis=1), jnp.concatenate([z_b, bottom], axis=1)], axis=0)


def kernel(x, c, ctx, c_ctx, l0_norm1, l0_norm2, l0_ada_w, l0_ada_b, l0_w_in, l0_shift_mu, l0_w0_f, l0_w0_b, l0_ww2_f, l0_ww2_b, l0_a0_f, l0_a0_b, l0_wa2_f, l0_wa2_b, l0_wg2, l0_k_k, l0_k_a, l0_r_k, l0_lnx_g, l0_lnx_b, l0_q_norm, l0_k_norm, l0_w_out, l0_mlp_w1, l0_mlp_w2, l1_norm1, l1_norm2, l1_ada_w, l1_ada_b, l1_w_qkv, l1_rpb, l1_w_out, l1_mlp_w1, l1_mlp_w2, final_norm):
    b, t, d = x.shape
    n_ctx = ctx.shape[1]
    dim = l0_w0_f.shape[0]
    q_width = l0_w_out.shape[0] - dim
    gqa_cols = l0_w_in.shape[1] - l0_shift_mu.shape[0]
    kv_width = (gqa_cols - q_width) // 2
    rw_cols = l0_shift_mu.shape[0]
    rw_pad = 3 * dim + 4 * LANES
    assert rw_cols <= rw_pad and 3 * dim + 2 * LANES == rw_cols - l0_wg2.shape[0]
    tm = min(ROW_TILE, t)
    tm_c = n_ctx

    assert b < SUBLANES
    cc = jnp.zeros((SUBLANES, d), F32).at[:b].set(c).at[b].set(c_ctx)

    def modulation(ada_w, ada_b):
        mod = _ada_mod(cc, ada_w, ada_b)
        lat = mod[:b].reshape(b, 6, 1, d)
        cx = jnp.broadcast_to(mod[b].reshape(1, 6, 1, d), (b, 6, 1, d))
        return [lat[:, i] for i in range(6)], [cx[:, i] for i in range(6)]

    mod_l, mod_c = modulation(l0_ada_w, l0_ada_b)

    w_in = l0_w_in.astype(MXU)
    w_rw = jnp.pad(w_in[:, gqa_cols:], ((0, 0), (0, rw_pad - rw_cols)))
    mu = jnp.pad(l0_shift_mu, (0, rw_pad - rw_cols)).reshape(1, rw_pad)
    lw_w = _block_diag2(l0_ww2_f, l0_ww2_b).astype(MXU)
    la_w = _block_diag2(l0_wa2_f, l0_wa2_b).astype(MXU)
    lg_w = jnp.pad(l0_wg2, ((0, 2 * LANES - l0_wg2.shape[0]), (0, 0))).astype(MXU)
    w0 = jnp.concatenate([l0_w0_f, l0_w0_b]).reshape(1, 2 * dim)
    a0 = jnp.concatenate([l0_a0_f, l0_a0_b]).reshape(1, 2 * dim)
    k_k = l0_k_k.reshape(1, dim)
    k_a = l0_k_a.reshape(1, dim)
    r_k = l0_r_k.reshape(1, dim)
    qg = jnp.tile(l0_q_norm, HEADS_PER_BLOCK).reshape(1, LANES)
    kg = jnp.tile(l0_k_norm, HEADS_PER_BLOCK).reshape(1, LANES)
    cos_l, sin_l = _rope_tables(t)
    cos_c, sin_c = jnp.ones((n_ctx, LANES), F32), jnp.zeros((n_ctx, LANES), F32)
    w_out = l0_w_out.astype(MXU)

    def half_layer0(xs, mod, tm_, cos, sin, is_ctx):
        pg = _inproj(xs, l0_norm1, mod[0], mod[1], w_in, tm_, F32, col0=0, n=gqa_cols)
        pr = _inproj(xs, l0_norm1, mod[0], mod[1], w_rw, tm_, F32)
        q, kd, vd = _gqa_prep(pg, cos, sin, qg, kg, q_width, kv_width, tm_)
        prep = _rwkv_prep(pr, mu, lw_w, la_w, lg_w, w0, a0, k_k, k_a, r_k, dim, RWKV_PREP_ROWS, is_ctx)
        return q, kd, vd, prep

    q_c, kd_c, vd_c, prep_c = half_layer0(ctx, mod_c, tm_c, cos_c, sin_c, True)
    q_l, kd_l, vd_l, prep_l = half_layer0(x, mod_l, tm, cos_l, sin_l, False)

    o_gqa_l = _gqa_attn(q_l, [(kd_l, vd_l), (kd_c, vd_c)], min(GQA_QUERY_TILE, t))
    o_gqa_c = _gqa_attn(q_c, [(kd_c, vd_c)], n_ctx)

    h0 = jnp.zeros((2, b, dim // LANES, LANES, LANES), F32)
    y_c, h_c = _rwkv_scan(*prep_c[:6], h0)
    y_l, _ = _rwkv_scan(*prep_l[:6], h_c)
    o_rw_l = _rwkv_finish(y_l, prep_l[6], prep_l[7], l0_lnx_g, l0_lnx_b, tm)
    o_rw_c = _rwkv_finish(y_c, prep_c[6], prep_c[7], l0_lnx_g, l0_lnx_b, tm_c)

    w1 = l0_mlp_w1.astype(MXU)
    w2 = l0_mlp_w2.astype(MXU)
    x = _outproj(x, mod_l[2], [(o_gqa_l, w_out, 0), (o_rw_l, w_out, q_width)], tm)
    x = _mlp(x, l0_norm2, mod_l[3], mod_l[4], mod_l[5], w1, w2, final_norm, min(MLP_ROW_TILE, t), False)
    ctx = _outproj(ctx, mod_c[2], [(o_gqa_c, w_out, 0), (o_rw_c, w_out, q_width)], tm_c)
    ctx = _mlp(ctx.reshape(1, b * n_ctx, d), l0_norm2, mod_c[3][:1], mod_c[4][:1], mod_c[5][:1], w1, w2, final_norm,
               _col_tile(b * n_ctx, MLP_ROW_TILE), False).reshape(b, n_ctx, d)

    mod_l, mod_c = modulation(l1_ada_w, l1_ada_b)
    width = l1_w_out.shape[0]
    n_heads = width // HEAD_DIM
    scale = jnp.concatenate([jnp.full((width,), ATTN_SCALE, F32), jnp.ones((2 * width,), F32)])
    w_qkv = (l1_w_qkv * scale).astype(MXU)
    qkv = _inproj(x, l1_norm1, mod_l[0], mod_l[1], w_qkv, min(2 * tm, t), MXU)
    kvc = _inproj(ctx, l1_norm1, mod_c[0], mod_c[1], w_qkv, tm_c, MXU, col0=width, n=2 * width)
    o_na = _na_attn(qkv, kvc, _na_col_table(l1_rpb, n_heads), width)
    x = _outproj(x, mod_l[2], [(o_na, l1_w_out.astype(MXU), 0)], tm)
    x = _mlp(x, l1_norm2, mod_l[3], mod_l[4], mod_l[5], l1_mlp_w1.astype(MXU), l1_mlp_w2.astype(MXU),
             final_norm, min(MLP_ROW_TILE, t), True)
    return x
```

```python
import functools

import jax
import jax.numpy as jnp
import numpy as np
from jax import lax
from jax.experimental import pallas as pl
from jax.experimental.pallas import tpu as pltpu

F32 = jnp.float32
MXU = jnp.bfloat16

LANES = 128
SUBLANES = 8
ROW_TILE = 512
RWKV_PREP_ROWS = 256
ADA_MAX_COLS = 1024
HEAD_DIM = 64
HEADS_PER_BLOCK = LANES // HEAD_DIM
GRID_W = 64
NORM_EPS = 1e-6
LNX_EPS = 64e-5
ROPE_THETA = 10000.0
NA_ROWS = 8
NA_COLS = 16
NA_GROUP = 4
NA_WIN = NA_GROUP + NA_ROWS
NA_STEP_BLOCKS = 4
GQA_STEP_KV_HEADS = 2
GQA_QUERY_TILE = 512
MASK_VALUE = -1e30
LOG2E = float(np.log2(np.e))
ATTN_SCALE = HEAD_DIM ** -0.5 * LOG2E
ATTN_KEY_CHUNK = 2048
INPROJ_MAX_COLS = 2048
INPROJ_RESIDENT_BYTES = 16 * 1024 * 1024
INPROJ_VMEM_BUDGET = 46 * 1024 * 1024
MLP_MAX_HIDDEN = 1024
MLP_ROW_TILE = 512
SCAN_CHUNK = 64
SCAN_PAIRS = 8
SCAN_STEP_CHUNKS = 4
VMEM_LIMIT = 56 * 1024 * 1024


def _cparams(sem):
    return pltpu.CompilerParams(dimension_semantics=sem, vmem_limit_bytes=VMEM_LIMIT)


def _mm(a, b):
    return jnp.dot(a.astype(MXU), b.astype(MXU), preferred_element_type=F32)


def _mm_nt(a, b):
    return lax.dot_general(a.astype(MXU), b.astype(MXU), (((1,), (1,)), ((), ())),
                           preferred_element_type=F32)


def _mm_tn(a, b):
    return lax.dot_general(a.astype(MXU), b.astype(MXU), (((0,), (0,)), ((), ())),
                           preferred_element_type=F32)


def _split3(x):
    hi = x.astype(MXU)
    r1 = x - hi.astype(F32)
    mid = r1.astype(MXU)
    lo = (r1 - mid.astype(F32)).astype(MXU)
    return hi, mid, lo


def _head_ones():
    r = lax.broadcasted_iota(jnp.int32, (LANES, LANES), 0) // HEAD_DIM
    c = lax.broadcasted_iota(jnp.int32, (LANES, LANES), 1) // HEAD_DIM
    return jnp.where(r == c, 1.0, 0.0).astype(MXU)


def _headsum(x, ones):
    hi, mid, lo = _split3(x)
    return (jnp.dot(hi, ones, preferred_element_type=F32) + jnp.dot(mid, ones, preferred_element_type=F32)
            + jnp.dot(lo, ones, preferred_element_type=F32))


def _lane(shape):
    return lax.broadcasted_iota(jnp.int32, shape, len(shape) - 1)


def _sigmoid(x):
    return 1.0 / (1.0 + jnp.exp(-x))


def _ada_kernel(c_ref, w_ref, b_ref, o_ref):
    c = c_ref[...]
    a = c * _sigmoid(c)
    o_ref[...] = _mm(a, w_ref[...]) + b_ref[...]


def _ada_mod(cc, w, bias):
    d, n = w.shape
    tn = _col_tile(n, ADA_MAX_COLS)
    return pl.pallas_call(
        _ada_kernel,
        out_shape=jax.ShapeDtypeStruct((cc.shape[0], n), F32),
        grid=(n // tn,),
        in_specs=[pl.BlockSpec((cc.shape[0], d), lambda j: (0, 0)),
                  pl.BlockSpec((d, tn), lambda j: (0, j)),
                  pl.BlockSpec((1, tn), lambda j: (0, j))],
        out_specs=pl.BlockSpec((cc.shape[0], tn), lambda j: (0, j)),
        compiler_params=_cparams(("arbitrary",)),
        name="ada_mod",
    )(cc, w, bias.reshape(1, n))


def _norm_mod(x, g, shift, scale):
    y = x * lax.rsqrt(jnp.mean(x * x, axis=-1, keepdims=True) + NORM_EPS)
    return (y * g) * (1.0 + scale) + shift


def _inproj_kernel(x_ref, g_ref, sh_ref, sc_ref, w_ref, o_ref, a_scr):
    @pl.when(pl.program_id(2) == 0)
    def _():
        a_scr[...] = _norm_mod(x_ref[0], g_ref[...], sh_ref[0], sc_ref[0]).astype(a_scr.dtype)

    o_ref[0] = jnp.dot(a_scr[...], w_ref[...], preferred_element_type=F32).astype(o_ref.dtype)


def _col_tile(n, cap):
    return max(tn for tn in range(LANES, cap + 1, LANES) if n % tn == 0)


def _inproj(x, g, shift, scale, w, tm, out_dtype, col0=0, n=None):
    b, t, d = x.shape
    n = w.shape[1] if n is None else n
    if d * n * w.dtype.itemsize <= INPROJ_RESIDENT_BYTES and col0 % n == 0:
        tn = n
    else:
        tn = _col_tile(int(np.gcd(n, col0)) if col0 else n, INPROJ_MAX_COLS)
    j0 = col0 // tn
    out_bytes = jnp.dtype(out_dtype).itemsize

    def vmem(tm_):
        return 2 * (tm_ * d * 4 + d * tn * w.dtype.itemsize + tm_ * tn * out_bytes) + tm_ * d * w.dtype.itemsize

    while vmem(tm) > INPROJ_VMEM_BUDGET and tm % 16 == 0:
        tm //= 2
    return pl.pallas_call(
        _inproj_kernel,
        out_shape=jax.ShapeDtypeStruct((b, t, n), out_dtype),
        grid=(b, t // tm, n // tn),
        in_specs=[pl.BlockSpec((1, tm, d), lambda bi, i, j: (bi, i, 0)),
                  pl.BlockSpec((1, d), lambda bi, i, j: (0, 0)),
                  pl.BlockSpec((1, 1, d), lambda bi, i, j: (bi, 0, 0)),
                  pl.BlockSpec((1, 1, d), lambda bi, i, j: (bi, 0, 0)),
                  pl.BlockSpec((d, tn), lambda bi, i, j: (0, j0 + j))],
        out_specs=pl.BlockSpec((1, tm, tn), lambda bi, i, j: (bi, i, j)),
        scratch_shapes=[pltpu.VMEM((tm, d), MXU)],
        compiler_params=_cparams(("parallel", "parallel", "arbitrary")),
        name="inproj",
    )(x, g.reshape(1, d), shift, scale, w)


def _swap_half_heads(x):
    first = (_lane(x.shape) % HEAD_DIM) < HEAD_DIM // 2
    return jnp.where(first, pltpu.roll(x, LANES - HEAD_DIM // 2, 1), pltpu.roll(x, HEAD_DIM // 2, 1))


def _gqa_prep_kernel(q_ref, kv_ref, cos_ref, sin_ref, qg_ref, kg_ref, qo_ref, ko_ref, vo_ref, *, n_q, n_kv):
    ones = _head_ones()
    cos = cos_ref[...]
    sin = sin_ref[...]
    first_head = _lane(cos.shape) < HEAD_DIM

    def norm_rope(x, g):
        ms = _headsum(x * x, ones) * (1.0 / HEAD_DIM)
        y = x * lax.rsqrt(ms + NORM_EPS) * g
        return y * cos + _swap_half_heads(y) * sin

    def dup(x, half):
        rolled = pltpu.roll(x, HEAD_DIM, 1)
        return jnp.where(first_head, x, rolled) if half == 0 else jnp.where(first_head, rolled, x)

    for j in range(n_q):
        x = q_ref[0, :, j * LANES:(j + 1) * LANES]
        qo_ref[0, :, j * LANES:(j + 1) * LANES] = (norm_rope(x, qg_ref[...]) * ATTN_SCALE).astype(qo_ref.dtype)
    for j in range(n_kv):
        k = norm_rope(kv_ref[0, :, j * LANES:(j + 1) * LANES], kg_ref[...])
        vt = kv_ref[0, :, (n_kv + j) * LANES:(n_kv + j + 1) * LANES].T
        for half in range(HEADS_PER_BLOCK):
            ko_ref[0, HEADS_PER_BLOCK * j + half] = dup(k, half).astype(ko_ref.dtype)
            vo_ref[0, HEADS_PER_BLOCK * j + half] = vt[half * HEAD_DIM:(half + 1) * HEAD_DIM].astype(vo_ref.dtype)


def _gqa_prep(p, cos, sin, qg, kg, q_width, kv_width, tm):
    b, t, _ = p.shape
    n_q = q_width // LANES
    n_kv = kv_width // LANES
    kvh = kv_width // HEAD_DIM
    kern = functools.partial(_gqa_prep_kernel, n_q=n_q, n_kv=n_kv)
    kv_spec = pl.BlockSpec((1, kvh, tm, LANES), lambda bi, i: (bi, 0, i, 0))
    vt_spec = pl.BlockSpec((1, kvh, HEAD_DIM, tm), lambda bi, i: (bi, 0, 0, i))
    return pl.pallas_call(
        kern,
        out_shape=(jax.ShapeDtypeStruct((b, t, q_width), MXU),
                   jax.ShapeDtypeStruct((b, kvh, t, LANES), MXU),
                   jax.ShapeDtypeStruct((b, kvh, HEAD_DIM, t), MXU)),
        grid=(b, t // tm),
        in_specs=[pl.BlockSpec((1, tm, q_width), lambda bi, i: (bi, i, 0)),
                  pl.BlockSpec((1, tm, 2 * kv_width), lambda bi, i: (bi, i, q_width // (2 * kv_width))),
                  pl.BlockSpec((tm, LANES), lambda bi, i: (i, 0)),
                  pl.BlockSpec((tm, LANES), lambda bi, i: (i, 0)),
                  pl.BlockSpec((1, LANES), lambda bi, i: (0, 0)),
                  pl.BlockSpec((1, LANES), lambda bi, i: (0, 0))],
        out_specs=(pl.BlockSpec((1, tm, q_width), lambda bi, i: (bi, i, 0)), kv_spec, vt_spec),
        compiler_params=_cparams(("parallel", "parallel")),
        name="gqa_prep",
    )(p, p, cos, sin, qg, kg)


def _softmax_pv(q_blocks, chunks, keys_on_rows):
    first_head = _lane(q_blocks[0].shape) < HEAD_DIM
    zero = jnp.zeros_like(q_blocks[0])
    n_heads = HEADS_PER_BLOCK * len(q_blocks)
    axis = 0 if keys_on_rows else 1
    s = [[None] * len(chunks) for _ in range(n_heads)]
    m = [None] * n_heads
    l = [None] * n_heads
    acc = [None] * n_heads
    for stage in range(n_heads + 1):
        ha, hb = stage, stage - 1
        if ha < n_heads:
            q = q_blocks[ha // HEADS_PER_BLOCK]
            qa = jnp.where(first_head, q, zero) if ha % HEADS_PER_BLOCK == 0 else jnp.where(first_head, zero, q)
        for c, (load_k, load_v, load_bias) in enumerate(chunks):
            if ha < n_heads:
                k = load_k(ha // HEADS_PER_BLOCK)
                sc = _mm_nt(k, qa) if keys_on_rows else _mm_nt(qa, k)
                if load_bias is not None:
                    sc = sc + load_bias(ha)
                s[ha][c] = sc
                mc = jnp.max(sc, axis=axis, keepdims=True)
                m[ha] = mc if m[ha] is None else jnp.maximum(m[ha], mc)
            if hb >= 0:
                p = jnp.exp2(s[hb][c] - m[hb])
                s[hb][c] = None
                lc = jnp.sum(p, axis=axis, keepdims=True)
                pv = _mm(load_v(hb), p) if keys_on_rows else _mm(p, load_v(hb))
                l[hb] = lc if l[hb] is None else l[hb] + lc
                acc[hb] = pv if acc[hb] is None else acc[hb] + pv
    o = [a / d for a, d in zip(acc, l)]
    if keys_on_rows:
        return [jnp.concatenate(o[2 * j:2 * j + 2], axis=0).T for j in range(len(q_blocks))]
    return [jnp.where(first_head, o[2 * j], o[2 * j + 1]) for j in range(len(q_blocks))]


def _gqa_attn_kernel(q_ref, *refs, n_sets):
    o_ref = refs[-1]
    n_blocks = q_ref.shape[2] // LANES
    heads_per_kv = HEADS_PER_BLOCK * n_blocks // refs[0].shape[1]
    chunks = []
    for i in range(n_sets):
        k_ref, vt_ref = refs[2 * i], refs[2 * i + 1]
        n_keys = k_ref.shape[2]
        for start in range(0, n_keys, ATTN_KEY_CHUNK):
            size = min(ATTN_KEY_CHUNK, n_keys - start)
            chunks.append((lambda blk, r=k_ref, s=start, z=size: r[0, HEADS_PER_BLOCK * blk // heads_per_kv, pl.ds(s, z), :],
                           lambda h, r=vt_ref, s=start, z=size: r[0, h // heads_per_kv, :, pl.ds(s, z)], None))
    outs = _softmax_pv([q_ref[0, :, j * LANES:(j + 1) * LANES] for j in range(n_blocks)], chunks, keys_on_rows=True)
    for j, o in enumerate(outs):
        o_ref[0, :, j * LANES:(j + 1) * LANES] = o.astype(o_ref.dtype)


def _gqa_attn(q, kv_sets, tq):
    b, t, width = q.shape
    kvh = kv_sets[0][0].shape[1]
    kv_step = min(GQA_STEP_KV_HEADS, kvh)
    group_width = (width // kvh) * kv_step
    in_specs = [pl.BlockSpec((1, tq, group_width), lambda bi, g, i: (bi, i, g))]
    args = [q]
    for k, v in kv_sets:
        s = k.shape[2]
        in_specs += [pl.BlockSpec((1, kv_step, s, LANES), lambda bi, g, i: (bi, g, 0, 0)),
                     pl.BlockSpec((1, kv_step, HEAD_DIM, s), lambda bi, g, i: (bi, g, 0, 0))]
        args += [k, v]
    return pl.pallas_call(
        functools.partial(_gqa_attn_kernel, n_sets=len(kv_sets)),
        out_shape=jax.ShapeDtypeStruct((b, t, width), MXU),
        grid=(b, kvh // kv_step, t // tq),
        in_specs=in_specs,
        out_specs=pl.BlockSpec((1, tq, group_width), lambda bi, g, i: (bi, i, g)),
        compiler_params=_cparams(("parallel", "parallel", "arbitrary")),
        name="gqa_attn",
    )(*args)


def _rwkv_prep_kernel(p_ref, up_ref, dn_ref, mu_ref, lw_w_ref, la_w_ref, lg_w_ref, w0_ref, a0_ref, kk_ref, ka_ref,
                      rk_ref, r_o, v_o, kk_o, lw_o, kd_o, bb_o, bonus_o, gate_o, xm_scr, *, is_ctx, dim):
    tm = p_ref.shape[1]
    n_all = p_ref.shape[2] // LANES
    n_dim = dim // LANES
    i = pl.program_id(1)
    n_i = pl.num_programs(1)
    row = lax.broadcasted_iota(jnp.int32, (tm, LANES), 0)
    cls = _lane((tm, LANES)) % 4
    if is_ctx:
        first = row == 0
        last = row == tm - 1
    else:
        first = (row % GRID_W) == 0
        last = (row % GRID_W) == GRID_W - 1
        up_ok = jnp.logical_or(row >= GRID_W, i > 0)
        dn_ok = jnp.logical_or(row < tm - GRID_W, i < n_i - 1)

    for j in range(n_all):
        sl = slice(j * LANES, (j + 1) * LANES)
        p = p_ref[0, :, sl]
        prev = jnp.where(first, 0.0, pltpu.roll(p, 1, 0))
        nxt = jnp.where(last, 0.0, pltpu.roll(p, tm - 1, 0))
        if is_ctx:
            sh = jnp.where(cls % 2 == 0, prev, nxt)
        else:
            up = jnp.concatenate([up_ref[0, :, sl], p[:tm - GRID_W]], axis=0)
            dn = jnp.concatenate([p[GRID_W:], dn_ref[0, :, sl]], axis=0)
            up = jnp.where(up_ok, up, 0.0)
            dn = jnp.where(dn_ok, dn, 0.0)
            sh = jnp.where(cls == 0, prev, jnp.where(cls == 1, nxt, jnp.where(cls == 2, up, dn)))
        xm_scr[:, sl] = p + mu_ref[:, sl] * (sh - p)

    ones = _head_ones()
    x_w = jnp.tanh(xm_scr[:, 3 * dim:3 * dim + LANES]).astype(MXU)
    x_a = xm_scr[:, 3 * dim + LANES:3 * dim + 2 * LANES].astype(MXU)
    x_g = _sigmoid(xm_scr[:, 3 * dim + 2 * LANES:3 * dim + 4 * LANES]).astype(MXU)
    for j in range(n_dim):
        sl = slice(j * LANES, (j + 1) * LANES)
        r = xm_scr[:, sl]
        k = xm_scr[:, dim + j * LANES:dim + (j + 1) * LANES]
        v = xm_scr[:, 2 * dim + j * LANES:2 * dim + (j + 1) * LANES]
        kkr = k * kk_ref[:, sl]
        kk = kkr * lax.rsqrt(jnp.maximum(_headsum(kkr * kkr, ones), 1e-12))
        r_o[0, :, sl] = r
        v_o[0, :, sl] = v
        kk_o[0, :, sl] = kk
        gate_o[0, :, sl] = jnp.dot(x_g, lg_w_ref[:, sl], preferred_element_type=F32)
        bonus = jnp.zeros_like(r)
        for d in range(2):
            dsl = slice(d * dim + j * LANES, d * dim + (j + 1) * LANES)
            z = w0_ref[:, dsl] + jnp.dot(x_w, lw_w_ref[:, dsl], preferred_element_type=F32)
            softplus = jnp.maximum(-z, 0.0) + jnp.log(1.0 + jnp.exp(-jnp.abs(z)))
            lw_o[d, 0, :, sl] = -jnp.exp(-softplus - 0.5)
            iclr = _sigmoid(a0_ref[:, dsl] + jnp.dot(x_a, la_w_ref[:, dsl], preferred_element_type=F32))
            kd = k * (1.0 + (iclr - 1.0) * ka_ref[:, sl])
            kd_o[d, 0, :, sl] = kd
            bb_o[d, 0, :, sl] = kk * iclr
            bonus = bonus + _headsum(r * kd * rk_ref[:, sl], ones) * v
        bonus_o[0, :, sl] = bonus


def _rwkv_prep(p, mu, lw_w, la_w, lg_w, w0, a0, k_k, k_a, r_k, dim, tm, is_ctx):
    b, t, width = p.shape
    if is_ctx:
        assert t == tm
    else:
        assert tm % GRID_W == 0 and t % tm == 0
    hb = tm // GRID_W
    n_halo = t // GRID_W
    kern = functools.partial(_rwkv_prep_kernel, is_ctx=is_ctx, dim=dim)
    vec = lambda n: pl.BlockSpec((1, n), lambda bi, i: (0, 0))
    mat = lambda k, n: pl.BlockSpec((k, n), lambda bi, i: (0, 0))
    one = jax.ShapeDtypeStruct((b, t, dim), F32)
    two = jax.ShapeDtypeStruct((2, b, t, dim), F32)
    one_spec = pl.BlockSpec((1, tm, dim), lambda bi, i: (bi, i, 0))
    two_spec = pl.BlockSpec((2, 1, tm, dim), lambda bi, i: (0, bi, i, 0))
    return pl.pallas_call(
        kern,
        out_shape=(one, one, one, two, two, two, one, one),
        grid=(b, t // tm),
        in_specs=[pl.BlockSpec((1, tm, width), lambda bi, i: (bi, i, 0)),
                  pl.BlockSpec((1, GRID_W, width), lambda bi, i: (bi, jnp.maximum(i * hb - 1, 0), 0)),
                  pl.BlockSpec((1, GRID_W, width), lambda bi, i: (bi, jnp.minimum((i + 1) * hb, n_halo - 1), 0)),
                  vec(width), mat(LANES, 2 * dim), mat(LANES, 2 * dim), mat(2 * LANES, dim),
                  vec(2 * dim), vec(2 * dim), vec(dim), vec(dim), vec(dim)],
        out_specs=(one_spec, one_spec, one_spec, two_spec, two_spec, two_spec, one_spec, one_spec),
        scratch_shapes=[pltpu.VMEM((tm, width), F32)],
        compiler_params=_cparams(("parallel", "parallel")),
        name="rwkv_prep_ctx" if is_ctx else "rwkv_prep",
    )(p, p, p, mu, lw_w, la_w, lg_w, w0, a0, k_k, k_a, r_k)


def _scan_chunk_terms(r, v, kk, lw, kd, bb, sgn):
    blocks = range(len(r))
    c = r[0].shape[0]
    n = 2 * c
    ri = lax.broadcasted_iota(jnp.int32, (n, n), 0)
    ci = lax.broadcasted_iota(jnp.int32, (n, n), 1)
    same = (ri // c) == (ci // c)
    dt = ((ri % c) - (ci % c)) * sgn
    strict = jnp.logical_and(same, dt > 0)
    incl = jnp.logical_and(same, dt >= 0)
    eye = ri == ci
    head_lanes = (ri // c) == (ci // HEAD_DIM)
    zero = jnp.zeros((n, n), F32)
    time = lax.broadcasted_iota(jnp.int32, (c, LANES), 0)
    sgn_f = sgn.astype(F32)
    rev_f = 0.5 - 0.5 * sgn_f

    def stack(x):
        return jnp.concatenate([x, x], axis=0)

    def own(x):
        return jnp.where(head_lanes, stack(x), zero)

    def prefix(x):
        step = 1
        while step < c:
            x = x + jnp.where(time >= step, pltpu.roll(x, step, 0), 0.0)
            step *= 2
        return x

    tot = [jnp.sum(lw[p], axis=0, keepdims=True) for p in blocks]
    pre = [prefix(lw[p]) for p in blocks]
    cum = [rev_f * (tot[p] + lw[p]) + sgn_f * pre[p] for p in blocks]
    e_inv = [jnp.exp(-cum[p]) for p in blocks]
    e_fin = [jnp.exp(tot[p] - cum[p]) for p in blocks]
    at = [own(-kk[p] * jnp.exp(cum[p] - lw[p])) for p in blocks]
    rt = [own(r[p] * jnp.exp(cum[p])) for p in blocks]
    bt = [stack(bb[p] * e_inv[p]) for p in blocks]
    kt = [stack(kd[p] * e_inv[p]) for p in blocks]
    bhat = [own(bb[p] * e_fin[p]) for p in blocks]
    khat = [own(kd[p] * e_fin[p]) for p in blocks]
    vbd = [own(v[p]) for p in blocks]

    full = [_mm_nt(jnp.concatenate([at[p], rt[p]], axis=0), jnp.concatenate([bt[p], kt[p]], axis=0)) for p in blocks]
    a_ab = [jnp.where(strict, full[p][:n, :n], zero) for p in blocks]
    a_ak = [jnp.where(strict, full[p][:n, n:], zero) for p in blocks]
    a_rb = [jnp.where(incl, full[p][n:, :n], zero) for p in blocks]
    a_rk = [jnp.where(incl, full[p][n:, n:], zero) for p in blocks]

    ident = jnp.where(eye, 1.0, 0.0)
    inv = [ident + a_ab[p] for p in blocks]
    power = a_ab
    for _ in range(int(np.log2(c)) - 1):
        power = [_mm(power[p], power[p]) for p in blocks]
        inv = [inv[p] + _mm(inv[p], power[p]) for p in blocks]

    xv = [_mm(jnp.concatenate([a_ak[p], a_rk[p]], axis=0), vbd[p]) for p in blocks]
    w12 = [_mm(inv[p], jnp.concatenate([at[p], xv[p][:n]], axis=1)) for p in blocks]
    yw = [_mm(a_rb[p], w12[p]) for p in blocks]
    mn = [_mm_tn(bhat[p], w12[p]) for p in blocks]
    py = [rt[p] + yw[p][:, :n] for p in blocks]
    yl = [yw[p][:, n:] + xv[p][n:] for p in blocks]
    m = [jnp.where(eye, jnp.exp(tot[p]), zero) + mn[p][:, :n] for p in blocks]
    nn = [mn[p][:, n:] + _mm_tn(khat[p], vbd[p]) for p in blocks]
    return py, yl, m, nn


def _scan_kernel(r_ref, v_ref, kk_ref, lw_ref, kd_ref, bb_ref, h0_ref, y_ref, ht_ref, h_scr, *, pairs, chunks):
    d = pl.program_id(0)
    ci = pl.program_id(2)
    sgn = 1 - 2 * d
    c = SCAN_CHUNK
    n = 2 * c

    @pl.when(ci == 0)
    def _():
        h_scr[...] = h0_ref[0, 0]

    rows = [pl.ds(pl.multiple_of((d * (chunks - 1) + sgn * s) * c, c), c) for s in range(chunks)]
    lanes = [slice(j * LANES, (j + 1) * LANES) for j in range(pairs)]
    probs = [(rw, ln) for rw in rows for ln in lanes]
    py, yl, m, nn = _scan_chunk_terms(
        [r_ref[0, rw, ln] for rw, ln in probs], [v_ref[0, rw, ln] for rw, ln in probs],
        [kk_ref[0, rw, ln] for rw, ln in probs], [lw_ref[0, 0, rw, ln] for rw, ln in probs],
        [kd_ref[0, 0, rw, ln] for rw, ln in probs], [bb_ref[0, 0, rw, ln] for rw, ln in probs], sgn)

    h = [h_scr[j] for j in range(pairs)]
    for s in range(chunks):
        idx = [s * pairs + j for j in range(pairs)]
        yh = [_mm(jnp.concatenate([py[i], m[i]], axis=0), h[j]) for j, i in enumerate(idx)]
        for j, i in enumerate(idx):
            y2 = yh[j][:n] + yl[i]
            y_ref[0, 0, rows[s], lanes[j]] = y2[:c] + y2[c:]
        h = [yh[j][n:] + nn[i] for j, i in enumerate(idx)]
    for j in range(pairs):
        h_scr[j] = h[j]

    @pl.when(ci == pl.num_programs(2) - 1)
    def _():
        ht_ref[0, 0] = h_scr[...]


def _rwkv_scan(r, v, kk, lw, kd, bb, h0):
    b, t, dim = r.shape
    c = SCAN_CHUNK * SCAN_STEP_CHUNKS
    assert t % c == 0
    nch = t // c
    width = SCAN_PAIRS * LANES
    nblk = dim // width

    def chunk(d, ci):
        return ci + d * (nch - 1 - 2 * ci)

    one = pl.BlockSpec((1, c, width), lambda d, g, ci: (g // nblk, chunk(d, ci), g % nblk))
    two = pl.BlockSpec((1, 1, c, width), lambda d, g, ci: (d, g // nblk, chunk(d, ci), g % nblk))
    st = pl.BlockSpec((1, 1, SCAN_PAIRS, LANES, LANES), lambda d, g, ci: (d, g // nblk, g % nblk, 0, 0))
    return pl.pallas_call(
        functools.partial(_scan_kernel, pairs=SCAN_PAIRS, chunks=SCAN_STEP_CHUNKS),
        out_shape=(jax.ShapeDtypeStruct((2, b, t, dim), F32), jax.ShapeDtypeStruct(h0.shape, F32)),
        grid=(2, b * nblk, nch),
        in_specs=[one, one, one, two, two, two, st],
        out_specs=(two, st),
        scratch_shapes=[pltpu.VMEM((SCAN_PAIRS, LANES, LANES), F32)],
        compiler_params=_cparams(("parallel", "parallel", "arbitrary")),
        name="rwkv_scan",
    )(r, v, kk, lw, kd, bb, h0)


def _rwkv_finish_kernel(y_ref, bonus_ref, gate_ref, g_ref, b_ref, o_ref):
    ones = _head_ones()
    for j in range(o_ref.shape[2] // LANES):
        sl = slice(j * LANES, (j + 1) * LANES)
        y = y_ref[0, 0, :, sl] + y_ref[1, 0, :, sl]
        mu = _headsum(y, ones) * (1.0 / HEAD_DIM)
        dlt = y - mu
        var = _headsum(dlt * dlt, ones) * (1.0 / HEAD_DIM)
        yn = dlt * lax.rsqrt(var + LNX_EPS)
        o_ref[0, :, sl] = ((yn * g_ref[:, sl] + b_ref[:, sl] + bonus_ref[0, :, sl]) * gate_ref[0, :, sl]).astype(o_ref.dtype)


def _rwkv_finish(y, bonus, gate, lnx_g, lnx_b, tm):
    _, b, t, dim = y.shape
    one_spec = pl.BlockSpec((1, tm, dim), lambda bi, i: (bi, i, 0))
    vec = pl.BlockSpec((1, dim), lambda bi, i: (0, 0))
    return pl.pallas_call(
        _rwkv_finish_kernel,
        out_shape=jax.ShapeDtypeStruct((b, t, dim), MXU),
        grid=(b, t // tm),
        in_specs=[pl.BlockSpec((2, 1, tm, dim), lambda bi, i: (0, bi, i, 0)), one_spec, one_spec, vec, vec],
        out_specs=one_spec,
        compiler_params=_cparams(("parallel", "parallel")),
        name="rwkv_finish",
    )(y, bonus, gate, lnx_g.reshape(1, dim), lnx_b.reshape(1, dim))


def _outproj_kernel(*refs, n_in):
    x_ref, gate_ref = refs[0], refs[1]
    o_ref = refs[-1]
    acc = None
    for i in range(n_in):
        part = jnp.dot(refs[2 + 2 * i][0], refs[3 + 2 * i][...], preferred_element_type=F32)
        acc = part if acc is None else acc + part
    o_ref[0] = x_ref[0] + gate_ref[0] * acc


def _outproj(x, gate, parts, tm):
    b, t, d = x.shape
    in_specs = [pl.BlockSpec((1, tm, d), lambda bi, i: (bi, i, 0)),
                pl.BlockSpec((1, 1, d), lambda bi, i: (bi, 0, 0))]
    args = [x, gate]
    for a, w, row0 in parts:
        k = a.shape[2]
        assert row0 % k == 0
        in_specs += [pl.BlockSpec((1, tm, k), lambda bi, i: (bi, i, 0)),
                     pl.BlockSpec((k, d), lambda bi, i, r=row0 // k: (r, 0))]
        args += [a, w]
    return pl.pallas_call(
        functools.partial(_outproj_kernel, n_in=len(parts)),
        out_shape=jax.ShapeDtypeStruct((b, t, d), F32),
        grid=(b, t // tm),
        in_specs=in_specs,
        out_specs=pl.BlockSpec((1, tm, d), lambda bi, i: (bi, i, 0)),
        compiler_params=_cparams(("parallel", "parallel")),
        name="outproj",
    )(*args)


def _mlp_kernel(x_ref, g_ref, sh_ref, sc_ref, gate_ref, w1_ref, w2_ref, fg_ref, o_ref, a_scr, *, final_norm):
    f = pl.program_id(2)

    @pl.when(f == 0)
    def _():
        a_scr[...] = _norm_mod(x_ref[0], g_ref[...], sh_ref[0], sc_ref[0]).astype(a_scr.dtype)
        o_ref[...] = jnp.zeros_like(o_ref)

    h = jnp.dot(a_scr[...], w1_ref[...], preferred_element_type=F32)
    h = jnp.square(jnp.maximum(h, 0.0))
    o_ref[0] += jnp.dot(h.astype(MXU), w2_ref[...], preferred_element_type=F32)

    @pl.when(f == pl.num_programs(2) - 1)
    def _():
        y = x_ref[0] + gate_ref[0] * o_ref[0]
        if final_norm:
            y = y * lax.rsqrt(jnp.mean(y * y, axis=-1, keepdims=True) + NORM_EPS) * fg_ref[...]
        o_ref[0] = y


def _mlp(x, g, shift, scale, gate, w1, w2, final_g, tm, final_norm):
    b, t, d = x.shape
    ff = w1.shape[1]
    tf = _col_tile(ff, MLP_MAX_HIDDEN)
    vec3 = pl.BlockSpec((1, 1, d), lambda bi, i, f: (bi, 0, 0))
    vec2 = pl.BlockSpec((1, d), lambda bi, i, f: (0, 0))
    return pl.pallas_call(
        functools.partial(_mlp_kernel, final_norm=final_norm),
        out_shape=jax.ShapeDtypeStruct((b, t, d), F32),
        grid=(b, t // tm, ff // tf),
        in_specs=[pl.BlockSpec((1, tm, d), lambda bi, i, f: (bi, i, 0)), vec2, vec3, vec3, vec3,
                  pl.BlockSpec((d, tf), lambda bi, i, f: (0, f)),
                  pl.BlockSpec((tf, d), lambda bi, i, f: (f, 0)), vec2],
        out_specs=pl.BlockSpec((1, tm, d), lambda bi, i, f: (bi, i, 0)),
        scratch_shapes=[pltpu.VMEM((tm, d), MXU)],
        compiler_params=_cparams(("parallel", "parallel", "arbitrary")),
        name="mlp",
    )(x, g.reshape(1, d), shift, scale, gate, w1, w2, final_g.reshape(1, d))


NA_GROUP_OFFSETS = (0, NA_ROWS // 2, NA_ROWS)
NA_STRIP_PAD = NA_WIN - NA_ROWS
NA_STRIP_BLOCKS = 2 * NA_WIN


def _na_window_lo(typ, local_row):
    return (0, local_row, NA_WIN - NA_ROWS)[typ]


def _na_kernel(q_ref, k_ref, v_ref, kc_ref, vc_ref, tc_ref, o_ref, bias_scr):
    g = pl.program_id(2)
    n_g = pl.num_programs(2)
    n_keys = NA_WIN * GRID_W
    n_blocks = q_ref.shape[2] // LANES
    tq = NA_GROUP * GRID_W

    @pl.when(jnp.logical_and(pl.program_id(1) == 0, g == 0))
    def _():
        key_row = _lane((GRID_W, n_keys)) // GRID_W
        for blk in range(n_blocks):
            for typ in range(3):
                for half in range(HEADS_PER_BLOCK):
                    for lr in range(NA_GROUP):
                        rho = NA_GROUP_OFFSETS[typ] + lr
                        off = (NA_WIN - 1 - rho) * GRID_W
                        strip = tc_ref[blk, half, :, off:off + n_keys]
                        lo = _na_window_lo(typ, lr)
                        ok = jnp.logical_and(key_row >= lo, key_row < lo + NA_ROWS)
                        r0 = (half * NA_GROUP + lr) * GRID_W
                        bias_scr[blk, typ, r0:r0 + GRID_W, :] = jnp.where(ok, strip, MASK_VALUE)

    win_row = jnp.clip(g * NA_GROUP - NA_ROWS // 2, 0, n_g * NA_GROUP - NA_WIN)
    typ = jnp.where(g == 0, 0, jnp.where(g == n_g - 1, 2, 1))
    start = pl.multiple_of(win_row * GRID_W, GRID_W)
    lanes = lambda blk: slice(blk * LANES, (blk + 1) * LANES)
    block = lambda h: lanes(h // HEADS_PER_BLOCK)
    chunks = [(lambda blk: k_ref[0, pl.ds(start, n_keys), lanes(blk)],
               lambda h: v_ref[0, pl.ds(start, n_keys), block(h)],
               lambda h: bias_scr[h // HEADS_PER_BLOCK, typ, pl.ds((h % HEADS_PER_BLOCK) * tq, tq), :]),
              (lambda blk: kc_ref[0, :, lanes(blk)], lambda h: vc_ref[0, :, block(h)], None)]
    outs = _softmax_pv([q_ref[0, :, lanes(blk)] for blk in range(n_blocks)], chunks, keys_on_rows=False)
    for blk, o in enumerate(outs):
        o_ref[0, :, lanes(blk)] = o.astype(o_ref.dtype)


def _na_col_table(rpb, n_heads):
    col = np.arange(GRID_W)
    c0 = np.clip(col - NA_COLS // 2, 0, GRID_W - NA_COLS)
    col_ok = (col[None, :] >= c0[:, None]) & (col[None, :] < c0[:, None] + NA_COLS)
    dcol = col[None, :] - col[:, None] + NA_COLS - 1
    n_dr, n_dc = 2 * NA_ROWS - 1, 2 * NA_COLS - 1
    onehot = np.concatenate([(dcol[None] == np.arange(n_dc)[:, None, None]) & col_ok[None],
                             ~col_ok[None], np.ones((1, GRID_W, GRID_W), bool)])
    vals = jnp.concatenate([rpb * LOG2E, jnp.full((n_heads, n_dr, 1), MASK_VALUE, F32),
                            jnp.zeros((n_heads, n_dr, 1), F32)], axis=2)
    pad_rows = jnp.zeros((n_heads, 1, n_dc + 2), F32).at[:, :, n_dc + 1].set(MASK_VALUE)
    vals = jnp.concatenate([jnp.tile(pad_rows, (1, NA_STRIP_PAD, 1)), vals,
                            jnp.tile(pad_rows, (1, NA_STRIP_BLOCKS - NA_STRIP_PAD - n_dr, 1))], axis=1)
    tc = jnp.einsum("hrd,dqk->hqrk", vals, jnp.asarray(onehot, F32), precision=lax.Precision.HIGHEST)
    return tc.reshape(n_heads // 2, HEADS_PER_BLOCK, GRID_W, NA_STRIP_BLOCKS * GRID_W)


def _na_attn(qkv, kvc, col_table, width):
    b, t, _ = qkv.shape
    ctx = kvc.shape[1]
    bw = NA_STEP_BLOCKS * LANES
    nb = width // bw
    tq = NA_GROUP * GRID_W
    rows = t // GRID_W
    assert rows % NA_GROUP == 0 and rows >= NA_WIN and width % bw == 0
    return pl.pallas_call(
        _na_kernel,
        out_shape=jax.ShapeDtypeStruct((b, t, width), MXU),
        grid=(nb, b, rows // NA_GROUP),
        in_specs=[pl.BlockSpec((1, tq, bw), lambda p, bi, g: (bi, g, p)),
                  pl.BlockSpec((1, t, bw), lambda p, bi, g: (bi, 0, nb + p)),
                  pl.BlockSpec((1, t, bw), lambda p, bi, g: (bi, 0, 2 * nb + p)),
                  pl.BlockSpec((1, ctx, bw), lambda p, bi, g: (bi, 0, p)),
                  pl.BlockSpec((1, ctx, bw), lambda p, bi, g: (bi, 0, nb + p)),
                  pl.BlockSpec((NA_STEP_BLOCKS, HEADS_PER_BLOCK, GRID_W, NA_STRIP_BLOCKS * GRID_W),
                               lambda p, bi, g: (p, 0, 0, 0))],
        out_specs=pl.BlockSpec((1, tq, bw), lambda p, bi, g: (bi, g, p)),
        scratch_shapes=[pltpu.VMEM((NA_STEP_BLOCKS, 3, HEADS_PER_BLOCK * tq, NA_WIN * GRID_W), F32)],
        compiler_params=_cparams(("arbitrary", "arbitrary", "arbitrary")),
        name="na_attn",
    )(qkv, qkv, qkv, kvc, kvc, col_table)


def _rope_tables(t):
    pos = jnp.arange(t, dtype=jnp.int32)
    row = (pos // GRID_W).astype(F32)
    col = (pos % GRID_W).astype(F32)
    pairs = HEAD_DIM // 4
    inv = ROPE_THETA ** (-jnp.arange(pairs, dtype=F32) / pairs)
    ang = jnp.concatenate([row[:, None] * inv, col[:, None] * inv], axis=-1)
    cos, sin = jnp.cos(ang), jnp.sin(ang)
    cos_h = jnp.concatenate([cos, cos], axis=-1)
    sin_h = jnp.concatenate([-sin, sin], axis=-1)
    return jnp.tile(cos_h, (1, HEADS_PER_BLOCK)), jnp.tile(sin_h, (1, HEADS_PER_BLOCK))


def _block_diag2(top, bottom):
    z_t = jnp.zeros_like(top)
    z_b = jnp.zeros_like(bottom)
    return jnp.concatenate([jnp.concatenate([top, z_t], axis=1), jnp.concatenate([z_b, bottom], axis=1)], axis=0)


def kernel(x, c, ctx, c_ctx, l0_norm1, l0_norm2, l0_ada_w, l0_ada_b, l0_w_in, l0_shift_mu, l0_w0_f, l0_w0_b, l0_ww2_f, l0_ww2_b, l0_a0_f, l0_a0_b, l0_wa2_f, l0_wa2_b, l0_wg2, l0_k_k, l0_k_a, l0_r_k, l0_lnx_g, l0_lnx_b, l0_q_norm, l0_k_norm, l0_w_out, l0_mlp_w1, l0_mlp_w2, l1_norm1, l1_norm2, l1_ada_w, l1_ada_b, l1_w_qkv, l1_rpb, l1_w_out, l1_mlp_w1, l1_mlp_w2, final_norm):
    b, t, d = x.shape
    n_ctx = ctx.shape[1]
    dim = l0_w0_f.shape[0]
    q_width = l0_w_out.shape[0] - dim
    gqa_cols = l0_w_in.shape[1] - l0_shift_mu.shape[0]
    kv_width = (gqa_cols - q_width) // 2
    rw_cols = l0_shift_mu.shape[0]
    rw_pad = 3 * dim + 4 * LANES
    assert rw_cols <= rw_pad and 3 * dim + 2 * LANES == rw_cols - l0_wg2.shape[0]
    tm = min(ROW_TILE, t)
    tm_c = n_ctx

    assert b < SUBLANES
    cc = jnp.zeros((SUBLANES, d), F32).at[:b].set(c).at[b].set(c_ctx)

    def modulation(ada_w, ada_b):
        mod = _ada_mod(cc, ada_w, ada_b)
        lat = mod[:b].reshape(b, 6, 1, d)
        cx = jnp.broadcast_to(mod[b].reshape(1, 6, 1, d), (b, 6, 1, d))
        return [lat[:, i] for i in range(6)], [cx[:, i] for i in range(6)]

    mod_l, mod_c = modulation(l0_ada_w, l0_ada_b)

    w_in = l0_w_in.astype(MXU)
    w_rw = jnp.pad(w_in[:, gqa_cols:], ((0, 0), (0, rw_pad - rw_cols)))
    mu = jnp.pad(l0_shift_mu, (0, rw_pad - rw_cols)).reshape(1, rw_pad)
    lw_w = _block_diag2(l0_ww2_f, l0_ww2_b).astype(MXU)
    la_w = _block_diag2(l0_wa2_f, l0_wa2_b).astype(MXU)
    lg_w = jnp.pad(l0_wg2, ((0, 2 * LANES - l0_wg2.shape[0]), (0, 0))).astype(MXU)
    w0 = jnp.concatenate([l0_w0_f, l0_w0_b]).reshape(1, 2 * dim)
    a0 = jnp.concatenate([l0_a0_f, l0_a0_b]).reshape(1, 2 * dim)
    k_k = l0_k_k.reshape(1, dim)
    k_a = l0_k_a.reshape(1, dim)
    r_k = l0_r_k.reshape(1, dim)
    qg = jnp.tile(l0_q_norm, HEADS_PER_BLOCK).reshape(1, LANES)
    kg = jnp.tile(l0_k_norm, HEADS_PER_BLOCK).reshape(1, LANES)
    cos_l, sin_l = _rope_tables(t)
    cos_c, sin_c = jnp.ones((n_ctx, LANES), F32), jnp.zeros((n_ctx, LANES), F32)
    w_out = l0_w_out.astype(MXU)

    def half_layer0(xs, mod, tm_, cos, sin, is_ctx):
        pg = _inproj(xs, l0_norm1, mod[0], mod[1], w_in, tm_, F32, col0=0, n=gqa_cols)
        pr = _inproj(xs, l0_norm1, mod[0], mod[1], w_rw, tm_, F32)
        q, kd, vd = _gqa_prep(pg, cos, sin, qg, kg, q_width, kv_width, tm_)
        prep = _rwkv_prep(pr, mu, lw_w, la_w, lg_w, w0, a0, k_k, k_a, r_k, dim, RWKV_PREP_ROWS, is_ctx)
        return q, kd, vd, prep

    q_c, kd_c, vd_c, prep_c = half_layer0(ctx, mod_c, tm_c, cos_c, sin_c, True)
    q_l, kd_l, vd_l, prep_l = half_layer0(x, mod_l, tm, cos_l, sin_l, False)

    o_gqa_l = _gqa_attn(q_l, [(kd_l, vd_l), (kd_c, vd_c)], min(GQA_QUERY_TILE, t))
    o_gqa_c = _gqa_attn(q_c, [(kd_c, vd_c)], n_ctx)

    h0 = jnp.zeros((2, b, dim // LANES, LANES, LANES), F32)
    y_c, h_c = _rwkv_scan(*prep_c[:6], h0)
    y_l, _ = _rwkv_scan(*prep_l[:6], h_c)
    o_rw_l = _rwkv_finish(y_l, prep_l[6], prep_l[7], l0_lnx_g, l0_lnx_b, tm)
    o_rw_c = _rwkv_finish(y_c, prep_c[6], prep_c[7], l0_lnx_g, l0_lnx_b, tm_c)

    w1 = l0_mlp_w1.astype(MXU)
    w2 = l0_mlp_w2.astype(MXU)
    x = _outproj(x, mod_l[2], [(o_gqa_l, w_out, 0), (o_rw_l, w_out, q_width)], tm)
    x = _mlp(x, l0_norm2, mod_l[3], mod_l[4], mod_l[5], w1, w2, final_norm, min(MLP_ROW_TILE, t), False)
    ctx = _outproj(ctx, mod_c[2], [(o_gqa_c, w_out, 0), (o_rw_c, w_out, q_width)], tm_c)
    ctx = _mlp(ctx.reshape(1, b * n_ctx, d), l0_norm2, mod_c[3][:1], mod_c[4][:1], mod_c[5][:1], w1, w2, final_norm,
               _col_tile(b * n_ctx, MLP_ROW_TILE), False).reshape(b, n_ctx, d)

    mod_l, mod_c = modulation(l1_ada_w, l1_ada_b)
    width = l1_w_out.shape[0]
    n_heads = width // HEAD_DIM
    scale = jnp.concatenate([jnp.full((width,), ATTN_SCALE, F32), jnp.ones((2 * width,), F32)])
    w_qkv = (l1_w_qkv * scale).astype(MXU)
    qkv = _inproj(x, l1_norm1, mod_l[0], mod_l[1], w_qkv, min(2 * tm, t), MXU)
    kvc = _inproj(ctx, l1_norm1, mod_c[0], mod_c[1], w_qkv, tm_c, MXU, col0=width, n=2 * width)
    o_na = _na_attn(qkv, kvc, _na_col_table(l1_rpb, n_heads), width)
    x = _outproj(x, mod_l[2], [(o_na, l1_w_out.astype(MXU), 0)], tm)
    x = _mlp(x, l1_norm2, mod_l[3], mod_l[4], mod_l[5], l1_mlp_w1.astype(MXU), l1_mlp_w2.astype(MXU),
             final_norm, min(MLP_ROW_TILE, t), True)
    return x
```

```python
import functools

import jax
import jax.numpy as jnp
import numpy as np
from jax import lax
from jax.experimental import pallas as pl
from jax.experimental.pallas import tpu as pltpu

F32 = jnp.float32
MXU = jnp.bfloat16

LANES = 128
SUBLANES = 8
ROW_TILE = 512
RWKV_PREP_ROWS = 256
ADA_MAX_COLS = 1024
HEAD_DIM = 64
HEADS_PER_BLOCK = LANES // HEAD_DIM
GRID_W = 64
NORM_EPS = 1e-6
LNX_EPS = 64e-5
ROPE_THETA = 10000.0
NA_ROWS = 8
NA_COLS = 16
NA_GROUP = 4
NA_WIN = NA_GROUP + NA_ROWS
NA_STEP_BLOCKS = 4
GQA_STEP_KV_HEADS = 2
GQA_QUERY_TILE = 512
MASK_VALUE = -1e30
LOG2E = float(np.log2(np.e))
ATTN_SCALE = HEAD_DIM ** -0.5 * LOG2E
ATTN_KEY_CHUNK = 2048
INPROJ_MAX_COLS = 2048
INPROJ_RESIDENT_BYTES = 16 * 1024 * 1024
INPROJ_VMEM_BUDGET = 46 * 1024 * 1024
MLP_MAX_HIDDEN = 1024
MLP_ROW_TILE = 512
SCAN_CHUNK = 64
SCAN_PAIRS = 8
SCAN_STEP_CHUNKS = 4
VMEM_LIMIT = 56 * 1024 * 1024


def _cparams(sem):
    return pltpu.CompilerParams(dimension_semantics=sem, vmem_limit_bytes=VMEM_LIMIT)


def _mm(a, b):
    return jnp.dot(a.astype(MXU), b.astype(MXU), preferred_element_type=F32)


def _mm_nt(a, b):
    return lax.dot_general(a.astype(MXU), b.astype(MXU), (((1,), (1,)), ((), ())),
                           preferred_element_type=F32)


def _mm_tn(a, b):
    return lax.dot_general(a.astype(MXU), b.astype(MXU), (((0,), (0,)), ((), ())),
                           preferred_element_type=F32)


def _split3(x):
    hi = x.astype(MXU)
    r1 = x - hi.astype(F32)
    mid = r1.astype(MXU)
    lo = (r1 - mid.astype(F32)).astype(MXU)
    return hi, mid, lo


def _head_ones():
    r = lax.broadcasted_iota(jnp.int32, (LANES, LANES), 0) // HEAD_DIM
    c = lax.broadcasted_iota(jnp.int32, (LANES, LANES), 1) // HEAD_DIM
    return jnp.where(r == c, 1.0, 0.0).astype(MXU)


def _headsum(x, ones):
    hi, mid, lo = _split3(x)
    return (jnp.dot(hi, ones, preferred_element_type=F32) + jnp.dot(mid, ones, preferred_element_type=F32)
            + jnp.dot(lo, ones, preferred_element_type=F32))


def _lane(shape):
    return lax.broadcasted_iota(jnp.int32, shape, len(shape) - 1)


def _sigmoid(x):
    return 1.0 / (1.0 + jnp.exp(-x))


def _ada_kernel(c_ref, w_ref, b_ref, o_ref):
    c = c_ref[...]
    a = c * _sigmoid(c)
    o_ref[...] = _mm(a, w_ref[...]) + b_ref[...]


def _ada_mod(cc, w, bias):
    d, n = w.shape
    tn = _col_tile(n, ADA_MAX_COLS)
    return pl.pallas_call(
        _ada_kernel,
        out_shape=jax.ShapeDtypeStruct((cc.shape[0], n), F32),
        grid=(n // tn,),
        in_specs=[pl.BlockSpec((cc.shape[0], d), lambda j: (0, 0)),
                  pl.BlockSpec((d, tn), lambda j: (0, j)),
                  pl.BlockSpec((1, tn), lambda j: (0, j))],
        out_specs=pl.BlockSpec((cc.shape[0], tn), lambda j: (0, j)),
        compiler_params=_cparams(("arbitrary",)),
        name="ada_mod",
    )(cc, w, bias.reshape(1, n))


def _norm_mod(x, g, shift, scale):
    y = x * lax.rsqrt(jnp.mean(x * x, axis=-1, keepdims=True) + NORM_EPS)
    return (y * g) * (1.0 + scale) + shift


def _inproj_kernel(x_ref, g_ref, sh_ref, sc_ref, w_ref, o_ref, a_scr):
    @pl.when(pl.program_id(2) == 0)
    def _():
        a_scr[...] = _norm_mod(x_ref[0], g_ref[...], sh_ref[0], sc_ref[0]).astype(a_scr.dtype)

    o_ref[0] = jnp.dot(a_scr[...], w_ref[...], preferred_element_type=F32).astype(o_ref.dtype)


def _col_tile(n, cap):
    return max(tn for tn in range(LANES, cap + 1, LANES) if n % tn == 0)


def _inproj(x, g, shift, scale, w, tm, out_dtype, col0=0, n=None):
    b, t, d = x.shape
    n = w.shape[1] if n is None else n
    if d * n * w.dtype.itemsize <= INPROJ_RESIDENT_BYTES and col0 % n == 0:
        tn = n
    else:
        tn = _col_tile(int(np.gcd(n, col0)) if col0 else n, INPROJ_MAX_COLS)
    j0 = col0 // tn
    out_bytes = jnp.dtype(out_dtype).itemsize

    def vmem(tm_):
        return 2 * (tm_ * d * 4 + d * tn * w.dtype.itemsize + tm_ * tn * out_bytes) + tm_ * d * w.dtype.itemsize

    while vmem(tm) > INPROJ_VMEM_BUDGET and tm % 16 == 0:
        tm //= 2
    return pl.pallas_call(
        _inproj_kernel,
        out_shape=jax.ShapeDtypeStruct((b, t, n), out_dtype),
        grid=(b, t // tm, n // tn),
        in_specs=[pl.BlockSpec((1, tm, d), lambda bi, i, j: (bi, i, 0)),
                  pl.BlockSpec((1, d), lambda bi, i, j: (0, 0)),
                  pl.BlockSpec((1, 1, d), lambda bi, i, j: (bi, 0, 0)),
                  pl.BlockSpec((1, 1, d), lambda bi, i, j: (bi, 0, 0)),
                  pl.BlockSpec((d, tn), lambda bi, i, j: (0, j0 + j))],
        out_specs=pl.BlockSpec((1, tm, tn), lambda bi, i, j: (bi, i, j)),
        scratch_shapes=[pltpu.VMEM((tm, d), MXU)],
        compiler_params=_cparams(("parallel", "parallel", "arbitrary")),
        name="inproj",
    )(x, g.reshape(1, d), shift, scale, w)


def _swap_half_heads(x):
    first = (_lane(x.shape) % HEAD_DIM) < HEAD_DIM // 2
    return jnp.where(first, pltpu.roll(x, LANES - HEAD_DIM // 2, 1), pltpu.roll(x, HEAD_DIM // 2, 1))


def _gqa_prep_kernel(q_ref, kv_ref, cos_ref, sin_ref, qg_ref, kg_ref, qo_ref, ko_ref, vo_ref, *, n_q, n_kv):
    ones = _head_ones()
    cos = cos_ref[...]
    sin = sin_ref[...]
    first_head = _lane(cos.shape) < HEAD_DIM

    def norm_rope(x, g):
        ms = _headsum(x * x, ones) * (1.0 / HEAD_DIM)
        y = x * lax.rsqrt(ms + NORM_EPS) * g
        return y * cos + _swap_half_heads(y) * sin

    def dup(x, half):
        rolled = pltpu.roll(x, HEAD_DIM, 1)
        return jnp.where(first_head, x, rolled) if half == 0 else jnp.where(first_head, rolled, x)

    for j in range(n_q):
        x = q_ref[0, :, j * LANES:(j + 1) * LANES]
        qo_ref[0, :, j * LANES:(j + 1) * LANES] = (norm_rope(x, qg_ref[...]) * ATTN_SCALE).astype(qo_ref.dtype)
    for j in range(n_kv):
        k = norm_rope(kv_ref[0, :, j * LANES:(j + 1) * LANES], kg_ref[...])
        vt = kv_ref[0, :, (n_kv + j) * LANES:(n_kv + j + 1) * LANES].T
        for half in range(HEADS_PER_BLOCK):
            ko_ref[0, HEADS_PER_BLOCK * j + half] = dup(k, half).astype(ko_ref.dtype)
            vo_ref[0, HEADS_PER_BLOCK * j + half] = vt[half * HEAD_DIM:(half + 1) * HEAD_DIM].astype(vo_ref.dtype)


def _gqa_prep(p, cos, sin, qg, kg, q_width, kv_width, tm):
    b, t, _ = p.shape
    n_q = q_width // LANES
    n_kv = kv_width // LANES
    kvh = kv_width // HEAD_DIM
    kern = functools.partial(_gqa_prep_kernel, n_q=n_q, n_kv=n_kv)
    kv_spec = pl.BlockSpec((1, kvh, tm, LANES), lambda bi, i: (bi, 0, i, 0))
    vt_spec = pl.BlockSpec((1, kvh, HEAD_DIM, tm), lambda bi, i: (bi, 0, 0, i))
    return pl.pallas_call(
        kern,
        out_shape=(jax.ShapeDtypeStruct((b, t, q_width), MXU),
                   jax.ShapeDtypeStruct((b, kvh, t, LANES), MXU),
                   jax.ShapeDtypeStruct((b, kvh, HEAD_DIM, t), MXU)),
        grid=(b, t // tm),
        in_specs=[pl.BlockSpec((1, tm, q_width), lambda bi, i: (bi, i, 0)),
                  pl.BlockSpec((1, tm, 2 * kv_width), lambda bi, i: (bi, i, q_width // (2 * kv_width))),
                  pl.BlockSpec((tm, LANES), lambda bi, i: (i, 0)),
                  pl.BlockSpec((tm, LANES), lambda bi, i: (i, 0)),
                  pl.BlockSpec((1, LANES), lambda bi, i: (0, 0)),
                  pl.BlockSpec((1, LANES), lambda bi, i: (0, 0))],
        out_specs=(pl.BlockSpec((1, tm, q_width), lambda bi, i: (bi, i, 0)), kv_spec, vt_spec),
        compiler_params=_cparams(("parallel", "parallel")),
        name="gqa_prep",
    )(p, p, cos, sin, qg, kg)


def _softmax_pv(q_blocks, chunks, keys_on_rows):
    first_head = _lane(q_blocks[0].shape) < HEAD_DIM
    zero = jnp.zeros_like(q_blocks[0])
    n_heads = HEADS_PER_BLOCK * len(q_blocks)
    axis = 0 if keys_on_rows else 1
    s = [[None] * len(chunks) for _ in range(n_heads)]
    m = [None] * n_heads
    l = [None] * n_heads
    acc = [None] * n_heads
    for stage in range(n_heads + 1):
        ha, hb = stage, stage - 1
        if ha < n_heads:
            q = q_blocks[ha // HEADS_PER_BLOCK]
            qa = jnp.where(first_head, q, zero) if ha % HEADS_PER_BLOCK == 0 else jnp.where(first_head, zero, q)
        for c, (load_k, load_v, load_bias) in enumerate(chunks):
            if ha < n_heads:
                k = load_k(ha // HEADS_PER_BLOCK)
                sc = _mm_nt(k, qa) if keys_on_rows else _mm_nt(qa, k)
                if load_bias is not None:
                    sc = sc + load_bias(ha)
                s[ha][c] = sc
                mc = jnp.max(sc, axis=axis, keepdims=True)
                m[ha] = mc if m[ha] is None else jnp.maximum(m[ha], mc)
            if hb >= 0:
                p = jnp.exp2(s[hb][c] - m[hb])
                s[hb][c] = None
                lc = jnp.sum(p, axis=axis, keepdims=True)
                pv = _mm(load_v(hb), p) if keys_on_rows else _mm(p, load_v(hb))
                l[hb] = lc if l[hb] is None else l[hb] + lc
                acc[hb] = pv if acc[hb] is None else acc[hb] + pv
    o = [a / d for a, d in zip(acc, l)]
    if keys_on_rows:
        return [jnp.concatenate(o[2 * j:2 * j + 2], axis=0).T for j in range(len(q_blocks))]
    return [jnp.where(first_head, o[2 * j], o[2 * j + 1]) for j in range(len(q_blocks))]


def _gqa_attn_kernel(q_ref, *refs, n_sets):
    o_ref = refs[-1]
    n_blocks = q_ref.shape[2] // LANES
    heads_per_kv = HEADS_PER_BLOCK * n_blocks // refs[0].shape[1]
    chunks = []
    for i in range(n_sets):
        k_ref, vt_ref = refs[2 * i], refs[2 * i + 1]
        n_keys = k_ref.shape[2]
        for start in range(0, n_keys, ATTN_KEY_CHUNK):
            size = min(ATTN_KEY_CHUNK, n_keys - start)
            chunks.append((lambda blk, r=k_ref, s=start, z=size: r[0, HEADS_PER_BLOCK * blk // heads_per_kv, pl.ds(s, z), :],
                           lambda h, r=vt_ref, s=start, z=size: r[0, h // heads_per_kv, :, pl.ds(s, z)], None))
    outs = _softmax_pv([q_ref[0, :, j * LANES:(j + 1) * LANES] for j in range(n_blocks)], chunks, keys_on_rows=True)
    for j, o in enumerate(outs):
        o_ref[0, :, j * LANES:(j + 1) * LANES] = o.astype(o_ref.dtype)


def _gqa_attn(q, kv_sets, tq):
    b, t, width = q.shape
    kvh = kv_sets[0][0].shape[1]
    kv_step = min(GQA_STEP_KV_HEADS, kvh)
    group_width = (width // kvh) * kv_step
    in_specs = [pl.BlockSpec((1, tq, group_width), lambda bi, g, i: (bi, i, g))]
    args = [q]
    for k, v in kv_sets:
        s = k.shape[2]
        in_specs += [pl.BlockSpec((1, kv_step, s, LANES), lambda bi, g, i: (bi, g, 0, 0)),
                     pl.BlockSpec((1, kv_step, HEAD_DIM, s), lambda bi, g, i: (bi, g, 0, 0))]
        args += [k, v]
    return pl.pallas_call(
        functools.partial(_gqa_attn_kernel, n_sets=len(kv_sets)),
        out_shape=jax.ShapeDtypeStruct((b, t, width), MXU),
        grid=(b, kvh // kv_step, t // tq),
        in_specs=in_specs,
        out_specs=pl.BlockSpec((1, tq, group_width), lambda bi, g, i: (bi, i, g)),
        compiler_params=_cparams(("parallel", "parallel", "arbitrary")),
        name="gqa_attn",
    )(*args)


def _rwkv_prep_kernel(p_ref, up_ref, dn_ref, mu_ref, lw_w_ref, la_w_ref, lg_w_ref, w0_ref, a0_ref, kk_ref, ka_ref,
                      rk_ref, r_o, v_o, kk_o, lw_o, kd_o, bb_o, bonus_o, gate_o, xm_scr, *, is_ctx, dim):
    tm = p_ref.shape[1]
    n_all = p_ref.shape[2] // LANES
    n_dim = dim // LANES
    i = pl.program_id(1)
    n_i = pl.num_programs(1)
    row = lax.broadcasted_iota(jnp.int32, (tm, LANES), 0)
    cls = _lane((tm, LANES)) % 4
    if is_ctx:
        first = row == 0
        last = row == tm - 1
    else:
        first = (row % GRID_W) == 0
        last = (row % GRID_W) == GRID_W - 1
        up_ok = jnp.logical_or(row >= GRID_W, i > 0)
        dn_ok = jnp.logical_or(row < tm - GRID_W, i < n_i - 1)

    for j in range(n_all):
        sl = slice(j * LANES, (j + 1) * LANES)
        p = p_ref[0, :, sl]
        prev = jnp.where(first, 0.0, pltpu.roll(p, 1, 0))
        nxt = jnp.where(last, 0.0, pltpu.roll(p, tm - 1, 0))
        if is_ctx:
            sh = jnp.where(cls % 2 == 0, prev, nxt)
        else:
            up = jnp.concatenate([up_ref[0, :, sl], p[:tm - GRID_W]], axis=0)
            dn = jnp.concatenate([p[GRID_W:], dn_ref[0, :, sl]], axis=0)
            up = jnp.where(up_ok, up, 0.0)
            dn = jnp.where(dn_ok, dn, 0.0)
            sh = jnp.where(cls == 0, prev, jnp.where(cls == 1, nxt, jnp.where(cls == 2, up, dn)))
        xm_scr[:, sl] = p + mu_ref[:, sl] * (sh - p)

    ones = _head_ones()
    x_w = jnp.tanh(xm_scr[:, 3 * dim:3 * dim + LANES]).astype(MXU)
    x_a = xm_scr[:, 3 * dim + LANES:3 * dim + 2 * LANES].astype(MXU)
    x_g = _sigmoid(xm_scr[:, 3 * dim + 2 * LANES:3 * dim + 4 * LANES]).astype(MXU)
    for j in range(n_dim):
        sl = slice(j * LANES, (j + 1) * LANES)
        r = xm_scr[:, sl]
        k = xm_scr[:, dim + j * LANES:dim + (j + 1) * LANES]
        v = xm_scr[:, 2 * dim + j * LANES:2 * dim + (j + 1) * LANES]
        kkr = k * kk_ref[:, sl]
        kk = kkr * lax.rsqrt(jnp.maximum(_headsum(kkr * kkr, ones), 1e-12))
        r_o[0, :, sl] = r.astype(r_o.dtype)
        v_o[0, :, sl] = v.astype(v_o.dtype)
        kk_o[0, :, sl] = kk.astype(kk_o.dtype)
        gate_o[0, :, sl] = jnp.dot(x_g, lg_w_ref[:, sl], preferred_element_type=F32)
        bonus = jnp.zeros_like(r)
        for d in range(2):
            dsl = slice(d * dim + j * LANES, d * dim + (j + 1) * LANES)
            z = w0_ref[:, dsl] + jnp.dot(x_w, lw_w_ref[:, dsl], preferred_element_type=F32)
            softplus = jnp.maximum(-z, 0.0) + jnp.log(1.0 + jnp.exp(-jnp.abs(z)))
            lw_o[d, 0, :, sl] = -jnp.exp(-softplus - 0.5)
            iclr = _sigmoid(a0_ref[:, dsl] + jnp.dot(x_a, la_w_ref[:, dsl], preferred_element_type=F32))
            kd = k * (1.0 + (iclr - 1.0) * ka_ref[:, sl])
            kd_o[d, 0, :, sl] = kd.astype(kd_o.dtype)
            bb_o[d, 0, :, sl] = (kk * iclr).astype(bb_o.dtype)
            bonus = bonus + _headsum(r * kd * rk_ref[:, sl], ones) * v
        bonus_o[0, :, sl] = bonus


def _rwkv_prep(p, mu, lw_w, la_w, lg_w, w0, a0, k_k, k_a, r_k, dim, tm, is_ctx):
    b, t, width = p.shape
    if is_ctx:
        assert t == tm
    else:
        assert tm % GRID_W == 0 and t % tm == 0
    hb = tm // GRID_W
    n_halo = t // GRID_W
    kern = functools.partial(_rwkv_prep_kernel, is_ctx=is_ctx, dim=dim)
    vec = lambda n: pl.BlockSpec((1, n), lambda bi, i: (0, 0))
    mat = lambda k, n: pl.BlockSpec((k, n), lambda bi, i: (0, 0))
    one = lambda dt: jax.ShapeDtypeStruct((b, t, dim), dt)
    two = lambda dt: jax.ShapeDtypeStruct((2, b, t, dim), dt)
    one_spec = pl.BlockSpec((1, tm, dim), lambda bi, i: (bi, i, 0))
    two_spec = pl.BlockSpec((2, 1, tm, dim), lambda bi, i: (0, bi, i, 0))
    return pl.pallas_call(
        kern,
        out_shape=(one(MXU), one(MXU), one(MXU), two(F32), two(MXU), two(MXU), one(F32), one(F32)),
        grid=(b, t // tm),
        in_specs=[pl.BlockSpec((1, tm, width), lambda bi, i: (bi, i, 0)),
                  pl.BlockSpec((1, GRID_W, width), lambda bi, i: (bi, jnp.maximum(i * hb - 1, 0), 0)),
                  pl.BlockSpec((1, GRID_W, width), lambda bi, i: (bi, jnp.minimum((i + 1) * hb, n_halo - 1), 0)),
                  vec(width), mat(LANES, 2 * dim), mat(LANES, 2 * dim), mat(2 * LANES, dim),
                  vec(2 * dim), vec(2 * dim), vec(dim), vec(dim), vec(dim)],
        out_specs=(one_spec, one_spec, one_spec, two_spec, two_spec, two_spec, one_spec, one_spec),
        scratch_shapes=[pltpu.VMEM((tm, width), F32)],
        compiler_params=_cparams(("parallel", "parallel")),
        name="rwkv_prep_ctx" if is_ctx else "rwkv_prep",
    )(p, p, p, mu, lw_w, la_w, lg_w, w0, a0, k_k, k_a, r_k)


def _scan_chunk_terms(r, v, kk, lw, kd, bb, sgn):
    blocks = range(len(r))
    c = r[0].shape[0]
    n = 2 * c
    ri = lax.broadcasted_iota(jnp.int32, (n, n), 0)
    ci = lax.broadcasted_iota(jnp.int32, (n, n), 1)
    same = (ri // c) == (ci // c)
    dt = ((ri % c) - (ci % c)) * sgn
    strict = jnp.logical_and(same, dt > 0)
    incl = jnp.logical_and(same, dt >= 0)
    eye = ri == ci
    head_lanes = (ri // c) == (ci // HEAD_DIM)
    zero = jnp.zeros((n, n), F32)
    time = lax.broadcasted_iota(jnp.int32, (c, LANES), 0)
    sgn_f = sgn.astype(F32)
    rev_f = 0.5 - 0.5 * sgn_f

    def stack(x):
        return jnp.concatenate([x, x], axis=0)

    def own(x):
        return jnp.where(head_lanes, stack(x), zero)

    def prefix(x):
        step = 1
        while step < c:
            x = x + jnp.where(time >= step, pltpu.roll(x, step, 0), 0.0)
            step *= 2
        return x

    tot = [jnp.sum(lw[p], axis=0, keepdims=True) for p in blocks]
    pre = [prefix(lw[p]) for p in blocks]
    cum = [rev_f * (tot[p] + lw[p]) + sgn_f * pre[p] for p in blocks]
    e_inv = [jnp.exp(-cum[p]) for p in blocks]
    e_fin = [jnp.exp(tot[p] - cum[p]) for p in blocks]
    at = [own(-kk[p] * jnp.exp(cum[p] - lw[p])) for p in blocks]
    rt = [own(r[p] * jnp.exp(cum[p])) for p in blocks]
    bt = [stack(bb[p] * e_inv[p]) for p in blocks]
    kt = [stack(kd[p] * e_inv[p]) for p in blocks]
    bhat = [own(bb[p] * e_fin[p]) for p in blocks]
    khat = [own(kd[p] * e_fin[p]) for p in blocks]
    vbd = [own(v[p]) for p in blocks]

    full = [_mm_nt(jnp.concatenate([at[p], rt[p]], axis=0), jnp.concatenate([bt[p], kt[p]], axis=0)) for p in blocks]
    a_ab = [jnp.where(strict, full[p][:n, :n], zero) for p in blocks]
    a_ak = [jnp.where(strict, full[p][:n, n:], zero) for p in blocks]
    a_rb = [jnp.where(incl, full[p][n:, :n], zero) for p in blocks]
    a_rk = [jnp.where(incl, full[p][n:, n:], zero) for p in blocks]

    ident = jnp.where(eye, 1.0, 0.0)
    inv = [ident + a_ab[p] for p in blocks]
    power = a_ab
    for _ in range(int(np.log2(c)) - 1):
        power = [_mm(power[p], power[p]) for p in blocks]
        inv = [inv[p] + _mm(inv[p], power[p]) for p in blocks]

    xv = [_mm(jnp.concatenate([a_ak[p], a_rk[p]], axis=0), vbd[p]) for p in blocks]
    w12 = [_mm(inv[p], jnp.concatenate([at[p], xv[p][:n]], axis=1)) for p in blocks]
    yw = [_mm(a_rb[p], w12[p]) for p in blocks]
    mn = [_mm_tn(bhat[p], w12[p]) for p in blocks]
    py = [rt[p] + yw[p][:, :n] for p in blocks]
    yl = [yw[p][:, n:] + xv[p][n:] for p in blocks]
    m = [jnp.where(eye, jnp.exp(tot[p]), zero) + mn[p][:, :n] for p in blocks]
    nn = [mn[p][:, n:] + _mm_tn(khat[p], vbd[p]) for p in blocks]
    return py, yl, m, nn


def _scan_kernel(r_ref, v_ref, kk_ref, lw_ref, kd_ref, bb_ref, h0_ref, y_ref, ht_ref, h_scr, *, pairs, chunks):
    d = pl.program_id(0)
    ci = pl.program_id(2)
    sgn = 1 - 2 * d
    c = SCAN_CHUNK
    n = 2 * c

    @pl.when(ci == 0)
    def _():
        h_scr[...] = h0_ref[0, 0]

    rows = [pl.ds(pl.multiple_of((d * (chunks - 1) + sgn * s) * c, c), c) for s in range(chunks)]
    lanes = [slice(j * LANES, (j + 1) * LANES) for j in range(pairs)]
    probs = [(rw, ln) for rw in rows for ln in lanes]
    py, yl, m, nn = _scan_chunk_terms(
        [r_ref[0, rw, ln] for rw, ln in probs], [v_ref[0, rw, ln] for rw, ln in probs],
        [kk_ref[0, rw, ln] for rw, ln in probs], [lw_ref[0, 0, rw, ln] for rw, ln in probs],
        [kd_ref[0, 0, rw, ln] for rw, ln in probs], [bb_ref[0, 0, rw, ln] for rw, ln in probs], sgn)

    h = [h_scr[j] for j in range(pairs)]
    for s in range(chunks):
        idx = [s * pairs + j for j in range(pairs)]
        yh = [_mm(jnp.concatenate([py[i], m[i]], axis=0), h[j]) for j, i in enumerate(idx)]
        for j, i in enumerate(idx):
            y2 = yh[j][:n] + yl[i]
            y_ref[0, 0, rows[s], lanes[j]] = y2[:c] + y2[c:]
        h = [yh[j][n:] + nn[i] for j, i in enumerate(idx)]
    for j in range(pairs):
        h_scr[j] = h[j]

    @pl.when(ci == pl.num_programs(2) - 1)
    def _():
        ht_ref[0, 0] = h_scr[...]


def _rwkv_scan(r, v, kk, lw, kd, bb, h0):
    b, t, dim = r.shape
    c = SCAN_CHUNK * SCAN_STEP_CHUNKS
    assert t % c == 0
    nch = t // c
    width = SCAN_PAIRS * LANES
    nblk = dim // width

    def chunk(d, ci):
        return ci + d * (nch - 1 - 2 * ci)

    one = pl.BlockSpec((1, c, width), lambda d, g, ci: (g // nblk, chunk(d, ci), g % nblk))
    two = pl.BlockSpec((1, 1, c, width), lambda d, g, ci: (d, g // nblk, chunk(d, ci), g % nblk))
    st = pl.BlockSpec((1, 1, SCAN_PAIRS, LANES, LANES), lambda d, g, ci: (d, g // nblk, g % nblk, 0, 0))
    return pl.pallas_call(
        functools.partial(_scan_kernel, pairs=SCAN_PAIRS, chunks=SCAN_STEP_CHUNKS),
        out_shape=(jax.ShapeDtypeStruct((2, b, t, dim), F32), jax.ShapeDtypeStruct(h0.shape, F32)),
        grid=(2, b * nblk, nch),
        in_specs=[one, one, one, two, two, two, st],
        out_specs=(two, st),
        scratch_shapes=[pltpu.VMEM((SCAN_PAIRS, LANES, LANES), F32)],
        compiler_params=_cparams(("parallel", "parallel", "arbitrary")),
        name="rwkv_scan",
    )(r, v, kk, lw, kd, bb, h0)


def _rwkv_finish_kernel(y_ref, bonus_ref, gate_ref, g_ref, b_ref, o_ref):
    ones = _head_ones()
    for j in range(o_ref.shape[2] // LANES):
        sl = slice(j * LANES, (j + 1) * LANES)
        y = y_ref[0, 0, :, sl] + y_ref[1, 0, :, sl]
        mu = _headsum(y, ones) * (1.0 / HEAD_DIM)
        dlt = y - mu
        var = _headsum(dlt * dlt, ones) * (1.0 / HEAD_DIM)
        yn = dlt * lax.rsqrt(var + LNX_EPS)
        o_ref[0, :, sl] = ((yn * g_ref[:, sl] + b_ref[:, sl] + bonus_ref[0, :, sl]) * gate_ref[0, :, sl]).astype(o_ref.dtype)


def _rwkv_finish(y, bonus, gate, lnx_g, lnx_b, tm):
    _, b, t, dim = y.shape
    one_spec = pl.BlockSpec((1, tm, dim), lambda bi, i: (bi, i, 0))
    vec = pl.BlockSpec((1, dim), lambda bi, i: (0, 0))
    return pl.pallas_call(
        _rwkv_finish_kernel,
        out_shape=jax.ShapeDtypeStruct((b, t, dim), MXU),
        grid=(b, t // tm),
        in_specs=[pl.BlockSpec((2, 1, tm, dim), lambda bi, i: (0, bi, i, 0)), one_spec, one_spec, vec, vec],
        out_specs=one_spec,
        compiler_params=_cparams(("parallel", "parallel")),
        name="rwkv_finish",
    )(y, bonus, gate, lnx_g.reshape(1, dim), lnx_b.reshape(1, dim))


def _outproj_kernel(*refs, n_in):
    x_ref, gate_ref = refs[0], refs[1]
    o_ref = refs[-1]
    acc = None
    for i in range(n_in):
        part = jnp.dot(refs[2 + 2 * i][0], refs[3 + 2 * i][...], preferred_element_type=F32)
        acc = part if acc is None else acc + part
    o_ref[0] = x_ref[0] + gate_ref[0] * acc


def _outproj(x, gate, parts, tm):
    b, t, d = x.shape
    in_specs = [pl.BlockSpec((1, tm, d), lambda bi, i: (bi, i, 0)),
                pl.BlockSpec((1, 1, d), lambda bi, i: (bi, 0, 0))]
    args = [x, gate]
    for a, w, row0 in parts:
        k = a.shape[2]
        assert row0 % k == 0
        in_specs += [pl.BlockSpec((1, tm, k), lambda bi, i: (bi, i, 0)),
                     pl.BlockSpec((k, d), lambda bi, i, r=row0 // k: (r, 0))]
        args += [a, w]
    return pl.pallas_call(
        functools.partial(_outproj_kernel, n_in=len(parts)),
        out_shape=jax.ShapeDtypeStruct((b, t, d), F32),
        grid=(b, t // tm),
        in_specs=in_specs,
        out_specs=pl.BlockSpec((1, tm, d), lambda bi, i: (bi, i, 0)),
        compiler_params=_cparams(("parallel", "parallel")),
        name="outproj",
    )(*args)


def _mlp_kernel(x_ref, g_ref, sh_ref, sc_ref, gate_ref, w1_ref, w2_ref, fg_ref, o_ref, a_scr, *, final_norm):
    f = pl.program_id(2)

    @pl.when(f == 0)
    def _():
        a_scr[...] = _norm_mod(x_ref[0], g_ref[...], sh_ref[0], sc_ref[0]).astype(a_scr.dtype)
        o_ref[...] = jnp.zeros_like(o_ref)

    h = jnp.dot(a_scr[...], w1_ref[...], preferred_element_type=F32)
    h = jnp.square(jnp.maximum(h, 0.0))
    o_ref[0] += jnp.dot(h.astype(MXU), w2_ref[...], preferred_element_type=F32)

    @pl.when(f == pl.num_programs(2) - 1)
    def _():
        y = x_ref[0] + gate_ref[0] * o_ref[0]
        if final_norm:
            y = y * lax.rsqrt(jnp.mean(y * y, axis=-1, keepdims=True) + NORM_EPS) * fg_ref[...]
        o_ref[0] = y


def _mlp(x, g, shift, scale, gate, w1, w2, final_g, tm, final_norm):
    b, t, d = x.shape
    ff = w1.shape[1]
    tf = _col_tile(ff, MLP_MAX_HIDDEN)
    vec3 = pl.BlockSpec((1, 1, d), lambda bi, i, f: (bi, 0, 0))
    vec2 = pl.BlockSpec((1, d), lambda bi, i, f: (0, 0))
    return pl.pallas_call(
        functools.partial(_mlp_kernel, final_norm=final_norm),
        out_shape=jax.ShapeDtypeStruct((b, t, d), F32),
        grid=(b, t // tm, ff // tf),
        in_specs=[pl.BlockSpec((1, tm, d), lambda bi, i, f: (bi, i, 0)), vec2, vec3, vec3, vec3,
                  pl.BlockSpec((d, tf), lambda bi, i, f: (0, f)),
                  pl.BlockSpec((tf, d), lambda bi, i, f: (f, 0)), vec2],
        out_specs=pl.BlockSpec((1, tm, d), lambda bi, i, f: (bi, i, 0)),
        scratch_shapes=[pltpu.VMEM((tm, d), MXU)],
        compiler_params=_cparams(("parallel", "parallel", "arbitrary")),
        name="mlp",
    )(x, g.reshape(1, d), shift, scale, gate, w1, w2, final_g.reshape(1, d))


NA_GROUP_OFFSETS = (0, NA_ROWS // 2, NA_ROWS)
NA_STRIP_PAD = NA_WIN - NA_ROWS
NA_STRIP_BLOCKS = 2 * NA_WIN


def _na_window_lo(typ, local_row):
    return (0, local_row, NA_WIN - NA_ROWS)[typ]


def _na_kernel(q_ref, k_ref, v_ref, kc_ref, vc_ref, tc_ref, o_ref, bias_scr):
    g = pl.program_id(2)
    n_g = pl.num_programs(2)
    n_keys = NA_WIN * GRID_W
    n_blocks = q_ref.shape[2] // LANES
    tq = NA_GROUP * GRID_W

    @pl.when(jnp.logical_and(pl.program_id(1) == 0, g == 0))
    def _():
        key_row = _lane((GRID_W, n_keys)) // GRID_W
        for blk in range(n_blocks):
            for typ in range(3):
                for half in range(HEADS_PER_BLOCK):
                    for lr in range(NA_GROUP):
                        rho = NA_GROUP_OFFSETS[typ] + lr
                        off = (NA_WIN - 1 - rho) * GRID_W
                        strip = tc_ref[blk, half, :, off:off + n_keys]
                        lo = _na_window_lo(typ, lr)
                        ok = jnp.logical_and(key_row >= lo, key_row < lo + NA_ROWS)
                        r0 = (half * NA_GROUP + lr) * GRID_W
                        bias_scr[blk, typ, r0:r0 + GRID_W, :] = jnp.where(ok, strip, MASK_VALUE)

    win_row = jnp.clip(g * NA_GROUP - NA_ROWS // 2, 0, n_g * NA_GROUP - NA_WIN)
    typ = jnp.where(g == 0, 0, jnp.where(g == n_g - 1, 2, 1))
    start = pl.multiple_of(win_row * GRID_W, GRID_W)
    lanes = lambda blk: slice(blk * LANES, (blk + 1) * LANES)
    block = lambda h: lanes(h // HEADS_PER_BLOCK)
    chunks = [(lambda blk: k_ref[0, pl.ds(start, n_keys), lanes(blk)],
               lambda h: v_ref[0, pl.ds(start, n_keys), block(h)],
               lambda h: bias_scr[h // HEADS_PER_BLOCK, typ, pl.ds((h % HEADS_PER_BLOCK) * tq, tq), :]),
              (lambda blk: kc_ref[0, :, lanes(blk)], lambda h: vc_ref[0, :, block(h)], None)]
    outs = _softmax_pv([q_ref[0, :, lanes(blk)] for blk in range(n_blocks)], chunks, keys_on_rows=False)
    for blk, o in enumerate(outs):
        o_ref[0, :, lanes(blk)] = o.astype(o_ref.dtype)


def _na_col_table(rpb, n_heads):
    col = np.arange(GRID_W)
    c0 = np.clip(col - NA_COLS // 2, 0, GRID_W - NA_COLS)
    col_ok = (col[None, :] >= c0[:, None]) & (col[None, :] < c0[:, None] + NA_COLS)
    dcol = col[None, :] - col[:, None] + NA_COLS - 1
    n_dr, n_dc = 2 * NA_ROWS - 1, 2 * NA_COLS - 1
    onehot = np.concatenate([(dcol[None] == np.arange(n_dc)[:, None, None]) & col_ok[None],
                             ~col_ok[None], np.ones((1, GRID_W, GRID_W), bool)])
    vals = jnp.concatenate([rpb * LOG2E, jnp.full((n_heads, n_dr, 1), MASK_VALUE, F32),
                            jnp.zeros((n_heads, n_dr, 1), F32)], axis=2)
    pad_rows = jnp.zeros((n_heads, 1, n_dc + 2), F32).at[:, :, n_dc + 1].set(MASK_VALUE)
    vals = jnp.concatenate([jnp.tile(pad_rows, (1, NA_STRIP_PAD, 1)), vals,
                            jnp.tile(pad_rows, (1, NA_STRIP_BLOCKS - NA_STRIP_PAD - n_dr, 1))], axis=1)
    tc = jnp.einsum("hrd,dqk->hqrk", vals, jnp.asarray(onehot, F32), precision=lax.Precision.HIGHEST)
    return tc.reshape(n_heads // 2, HEADS_PER_BLOCK, GRID_W, NA_STRIP_BLOCKS * GRID_W)


def _na_attn(qkv, kvc, col_table, width):
    b, t, _ = qkv.shape
    ctx = kvc.shape[1]
    bw = NA_STEP_BLOCKS * LANES
    nb = width // bw
    tq = NA_GROUP * GRID_W
    rows = t // GRID_W
    assert rows % NA_GROUP == 0 and rows >= NA_WIN and width % bw == 0
    return pl.pallas_call(
        _na_kernel,
        out_shape=jax.ShapeDtypeStruct((b, t, width), MXU),
        grid=(nb, b, rows // NA_GROUP),
        in_specs=[pl.BlockSpec((1, tq, bw), lambda p, bi, g: (bi, g, p)),
                  pl.BlockSpec((1, t, bw), lambda p, bi, g: (bi, 0, nb + p)),
                  pl.BlockSpec((1, t, bw), lambda p, bi, g: (bi, 0, 2 * nb + p)),
                  pl.BlockSpec((1, ctx, bw), lambda p, bi, g: (bi, 0, p)),
                  pl.BlockSpec((1, ctx, bw), lambda p, bi, g: (bi, 0, nb + p)),
                  pl.BlockSpec((NA_STEP_BLOCKS, HEADS_PER_BLOCK, GRID_W, NA_STRIP_BLOCKS * GRID_W),
                               lambda p, bi, g: (p, 0, 0, 0))],
        out_specs=pl.BlockSpec((1, tq, bw), lambda p, bi, g: (bi, g, p)),
        scratch_shapes=[pltpu.VMEM((NA_STEP_BLOCKS, 3, HEADS_PER_BLOCK * tq, NA_WIN * GRID_W), F32)],
        compiler_params=_cparams(("arbitrary", "arbitrary", "arbitrary")),
        name="na_attn",
    )(qkv, qkv, qkv, kvc, kvc, col_table)


def _rope_tables(t):
    pos = jnp.arange(t, dtype=jnp.int32)
    row = (pos // GRID_W).astype(F32)
    col = (pos % GRID_W).astype(F32)
    pairs = HEAD_DIM // 4
    inv = ROPE_THETA ** (-jnp.arange(pairs, dtype=F32) / pairs)
    ang = jnp.concatenate([row[:, None] * inv, col[:, None] * inv], axis=-1)
    cos, sin = jnp.cos(ang), jnp.sin(ang)
    cos_h = jnp.concatenate([cos, cos], axis=-1)
    sin_h = jnp.concatenate([-sin, sin], axis=-1)
    return jnp.tile(cos_h, (1, HEADS_PER_BLOCK)), jnp.tile(sin_h, (1, HEADS_PER_BLOCK))


def _block_diag2(top, bottom):
    z_t = jnp.zeros_like(top)
    z_b = jnp.zeros_like(bottom)
    return jnp.concatenate([jnp.concatenate([top, z_t], axis=1), jnp.concatenate([z_b, bottom], axis=1)], axis=0)


def kernel(x, c, ctx, c_ctx, l0_norm1, l0_norm2, l0_ada_w, l0_ada_b, l0_w_in, l0_shift_mu, l0_w0_f, l0_w0_b, l0_ww2_f, l0_ww2_b, l0_a0_f, l0_a0_b, l0_wa2_f, l0_wa2_b, l0_wg2, l0_k_k, l0_k_a, l0_r_k, l0_lnx_g, l0_lnx_b, l0_q_norm, l0_k_norm, l0_w_out, l0_mlp_w1, l0_mlp_w2, l1_norm1, l1_norm2, l1_ada_w, l1_ada_b, l1_w_qkv, l1_rpb, l1_w_out, l1_mlp_w1, l1_mlp_w2, final_norm):
    b, t, d = x.shape
    n_ctx = ctx.shape[1]
    dim = l0_w0_f.shape[0]
    q_width = l0_w_out.shape[0] - dim
    gqa_cols = l0_w_in.shape[1] - l0_shift_mu.shape[0]
    kv_width = (gqa_cols - q_width) // 2
    rw_cols = l0_shift_mu.shape[0]
    rw_pad = 3 * dim + 4 * LANES
    assert rw_cols <= rw_pad and 3 * dim + 2 * LANES == rw_cols - l0_wg2.shape[0]
    tm = min(ROW_TILE, t)
    tm_c = n_ctx

    assert b < SUBLANES
    cc = jnp.zeros((SUBLANES, d), F32).at[:b].set(c).at[b].set(c_ctx)

    def modulation(ada_w, ada_b):
        mod = _ada_mod(cc, ada_w, ada_b)
        lat = mod[:b].reshape(b, 6, 1, d)
        cx = jnp.broadcast_to(mod[b].reshape(1, 6, 1, d), (b, 6, 1, d))
        return [lat[:, i] for i in range(6)], [cx[:, i] for i in range(6)]

    mod_l, mod_c = modulation(l0_ada_w, l0_ada_b)

    w_in = l0_w_in.astype(MXU)
    w_rw = jnp.pad(w_in[:, gqa_cols:], ((0, 0), (0, rw_pad - rw_cols)))
    mu = jnp.pad(l0_shift_mu, (0, rw_pad - rw_cols)).reshape(1, rw_pad)
    lw_w = _block_diag2(l0_ww2_f, l0_ww2_b).astype(MXU)
    la_w = _block_diag2(l0_wa2_f, l0_wa2_b).astype(MXU)
    lg_w = jnp.pad(l0_wg2, ((0, 2 * LANES - l0_wg2.shape[0]), (0, 0))).astype(MXU)
    w0 = jnp.concatenate([l0_w0_f, l0_w0_b]).reshape(1, 2 * dim)
    a0 = jnp.concatenate([l0_a0_f, l0_a0_b]).reshape(1, 2 * dim)
    k_k = l0_k_k.reshape(1, dim)
    k_a = l0_k_a.reshape(1, dim)
    r_k = l0_r_k.reshape(1, dim)
    qg = jnp.tile(l0_q_norm, HEADS_PER_BLOCK).reshape(1, LANES)
    kg = jnp.tile(l0_k_norm, HEADS_PER_BLOCK).reshape(1, LANES)
    cos_l, sin_l = _rope_tables(t)
    cos_c, sin_c = jnp.ones((n_ctx, LANES), F32), jnp.zeros((n_ctx, LANES), F32)
    w_out = l0_w_out.astype(MXU)

    def half_layer0(xs, mod, tm_, cos, sin, is_ctx):
        pg = _inproj(xs, l0_norm1, mod[0], mod[1], w_in, tm_, F32, col0=0, n=gqa_cols)
        pr = _inproj(xs, l0_norm1, mod[0], mod[1], w_rw, tm_, F32)
        q, kd, vd = _gqa_prep(pg, cos, sin, qg, kg, q_width, kv_width, tm_)
        prep = _rwkv_prep(pr, mu, lw_w, la_w, lg_w, w0, a0, k_k, k_a, r_k, dim, RWKV_PREP_ROWS, is_ctx)
        return q, kd, vd, prep

    q_c, kd_c, vd_c, prep_c = half_layer0(ctx, mod_c, tm_c, cos_c, sin_c, True)
    q_l, kd_l, vd_l, prep_l = half_layer0(x, mod_l, tm, cos_l, sin_l, False)

    o_gqa_l = _gqa_attn(q_l, [(kd_l, vd_l), (kd_c, vd_c)], min(GQA_QUERY_TILE, t))
    o_gqa_c = _gqa_attn(q_c, [(kd_c, vd_c)], n_ctx)

    h0 = jnp.zeros((2, b, dim // LANES, LANES, LANES), F32)
    y_c, h_c = _rwkv_scan(*prep_c[:6], h0)
    y_l, _ = _rwkv_scan(*prep_l[:6], h_c)
    o_rw_l = _rwkv_finish(y_l, prep_l[6], prep_l[7], l0_lnx_g, l0_lnx_b, tm)
    o_rw_c = _rwkv_finish(y_c, prep_c[6], prep_c[7], l0_lnx_g, l0_lnx_b, tm_c)

    w1 = l0_mlp_w1.astype(MXU)
    w2 = l0_mlp_w2.astype(MXU)
    x = _outproj(x, mod_l[2], [(o_gqa_l, w_out, 0), (o_rw_l, w_out, q_width)], tm)
    x = _mlp(x, l0_norm2, mod_l[3], mod_l[4], mod_l[5], w1, w2, final_norm, min(MLP_ROW_TILE, t), False)
    ctx = _outproj(ctx, mod_c[2], [(o_gqa_c, w_out, 0), (o_rw_c, w_out, q_width)], tm_c)
    ctx = _mlp(ctx.reshape(1, b * n_ctx, d), l0_norm2, mod_c[3][:1], mod_c[4][:1], mod_c[5][:1], w1, w2, final_norm,
               _col_tile(b * n_ctx, MLP_ROW_TILE), False).reshape(b, n_ctx, d)

    mod_l, mod_c = modulation(l1_ada_w, l1_ada_b)
    width = l1_w_out.shape[0]
    n_heads = width // HEAD_DIM
    scale = jnp.concatenate([jnp.full((width,), ATTN_SCALE, F32), jnp.ones((2 * width,), F32)])
    w_qkv = (l1_w_qkv * scale).astype(MXU)
    qkv = _inproj(x, l1_norm1, mod_l[0], mod_l[1], w_qkv, min(2 * tm, t), MXU)
    kvc = _inproj(ctx, l1_norm1, mod_c[0], mod_c[1], w_qkv, tm_c, MXU, col0=width, n=2 * width)
    o_na = _na_attn(qkv, kvc, _na_col_table(l1_rpb, n_heads), width)
    x = _outproj(x, mod_l[2], [(o_na, l1_w_out.astype(MXU), 0)], tm)
    x = _mlp(x, l1_norm2, mod_l[3], mod_l[4], mod_l[5], l1_mlp_w1.astype(MXU), l1_mlp_w2.astype(MXU),
             final_norm, min(MLP_ROW_TILE, t), True)
    return x
```

```python
import functools

import jax
import jax.numpy as jnp
import numpy as np
from jax import lax
from jax.experimental import pallas as pl
from jax.experimental.pallas import tpu as pltpu

F32 = jnp.float32
MXU = jnp.bfloat16

LANES = 128
SUBLANES = 8
ROW_TILE = 512
RWKV_PREP_ROWS = 256
ADA_MAX_COLS = 1024
HEAD_DIM = 64
HEADS_PER_BLOCK = LANES // HEAD_DIM
GRID_W = 64
NORM_EPS = 1e-6
LNX_EPS = 64e-5
ROPE_THETA = 10000.0
NA_ROWS = 8
NA_COLS = 16
NA_GROUP = 4
NA_WIN = NA_GROUP + NA_ROWS
NA_STEP_BLOCKS = 4
GQA_STEP_KV_HEADS = 2
GQA_QUERY_TILE = 512
MASK_VALUE = -1e30
LOG2E = float(np.log2(np.e))
ATTN_SCALE = HEAD_DIM ** -0.5 * LOG2E
ATTN_KEY_CHUNK = 2048
INPROJ_MAX_COLS = 2048
INPROJ_RESIDENT_BYTES = 16 * 1024 * 1024
INPROJ_VMEM_BUDGET = 46 * 1024 * 1024
MLP_MAX_HIDDEN = 512
MLP_ROW_TILE = 1024
SCAN_CHUNK = 64
SCAN_PAIRS = 8
SCAN_STEP_CHUNKS = 4
VMEM_LIMIT = 56 * 1024 * 1024


def _cparams(sem):
    return pltpu.CompilerParams(dimension_semantics=sem, vmem_limit_bytes=VMEM_LIMIT)


def _mm(a, b):
    return jnp.dot(a.astype(MXU), b.astype(MXU), preferred_element_type=F32)


def _mm_nt(a, b):
    return lax.dot_general(a.astype(MXU), b.astype(MXU), (((1,), (1,)), ((), ())),
                           preferred_element_type=F32)


def _mm_tn(a, b):
    return lax.dot_general(a.astype(MXU), b.astype(MXU), (((0,), (0,)), ((), ())),
                           preferred_element_type=F32)


def _split3(x):
    hi = x.astype(MXU)
    r1 = x - hi.astype(F32)
    mid = r1.astype(MXU)
    lo = (r1 - mid.astype(F32)).astype(MXU)
    return hi, mid, lo


def _head_ones():
    r = lax.broadcasted_iota(jnp.int32, (LANES, LANES), 0) // HEAD_DIM
    c = lax.broadcasted_iota(jnp.int32, (LANES, LANES), 1) // HEAD_DIM
    return jnp.where(r == c, 1.0, 0.0).astype(MXU)


def _headsum(x, ones):
    hi, mid, lo = _split3(x)
    return (jnp.dot(hi, ones, preferred_element_type=F32) + jnp.dot(mid, ones, preferred_element_type=F32)
            + jnp.dot(lo, ones, preferred_element_type=F32))


def _lane(shape):
    return lax.broadcasted_iota(jnp.int32, shape, len(shape) - 1)


def _sigmoid(x):
    return 1.0 / (1.0 + jnp.exp(-x))


def _ada_kernel(c_ref, w_ref, b_ref, o_ref):
    c = c_ref[...]
    a = c * _sigmoid(c)
    o_ref[...] = _mm(a, w_ref[...]) + b_ref[...]


def _ada_mod(cc, w, bias):
    d, n = w.shape
    tn = _col_tile(n, ADA_MAX_COLS)
    return pl.pallas_call(
        _ada_kernel,
        out_shape=jax.ShapeDtypeStruct((cc.shape[0], n), F32),
        grid=(n // tn,),
        in_specs=[pl.BlockSpec((cc.shape[0], d), lambda j: (0, 0)),
                  pl.BlockSpec((d, tn), lambda j: (0, j)),
                  pl.BlockSpec((1, tn), lambda j: (0, j))],
        out_specs=pl.BlockSpec((cc.shape[0], tn), lambda j: (0, j)),
        compiler_params=_cparams(("arbitrary",)),
        name="ada_mod",
    )(cc, w, bias.reshape(1, n))


def _norm_mod(x, g, shift, scale):
    y = x * lax.rsqrt(jnp.mean(x * x, axis=-1, keepdims=True) + NORM_EPS)
    return (y * g) * (1.0 + scale) + shift


def _inproj_kernel(x_ref, g_ref, sh_ref, sc_ref, w_ref, o_ref, a_scr):
    @pl.when(pl.program_id(2) == 0)
    def _():
        a_scr[...] = _norm_mod(x_ref[0], g_ref[...], sh_ref[0], sc_ref[0]).astype(a_scr.dtype)

    o_ref[0] = jnp.dot(a_scr[...], w_ref[...], preferred_element_type=F32).astype(o_ref.dtype)


def _col_tile(n, cap):
    return max(tn for tn in range(LANES, cap + 1, LANES) if n % tn == 0)


def _inproj(x, g, shift, scale, w, tm, out_dtype, col0=0, n=None):
    b, t, d = x.shape
    n = w.shape[1] if n is None else n
    if d * n * w.dtype.itemsize <= INPROJ_RESIDENT_BYTES and col0 % n == 0:
        tn = n
    else:
        tn = _col_tile(int(np.gcd(n, col0)) if col0 else n, INPROJ_MAX_COLS)
    j0 = col0 // tn
    out_bytes = jnp.dtype(out_dtype).itemsize

    def vmem(tm_):
        return 2 * (tm_ * d * 4 + d * tn * w.dtype.itemsize + tm_ * tn * out_bytes) + tm_ * d * w.dtype.itemsize

    while vmem(tm) > INPROJ_VMEM_BUDGET and tm % 16 == 0:
        tm //= 2
    return pl.pallas_call(
        _inproj_kernel,
        out_shape=jax.ShapeDtypeStruct((b, t, n), out_dtype),
        grid=(b, t // tm, n // tn),
        in_specs=[pl.BlockSpec((1, tm, d), lambda bi, i, j: (bi, i, 0)),
                  pl.BlockSpec((1, d), lambda bi, i, j: (0, 0)),
                  pl.BlockSpec((1, 1, d), lambda bi, i, j: (bi, 0, 0)),
                  pl.BlockSpec((1, 1, d), lambda bi, i, j: (bi, 0, 0)),
                  pl.BlockSpec((d, tn), lambda bi, i, j: (0, j0 + j))],
        out_specs=pl.BlockSpec((1, tm, tn), lambda bi, i, j: (bi, i, j)),
        scratch_shapes=[pltpu.VMEM((tm, d), MXU)],
        compiler_params=_cparams(("parallel", "parallel", "arbitrary")),
        name="inproj",
    )(x, g.reshape(1, d), shift, scale, w)


def _swap_half_heads(x):
    first = (_lane(x.shape) % HEAD_DIM) < HEAD_DIM // 2
    return jnp.where(first, pltpu.roll(x, LANES - HEAD_DIM // 2, 1), pltpu.roll(x, HEAD_DIM // 2, 1))


def _gqa_prep_kernel(q_ref, kv_ref, cos_ref, sin_ref, qg_ref, kg_ref, qo_ref, ko_ref, vo_ref, *, n_q, n_kv):
    ones = _head_ones()
    cos = cos_ref[...]
    sin = sin_ref[...]
    first_head = _lane(cos.shape) < HEAD_DIM

    def norm_rope(x, g):
        ms = _headsum(x * x, ones) * (1.0 / HEAD_DIM)
        y = x * lax.rsqrt(ms + NORM_EPS) * g
        return y * cos + _swap_half_heads(y) * sin

    def dup(x, half):
        rolled = pltpu.roll(x, HEAD_DIM, 1)
        return jnp.where(first_head, x, rolled) if half == 0 else jnp.where(first_head, rolled, x)

    for j in range(n_q):
        x = q_ref[0, :, j * LANES:(j + 1) * LANES]
        qo_ref[0, :, j * LANES:(j + 1) * LANES] = (norm_rope(x, qg_ref[...]) * ATTN_SCALE).astype(qo_ref.dtype)
    for j in range(n_kv):
        k = norm_rope(kv_ref[0, :, j * LANES:(j + 1) * LANES], kg_ref[...])
        vt = kv_ref[0, :, (n_kv + j) * LANES:(n_kv + j + 1) * LANES].T
        for half in range(HEADS_PER_BLOCK):
            ko_ref[0, HEADS_PER_BLOCK * j + half] = dup(k, half).astype(ko_ref.dtype)
            vo_ref[0, HEADS_PER_BLOCK * j + half] = vt[half * HEAD_DIM:(half + 1) * HEAD_DIM].astype(vo_ref.dtype)


def _gqa_prep(p, cos, sin, qg, kg, q_width, kv_width, tm):
    b, t, _ = p.shape
    n_q = q_width // LANES
    n_kv = kv_width // LANES
    kvh = kv_width // HEAD_DIM
    kern = functools.partial(_gqa_prep_kernel, n_q=n_q, n_kv=n_kv)
    kv_spec = pl.BlockSpec((1, kvh, tm, LANES), lambda bi, i: (bi, 0, i, 0))
    vt_spec = pl.BlockSpec((1, kvh, HEAD_DIM, tm), lambda bi, i: (bi, 0, 0, i))
    return pl.pallas_call(
        kern,
        out_shape=(jax.ShapeDtypeStruct((b, t, q_width), MXU),
                   jax.ShapeDtypeStruct((b, kvh, t, LANES), MXU),
                   jax.ShapeDtypeStruct((b, kvh, HEAD_DIM, t), MXU)),
        grid=(b, t // tm),
        in_specs=[pl.BlockSpec((1, tm, q_width), lambda bi, i: (bi, i, 0)),
                  pl.BlockSpec((1, tm, 2 * kv_width), lambda bi, i: (bi, i, q_width // (2 * kv_width))),
                  pl.BlockSpec((tm, LANES), lambda bi, i: (i, 0)),
                  pl.BlockSpec((tm, LANES), lambda bi, i: (i, 0)),
                  pl.BlockSpec((1, LANES), lambda bi, i: (0, 0)),
                  pl.BlockSpec((1, LANES), lambda bi, i: (0, 0))],
        out_specs=(pl.BlockSpec((1, tm, q_width), lambda bi, i: (bi, i, 0)), kv_spec, vt_spec),
        compiler_params=_cparams(("parallel", "parallel")),
        name="gqa_prep",
    )(p, p, cos, sin, qg, kg)


def _softmax_pv(q_blocks, chunks, keys_on_rows):
    first_head = _lane(q_blocks[0].shape) < HEAD_DIM
    zero = jnp.zeros_like(q_blocks[0])
    n_heads = HEADS_PER_BLOCK * len(q_blocks)
    axis = 0 if keys_on_rows else 1
    s = [[None] * len(chunks) for _ in range(n_heads)]
    m = [None] * n_heads
    l = [None] * n_heads
    acc = [None] * n_heads
    for stage in range(n_heads + 1):
        ha, hb = stage, stage - 1
        if ha < n_heads:
            q = q_blocks[ha // HEADS_PER_BLOCK]
            qa = jnp.where(first_head, q, zero) if ha % HEADS_PER_BLOCK == 0 else jnp.where(first_head, zero, q)
        for c, (load_k, load_v, load_bias) in enumerate(chunks):
            if ha < n_heads:
                k = load_k(ha // HEADS_PER_BLOCK)
                sc = _mm_nt(k, qa) if keys_on_rows else _mm_nt(qa, k)
                if load_bias is not None:
                    sc = sc + load_bias(ha)
                s[ha][c] = sc
                mc = jnp.max(sc, axis=axis, keepdims=True)
                m[ha] = mc if m[ha] is None else jnp.maximum(m[ha], mc)
            if hb >= 0:
                p = jnp.exp2(s[hb][c] - m[hb])
                s[hb][c] = None
                lc = jnp.sum(p, axis=axis, keepdims=True)
                pv = _mm(load_v(hb), p) if keys_on_rows else _mm(p, load_v(hb))
                l[hb] = lc if l[hb] is None else l[hb] + lc
                acc[hb] = pv if acc[hb] is None else acc[hb] + pv
    o = [a / d for a, d in zip(acc, l)]
    if keys_on_rows:
        return [jnp.concatenate(o[2 * j:2 * j + 2], axis=0).T for j in range(len(q_blocks))]
    return [jnp.where(first_head, o[2 * j], o[2 * j + 1]) for j in range(len(q_blocks))]


def _gqa_attn_kernel(q_ref, *refs, n_sets):
    o_ref = refs[-1]
    n_blocks = q_ref.shape[2] // LANES
    heads_per_kv = HEADS_PER_BLOCK * n_blocks // refs[0].shape[1]
    chunks = []
    for i in range(n_sets):
        k_ref, vt_ref = refs[2 * i], refs[2 * i + 1]
        n_keys = k_ref.shape[2]
        for start in range(0, n_keys, ATTN_KEY_CHUNK):
            size = min(ATTN_KEY_CHUNK, n_keys - start)
            chunks.append((lambda blk, r=k_ref, s=start, z=size: r[0, HEADS_PER_BLOCK * blk // heads_per_kv, pl.ds(s, z), :],
                           lambda h, r=vt_ref, s=start, z=size: r[0, h // heads_per_kv, :, pl.ds(s, z)], None))
    outs = _softmax_pv([q_ref[0, :, j * LANES:(j + 1) * LANES] for j in range(n_blocks)], chunks, keys_on_rows=True)
    for j, o in enumerate(outs):
        o_ref[0, :, j * LANES:(j + 1) * LANES] = o.astype(o_ref.dtype)


def _gqa_attn(q, kv_sets, tq):
    b, t, width = q.shape
    kvh = kv_sets[0][0].shape[1]
    kv_step = min(GQA_STEP_KV_HEADS, kvh)
    group_width = (width // kvh) * kv_step
    in_specs = [pl.BlockSpec((1, tq, group_width), lambda bi, g, i: (bi, i, g))]
    args = [q]
    for k, v in kv_sets:
        s = k.shape[2]
        in_specs += [pl.BlockSpec((1, kv_step, s, LANES), lambda bi, g, i: (bi, g, 0, 0)),
                     pl.BlockSpec((1, kv_step, HEAD_DIM, s), lambda bi, g, i: (bi, g, 0, 0))]
        args += [k, v]
    return pl.pallas_call(
        functools.partial(_gqa_attn_kernel, n_sets=len(kv_sets)),
        out_shape=jax.ShapeDtypeStruct((b, t, width), MXU),
        grid=(b, kvh // kv_step, t // tq),
        in_specs=in_specs,
        out_specs=pl.BlockSpec((1, tq, group_width), lambda bi, g, i: (bi, i, g)),
        compiler_params=_cparams(("parallel", "parallel", "arbitrary")),
        name="gqa_attn",
    )(*args)


def _rwkv_prep_kernel(p_ref, up_ref, dn_ref, mu_ref, lw_w_ref, la_w_ref, lg_w_ref, w0_ref, a0_ref, kk_ref, ka_ref,
                      rk_ref, r_o, v_o, kk_o, lw_o, kd_o, bb_o, bonus_o, gate_o, xm_scr, *, is_ctx, dim):
    tm = p_ref.shape[1]
    n_all = p_ref.shape[2] // LANES
    n_dim = dim // LANES
    i = pl.program_id(1)
    n_i = pl.num_programs(1)
    row = lax.broadcasted_iota(jnp.int32, (tm, LANES), 0)
    cls = _lane((tm, LANES)) % 4
    if is_ctx:
        first = row == 0
        last = row == tm - 1
    else:
        first = (row % GRID_W) == 0
        last = (row % GRID_W) == GRID_W - 1
        up_ok = jnp.logical_or(row >= GRID_W, i > 0)
        dn_ok = jnp.logical_or(row < tm - GRID_W, i < n_i - 1)

    for j in range(n_all):
        sl = slice(j * LANES, (j + 1) * LANES)
        p = p_ref[0, :, sl]
        prev = jnp.where(first, 0.0, pltpu.roll(p, 1, 0))
        nxt = jnp.where(last, 0.0, pltpu.roll(p, tm - 1, 0))
        if is_ctx:
            sh = jnp.where(cls % 2 == 0, prev, nxt)
        else:
            up = jnp.concatenate([up_ref[0, :, sl], p[:tm - GRID_W]], axis=0)
            dn = jnp.concatenate([p[GRID_W:], dn_ref[0, :, sl]], axis=0)
            up = jnp.where(up_ok, up, 0.0)
            dn = jnp.where(dn_ok, dn, 0.0)
            sh = jnp.where(cls == 0, prev, jnp.where(cls == 1, nxt, jnp.where(cls == 2, up, dn)))
        xm_scr[:, sl] = p + mu_ref[:, sl] * (sh - p)

    ones = _head_ones()
    x_w = jnp.tanh(xm_scr[:, 3 * dim:3 * dim + LANES]).astype(MXU)
    x_a = xm_scr[:, 3 * dim + LANES:3 * dim + 2 * LANES].astype(MXU)
    x_g = _sigmoid(xm_scr[:, 3 * dim + 2 * LANES:3 * dim + 4 * LANES]).astype(MXU)
    for j in range(n_dim):
        sl = slice(j * LANES, (j + 1) * LANES)
        r = xm_scr[:, sl]
        k = xm_scr[:, dim + j * LANES:dim + (j + 1) * LANES]
        v = xm_scr[:, 2 * dim + j * LANES:2 * dim + (j + 1) * LANES]
        kkr = k * kk_ref[:, sl]
        kk = kkr * lax.rsqrt(jnp.maximum(_headsum(kkr * kkr, ones), 1e-12))
        r_o[0, :, sl] = r
        v_o[0, :, sl] = v
        kk_o[0, :, sl] = kk
        gate_o[0, :, sl] = jnp.dot(x_g, lg_w_ref[:, sl], preferred_element_type=F32)
        bonus = jnp.zeros_like(r)
        for d in range(2):
            dsl = slice(d * dim + j * LANES, d * dim + (j + 1) * LANES)
            z = w0_ref[:, dsl] + jnp.dot(x_w, lw_w_ref[:, dsl], preferred_element_type=F32)
            softplus = jnp.maximum(-z, 0.0) + jnp.log(1.0 + jnp.exp(-jnp.abs(z)))
            lw_o[d, 0, :, sl] = -jnp.exp(-softplus - 0.5)
            iclr = _sigmoid(a0_ref[:, dsl] + jnp.dot(x_a, la_w_ref[:, dsl], preferred_element_type=F32))
            kd = k * (1.0 + (iclr - 1.0) * ka_ref[:, sl])
            kd_o[d, 0, :, sl] = kd
            bb_o[d, 0, :, sl] = kk * iclr
            bonus = bonus + _headsum(r * kd * rk_ref[:, sl], ones) * v
        bonus_o[0, :, sl] = bonus


def _rwkv_prep(p, mu, lw_w, la_w, lg_w, w0, a0, k_k, k_a, r_k, dim, tm, is_ctx):
    b, t, width = p.shape
    if is_ctx:
        assert t == tm
    else:
        assert tm % GRID_W == 0 and t % tm == 0
    hb = tm // GRID_W
    n_halo = t // GRID_W
    kern = functools.partial(_rwkv_prep_kernel, is_ctx=is_ctx, dim=dim)
    vec = lambda n: pl.BlockSpec((1, n), lambda bi, i: (0, 0))
    mat = lambda k, n: pl.BlockSpec((k, n), lambda bi, i: (0, 0))
    one = jax.ShapeDtypeStruct((b, t, dim), F32)
    two = jax.ShapeDtypeStruct((2, b, t, dim), F32)
    one_spec = pl.BlockSpec((1, tm, dim), lambda bi, i: (bi, i, 0))
    two_spec = pl.BlockSpec((2, 1, tm, dim), lambda bi, i: (0, bi, i, 0))
    return pl.pallas_call(
        kern,
        out_shape=(one, one, one, two, two, two, one, one),
        grid=(b, t // tm),
        in_specs=[pl.BlockSpec((1, tm, width), lambda bi, i: (bi, i, 0)),
                  pl.BlockSpec((1, GRID_W, width), lambda bi, i: (bi, jnp.maximum(i * hb - 1, 0), 0)),
                  pl.BlockSpec((1, GRID_W, width), lambda bi, i: (bi, jnp.minimum((i + 1) * hb, n_halo - 1), 0)),
                  vec(width), mat(LANES, 2 * dim), mat(LANES, 2 * dim), mat(2 * LANES, dim),
                  vec(2 * dim), vec(2 * dim), vec(dim), vec(dim), vec(dim)],
        out_specs=(one_spec, one_spec, one_spec, two_spec, two_spec, two_spec, one_spec, one_spec),
        scratch_shapes=[pltpu.VMEM((tm, width), F32)],
        compiler_params=_cparams(("parallel", "parallel")),
        name="rwkv_prep_ctx" if is_ctx else "rwkv_prep",
    )(p, p, p, mu, lw_w, la_w, lg_w, w0, a0, k_k, k_a, r_k)


def _scan_chunk_terms(r, v, kk, lw, kd, bb, sgn):
    blocks = range(len(r))
    c = r[0].shape[0]
    n = 2 * c
    ri = lax.broadcasted_iota(jnp.int32, (n, n), 0)
    ci = lax.broadcasted_iota(jnp.int32, (n, n), 1)
    same = (ri // c) == (ci // c)
    dt = ((ri % c) - (ci % c)) * sgn
    strict = jnp.logical_and(same, dt > 0)
    incl = jnp.logical_and(same, dt >= 0)
    eye = ri == ci
    head_lanes = (ri // c) == (ci // HEAD_DIM)
    zero = jnp.zeros((n, n), F32)
    time = lax.broadcasted_iota(jnp.int32, (c, LANES), 0)
    sgn_f = sgn.astype(F32)
    rev_f = 0.5 - 0.5 * sgn_f

    def stack(x):
        return jnp.concatenate([x, x], axis=0)

    def own(x):
        return jnp.where(head_lanes, stack(x), zero)

    def prefix(x):
        step = 1
        while step < c:
            x = x + jnp.where(time >= step, pltpu.roll(x, step, 0), 0.0)
            step *= 2
        return x

    tot = [jnp.sum(lw[p], axis=0, keepdims=True) for p in blocks]
    pre = [prefix(lw[p]) for p in blocks]
    cum = [rev_f * (tot[p] + lw[p]) + sgn_f * pre[p] for p in blocks]
    e_inv = [jnp.exp(-cum[p]) for p in blocks]
    e_fin = [jnp.exp(tot[p] - cum[p]) for p in blocks]
    at = [own(-kk[p] * jnp.exp(cum[p] - lw[p])) for p in blocks]
    rt = [own(r[p] * jnp.exp(cum[p])) for p in blocks]
    bt = [stack(bb[p] * e_inv[p]) for p in blocks]
    kt = [stack(kd[p] * e_inv[p]) for p in blocks]
    bhat = [own(bb[p] * e_fin[p]) for p in blocks]
    khat = [own(kd[p] * e_fin[p]) for p in blocks]
    vbd = [own(v[p]) for p in blocks]

    full = [_mm_nt(jnp.concatenate([at[p], rt[p]], axis=0), jnp.concatenate([bt[p], kt[p]], axis=0)) for p in blocks]
    a_ab = [jnp.where(strict, full[p][:n, :n], zero) for p in blocks]
    a_ak = [jnp.where(strict, full[p][:n, n:], zero) for p in blocks]
    a_rb = [jnp.where(incl, full[p][n:, :n], zero) for p in blocks]
    a_rk = [jnp.where(incl, full[p][n:, n:], zero) for p in blocks]

    ident = jnp.where(eye, 1.0, 0.0)
    inv = [ident + a_ab[p] for p in blocks]
    power = a_ab
    for _ in range(int(np.log2(c)) - 1):
        power = [_mm(power[p], power[p]) for p in blocks]
        inv = [inv[p] + _mm(inv[p], power[p]) for p in blocks]

    xv = [_mm(jnp.concatenate([a_ak[p], a_rk[p]], axis=0), vbd[p]) for p in blocks]
    w12 = [_mm(inv[p], jnp.concatenate([at[p], xv[p][:n]], axis=1)) for p in blocks]
    yw = [_mm(a_rb[p], w12[p]) for p in blocks]
    mn = [_mm_tn(bhat[p], w12[p]) for p in blocks]
    py = [rt[p] + yw[p][:, :n] for p in blocks]
    yl = [yw[p][:, n:] + xv[p][n:] for p in blocks]
    m = [jnp.where(eye, jnp.exp(tot[p]), zero) + mn[p][:, :n] for p in blocks]
    nn = [mn[p][:, n:] + _mm_tn(khat[p], vbd[p]) for p in blocks]
    return py, yl, m, nn


def _scan_kernel(r_ref, v_ref, kk_ref, lw_ref, kd_ref, bb_ref, h0_ref, y_ref, ht_ref, h_scr, *, pairs, chunks):
    d = pl.program_id(0)
    ci = pl.program_id(2)
    sgn = 1 - 2 * d
    c = SCAN_CHUNK
    n = 2 * c

    @pl.when(ci == 0)
    def _():
        h_scr[...] = h0_ref[0, 0]

    rows = [pl.ds(pl.multiple_of((d * (chunks - 1) + sgn * s) * c, c), c) for s in range(chunks)]
    lanes = [slice(j * LANES, (j + 1) * LANES) for j in range(pairs)]
    probs = [(rw, ln) for rw in rows for ln in lanes]
    py, yl, m, nn = _scan_chunk_terms(
        [r_ref[0, rw, ln] for rw, ln in probs], [v_ref[0, rw, ln] for rw, ln in probs],
        [kk_ref[0, rw, ln] for rw, ln in probs], [lw_ref[0, 0, rw, ln] for rw, ln in probs],
        [kd_ref[0, 0, rw, ln] for rw, ln in probs], [bb_ref[0, 0, rw, ln] for rw, ln in probs], sgn)

    h = [h_scr[j] for j in range(pairs)]
    for s in range(chunks):
        idx = [s * pairs + j for j in range(pairs)]
        yh = [_mm(jnp.concatenate([py[i], m[i]], axis=0), h[j]) for j, i in enumerate(idx)]
        for j, i in enumerate(idx):
            y2 = yh[j][:n] + yl[i]
            y_ref[0, 0, rows[s], lanes[j]] = y2[:c] + y2[c:]
        h = [yh[j][n:] + nn[i] for j, i in enumerate(idx)]
    for j in range(pairs):
        h_scr[j] = h[j]

    @pl.when(ci == pl.num_programs(2) - 1)
    def _():
        ht_ref[0, 0] = h_scr[...]


def _rwkv_scan(r, v, kk, lw, kd, bb, h0):
    b, t, dim = r.shape
    c = SCAN_CHUNK * SCAN_STEP_CHUNKS
    assert t % c == 0
    nch = t // c
    width = SCAN_PAIRS * LANES
    nblk = dim // width

    def chunk(d, ci):
        return ci + d * (nch - 1 - 2 * ci)

    one = pl.BlockSpec((1, c, width), lambda d, g, ci: (g // nblk, chunk(d, ci), g % nblk))
    two = pl.BlockSpec((1, 1, c, width), lambda d, g, ci: (d, g // nblk, chunk(d, ci), g % nblk))
    st = pl.BlockSpec((1, 1, SCAN_PAIRS, LANES, LANES), lambda d, g, ci: (d, g // nblk, g % nblk, 0, 0))
    return pl.pallas_call(
        functools.partial(_scan_kernel, pairs=SCAN_PAIRS, chunks=SCAN_STEP_CHUNKS),
        out_shape=(jax.ShapeDtypeStruct((2, b, t, dim), F32), jax.ShapeDtypeStruct(h0.shape, F32)),
        grid=(2, b * nblk, nch),
        in_specs=[one, one, one, two, two, two, st],
        out_specs=(two, st),
        scratch_shapes=[pltpu.VMEM((SCAN_PAIRS, LANES, LANES), F32)],
        compiler_params=_cparams(("parallel", "parallel", "arbitrary")),
        name="rwkv_scan",
    )(r, v, kk, lw, kd, bb, h0)


def _rwkv_finish_kernel(y_ref, bonus_ref, gate_ref, g_ref, b_ref, o_ref):
    ones = _head_ones()
    for j in range(o_ref.shape[2] // LANES):
        sl = slice(j * LANES, (j + 1) * LANES)
        y = y_ref[0, 0, :, sl] + y_ref[1, 0, :, sl]
        mu = _headsum(y, ones) * (1.0 / HEAD_DIM)
        dlt = y - mu
        var = _headsum(dlt * dlt, ones) * (1.0 / HEAD_DIM)
        yn = dlt * lax.rsqrt(var + LNX_EPS)
        o_ref[0, :, sl] = ((yn * g_ref[:, sl] + b_ref[:, sl] + bonus_ref[0, :, sl]) * gate_ref[0, :, sl]).astype(o_ref.dtype)


def _rwkv_finish(y, bonus, gate, lnx_g, lnx_b, tm):
    _, b, t, dim = y.shape
    one_spec = pl.BlockSpec((1, tm, dim), lambda bi, i: (bi, i, 0))
    vec = pl.BlockSpec((1, dim), lambda bi, i: (0, 0))
    return pl.pallas_call(
        _rwkv_finish_kernel,
        out_shape=jax.ShapeDtypeStruct((b, t, dim), MXU),
        grid=(b, t // tm),
        in_specs=[pl.BlockSpec((2, 1, tm, dim), lambda bi, i: (0, bi, i, 0)), one_spec, one_spec, vec, vec],
        out_specs=one_spec,
        compiler_params=_cparams(("parallel", "parallel")),
        name="rwkv_finish",
    )(y, bonus, gate, lnx_g.reshape(1, dim), lnx_b.reshape(1, dim))


def _outproj_kernel(*refs, n_in):
    x_ref, gate_ref = refs[0], refs[1]
    o_ref = refs[-1]
    acc = None
    for i in range(n_in):
        part = jnp.dot(refs[2 + 2 * i][0], refs[3 + 2 * i][...], preferred_element_type=F32)
        acc = part if acc is None else acc + part
    o_ref[0] = x_ref[0] + gate_ref[0] * acc


def _outproj(x, gate, parts, tm):
    b, t, d = x.shape
    in_specs = [pl.BlockSpec((1, tm, d), lambda bi, i: (bi, i, 0)),
                pl.BlockSpec((1, 1, d), lambda bi, i: (bi, 0, 0))]
    args = [x, gate]
    for a, w, row0 in parts:
        k = a.shape[2]
        assert row0 % k == 0
        in_specs += [pl.BlockSpec((1, tm, k), lambda bi, i: (bi, i, 0)),
                     pl.BlockSpec((k, d), lambda bi, i, r=row0 // k: (r, 0))]
        args += [a, w]
    return pl.pallas_call(
        functools.partial(_outproj_kernel, n_in=len(parts)),
        out_shape=jax.ShapeDtypeStruct((b, t, d), F32),
        grid=(b, t // tm),
        in_specs=in_specs,
        out_specs=pl.BlockSpec((1, tm, d), lambda bi, i: (bi, i, 0)),
        compiler_params=_cparams(("parallel", "parallel")),
        name="outproj",
    )(*args)


def _mlp_kernel(x_ref, g_ref, sh_ref, sc_ref, gate_ref, w1_ref, w2_ref, fg_ref, o_ref, a_scr, *, final_norm):
    f = pl.program_id(2)

    @pl.when(f == 0)
    def _():
        a_scr[...] = _norm_mod(x_ref[0], g_ref[...], sh_ref[0], sc_ref[0]).astype(a_scr.dtype)
        o_ref[...] = jnp.zeros_like(o_ref)

    h = jnp.dot(a_scr[...], w1_ref[...], preferred_element_type=F32)
    h = jnp.square(jnp.maximum(h, 0.0))
    o_ref[0] += jnp.dot(h.astype(MXU), w2_ref[...], preferred_element_type=F32)

    @pl.when(f == pl.num_programs(2) - 1)
    def _():
        y = x_ref[0] + gate_ref[0] * o_ref[0]
        if final_norm:
            y = y * lax.rsqrt(jnp.mean(y * y, axis=-1, keepdims=True) + NORM_EPS) * fg_ref[...]
        o_ref[0] = y


def _mlp(x, g, shift, scale, gate, w1, w2, final_g, tm, final_norm):
    b, t, d = x.shape
    ff = w1.shape[1]
    tf = _col_tile(ff, MLP_MAX_HIDDEN)
    vec3 = pl.BlockSpec((1, 1, d), lambda bi, i, f: (bi, 0, 0))
    vec2 = pl.BlockSpec((1, d), lambda bi, i, f: (0, 0))
    return pl.pallas_call(
        functools.partial(_mlp_kernel, final_norm=final_norm),
        out_shape=jax.ShapeDtypeStruct((b, t, d), F32),
        grid=(b, t // tm, ff // tf),
        in_specs=[pl.BlockSpec((1, tm, d), lambda bi, i, f: (bi, i, 0)), vec2, vec3, vec3, vec3,
                  pl.BlockSpec((d, tf), lambda bi, i, f: (0, f)),
                  pl.BlockSpec((tf, d), lambda bi, i, f: (f, 0)), vec2],
        out_specs=pl.BlockSpec((1, tm, d), lambda bi, i, f: (bi, i, 0)),
        scratch_shapes=[pltpu.VMEM((tm, d), MXU)],
        compiler_params=_cparams(("parallel", "parallel", "arbitrary")),
        name="mlp",
    )(x, g.reshape(1, d), shift, scale, gate, w1, w2, final_g.reshape(1, d))


NA_GROUP_OFFSETS = (0, NA_ROWS // 2, NA_ROWS)
NA_STRIP_PAD = NA_WIN - NA_ROWS
NA_STRIP_BLOCKS = 2 * NA_WIN


def _na_window_lo(typ, local_row):
    return (0, local_row, NA_WIN - NA_ROWS)[typ]


def _na_kernel(q_ref, k_ref, v_ref, kc_ref, vc_ref, tc_ref, o_ref, bias_scr):
    g = pl.program_id(2)
    n_g = pl.num_programs(2)
    n_keys = NA_WIN * GRID_W
    n_blocks = q_ref.shape[2] // LANES
    tq = NA_GROUP * GRID_W

    @pl.when(jnp.logical_and(pl.program_id(1) == 0, g == 0))
    def _():
        key_row = _lane((GRID_W, n_keys)) // GRID_W
        for blk in range(n_blocks):
            for typ in range(3):
                for half in range(HEADS_PER_BLOCK):
                    for lr in range(NA_GROUP):
                        rho = NA_GROUP_OFFSETS[typ] + lr
                        off = (NA_WIN - 1 - rho) * GRID_W
                        strip = tc_ref[blk, half, :, off:off + n_keys]
                        lo = _na_window_lo(typ, lr)
                        ok = jnp.logical_and(key_row >= lo, key_row < lo + NA_ROWS)
                        r0 = (half * NA_GROUP + lr) * GRID_W
                        bias_scr[blk, typ, r0:r0 + GRID_W, :] = jnp.where(ok, strip, MASK_VALUE)

    win_row = jnp.clip(g * NA_GROUP - NA_ROWS // 2, 0, n_g * NA_GROUP - NA_WIN)
    typ = jnp.where(g == 0, 0, jnp.where(g == n_g - 1, 2, 1))
    start = pl.multiple_of(win_row * GRID_W, GRID_W)
    lanes = lambda blk: slice(blk * LANES, (blk + 1) * LANES)
    block = lambda h: lanes(h // HEADS_PER_BLOCK)
    chunks = [(lambda blk: k_ref[0, pl.ds(start, n_keys), lanes(blk)],
               lambda h: v_ref[0, pl.ds(start, n_keys), block(h)],
               lambda h: bias_scr[h // HEADS_PER_BLOCK, typ, pl.ds((h % HEADS_PER_BLOCK) * tq, tq), :]),
              (lambda blk: kc_ref[0, :, lanes(blk)], lambda h: vc_ref[0, :, block(h)], None)]
    outs = _softmax_pv([q_ref[0, :, lanes(blk)] for blk in range(n_blocks)], chunks, keys_on_rows=False)
    for blk, o in enumerate(outs):
        o_ref[0, :, lanes(blk)] = o.astype(o_ref.dtype)


def _na_col_table(rpb, n_heads):
    col = np.arange(GRID_W)
    c0 = np.clip(col - NA_COLS // 2, 0, GRID_W - NA_COLS)
    col_ok = (col[None, :] >= c0[:, None]) & (col[None, :] < c0[:, None] + NA_COLS)
    dcol = col[None, :] - col[:, None] + NA_COLS - 1
    n_dr, n_dc = 2 * NA_ROWS - 1, 2 * NA_COLS - 1
    onehot = np.concatenate([(dcol[None] == np.arange(n_dc)[:, None, None]) & col_ok[None],
                             ~col_ok[None], np.ones((1, GRID_W, GRID_W), bool)])
    vals = jnp.concatenate([rpb * LOG2E, jnp.full((n_heads, n_dr, 1), MASK_VALUE, F32),
                            jnp.zeros((n_heads, n_dr, 1), F32)], axis=2)
    pad_rows = jnp.zeros((n_heads, 1, n_dc + 2), F32).at[:, :, n_dc + 1].set(MASK_VALUE)
    vals = jnp.concatenate([jnp.tile(pad_rows, (1, NA_STRIP_PAD, 1)), vals,
                            jnp.tile(pad_rows, (1, NA_STRIP_BLOCKS - NA_STRIP_PAD - n_dr, 1))], axis=1)
    tc = jnp.einsum("hrd,dqk->hqrk", vals, jnp.asarray(onehot, F32), precision=lax.Precision.HIGHEST)
    return tc.reshape(n_heads // 2, HEADS_PER_BLOCK, GRID_W, NA_STRIP_BLOCKS * GRID_W)


def _na_attn(qkv, kvc, col_table, width):
    b, t, _ = qkv.shape
    ctx = kvc.shape[1]
    bw = NA_STEP_BLOCKS * LANES
    nb = width // bw
    tq = NA_GROUP * GRID_W
    rows = t // GRID_W
    assert rows % NA_GROUP == 0 and rows >= NA_WIN and width % bw == 0
    return pl.pallas_call(
        _na_kernel,
        out_shape=jax.ShapeDtypeStruct((b, t, width), MXU),
        grid=(nb, b, rows // NA_GROUP),
        in_specs=[pl.BlockSpec((1, tq, bw), lambda p, bi, g: (bi, g, p)),
                  pl.BlockSpec((1, t, bw), lambda p, bi, g: (bi, 0, nb + p)),
                  pl.BlockSpec((1, t, bw), lambda p, bi, g: (bi, 0, 2 * nb + p)),
                  pl.BlockSpec((1, ctx, bw), lambda p, bi, g: (bi, 0, p)),
                  pl.BlockSpec((1, ctx, bw), lambda p, bi, g: (bi, 0, nb + p)),
                  pl.BlockSpec((NA_STEP_BLOCKS, HEADS_PER_BLOCK, GRID_W, NA_STRIP_BLOCKS * GRID_W),
                               lambda p, bi, g: (p, 0, 0, 0))],
        out_specs=pl.BlockSpec((1, tq, bw), lambda p, bi, g: (bi, g, p)),
        scratch_shapes=[pltpu.VMEM((NA_STEP_BLOCKS, 3, HEADS_PER_BLOCK * tq, NA_WIN * GRID_W), F32)],
        compiler_params=_cparams(("arbitrary", "arbitrary", "arbitrary")),
        name="na_attn",
    )(qkv, qkv, qkv, kvc, kvc, col_table)


def _rope_tables(t):
    pos = jnp.arange(t, dtype=jnp.int32)
    row = (pos // GRID_W).astype(F32)
    col = (pos % GRID_W).astype(F32)
    pairs = HEAD_DIM // 4
    inv = ROPE_THETA ** (-jnp.arange(pairs, dtype=F32) / pairs)
    ang = jnp.concatenate([row[:, None] * inv, col[:, None] * inv], axis=-1)
    cos, sin = jnp.cos(ang), jnp.sin(ang)
    cos_h = jnp.concatenate([cos, cos], axis=-1)
    sin_h = jnp.concatenate([-sin, sin], axis=-1)
    return jnp.tile(cos_h, (1, HEADS_PER_BLOCK)), jnp.tile(sin_h, (1, HEADS_PER_BLOCK))


def _block_diag2(top, bottom):
    z_t = jnp.zeros_like(top)
    z_b = jnp.zeros_like(bottom)
    return jnp.concatenate([jnp.concatenate([top, z_t], axis=1), jnp.concatenate([z_b, bottom], axis=1)], axis=0)


def kernel(x, c, ctx, c_ctx, l0_norm1, l0_norm2, l0_ada_w, l0_ada_b, l0_w_in, l0_shift_mu, l0_w0_f, l0_w0_b, l0_ww2_f, l0_ww2_b, l0_a0_f, l0_a0_b, l0_wa2_f, l0_wa2_b, l0_wg2, l0_k_k, l0_k_a, l0_r_k, l0_lnx_g, l0_lnx_b, l0_q_norm, l0_k_norm, l0_w_out, l0_mlp_w1, l0_mlp_w2, l1_norm1, l1_norm2, l1_ada_w, l1_ada_b, l1_w_qkv, l1_rpb, l1_w_out, l1_mlp_w1, l1_mlp_w2, final_norm):
    b, t, d = x.shape
    n_ctx = ctx.shape[1]
    dim = l0_w0_f.shape[0]
    q_width = l0_w_out.shape[0] - dim
    gqa_cols = l0_w_in.shape[1] - l0_shift_mu.shape[0]
    kv_width = (gqa_cols - q_width) // 2
    rw_cols = l0_shift_mu.shape[0]
    rw_pad = 3 * dim + 4 * LANES
    assert rw_cols <= rw_pad and 3 * dim + 2 * LANES == rw_cols - l0_wg2.shape[0]
    tm = min(ROW_TILE, t)
    tm_c = n_ctx

    assert b < SUBLANES
    cc = jnp.zeros((SUBLANES, d), F32).at[:b].set(c).at[b].set(c_ctx)

    def modulation(ada_w, ada_b):
        mod = _ada_mod(cc, ada_w, ada_b)
        lat = mod[:b].reshape(b, 6, 1, d)
        cx = jnp.broadcast_to(mod[b].reshape(1, 6, 1, d), (b, 6, 1, d))
        return [lat[:, i] for i in range(6)], [cx[:, i] for i in range(6)]

    mod_l, mod_c = modulation(l0_ada_w, l0_ada_b)

    w_in = l0_w_in.astype(MXU)
    w_rw = jnp.pad(w_in[:, gqa_cols:], ((0, 0), (0, rw_pad - rw_cols)))
    mu = jnp.pad(l0_shift_mu, (0, rw_pad - rw_cols)).reshape(1, rw_pad)
    lw_w = _block_diag2(l0_ww2_f, l0_ww2_b).astype(MXU)
    la_w = _block_diag2(l0_wa2_f, l0_wa2_b).astype(MXU)
    lg_w = jnp.pad(l0_wg2, ((0, 2 * LANES - l0_wg2.shape[0]), (0, 0))).astype(MXU)
    w0 = jnp.concatenate([l0_w0_f, l0_w0_b]).reshape(1, 2 * dim)
    a0 = jnp.concatenate([l0_a0_f, l0_a0_b]).reshape(1, 2 * dim)
    k_k = l0_k_k.reshape(1, dim)
    k_a = l0_k_a.reshape(1, dim)
    r_k = l0_r_k.reshape(1, dim)
    qg = jnp.tile(l0_q_norm, HEADS_PER_BLOCK).reshape(1, LANES)
    kg = jnp.tile(l0_k_norm, HEADS_PER_BLOCK).reshape(1, LANES)
    cos_l, sin_l = _rope_tables(t)
    cos_c, sin_c = jnp.ones((n_ctx, LANES), F32), jnp.zeros((n_ctx, LANES), F32)
    w_out = l0_w_out.astype(MXU)

    def half_layer0(xs, mod, tm_, cos, sin, is_ctx):
        pg = _inproj(xs, l0_norm1, mod[0], mod[1], w_in, tm_, F32, col0=0, n=gqa_cols)
        pr = _inproj(xs, l0_norm1, mod[0], mod[1], w_rw, tm_, F32)
        q, kd, vd = _gqa_prep(pg, cos, sin, qg, kg, q_width, kv_width, tm_)
        prep = _rwkv_prep(pr, mu, lw_w, la_w, lg_w, w0, a0, k_k, k_a, r_k, dim, RWKV_PREP_ROWS, is_ctx)
        return q, kd, vd, prep

    q_c, kd_c, vd_c, prep_c = half_layer0(ctx, mod_c, tm_c, cos_c, sin_c, True)
    q_l, kd_l, vd_l, prep_l = half_layer0(x, mod_l, tm, cos_l, sin_l, False)

    o_gqa_l = _gqa_attn(q_l, [(kd_l, vd_l), (kd_c, vd_c)], min(GQA_QUERY_TILE, t))
    o_gqa_c = _gqa_attn(q_c, [(kd_c, vd_c)], n_ctx)

    h0 = jnp.zeros((2, b, dim // LANES, LANES, LANES), F32)
    y_c, h_c = _rwkv_scan(*prep_c[:6], h0)
    y_l, _ = _rwkv_scan(*prep_l[:6], h_c)
    o_rw_l = _rwkv_finish(y_l, prep_l[6], prep_l[7], l0_lnx_g, l0_lnx_b, tm)
    o_rw_c = _rwkv_finish(y_c, prep_c[6], prep_c[7], l0_lnx_g, l0_lnx_b, tm_c)

    w1 = l0_mlp_w1.astype(MXU)
    w2 = l0_mlp_w2.astype(MXU)
    x = _outproj(x, mod_l[2], [(o_gqa_l, w_out, 0), (o_rw_l, w_out, q_width)], tm)
    x = _mlp(x, l0_norm2, mod_l[3], mod_l[4], mod_l[5], w1, w2, final_norm, min(MLP_ROW_TILE, t), False)
    ctx = _outproj(ctx, mod_c[2], [(o_gqa_c, w_out, 0), (o_rw_c, w_out, q_width)], tm_c)
    ctx = _mlp(ctx.reshape(1, b * n_ctx, d), l0_norm2, mod_c[3][:1], mod_c[4][:1], mod_c[5][:1], w1, w2, final_norm,
               _col_tile(b * n_ctx, MLP_ROW_TILE), False).reshape(b, n_ctx, d)

    mod_l, mod_c = modulation(l1_ada_w, l1_ada_b)
    width = l1_w_out.shape[0]
    n_heads = width // HEAD_DIM
    scale = jnp.concatenate([jnp.full((width,), ATTN_SCALE, F32), jnp.ones((2 * width,), F32)])
    w_qkv = (l1_w_qkv * scale).astype(MXU)
    qkv = _inproj(x, l1_norm1, mod_l[0], mod_l[1], w_qkv, min(2 * tm, t), MXU)
    kvc = _inproj(ctx, l1_norm1, mod_c[0], mod_c[1], w_qkv, tm_c, MXU, col0=width, n=2 * width)
    o_na = _na_attn(qkv, kvc, _na_col_table(l1_rpb, n_heads), width)
    x = _outproj(x, mod_l[2], [(o_na, l1_w_out.astype(MXU), 0)], tm)
    x = _mlp(x, l1_norm2, mod_l[3], mod_l[4], mod_l[5], l1_mlp_w1.astype(MXU), l1_mlp_w2.astype(MXU),
             final_norm, min(MLP_ROW_TILE, t), True)
    return x
```

```python
import functools

import jax
import jax.numpy as jnp
import numpy as np
from jax import lax
from jax.experimental import pallas as pl
from jax.experimental.pallas import tpu as pltpu

F32 = jnp.float32
MXU = jnp.bfloat16

LANES = 128
SUBLANES = 8
ROW_TILE = 512
RWKV_PREP_ROWS = 256
ADA_MAX_COLS = 1024
HEAD_DIM = 64
HEADS_PER_BLOCK = LANES // HEAD_DIM
GRID_W = 64
NORM_EPS = 1e-6
LNX_EPS = 64e-5
ROPE_THETA = 10000.0
NA_ROWS = 8
NA_COLS = 16
NA_GROUP = 4
NA_WIN = NA_GROUP + NA_ROWS
NA_STEP_BLOCKS = 4
GQA_STEP_KV_HEADS = 2
GQA_QUERY_TILE = 512
MASK_VALUE = -1e30
LOG2E = float(np.log2(np.e))
ATTN_SCALE = HEAD_DIM ** -0.5 * LOG2E
ATTN_KEY_CHUNK = 2048
INPROJ_MAX_COLS = 2048
INPROJ_RESIDENT_BYTES = 16 * 1024 * 1024
INPROJ_VMEM_BUDGET = 46 * 1024 * 1024
MLP_MAX_HIDDEN = 1024
MLP_ROW_TILE = 512
SCAN_CHUNK = 64
SCAN_PAIRS = 8
SCAN_STEP_CHUNKS = 4
VMEM_LIMIT = 56 * 1024 * 1024


def _cparams(sem):
    return pltpu.CompilerParams(dimension_semantics=sem, vmem_limit_bytes=VMEM_LIMIT)


def _mm(a, b):
    return jnp.dot(a.astype(MXU), b.astype(MXU), preferred_element_type=F32)


def _mm_nt(a, b):
    return lax.dot_general(a.astype(MXU), b.astype(MXU), (((1,), (1,)), ((), ())),
                           preferred_element_type=F32)


def _mm_tn(a, b):
    return lax.dot_general(a.astype(MXU), b.astype(MXU), (((0,), (0,)), ((), ())),
                           preferred_element_type=F32)


def _split3(x):
    hi = x.astype(MXU)
    r1 = x - hi.astype(F32)
    mid = r1.astype(MXU)
    lo = (r1 - mid.astype(F32)).astype(MXU)
    return hi, mid, lo


def _head_ones():
    r = lax.broadcasted_iota(jnp.int32, (LANES, LANES), 0) // HEAD_DIM
    c = lax.broadcasted_iota(jnp.int32, (LANES, LANES), 1) // HEAD_DIM
    return jnp.where(r == c, 1.0, 0.0).astype(MXU)


def _headsum(x, ones):
    hi, mid, lo = _split3(x)
    return (jnp.dot(hi, ones, preferred_element_type=F32) + jnp.dot(mid, ones, preferred_element_type=F32)
            + jnp.dot(lo, ones, preferred_element_type=F32))


def _lane(shape):
    return lax.broadcasted_iota(jnp.int32, shape, len(shape) - 1)


def _sigmoid(x):
    return 1.0 / (1.0 + jnp.exp(-x))


def _ada_kernel(c_ref, w_ref, b_ref, o_ref):
    c = c_ref[...]
    a = c * _sigmoid(c)
    o_ref[...] = _mm(a, w_ref[...]) + b_ref[...]


def _ada_mod(cc, w, bias):
    d, n = w.shape
    tn = _col_tile(n, ADA_MAX_COLS)
    return pl.pallas_call(
        _ada_kernel,
        out_shape=jax.ShapeDtypeStruct((cc.shape[0], n), F32),
        grid=(n // tn,),
        in_specs=[pl.BlockSpec((cc.shape[0], d), lambda j: (0, 0)),
                  pl.BlockSpec((d, tn), lambda j: (0, j)),
                  pl.BlockSpec((1, tn), lambda j: (0, j))],
        out_specs=pl.BlockSpec((cc.shape[0], tn), lambda j: (0, j)),
        compiler_params=_cparams(("arbitrary",)),
        name="ada_mod",
    )(cc, w, bias.reshape(1, n))


def _norm_mod(x, g, shift, scale):
    y = x * lax.rsqrt(jnp.mean(x * x, axis=-1, keepdims=True) + NORM_EPS)
    return (y * g) * (1.0 + scale) + shift


def _inproj_kernel(x_ref, g_ref, sh_ref, sc_ref, w_ref, o_ref, a_scr):
    @pl.when(pl.program_id(2) == 0)
    def _():
        a_scr[...] = _norm_mod(x_ref[0], g_ref[...], sh_ref[0], sc_ref[0]).astype(a_scr.dtype)

    o_ref[0] = jnp.dot(a_scr[...], w_ref[...], preferred_element_type=F32).astype(o_ref.dtype)


def _col_tile(n, cap):
    return max(tn for tn in range(LANES, cap + 1, LANES) if n % tn == 0)


def _inproj(x, g, shift, scale, w, tm, out_dtype, col0=0, n=None):
    b, t, d = x.shape
    n = w.shape[1] if n is None else n
    if d * n * w.dtype.itemsize <= INPROJ_RESIDENT_BYTES and col0 % n == 0:
        tn = n
    else:
        tn = _col_tile(int(np.gcd(n, col0)) if col0 else n, INPROJ_MAX_COLS)
    j0 = col0 // tn
    out_bytes = jnp.dtype(out_dtype).itemsize

    def vmem(tm_):
        return 2 * (tm_ * d * 4 + d * tn * w.dtype.itemsize + tm_ * tn * out_bytes) + tm_ * d * w.dtype.itemsize

    while vmem(tm) > INPROJ_VMEM_BUDGET and tm % 16 == 0:
        tm //= 2
    return pl.pallas_call(
        _inproj_kernel,
        out_shape=jax.ShapeDtypeStruct((b, t, n), out_dtype),
        grid=(b, t // tm, n // tn),
        in_specs=[pl.BlockSpec((1, tm, d), lambda bi, i, j: (bi, i, 0)),
                  pl.BlockSpec((1, d), lambda bi, i, j: (0, 0)),
                  pl.BlockSpec((1, 1, d), lambda bi, i, j: (bi, 0, 0)),
                  pl.BlockSpec((1, 1, d), lambda bi, i, j: (bi, 0, 0)),
                  pl.BlockSpec((d, tn), lambda bi, i, j: (0, j0 + j))],
        out_specs=pl.BlockSpec((1, tm, tn), lambda bi, i, j: (bi, i, j)),
        scratch_shapes=[pltpu.VMEM((tm, d), MXU)],
        compiler_params=_cparams(("parallel", "parallel", "arbitrary")),
        name="inproj",
    )(x, g.reshape(1, d), shift, scale, w)


def _swap_half_heads(x):
    first = (_lane(x.shape) % HEAD_DIM) < HEAD_DIM // 2
    return jnp.where(first, pltpu.roll(x, LANES - HEAD_DIM // 2, 1), pltpu.roll(x, HEAD_DIM // 2, 1))


def _inproj_gqa_kernel(x_ref, g_ref, sh_ref, sc_ref, w_ref, cos_ref, sin_ref, qg_ref, kg_ref, qo_ref, ko_ref, vo_ref,
                       p_scr, *, n_q, n_kv):
    a = _norm_mod(x_ref[0], g_ref[...], sh_ref[0], sc_ref[0]).astype(MXU)
    p_scr[...] = jnp.dot(a, w_ref[...], preferred_element_type=F32)
    block = lambda j: p_scr[:, j * LANES:(j + 1) * LANES]
    ones = _head_ones()
    cos = cos_ref[...]
    sin = sin_ref[...]
    first_head = _lane(cos.shape) < HEAD_DIM

    def norm_rope(x, g):
        ms = _headsum(x * x, ones) * (1.0 / HEAD_DIM)
        y = x * lax.rsqrt(ms + NORM_EPS) * g
        return y * cos + _swap_half_heads(y) * sin

    def dup(x, half):
        rolled = pltpu.roll(x, HEAD_DIM, 1)
        return jnp.where(first_head, x, rolled) if half == 0 else jnp.where(first_head, rolled, x)

    for j in range(n_q):
        qo_ref[0, :, j * LANES:(j + 1) * LANES] = (norm_rope(block(j), qg_ref[...]) * ATTN_SCALE).astype(qo_ref.dtype)
    for j in range(n_kv):
        k = norm_rope(block(n_q + j), kg_ref[...])
        vt = block(n_q + n_kv + j).T
        for half in range(HEADS_PER_BLOCK):
            ko_ref[0, HEADS_PER_BLOCK * j + half] = dup(k, half).astype(ko_ref.dtype)
            vo_ref[0, HEADS_PER_BLOCK * j + half] = vt[half * HEAD_DIM:(half + 1) * HEAD_DIM].astype(vo_ref.dtype)


def _inproj_gqa(x, g, shift, scale, w, cos, sin, qg, kg, q_width, kv_width, tm):
    b, t, d = x.shape
    n = q_width + 2 * kv_width
    n_q = q_width // LANES
    n_kv = kv_width // LANES
    kvh = kv_width // HEAD_DIM
    kern = functools.partial(_inproj_gqa_kernel, n_q=n_q, n_kv=n_kv)
    kv_spec = pl.BlockSpec((1, kvh, tm, LANES), lambda bi, i: (bi, 0, i, 0))
    vt_spec = pl.BlockSpec((1, kvh, HEAD_DIM, tm), lambda bi, i: (bi, 0, 0, i))
    return pl.pallas_call(
        kern,
        out_shape=(jax.ShapeDtypeStruct((b, t, q_width), MXU),
                   jax.ShapeDtypeStruct((b, kvh, t, LANES), MXU),
                   jax.ShapeDtypeStruct((b, kvh, HEAD_DIM, t), MXU)),
        grid=(b, t // tm),
        in_specs=[pl.BlockSpec((1, tm, d), lambda bi, i: (bi, i, 0)),
                  pl.BlockSpec((1, d), lambda bi, i: (0, 0)),
                  pl.BlockSpec((1, 1, d), lambda bi, i: (bi, 0, 0)),
                  pl.BlockSpec((1, 1, d), lambda bi, i: (bi, 0, 0)),
                  pl.BlockSpec((d, n), lambda bi, i: (0, 0)),
                  pl.BlockSpec((tm, LANES), lambda bi, i: (i, 0)),
                  pl.BlockSpec((tm, LANES), lambda bi, i: (i, 0)),
                  pl.BlockSpec((1, LANES), lambda bi, i: (0, 0)),
                  pl.BlockSpec((1, LANES), lambda bi, i: (0, 0))],
        out_specs=(pl.BlockSpec((1, tm, q_width), lambda bi, i: (bi, i, 0)), kv_spec, vt_spec),
        scratch_shapes=[pltpu.VMEM((tm, n), F32)],
        compiler_params=_cparams(("parallel", "parallel")),
        name="inproj_gqa",
    )(x, g.reshape(1, d), shift, scale, w, cos, sin, qg, kg)


def _softmax_pv(q_blocks, chunks, keys_on_rows):
    first_head = _lane(q_blocks[0].shape) < HEAD_DIM
    zero = jnp.zeros_like(q_blocks[0])
    n_heads = HEADS_PER_BLOCK * len(q_blocks)
    axis = 0 if keys_on_rows else 1
    s = [[None] * len(chunks) for _ in range(n_heads)]
    m = [None] * n_heads
    l = [None] * n_heads
    acc = [None] * n_heads
    for stage in range(n_heads + 1):
        ha, hb = stage, stage - 1
        if ha < n_heads:
            q = q_blocks[ha // HEADS_PER_BLOCK]
            qa = jnp.where(first_head, q, zero) if ha % HEADS_PER_BLOCK == 0 else jnp.where(first_head, zero, q)
        for c, (load_k, load_v, load_bias) in enumerate(chunks):
            if ha < n_heads:
                k = load_k(ha // HEADS_PER_BLOCK)
                sc = _mm_nt(k, qa) if keys_on_rows else _mm_nt(qa, k)
                if load_bias is not None:
                    sc = sc + load_bias(ha)
                s[ha][c] = sc
                mc = jnp.max(sc, axis=axis, keepdims=True)
                m[ha] = mc if m[ha] is None else jnp.maximum(m[ha], mc)
            if hb >= 0:
                p = jnp.exp2(s[hb][c] - m[hb])
                s[hb][c] = None
                lc = jnp.sum(p, axis=axis, keepdims=True)
                pv = _mm(load_v(hb), p) if keys_on_rows else _mm(p, load_v(hb))
                l[hb] = lc if l[hb] is None else l[hb] + lc
                acc[hb] = pv if acc[hb] is None else acc[hb] + pv
    o = [a / d for a, d in zip(acc, l)]
    if keys_on_rows:
        return [jnp.concatenate(o[2 * j:2 * j + 2], axis=0).T for j in range(len(q_blocks))]
    return [jnp.where(first_head, o[2 * j], o[2 * j + 1]) for j in range(len(q_blocks))]


def _gqa_attn_kernel(q_ref, *refs, n_sets):
    o_ref = refs[-1]
    n_blocks = q_ref.shape[2] // LANES
    heads_per_kv = HEADS_PER_BLOCK * n_blocks // refs[0].shape[1]
    chunks = []
    for i in range(n_sets):
        k_ref, vt_ref = refs[2 * i], refs[2 * i + 1]
        n_keys = k_ref.shape[2]
        for start in range(0, n_keys, ATTN_KEY_CHUNK):
            size = min(ATTN_KEY_CHUNK, n_keys - start)
            chunks.append((lambda blk, r=k_ref, s=start, z=size: r[0, HEADS_PER_BLOCK * blk // heads_per_kv, pl.ds(s, z), :],
                           lambda h, r=vt_ref, s=start, z=size: r[0, h // heads_per_kv, :, pl.ds(s, z)], None))
    outs = _softmax_pv([q_ref[0, :, j * LANES:(j + 1) * LANES] for j in range(n_blocks)], chunks, keys_on_rows=True)
    for j, o in enumerate(outs):
        o_ref[0, :, j * LANES:(j + 1) * LANES] = o.astype(o_ref.dtype)


def _gqa_attn(q, kv_sets, tq):
    b, t, width = q.shape
    kvh = kv_sets[0][0].shape[1]
    kv_step = min(GQA_STEP_KV_HEADS, kvh)
    group_width = (width // kvh) * kv_step
    in_specs = [pl.BlockSpec((1, tq, group_width), lambda bi, g, i: (bi, i, g))]
    args = [q]
    for k, v in kv_sets:
        s = k.shape[2]
        in_specs += [pl.BlockSpec((1, kv_step, s, LANES), lambda bi, g, i: (bi, g, 0, 0)),
                     pl.BlockSpec((1, kv_step, HEAD_DIM, s), lambda bi, g, i: (bi, g, 0, 0))]
        args += [k, v]
    return pl.pallas_call(
        functools.partial(_gqa_attn_kernel, n_sets=len(kv_sets)),
        out_shape=jax.ShapeDtypeStruct((b, t, width), MXU),
        grid=(b, kvh // kv_step, t // tq),
        in_specs=in_specs,
        out_specs=pl.BlockSpec((1, tq, group_width), lambda bi, g, i: (bi, i, g)),
        compiler_params=_cparams(("parallel", "parallel", "arbitrary")),
        name="gqa_attn",
    )(*args)


def _rwkv_prep_kernel(p_ref, up_ref, dn_ref, mu_ref, lw_w_ref, la_w_ref, lg_w_ref, w0_ref, a0_ref, kk_ref, ka_ref,
                      rk_ref, r_o, v_o, kk_o, lw_o, kd_o, bb_o, bonus_o, gate_o, xm_scr, *, is_ctx, dim):
    tm = p_ref.shape[1]
    n_all = p_ref.shape[2] // LANES
    n_dim = dim // LANES
    i = pl.program_id(1)
    n_i = pl.num_programs(1)
    row = lax.broadcasted_iota(jnp.int32, (tm, LANES), 0)
    cls = _lane((tm, LANES)) % 4
    if is_ctx:
        first = row == 0
        last = row == tm - 1
    else:
        first = (row % GRID_W) == 0
        last = (row % GRID_W) == GRID_W - 1
        up_ok = jnp.logical_or(row >= GRID_W, i > 0)
        dn_ok = jnp.logical_or(row < tm - GRID_W, i < n_i - 1)

    for j in range(n_all):
        sl = slice(j * LANES, (j + 1) * LANES)
        p = p_ref[0, :, sl]
        prev = jnp.where(first, 0.0, pltpu.roll(p, 1, 0))
        nxt = jnp.where(last, 0.0, pltpu.roll(p, tm - 1, 0))
        if is_ctx:
            sh = jnp.where(cls % 2 == 0, prev, nxt)
        else:
            up = jnp.concatenate([up_ref[0, :, sl], p[:tm - GRID_W]], axis=0)
            dn = jnp.concatenate([p[GRID_W:], dn_ref[0, :, sl]], axis=0)
            up = jnp.where(up_ok, up, 0.0)
            dn = jnp.where(dn_ok, dn, 0.0)
            sh = jnp.where(cls == 0, prev, jnp.where(cls == 1, nxt, jnp.where(cls == 2, up, dn)))
        xm_scr[:, sl] = p + mu_ref[:, sl] * (sh - p)

    ones = _head_ones()
    x_w = jnp.tanh(xm_scr[:, 3 * dim:3 * dim + LANES]).astype(MXU)
    x_a = xm_scr[:, 3 * dim + LANES:3 * dim + 2 * LANES].astype(MXU)
    x_g = _sigmoid(xm_scr[:, 3 * dim + 2 * LANES:3 * dim + 4 * LANES]).astype(MXU)
    for j in range(n_dim):
        sl = slice(j * LANES, (j + 1) * LANES)
        r = xm_scr[:, sl]
        k = xm_scr[:, dim + j * LANES:dim + (j + 1) * LANES]
        v = xm_scr[:, 2 * dim + j * LANES:2 * dim + (j + 1) * LANES]
        kkr = k * kk_ref[:, sl]
        kk = kkr * lax.rsqrt(jnp.maximum(_headsum(kkr * kkr, ones), 1e-12))
        r_o[0, :, sl] = r
        v_o[0, :, sl] = v
        kk_o[0, :, sl] = kk
        gate_o[0, :, sl] = jnp.dot(x_g, lg_w_ref[:, sl], preferred_element_type=F32)
        bonus = jnp.zeros_like(r)
        for d in range(2):
            dsl = slice(d * dim + j * LANES, d * dim + (j + 1) * LANES)
            z = w0_ref[:, dsl] + jnp.dot(x_w, lw_w_ref[:, dsl], preferred_element_type=F32)
            softplus = jnp.maximum(-z, 0.0) + jnp.log(1.0 + jnp.exp(-jnp.abs(z)))
            lw_o[d, 0, :, sl] = -jnp.exp(-softplus - 0.5)
            iclr = _sigmoid(a0_ref[:, dsl] + jnp.dot(x_a, la_w_ref[:, dsl], preferred_element_type=F32))
            kd = k * (1.0 + (iclr - 1.0) * ka_ref[:, sl])
            kd_o[d, 0, :, sl] = kd
            bb_o[d, 0, :, sl] = kk * iclr
            bonus = bonus + _headsum(r * kd * rk_ref[:, sl], ones) * v
        bonus_o[0, :, sl] = bonus


def _rwkv_prep(p, mu, lw_w, la_w, lg_w, w0, a0, k_k, k_a, r_k, dim, tm, is_ctx):
    b, t, width = p.shape
    if is_ctx:
        assert t == tm
    else:
        assert tm % GRID_W == 0 and t % tm == 0
    hb = tm // GRID_W
    n_halo = t // GRID_W
    kern = functools.partial(_rwkv_prep_kernel, is_ctx=is_ctx, dim=dim)
    vec = lambda n: pl.BlockSpec((1, n), lambda bi, i: (0, 0))
    mat = lambda k, n: pl.BlockSpec((k, n), lambda bi, i: (0, 0))
    one = jax.ShapeDtypeStruct((b, t, dim), F32)
    two = jax.ShapeDtypeStruct((2, b, t, dim), F32)
    one_spec = pl.BlockSpec((1, tm, dim), lambda bi, i: (bi, i, 0))
    two_spec = pl.BlockSpec((2, 1, tm, dim), lambda bi, i: (0, bi, i, 0))
    return pl.pallas_call(
        kern,
        out_shape=(one, one, one, two, two, two, one, one),
        grid=(b, t // tm),
        in_specs=[pl.BlockSpec((1, tm, width), lambda bi, i: (bi, i, 0)),
                  pl.BlockSpec((1, GRID_W, width), lambda bi, i: (bi, jnp.maximum(i * hb - 1, 0), 0)),
                  pl.BlockSpec((1, GRID_W, width), lambda bi, i: (bi, jnp.minimum((i + 1) * hb, n_halo - 1), 0)),
                  vec(width), mat(LANES, 2 * dim), mat(LANES, 2 * dim), mat(2 * LANES, dim),
                  vec(2 * dim), vec(2 * dim), vec(dim), vec(dim), vec(dim)],
        out_specs=(one_spec, one_spec, one_spec, two_spec, two_spec, two_spec, one_spec, one_spec),
        scratch_shapes=[pltpu.VMEM((tm, width), F32)],
        compiler_params=_cparams(("parallel", "parallel")),
        name="rwkv_prep_ctx" if is_ctx else "rwkv_prep",
    )(p, p, p, mu, lw_w, la_w, lg_w, w0, a0, k_k, k_a, r_k)


def _scan_chunk_terms(r, v, kk, lw, kd, bb, sgn):
    blocks = range(len(r))
    c = r[0].shape[0]
    n = 2 * c
    ri = lax.broadcasted_iota(jnp.int32, (n, n), 0)
    ci = lax.broadcasted_iota(jnp.int32, (n, n), 1)
    same = (ri // c) == (ci // c)
    dt = ((ri % c) - (ci % c)) * sgn
    strict = jnp.logical_and(same, dt > 0)
    incl = jnp.logical_and(same, dt >= 0)
    eye = ri == ci
    head_lanes = (ri // c) == (ci // HEAD_DIM)
    zero = jnp.zeros((n, n), F32)
    time = lax.broadcasted_iota(jnp.int32, (c, LANES), 0)
    sgn_f = sgn.astype(F32)
    rev_f = 0.5 - 0.5 * sgn_f

    def stack(x):
        return jnp.concatenate([x, x], axis=0)

    def own(x):
        return jnp.where(head_lanes, stack(x), zero)

    def prefix(x):
        step = 1
        while step < c:
            x = x + jnp.where(time >= step, pltpu.roll(x, step, 0), 0.0)
            step *= 2
        return x

    tot = [jnp.sum(lw[p], axis=0, keepdims=True) for p in blocks]
    pre = [prefix(lw[p]) for p in blocks]
    cum = [rev_f * (tot[p] + lw[p]) + sgn_f * pre[p] for p in blocks]
    e_inv = [jnp.exp(-cum[p]) for p in blocks]
    e_fin = [jnp.exp(tot[p] - cum[p]) for p in blocks]
    at = [own(-kk[p] * jnp.exp(cum[p] - lw[p])) for p in blocks]
    rt = [own(r[p] * jnp.exp(cum[p])) for p in blocks]
    bt = [stack(bb[p] * e_inv[p]) for p in blocks]
    kt = [stack(kd[p] * e_inv[p]) for p in blocks]
    bhat = [own(bb[p] * e_fin[p]) for p in blocks]
    khat = [own(kd[p] * e_fin[p]) for p in blocks]
    vbd = [own(v[p]) for p in blocks]

    full = [_mm_nt(jnp.concatenate([at[p], rt[p]], axis=0), jnp.concatenate([bt[p], kt[p]], axis=0)) for p in blocks]
    a_ab = [jnp.where(strict, full[p][:n, :n], zero) for p in blocks]
    a_ak = [jnp.where(strict, full[p][:n, n:], zero) for p in blocks]
    a_rb = [jnp.where(incl, full[p][n:, :n], zero) for p in blocks]
    a_rk = [jnp.where(incl, full[p][n:, n:], zero) for p in blocks]

    ident = jnp.where(eye, 1.0, 0.0)
    inv = [ident + a_ab[p] for p in blocks]
    power = a_ab
    for _ in range(int(np.log2(c)) - 1):
        power = [_mm(power[p], power[p]) for p in blocks]
        inv = [inv[p] + _mm(inv[p], power[p]) for p in blocks]

    xv = [_mm(jnp.concatenate([a_ak[p], a_rk[p]], axis=0), vbd[p]) for p in blocks]
    w12 = [_mm(inv[p], jnp.concatenate([at[p], xv[p][:n]], axis=1)) for p in blocks]
    yw = [_mm(a_rb[p], w12[p]) for p in blocks]
    mn = [_mm_tn(bhat[p], w12[p]) for p in blocks]
    py = [rt[p] + yw[p][:, :n] for p in blocks]
    yl = [yw[p][:, n:] + xv[p][n:] for p in blocks]
    m = [jnp.where(eye, jnp.exp(tot[p]), zero) + mn[p][:, :n] for p in blocks]
    nn = [mn[p][:, n:] + _mm_tn(khat[p], vbd[p]) for p in blocks]
    return py, yl, m, nn


def _scan_kernel(r_ref, v_ref, kk_ref, lw_ref, kd_ref, bb_ref, h0_ref, y_ref, ht_ref, h_scr, *, pairs, chunks):
    d = pl.program_id(0)
    ci = pl.program_id(2)
    sgn = 1 - 2 * d
    c = SCAN_CHUNK
    n = 2 * c

    @pl.when(ci == 0)
    def _():
        h_scr[...] = h0_ref[0, 0]

    rows = [pl.ds(pl.multiple_of((d * (chunks - 1) + sgn * s) * c, c), c) for s in range(chunks)]
    lanes = [slice(j * LANES, (j + 1) * LANES) for j in range(pairs)]
    probs = [(rw, ln) for rw in rows for ln in lanes]
    py, yl, m, nn = _scan_chunk_terms(
        [r_ref[0, rw, ln] for rw, ln in probs], [v_ref[0, rw, ln] for rw, ln in probs],
        [kk_ref[0, rw, ln] for rw, ln in probs], [lw_ref[0, 0, rw, ln] for rw, ln in probs],
        [kd_ref[0, 0, rw, ln] for rw, ln in probs], [bb_ref[0, 0, rw, ln] for rw, ln in probs], sgn)

    h = [h_scr[j] for j in range(pairs)]
    for s in range(chunks):
        idx = [s * pairs + j for j in range(pairs)]
        yh = [_mm(jnp.concatenate([py[i], m[i]], axis=0), h[j]) for j, i in enumerate(idx)]
        for j, i in enumerate(idx):
            y2 = yh[j][:n] + yl[i]
            y_ref[0, 0, rows[s], lanes[j]] = y2[:c] + y2[c:]
        h = [yh[j][n:] + nn[i] for j, i in enumerate(idx)]
    for j in range(pairs):
        h_scr[j] = h[j]

    @pl.when(ci == pl.num_programs(2) - 1)
    def _():
        ht_ref[0, 0] = h_scr[...]


def _rwkv_scan(r, v, kk, lw, kd, bb, h0):
    b, t, dim = r.shape
    c = SCAN_CHUNK * SCAN_STEP_CHUNKS
    assert t % c == 0
    nch = t // c
    width = SCAN_PAIRS * LANES
    nblk = dim // width

    def chunk(d, ci):
        return ci + d * (nch - 1 - 2 * ci)

    one = pl.BlockSpec((1, c, width), lambda d, g, ci: (g // nblk, chunk(d, ci), g % nblk))
    two = pl.BlockSpec((1, 1, c, width), lambda d, g, ci: (d, g // nblk, chunk(d, ci), g % nblk))
    st = pl.BlockSpec((1, 1, SCAN_PAIRS, LANES, LANES), lambda d, g, ci: (d, g // nblk, g % nblk, 0, 0))
    return pl.pallas_call(
        functools.partial(_scan_kernel, pairs=SCAN_PAIRS, chunks=SCAN_STEP_CHUNKS),
        out_shape=(jax.ShapeDtypeStruct((2, b, t, dim), F32), jax.ShapeDtypeStruct(h0.shape, F32)),
        grid=(2, b * nblk, nch),
        in_specs=[one, one, one, two, two, two, st],
        out_specs=(two, st),
        scratch_shapes=[pltpu.VMEM((SCAN_PAIRS, LANES, LANES), F32)],
        compiler_params=_cparams(("parallel", "parallel", "arbitrary")),
        name="rwkv_scan",
    )(r, v, kk, lw, kd, bb, h0)


def _rwkv_finish_kernel(y_ref, bonus_ref, gate_ref, g_ref, b_ref, o_ref):
    ones = _head_ones()
    for j in range(o_ref.shape[2] // LANES):
        sl = slice(j * LANES, (j + 1) * LANES)
        y = y_ref[0, 0, :, sl] + y_ref[1, 0, :, sl]
        mu = _headsum(y, ones) * (1.0 / HEAD_DIM)
        dlt = y - mu
        var = _headsum(dlt * dlt, ones) * (1.0 / HEAD_DIM)
        yn = dlt * lax.rsqrt(var + LNX_EPS)
        o_ref[0, :, sl] = ((yn * g_ref[:, sl] + b_ref[:, sl] + bonus_ref[0, :, sl]) * gate_ref[0, :, sl]).astype(o_ref.dtype)


def _rwkv_finish(y, bonus, gate, lnx_g, lnx_b, tm):
    _, b, t, dim = y.shape
    one_spec = pl.BlockSpec((1, tm, dim), lambda bi, i: (bi, i, 0))
    vec = pl.BlockSpec((1, dim), lambda bi, i: (0, 0))
    return pl.pallas_call(
        _rwkv_finish_kernel,
        out_shape=jax.ShapeDtypeStruct((b, t, dim), MXU),
        grid=(b, t // tm),
        in_specs=[pl.BlockSpec((2, 1, tm, dim), lambda bi, i: (0, bi, i, 0)), one_spec, one_spec, vec, vec],
        out_specs=one_spec,
        compiler_params=_cparams(("parallel", "parallel")),
        name="rwkv_finish",
    )(y, bonus, gate, lnx_g.reshape(1, dim), lnx_b.reshape(1, dim))


def _outproj_kernel(*refs, n_in):
    x_ref, gate_ref = refs[0], refs[1]
    o_ref = refs[-1]
    acc = None
    for i in range(n_in):
        part = jnp.dot(refs[2 + 2 * i][0], refs[3 + 2 * i][...], preferred_element_type=F32)
        acc = part if acc is None else acc + part
    o_ref[0] = x_ref[0] + gate_ref[0] * acc


def _outproj(x, gate, parts, tm):
    b, t, d = x.shape
    in_specs = [pl.BlockSpec((1, tm, d), lambda bi, i: (bi, i, 0)),
                pl.BlockSpec((1, 1, d), lambda bi, i: (bi, 0, 0))]
    args = [x, gate]
    for a, w, row0 in parts:
        k = a.shape[2]
        assert row0 % k == 0
        in_specs += [pl.BlockSpec((1, tm, k), lambda bi, i: (bi, i, 0)),
                     pl.BlockSpec((k, d), lambda bi, i, r=row0 // k: (r, 0))]
        args += [a, w]
    return pl.pallas_call(
        functools.partial(_outproj_kernel, n_in=len(parts)),
        out_shape=jax.ShapeDtypeStruct((b, t, d), F32),
        grid=(b, t // tm),
        in_specs=in_specs,
        out_specs=pl.BlockSpec((1, tm, d), lambda bi, i: (bi, i, 0)),
        compiler_params=_cparams(("parallel", "parallel")),
        name="outproj",
    )(*args)


def _mlp_kernel(x_ref, g_ref, sh_ref, sc_ref, gate_ref, w1_ref, w2_ref, fg_ref, o_ref, a_scr, *, final_norm):
    f = pl.program_id(2)

    @pl.when(f == 0)
    def _():
        a_scr[...] = _norm_mod(x_ref[0], g_ref[...], sh_ref[0], sc_ref[0]).astype(a_scr.dtype)
        o_ref[...] = jnp.zeros_like(o_ref)

    h = jnp.dot(a_scr[...], w1_ref[...], preferred_element_type=F32)
    h = jnp.square(jnp.maximum(h, 0.0))
    o_ref[0] += jnp.dot(h.astype(MXU), w2_ref[...], preferred_element_type=F32)

    @pl.when(f == pl.num_programs(2) - 1)
    def _():
        y = x_ref[0] + gate_ref[0] * o_ref[0]
        if final_norm:
            y = y * lax.rsqrt(jnp.mean(y * y, axis=-1, keepdims=True) + NORM_EPS) * fg_ref[...]
        o_ref[0] = y


def _mlp(x, g, shift, scale, gate, w1, w2, final_g, tm, final_norm):
    b, t, d = x.shape
    ff = w1.shape[1]
    tf = _col_tile(ff, MLP_MAX_HIDDEN)
    vec3 = pl.BlockSpec((1, 1, d), lambda bi, i, f: (bi, 0, 0))
    vec2 = pl.BlockSpec((1, d), lambda bi, i, f: (0, 0))
    return pl.pallas_call(
        functools.partial(_mlp_kernel, final_norm=final_norm),
        out_shape=jax.ShapeDtypeStruct((b, t, d), F32),
        grid=(b, t // tm, ff // tf),
        in_specs=[pl.BlockSpec((1, tm, d), lambda bi, i, f: (bi, i, 0)), vec2, vec3, vec3, vec3,
                  pl.BlockSpec((d, tf), lambda bi, i, f: (0, f)),
                  pl.BlockSpec((tf, d), lambda bi, i, f: (f, 0)), vec2],
        out_specs=pl.BlockSpec((1, tm, d), lambda bi, i, f: (bi, i, 0)),
        scratch_shapes=[pltpu.VMEM((tm, d), MXU)],
        compiler_params=_cparams(("parallel", "parallel", "arbitrary")),
        name="mlp",
    )(x, g.reshape(1, d), shift, scale, gate, w1, w2, final_g.reshape(1, d))


NA_GROUP_OFFSETS = (0, NA_ROWS // 2, NA_ROWS)
NA_STRIP_PAD = NA_WIN - NA_ROWS
NA_STRIP_BLOCKS = 2 * NA_WIN


def _na_window_lo(typ, local_row):
    return (0, local_row, NA_WIN - NA_ROWS)[typ]


def _na_kernel(q_ref, k_ref, v_ref, kc_ref, vc_ref, tc_ref, o_ref, bias_scr):
    g = pl.program_id(2)
    n_g = pl.num_programs(2)
    n_keys = NA_WIN * GRID_W
    n_blocks = q_ref.shape[2] // LANES
    tq = NA_GROUP * GRID_W

    @pl.when(jnp.logical_and(pl.program_id(1) == 0, g == 0))
    def _():
        key_row = _lane((GRID_W, n_keys)) // GRID_W
        for blk in range(n_blocks):
            for typ in range(3):
                for half in range(HEADS_PER_BLOCK):
                    for lr in range(NA_GROUP):
                        rho = NA_GROUP_OFFSETS[typ] + lr
                        off = (NA_WIN - 1 - rho) * GRID_W
                        strip = tc_ref[blk, half, :, off:off + n_keys]
                        lo = _na_window_lo(typ, lr)
                        ok = jnp.logical_and(key_row >= lo, key_row < lo + NA_ROWS)
                        r0 = (half * NA_GROUP + lr) * GRID_W
                        bias_scr[blk, typ, r0:r0 + GRID_W, :] = jnp.where(ok, strip, MASK_VALUE)

    win_row = jnp.clip(g * NA_GROUP - NA_ROWS // 2, 0, n_g * NA_GROUP - NA_WIN)
    typ = jnp.where(g == 0, 0, jnp.where(g == n_g - 1, 2, 1))
    start = pl.multiple_of(win_row * GRID_W, GRID_W)
    lanes = lambda blk: slice(blk * LANES, (blk + 1) * LANES)
    block = lambda h: lanes(h // HEADS_PER_BLOCK)
    chunks = [(lambda blk: k_ref[0, pl.ds(start, n_keys), lanes(blk)],
               lambda h: v_ref[0, pl.ds(start, n_keys), block(h)],
               lambda h: bias_scr[h // HEADS_PER_BLOCK, typ, pl.ds((h % HEADS_PER_BLOCK) * tq, tq), :]),
              (lambda blk: kc_ref[0, :, lanes(blk)], lambda h: vc_ref[0, :, block(h)], None)]
    outs = _softmax_pv([q_ref[0, :, lanes(blk)] for blk in range(n_blocks)], chunks, keys_on_rows=False)
    for blk, o in enumerate(outs):
        o_ref[0, :, lanes(blk)] = o.astype(o_ref.dtype)


def _na_col_table(rpb, n_heads):
    col = np.arange(GRID_W)
    c0 = np.clip(col - NA_COLS // 2, 0, GRID_W - NA_COLS)
    col_ok = (col[None, :] >= c0[:, None]) & (col[None, :] < c0[:, None] + NA_COLS)
    dcol = col[None, :] - col[:, None] + NA_COLS - 1
    n_dr, n_dc = 2 * NA_ROWS - 1, 2 * NA_COLS - 1
    onehot = np.concatenate([(dcol[None] == np.arange(n_dc)[:, None, None]) & col_ok[None],
                             ~col_ok[None], np.ones((1, GRID_W, GRID_W), bool)])
    vals = jnp.concatenate([rpb * LOG2E, jnp.full((n_heads, n_dr, 1), MASK_VALUE, F32),
                            jnp.zeros((n_heads, n_dr, 1), F32)], axis=2)
    pad_rows = jnp.zeros((n_heads, 1, n_dc + 2), F32).at[:, :, n_dc + 1].set(MASK_VALUE)
    vals = jnp.concatenate([jnp.tile(pad_rows, (1, NA_STRIP_PAD, 1)), vals,
                            jnp.tile(pad_rows, (1, NA_STRIP_BLOCKS - NA_STRIP_PAD - n_dr, 1))], axis=1)
    tc = jnp.einsum("hrd,dqk->hqrk", vals, jnp.asarray(onehot, F32), precision=lax.Precision.HIGHEST)
    return tc.reshape(n_heads // 2, HEADS_PER_BLOCK, GRID_W, NA_STRIP_BLOCKS * GRID_W)


def _na_attn(qkv, kvc, col_table, width):
    b, t, _ = qkv.shape
    ctx = kvc.shape[1]
    bw = NA_STEP_BLOCKS * LANES
    nb = width // bw
    tq = NA_GROUP * GRID_W
    rows = t // GRID_W
    assert rows % NA_GROUP == 0 and rows >= NA_WIN and width % bw == 0
    return pl.pallas_call(
        _na_kernel,
        out_shape=jax.ShapeDtypeStruct((b, t, width), MXU),
        grid=(nb, b, rows // NA_GROUP),
        in_specs=[pl.BlockSpec((1, tq, bw), lambda p, bi, g: (bi, g, p)),
                  pl.BlockSpec((1, t, bw), lambda p, bi, g: (bi, 0, nb + p)),
                  pl.BlockSpec((1, t, bw), lambda p, bi, g: (bi, 0, 2 * nb + p)),
                  pl.BlockSpec((1, ctx, bw), lambda p, bi, g: (bi, 0, p)),
                  pl.BlockSpec((1, ctx, bw), lambda p, bi, g: (bi, 0, nb + p)),
                  pl.BlockSpec((NA_STEP_BLOCKS, HEADS_PER_BLOCK, GRID_W, NA_STRIP_BLOCKS * GRID_W),
                               lambda p, bi, g: (p, 0, 0, 0))],
        out_specs=pl.BlockSpec((1, tq, bw), lambda p, bi, g: (bi, g, p)),
        scratch_shapes=[pltpu.VMEM((NA_STEP_BLOCKS, 3, HEADS_PER_BLOCK * tq, NA_WIN * GRID_W), F32)],
        compiler_params=_cparams(("arbitrary", "arbitrary", "arbitrary")),
        name="na_attn",
    )(qkv, qkv, qkv, kvc, kvc, col_table)


def _rope_tables(t):
    pos = jnp.arange(t, dtype=jnp.int32)
    row = (pos // GRID_W).astype(F32)
    col = (pos % GRID_W).astype(F32)
    pairs = HEAD_DIM // 4
    inv = ROPE_THETA ** (-jnp.arange(pairs, dtype=F32) / pairs)
    ang = jnp.concatenate([row[:, None] * inv, col[:, None] * inv], axis=-1)
    cos, sin = jnp.cos(ang), jnp.sin(ang)
    cos_h = jnp.concatenate([cos, cos], axis=-1)
    sin_h = jnp.concatenate([-sin, sin], axis=-1)
    return jnp.tile(cos_h, (1, HEADS_PER_BLOCK)), jnp.tile(sin_h, (1, HEADS_PER_BLOCK))


def _block_diag2(top, bottom):
    z_t = jnp.zeros_like(top)
    z_b = jnp.zeros_like(bottom)
    return jnp.concatenate([jnp.concatenate([top, z_t], axis=1), jnp.concatenate([z_b, bottom], axis=1)], axis=0)


def kernel(x, c, ctx, c_ctx, l0_norm1, l0_norm2, l0_ada_w, l0_ada_b, l0_w_in, l0_shift_mu, l0_w0_f, l0_w0_b, l0_ww2_f, l0_ww2_b, l0_a0_f, l0_a0_b, l0_wa2_f, l0_wa2_b, l0_wg2, l0_k_k, l0_k_a, l0_r_k, l0_lnx_g, l0_lnx_b, l0_q_norm, l0_k_norm, l0_w_out, l0_mlp_w1, l0_mlp_w2, l1_norm1, l1_norm2, l1_ada_w, l1_ada_b, l1_w_qkv, l1_rpb, l1_w_out, l1_mlp_w1, l1_mlp_w2, final_norm):
    b, t, d = x.shape
    n_ctx = ctx.shape[1]
    dim = l0_w0_f.shape[0]
    q_width = l0_w_out.shape[0] - dim
    gqa_cols = l0_w_in.shape[1] - l0_shift_mu.shape[0]
    kv_width = (gqa_cols - q_width) // 2
    rw_cols = l0_shift_mu.shape[0]
    rw_pad = 3 * dim + 4 * LANES
    assert rw_cols <= rw_pad and 3 * dim + 2 * LANES == rw_cols - l0_wg2.shape[0]
    tm = min(ROW_TILE, t)
    tm_c = n_ctx

    assert b < SUBLANES
    cc = jnp.zeros((SUBLANES, d), F32).at[:b].set(c).at[b].set(c_ctx)

    def modulation(ada_w, ada_b):
        mod = _ada_mod(cc, ada_w, ada_b)
        lat = mod[:b].reshape(b, 6, 1, d)
        cx = jnp.broadcast_to(mod[b].reshape(1, 6, 1, d), (b, 6, 1, d))
        return [lat[:, i] for i in range(6)], [cx[:, i] for i in range(6)]

    mod_l, mod_c = modulation(l0_ada_w, l0_ada_b)

    w_in = l0_w_in.astype(MXU)
    w_rw = jnp.pad(w_in[:, gqa_cols:], ((0, 0), (0, rw_pad - rw_cols)))
    mu = jnp.pad(l0_shift_mu, (0, rw_pad - rw_cols)).reshape(1, rw_pad)
    lw_w = _block_diag2(l0_ww2_f, l0_ww2_b).astype(MXU)
    la_w = _block_diag2(l0_wa2_f, l0_wa2_b).astype(MXU)
    lg_w = jnp.pad(l0_wg2, ((0, 2 * LANES - l0_wg2.shape[0]), (0, 0))).astype(MXU)
    w0 = jnp.concatenate([l0_w0_f, l0_w0_b]).reshape(1, 2 * dim)
    a0 = jnp.concatenate([l0_a0_f, l0_a0_b]).reshape(1, 2 * dim)
    k_k = l0_k_k.reshape(1, dim)
    k_a = l0_k_a.reshape(1, dim)
    r_k = l0_r_k.reshape(1, dim)
    qg = jnp.tile(l0_q_norm, HEADS_PER_BLOCK).reshape(1, LANES)
    kg = jnp.tile(l0_k_norm, HEADS_PER_BLOCK).reshape(1, LANES)
    cos_l, sin_l = _rope_tables(t)
    cos_c, sin_c = jnp.ones((n_ctx, LANES), F32), jnp.zeros((n_ctx, LANES), F32)
    w_out = l0_w_out.astype(MXU)

    def half_layer0(xs, mod, tm_, cos, sin, is_ctx):
        q, kd, vd = _inproj_gqa(xs, l0_norm1, mod[0], mod[1], w_in, cos, sin, qg, kg, q_width, kv_width, tm_)
        pr = _inproj(xs, l0_norm1, mod[0], mod[1], w_rw, tm_, F32)
        prep = _rwkv_prep(pr, mu, lw_w, la_w, lg_w, w0, a0, k_k, k_a, r_k, dim, RWKV_PREP_ROWS, is_ctx)
        return q, kd, vd, prep

    q_c, kd_c, vd_c, prep_c = half_layer0(ctx, mod_c, tm_c, cos_c, sin_c, True)
    q_l, kd_l, vd_l, prep_l = half_layer0(x, mod_l, tm, cos_l, sin_l, False)

    o_gqa_l = _gqa_attn(q_l, [(kd_l, vd_l), (kd_c, vd_c)], min(GQA_QUERY_TILE, t))
    o_gqa_c = _gqa_attn(q_c, [(kd_c, vd_c)], n_ctx)

    h0 = jnp.zeros((2, b, dim // LANES, LANES, LANES), F32)
    y_c, h_c = _rwkv_scan(*prep_c[:6], h0)
    y_l, _ = _rwkv_scan(*prep_l[:6], h_c)
    o_rw_l = _rwkv_finish(y_l, prep_l[6], prep_l[7], l0_lnx_g, l0_lnx_b, tm)
    o_rw_c = _rwkv_finish(y_c, prep_c[6], prep_c[7], l0_lnx_g, l0_lnx_b, tm_c)

    w1 = l0_mlp_w1.astype(MXU)
    w2 = l0_mlp_w2.astype(MXU)
    x = _outproj(x, mod_l[2], [(o_gqa_l, w_out, 0), (o_rw_l, w_out, q_width)], tm)
    x = _mlp(x, l0_norm2, mod_l[3], mod_l[4], mod_l[5], w1, w2, final_norm, min(MLP_ROW_TILE, t), False)
    ctx = _outproj(ctx, mod_c[2], [(o_gqa_c, w_out, 0), (o_rw_c, w_out, q_width)], tm_c)
    ctx = _mlp(ctx.reshape(1, b * n_ctx, d), l0_norm2, mod_c[3][:1], mod_c[4][:1], mod_c[5][:1], w1, w2, final_norm,
               _col_tile(b * n_ctx, MLP_ROW_TILE), False).reshape(b, n_ctx, d)

    mod_l, mod_c = modulation(l1_ada_w, l1_ada_b)
    width = l1_w_out.shape[0]
    n_heads = width // HEAD_DIM
    scale = jnp.concatenate([jnp.full((width,), ATTN_SCALE, F32), jnp.ones((2 * width,), F32)])
    w_qkv = (l1_w_qkv * scale).astype(MXU)
    qkv = _inproj(x, l1_norm1, mod_l[0], mod_l[1], w_qkv, min(2 * tm, t), MXU)
    kvc = _inproj(ctx, l1_norm1, mod_c[0], mod_c[1], w_qkv, tm_c, MXU, col0=width, n=2 * width)
    o_na = _na_attn(qkv, kvc, _na_col_table(l1_rpb, n_heads), width)
    x = _outproj(x, mod_l[2], [(o_na, l1_w_out.astype(MXU), 0)], tm)
    x = _mlp(x, l1_norm2, mod_l[3], mod_l[4], mod_l[5], l1_mlp_w1.astype(MXU), l1_mlp_w2.astype(MXU),
             final_norm, min(MLP_ROW_TILE, t), True)
    return x
```
